```python
import math
import jax
import jax.numpy as jnp
from jax import lax
import numpy as np

D_MODEL = 1024
BATCH = 4
SEQ = 8192
DEPTH = 4

GRID_W = 64
CTX_LEN = 256

GLA_HEADS = 4
GLA_DK = 64
GLA_DV = 128
GLA_RANK = 16
GLA_TAU = 16.0
GLA_CHUNK = 64
GQA_HEADS = 8
GQA_KV_HEADS = 2
GQA_HD = 64
Q_BLOCK = 128
ROPE_THETA = 10000.0
SSD_HEADS = 12
SSD_HD = 64
SSD_GROUPS = 2
SSD_HPG = SSD_HEADS // SSD_GROUPS
SSD_STATE = 128
SSD_CONV = 5
SSD_CHUNK = 64
FNET_GROUPS = 4
FNET_GC = 64
N_EXPERTS = 16
N_EXPERT_GROUPS = 4
EXPERTS_PER_GROUP = N_EXPERTS // N_EXPERT_GROUPS
TOP_K = 2
D_EXPERT = 768
MOE_BLOCK = 256

GLA_QK = GLA_HEADS * GLA_DK
GLA_V = GLA_HEADS * GLA_DV
GQA_Q = GQA_HEADS * GQA_HD
GQA_KV = GQA_KV_HEADS * GQA_HD
EVEN_SPLITS = (GLA_QK, GLA_QK, GLA_V, GLA_V, GLA_RANK, GLA_RANK, GQA_Q, GQA_KV, GQA_KV)
EVEN_IN = sum(EVEN_SPLITS)
EVEN_OUT = GLA_V + GQA_Q
SSD_INNER = SSD_HEADS * SSD_HD
SSD_BC = SSD_GROUPS * SSD_STATE
SSD_CONV_CH = SSD_INNER + 2 * SSD_BC
FNET_W = FNET_GROUPS * FNET_GC
ODD_SPLITS = (SSD_INNER, SSD_INNER, SSD_BC, SSD_BC, SSD_HEADS, SSD_HEADS, FNET_W)
ODD_IN = sum(ODD_SPLITS)
ODD_OUT = SSD_INNER + FNET_W

kernel_name = 'hybrid_gla_gqa_ssd_fnet_moe_dit'

F32 = jnp.float32


def layer_norm(x, g, b, eps=1e-6):
    xf = x.astype(F32)
    mu = jnp.mean(xf, -1, keepdims=True)
    var = jnp.mean(jnp.square(xf - mu), -1, keepdims=True)
    return ((xf - mu) * lax.rsqrt(var + eps)).astype(x.dtype) * g + b


def rms_norm(x, g, eps=1e-6):
    xf = x.astype(F32)
    return (xf * lax.rsqrt(jnp.mean(xf * xf, -1, keepdims=True) + eps)).astype(x.dtype) * g


def modulate(x, shift, scale):
    return x * (1 + scale) + shift


def split_cols(t, sizes):
    return jnp.split(t, [int(s) for s in np.cumsum(sizes)[:-1]], axis=-1)


def to_heads(t, n_heads):
    b, s, _ = t.shape
    return t.reshape(b, s, n_heads, -1).transpose(0, 2, 1, 3)


def from_heads(t):
    b, h, s, d = t.shape
    return t.transpose(0, 2, 1, 3).reshape(b, s, h * d)


def axial_rope_tables(rows, head_dim):
    r = jnp.repeat(jnp.arange(rows, dtype=F32), GRID_W)
    col = jnp.tile(jnp.arange(GRID_W, dtype=F32), rows)
    half = head_dim // 2
    inv = ROPE_THETA ** (-jnp.arange(0, half, 2, dtype=F32) / half)
    ar = r[:, None] * inv
    ac = col[:, None] * inv
    ang = jnp.concatenate([ar, ar, ac, ac], -1)
    return jnp.cos(ang), jnp.sin(ang)


def apply_axial_rope(x, cos, sin):
    x1, x2, x3, x4 = jnp.split(x, 4, axis=-1)
    rot = jnp.concatenate([-x2, x1, -x4, x3], -1)
    return (x * cos + rot * sin).astype(x.dtype)


def gla_scan(q, k, v, log_a, s0):
    bsz, h, t, dk = q.shape
    dv = v.shape[-1]
    L = GLA_CHUNK
    n = t // L
    qf = q.astype(F32).reshape(bsz, h, n, L, dk)
    kf = k.astype(F32).reshape(bsz, h, n, L, dk)
    vf = v.astype(F32).reshape(bsz, h, n, L, dv)
    b = jnp.cumsum(log_a.astype(F32).reshape(bsz, h, n, L, dk), axis=3)
    b_mid = b[:, :, :, L // 2 - 1:L // 2]
    b_end = b[:, :, :, -1:]
    mask = jnp.tril(jnp.ones((L, L), bool))
    att = jnp.einsum('bhnld,bhnmd->bhnlm', qf * jnp.exp(b - b_mid), kf * jnp.exp(b_mid - b))
    o_intra = jnp.einsum('bhnlm,bhnmv->bhnlv', jnp.where(mask, att, 0.0), vf)
    ds = jnp.einsum('bhnld,bhnlv->bhndv', kf * jnp.exp(b_end - b), vf)
    decay = jnp.exp(b_end[:, :, :, 0])

    def step(s, inp):
        d, dsn = inp
        return d[..., None] * s + dsn, s

    s_fin, s_starts = lax.scan(step, s0.astype(F32), (jnp.moveaxis(decay, 2, 0), jnp.moveaxis(ds, 2, 0)))
    s_starts = jnp.moveaxis(s_starts, 0, 2)
    o_inter = jnp.einsum('bhnld,bhndv->bhnlv', qf * jnp.exp(b), s_starts)
    return (o_intra + o_inter).reshape(bsz, h, t, dv).astype(v.dtype), s_fin


def ssd_scan(x, a, bm, cm, h0):
    bsz, t, g, hg, p = x.shape
    ns = bm.shape[-1]
    L = SSD_CHUNK
    n = t // L
    x = x.astype(F32).reshape(bsz, n, L, g, hg, p)
    a = a.astype(F32).reshape(bsz, n, L, g, hg)
    bm = bm.astype(F32).reshape(bsz, n, L, g, ns)
    cm = cm.astype(F32).reshape(bsz, n, L, g, ns)
    cum = jnp.cumsum(a, axis=2)
    seg = cum[:, :, :, None] - cum[:, :, None, :]
    mask = jnp.tril(jnp.ones((L, L), bool))[None, None, :, :, None, None]
    decay_ts = jnp.exp(jnp.where(mask, seg, -jnp.inf))
    cb = jnp.einsum('bnlgk,bnmgk->bnlmg', cm, bm)
    y_intra = jnp.einsum('bnlmgh,bnmghp->bnlghp', cb[..., None] * decay_ts, x)
    x_end = x * jnp.exp(cum[:, :, -1:] - cum)[..., None]
    dh = jnp.einsum('bnlgk,bnlghp->bnghpk', bm, x_end)
    chunk_decay = jnp.exp(cum[:, :, -1])

    def step(hs, inp):
        d, dhn = inp
        return d[..., None, None] * hs + dhn, hs

    h_fin, h_starts = lax.scan(step, h0.astype(F32), (jnp.moveaxis(chunk_decay, 1, 0), jnp.moveaxis(dh, 1, 0)))
    h_starts = jnp.moveaxis(h_starts, 0, 1)
    y_inter = jnp.einsum('bnlgk,bnghpk->bnlghp', cm, h_starts) * jnp.exp(cum)[..., None]
    return (y_intra + y_inter).reshape(bsz, t, g, hg, p), h_fin


def attend(q, k, v):
    s = jnp.einsum('bgrqd,bgkd->bgrqk', q, k).astype(F32) * (q.shape[-1] ** -0.5)
    p = jax.nn.softmax(s, axis=-1).astype(v.dtype)
    return jnp.einsum('bgrqk,bgkd->bgrqd', p, v)


def gqa_attention(q, k, v, block):
    b, hq, t, hd = q.shape
    hkv = k.shape[1]
    nb = t // block
    qb = jnp.moveaxis(q.reshape(b, hkv, hq // hkv, nb, block, hd), 3, 0)
    o = lax.map(lambda qi: attend(qi, k, v), qb)
    return jnp.moveaxis(o, 0, 3).reshape(b, hq, t, hd)


def dwconv_centered(x, w, bias):
    kw, ch = w.shape
    y = lax.conv_general_dilated(x, w[:, None, :].astype(x.dtype), window_strides=(1,),
                                 padding=[((kw - 1) // 2, kw // 2)],
                                 dimension_numbers=('NWC', 'WIO', 'NWC'), feature_group_count=ch)
    return y + bias


def fourier_mix(u, g):
    b, t, _ = u.shape
    un = rms_norm(u.reshape(b, t, FNET_GROUPS, FNET_GC), g.reshape(FNET_GROUPS, FNET_GC))
    f = jnp.fft.fft2(un.astype(F32), axes=(1, 3), norm='ortho').real
    return f.reshape(b, t, FNET_W).astype(u.dtype)


def even_mixer(u_ctx, u_lat, w_in, w_o, w_dec, b_dec, gla_g, qn_g, kn_g, cos, sin, last):
    pc = split_cols(u_ctx @ w_in, EVEN_SPLITS)
    pl = split_cols(u_lat @ w_in, EVEN_SPLITS)

    def gla_inputs(p):
        q, k, v, r, lf, lb = p[:6]
        ld = [to_heads(jax.nn.log_sigmoid((lr @ w_dec[di] + b_dec[di]).astype(F32)) / GLA_TAU, GLA_HEADS)
              for di, lr in enumerate((lf, lb))]
        return (to_heads(q, GLA_HEADS) * (GLA_DK ** -0.5), to_heads(k, GLA_HEADS), to_heads(v, GLA_HEADS),
                ld[0], ld[1], r)

    qc, kc, vc, lfc, lbc, rc = gla_inputs(pc)
    ql, kl, vl, lfl, lbl, rl = gla_inputs(pl)
    fl = lambda t: jnp.flip(t, 2)
    zero = jnp.zeros((qc.shape[0], GLA_HEADS, GLA_DK, GLA_DV), F32)
    oc_f, sc_f = gla_scan(qc, kc, vc, lfc, zero)
    oc_b, sc_b = gla_scan(fl(qc), fl(kc), fl(vc), fl(lbc), zero)
    ol_f, _ = gla_scan(ql, kl, vl, lfl, sc_f)
    ol_b, _ = gla_scan(fl(ql), fl(kl), fl(vl), fl(lbl), sc_b)

    def gla_out(of, ob_rev, r):
        return from_heads(rms_norm(of + fl(ob_rev), gla_g)) * jax.nn.silu(r)

    def gqa_inputs(p, rope):
        q, k, v = p[6:]
        q = rms_norm(to_heads(q, GQA_HEADS), qn_g)
        k = rms_norm(to_heads(k, GQA_KV_HEADS), kn_g)
        if rope:
            q = apply_axial_rope(q, cos, sin)
            k = apply_axial_rope(k, cos, sin)
        return q, k, to_heads(v, GQA_KV_HEADS)

    qgc, kgc, vgc = gqa_inputs(pc, False)
    qgl, kgl, vgl = gqa_inputs(pl, True)
    b_lat = from_heads(gqa_attention(qgl, jnp.concatenate([kgc, kgl], 2), jnp.concatenate([vgc, vgl], 2), Q_BLOCK))
    o_lat = jnp.concatenate([gla_out(ol_f, ol_b, rl), b_lat], -1) @ w_o
    if last:
        return None, o_lat
    b_ctx = from_heads(gqa_attention(qgc, kgc, vgc, qgc.shape[2]))
    o_ctx = jnp.concatenate([gla_out(oc_f, oc_b, rc), b_ctx], -1) @ w_o
    return o_ctx, o_lat


def odd_mixer(u_ctx, u_lat, w_in, w_o, conv_w, conv_b, dt_bias, a_log, d_skip, ssd_g, fnet_g, last):
    def prep(u):
        bsz, t, _ = u.shape
        z, xs, bs, cs, dtf, dtb, f = split_cols(u @ w_in, ODD_SPLITS)
        xbc = jax.nn.silu(dwconv_centered(jnp.concatenate([xs, bs, cs], -1), conv_w, conv_b))
        xs, bs, cs = split_cols(xbc, (SSD_INNER, SSD_BC, SSD_BC))
        xs = xs.reshape(bsz, t, SSD_GROUPS, SSD_HPG, SSD_HD)
        bs = bs.reshape(bsz, t, SSD_GROUPS, SSD_STATE)
        cs = cs.reshape(bsz, t, SSD_GROUPS, SSD_STATE)
        dirs = []
        for di, dt_raw in enumerate((dtf, dtb)):
            dt = jax.nn.softplus(dt_raw.astype(F32) + dt_bias[di]).reshape(bsz, t, SSD_GROUPS, SSD_HPG)
            a = dt * (-jnp.exp(a_log[di].astype(F32))).reshape(SSD_GROUPS, SSD_HPG)
            dirs.append((xs * dt[..., None], a))
        return z, xs, bs, cs, dirs, f

    zc, xc, bc, cc, dc, fc = prep(u_ctx)
    zl, xl, bl, cl, dl, fl_ = prep(u_lat)
    fl = lambda t: jnp.flip(t, 1)
    h0 = jnp.zeros((xc.shape[0], SSD_GROUPS, SSD_HPG, SSD_HD, SSD_STATE), F32)
    yc_f, hc_f = ssd_scan(dc[0][0], dc[0][1], bc, cc, h0)
    yc_b, hc_b = ssd_scan(fl(dc[1][0]), fl(dc[1][1]), fl(bc), fl(cc), h0)
    yl_f, _ = ssd_scan(dl[0][0], dl[0][1], bl, cl, hc_f)
    yl_b, _ = ssd_scan(fl(dl[1][0]), fl(dl[1][1]), fl(bl), fl(cl), hc_b)

    def ssd_out(yf, yb_rev, xs, z):
        bsz, t, _ = z.shape
        y = yf + fl(yb_rev) + xs.astype(F32) * d_skip.astype(F32).reshape(SSD_GROUPS, SSD_HPG, 1)
        y = y.reshape(bsz, t, SSD_INNER).astype(z.dtype) * jax.nn.silu(z)
        return rms_norm(y.reshape(bsz, t, SSD_GROUPS, -1), ssd_g.reshape(SSD_GROUPS, -1)).reshape(bsz, t, SSD_INNER)

    o_lat = jnp.concatenate([ssd_out(yl_f, yl_b, xl, zl), fourier_mix(fl_, fnet_g)], -1) @ w_o
    if last:
        return None, o_lat
    o_ctx = jnp.concatenate([ssd_out(yc_f, yc_b, xc, zc), fourier_mix(fc, fnet_g)], -1) @ w_o
    return o_ctx, o_lat


def route(h, router_w, router_b):
    n = h.shape[0]
    s = jax.nn.sigmoid((h @ router_w).astype(F32))
    sel = (s + router_b).reshape(n, N_EXPERT_GROUPS, EXPERTS_PER_GROUP)
    g_idx = jnp.argmax(lax.top_k(sel, TOP_K)[0].sum(-1), axis=-1)
    in_group = jnp.take_along_axis(sel, g_idx[:, None, None], axis=1)[:, 0]
    _, loc = lax.top_k(in_group, TOP_K)
    idx = (g_idx[:, None] * EXPERTS_PER_GROUP + loc).astype(jnp.int32)
    w = jnp.take_along_axis(s, idx, axis=1)
    return idx, w / jnp.sum(w, -1, keepdims=True)


def moe_ffn(h, idx, w, w_gate, w_up, w_down):
    n, d = h.shape
    e = w_gate.shape[0]
    m = MOE_BLOCK
    na = n * TOP_K
    flat_e = idx.reshape(-1)
    order = jnp.argsort(flat_e)
    e_sorted = flat_e[order]
    counts = jnp.zeros((e,), jnp.int32).at[flat_e].add(1)
    padded = (counts + m - 1) // m * m
    pad_end = jnp.cumsum(padded)
    pad_start = pad_end - padded
    start = jnp.cumsum(counts) - counts
    dest = pad_start[e_sorted] + jnp.arange(na, dtype=jnp.int32) - start[e_sorted]
    n_blocks = -(-na // m) + e
    tok = order // TOP_K
    buf = jnp.zeros((n_blocks * m, d), h.dtype).at[dest].set(h[tok])
    block_e = jnp.minimum(jnp.searchsorted(pad_end, jnp.arange(n_blocks, dtype=jnp.int32) * m, side='right'), e - 1)

    def expert_block(args):
        xb, ei = args
        return (jax.nn.silu(xb @ w_gate[ei]) * (xb @ w_up[ei])) @ w_down[ei]

    yb = lax.map(expert_block, (buf.reshape(n_blocks, m, d), block_e)).reshape(n_blocks * m, d)
    y = yb[dest] * w.reshape(-1)[order][:, None].astype(h.dtype)
    return jnp.zeros_like(h).at[tok].add(y)


def setup_inputs(seed: int = 0) -> dict:
    key = jax.random.key(seed)
    ks = iter(jax.random.split(key, 40))
    nrm = lambda shape, scale: jax.random.normal(next(ks), shape, F32) * scale
    beta = (8.0 * DEPTH) ** -0.25
    ne, no = (DEPTH + 1) // 2, DEPTH // 2
    D = D_MODEL
    dt0 = jnp.exp(jax.random.uniform(next(ks), (no, 2, SSD_HEADS), F32, math.log(1e-3), math.log(1e-1)))
    return {
        'x': nrm((BATCH, SEQ, D), 1.0),
        'c': nrm((BATCH, D), 1.0),
        'ctx': nrm((BATCH, CTX_LEN, D), 1.0),
        'c_ctx': nrm((D,), 1.0),
        'ada_w': nrm((DEPTH, D, 6 * D), 0.5 * D ** -0.5),
        'ada_b': nrm((DEPTH, 6 * D), 0.02),
        'ln_g': 1.0 + nrm((DEPTH, 2, D), 0.02),
        'ln_b': nrm((DEPTH, 2, D), 0.02),
        'ev_w_in': nrm((ne, D, EVEN_IN), D ** -0.5),
        'ev_w_o': nrm((ne, EVEN_OUT, D), beta * EVEN_OUT ** -0.5),
        'gla_w_decay': nrm((ne, 2, GLA_RANK, GLA_QK), GLA_RANK ** -0.5),
        'gla_b_decay': 1.0 + nrm((ne, 2, GLA_QK), 0.1),
        'gla_norm_g': 1.0 + nrm((ne, GLA_DV), 0.02),
        'gqa_q_norm_g': 1.0 + nrm((ne, GQA_HD), 0.02),
        'gqa_k_norm_g': 1.0 + nrm((ne, GQA_HD), 0.02),
        'od_w_in': nrm((no, D, ODD_IN), D ** -0.5),
        'od_w_o': nrm((no, ODD_OUT, D), beta * ODD_OUT ** -0.5),
        'ssd_conv_w': nrm((no, SSD_CONV, SSD_CONV_CH), SSD_CONV ** -0.5),
        'ssd_conv_b': nrm((no, SSD_CONV_CH), 0.02),
        'ssd_dt_bias': dt0 + jnp.log(-jnp.expm1(-dt0)),
        'ssd_a_log': jnp.log(jax.random.uniform(next(ks), (no, 2, SSD_HEADS), F32, 1.0, 16.0)),
        'ssd_d': 1.0 + nrm((no, SSD_HEADS), 0.02),
        'ssd_norm_g': 1.0 + nrm((no, SSD_INNER), 0.02),
        'fnet_norm_g': 1.0 + nrm((no, FNET_W), 0.02),
        'router_w': nrm((D, N_EXPERTS), D ** -0.5),
        'router_b': nrm((N_EXPERTS,), 0.01),
        'exp_w_gate': nrm((DEPTH, N_EXPERTS, D, D_EXPERT), D ** -0.5),
        'exp_w_up': nrm((DEPTH, N_EXPERTS, D, D_EXPERT), D ** -0.5),
        'exp_w_down': nrm((DEPTH, N_EXPERTS, D_EXPERT, D), beta * D_EXPERT ** -0.5),
    }


def reference(x, c, ctx, c_ctx, ada_w, ada_b, ln_g, ln_b, ev_w_in, ev_w_o, gla_w_decay, gla_b_decay,
              gla_norm_g, gqa_q_norm_g, gqa_k_norm_g, od_w_in, od_w_o, ssd_conv_w, ssd_conv_b,
              ssd_dt_bias, ssd_a_log, ssd_d, ssd_norm_g, fnet_norm_g, router_w, router_b,
              exp_w_gate, exp_w_up, exp_w_down):
    alpha = (2.0 * DEPTH) ** 0.25
    bsz, seq, d = x.shape
    rows = seq // GRID_W
    cos, sin = axial_rope_tables(rows, GQA_HD)
    xc = ctx
    for layer in range(DEPTH):
        last = layer == DEPTH - 1
        m_lat = jnp.split((jax.nn.silu(c) @ ada_w[layer] + ada_b[layer])[:, None, :], 6, axis=-1)
        m_ctx = jnp.split(jax.nn.silu(c_ctx) @ ada_w[layer] + ada_b[layer], 6, axis=-1)
        u_lat = modulate(x, m_lat[0], m_lat[1])
        u_ctx = modulate(xc, m_ctx[0], m_ctx[1])
        i = layer // 2
        if layer % 2 == 0:
            o_ctx, o_lat = even_mixer(u_ctx, u_lat, ev_w_in[i], ev_w_o[i], gla_w_decay[i], gla_b_decay[i],
                                      gla_norm_g[i], gqa_q_norm_g[i], gqa_k_norm_g[i], cos, sin, last)
        else:
            o_ctx, o_lat = odd_mixer(u_ctx, u_lat, od_w_in[i], od_w_o[i], ssd_conv_w[i], ssd_conv_b[i],
                                     ssd_dt_bias[i], ssd_a_log[i], ssd_d[i], ssd_norm_g[i], fnet_norm_g[i], last)
        x = layer_norm(alpha * x + m_lat[2] * o_lat, ln_g[layer, 0], ln_b[layer, 0])
        v_lat = modulate(x, m_lat[3], m_lat[4]).reshape(-1, d)
        if last:
            tokens = v_lat
            n_ctx = 0
        else:
            xc = layer_norm(alpha * xc + m_ctx[2] * o_ctx, ln_g[layer, 0], ln_b[layer, 0])
            tokens = jnp.concatenate([modulate(xc, m_ctx[3], m_ctx[4]).reshape(-1, d), v_lat], 0)
            n_ctx = xc.shape[0] * xc.shape[1]
        idx, gates = route(tokens, router_w, router_b)
        y = moe_ffn(tokens, idx, gates, exp_w_gate[layer], exp_w_up[layer], exp_w_down[layer])
        x = layer_norm(alpha * x + m_lat[5] * y[n_ctx:].reshape(bsz, seq, d), ln_g[layer, 1], ln_b[layer, 1])
        if not last:
            xc = layer_norm(alpha * xc + m_ctx[5] * y[:n_ctx].reshape(xc.shape), ln_g[layer, 1], ln_b[layer, 1])
    return x
```

```python
import functools
import math

import jax
import jax.numpy as jnp
import numpy as np
from jax import lax
from jax.experimental import pallas as pl
from jax.experimental.pallas import tpu as pltpu

F32 = jnp.float32
BF16 = jnp.bfloat16
HI = lax.Precision.HIGHEST

D_MODEL = 1024
DEPTH = 4
GRID_W = 64
GLA_HEADS, GLA_DK, GLA_DV, GLA_RANK, GLA_TAU, GLA_CHUNK = 4, 64, 128, 16, 16.0, 64
GQA_HEADS, GQA_KV_HEADS, GQA_HD = 8, 2, 64
ROPE_THETA = 10000.0
SSD_HEADS, SSD_HD, SSD_GROUPS, SSD_STATE, SSD_CONV, SSD_CHUNK = 12, 64, 2, 128, 5, 64
SSD_HPG = SSD_HEADS // SSD_GROUPS
SSD_INNER = SSD_HEADS * SSD_HD
SSD_BC = SSD_GROUPS * SSD_STATE
SSD_CONV_CH = SSD_INNER + 2 * SSD_BC
FNET_GROUPS, FNET_GC = 4, 64
FNET_W = FNET_GROUPS * FNET_GC
N_EXPERTS, N_EXPERT_GROUPS, TOP_K, D_EXPERT = 16, 4, 2, 768
EXPERTS_PER_GROUP = N_EXPERTS // N_EXPERT_GROUPS
GLA_QK = GLA_HEADS * GLA_DK
GLA_V = GLA_HEADS * GLA_DV
GQA_Q = GQA_HEADS * GQA_HD
GQA_KV = GQA_KV_HEADS * GQA_HD
EPS = 1e-6
ALPHA = (2.0 * DEPTH) ** 0.25

LANES = 128
TM = 256
EVEN_W = 2560
ODD_W = 2432
VMEM_LIMIT = 48 * 1024 * 1024


def _cparams(sem):
    return pltpu.CompilerParams(dimension_semantics=sem, vmem_limit_bytes=VMEM_LIMIT)


def _silu(x):
    return x * (1.0 / (1.0 + jnp.exp(-x)))


def _softplus(x):
    return jnp.maximum(x, 0.0) + jnp.log1p(jnp.exp(-jnp.abs(x)))


def _mod_row(i, tiles_per_seq, n_batch):
    return jnp.where(i % tiles_per_seq == 0, n_batch, i // tiles_per_seq)


def _ada_kernel(c_ref, w_ref, b_ref, o_ref):
    s = _silu(c_ref[...])
    o_ref[0] = jnp.dot(s, w_ref[0], precision=HI, preferred_element_type=F32) + b_ref[0]


def _ada_all(cc, ada_w, ada_b):
    depth, d, n = ada_w.shape
    tn = 512
    return pl.pallas_call(
        _ada_kernel,
        grid=(depth, n // tn),
        in_specs=[pl.BlockSpec((8, d), lambda l, j: (0, 0)),
                  pl.BlockSpec((1, d, tn), lambda l, j: (l, 0, j)),
                  pl.BlockSpec((1, 1, tn), lambda l, j: (l, 0, j))],
        out_specs=pl.BlockSpec((1, 8, tn), lambda l, j: (l, 0, j)),
        out_shape=jax.ShapeDtypeStruct((depth, 8, n), F32),
        compiler_params=_cparams(("arbitrary", "arbitrary")),
        name="adaln",
    )(cc, ada_w, ada_b.reshape(depth, 1, n))


def _inproj_kernel(x_ref, mod_ref, w_ref, o_ref):
    shift = mod_ref[0, 0:1, :]
    scale = mod_ref[0, 1:2, :]
    u = (x_ref[...] * (1.0 + scale) + shift).astype(BF16)
    o_ref[...] = jnp.dot(u, w_ref[...], preferred_element_type=F32).astype(o_ref.dtype)


def _inproj(x2, mod, w, tps, nb):
    ntok, d = x2.shape
    nw = w.shape[1]
    return pl.pallas_call(
        _inproj_kernel,
        grid=(ntok // TM,),
        in_specs=[pl.BlockSpec((TM, d), lambda i: (i, 0)),
                  pl.BlockSpec((1, 6, d), lambda i: (_mod_row(i, tps, nb), 0, 0)),
                  pl.BlockSpec((d, nw), lambda i: (0, 0))],
        out_specs=pl.BlockSpec((TM, nw), lambda i: (i, 0)),
        out_shape=jax.ShapeDtypeStruct((ntok, nw), BF16),
        compiler_params=_cparams(("arbitrary",)),
        name="inproj",
    )(x2, mod, w)


def _scan_block(i, nblk, reverse):
    if not reverse:
        return i
    return jnp.where(i == 0, 0, nblk - i)


def _tri(n, reverse):
    r = lax.broadcasted_iota(jnp.int32, (n, n), 0)
    c = lax.broadcasted_iota(jnp.int32, (n, n), 1)
    return (c >= r) if reverse else (c <= r)


def _gla_kernel(q_ref, k_ref, v_ref, lr_ref, wd_ref, bd_ref, o_ref, s_ref, *, reverse):
    L = GLA_CHUNK

    @pl.when(pl.program_id(2) == 0)
    def _():
        s_ref[...] = jnp.zeros_like(s_ref)

    tri = _tri(L, reverse)
    cmat = tri.astype(F32)
    lane = lax.broadcasted_iota(jnp.int32, (L, LANES), 1)
    ones = jnp.ones((L, LANES), F32)
    end = 0 if reverse else L - 1
    mid = L // 2 if reverse else L // 2 - 1
    nchunk = TM // L
    order = range(nchunk - 1, -1, -1) if reverse else range(nchunk)
    for c in order:
        sl = pl.ds(c * L, L)
        q = q_ref[sl, :].astype(F32) * (GLA_DK ** -0.5)
        k = k_ref[sl, :].astype(F32)
        v = v_ref[sl, :]
        z = jnp.dot(lr_ref[sl, :].astype(F32), wd_ref[...], precision=HI, preferred_element_type=F32) + bd_ref[...]
        la = -_softplus(-z) * (1.0 / GLA_TAU)
        b = jnp.dot(cmat, la, precision=HI, preferred_element_type=F32)
        bmid = b[mid:mid + 1, :]
        bend = b[end:end + 1, :]
        qs = q * jnp.exp(b - bmid)
        ks = (k * jnp.exp(bmid - b)).astype(BF16)
        qi = q * jnp.exp(b)
        kend = (k * jnp.exp(bend - b)).astype(BF16)
        s_old = s_ref[...]
        s_bf = s_old.astype(BF16)
        outs = []
        for h in range(2):
            m = (lane < GLA_DK) if h == 0 else (lane >= GLA_DK)
            qh = jnp.where(m, qs, 0.0).astype(BF16)
            att = lax.dot_general(qh, ks, (((1,), (1,)), ((), ())), preferred_element_type=F32)
            att = jnp.where(tri, att, 0.0).astype(BF16)
            vh = v[:, h * GLA_DV:(h + 1) * GLA_DV]
            oh = jnp.dot(att, vh, preferred_element_type=F32)
            oh = oh + jnp.dot(jnp.where(m, qi, 0.0).astype(BF16), s_bf[:, h * GLA_DV:(h + 1) * GLA_DV],
                              preferred_element_type=F32)
            outs.append(oh)
        o_ref[sl, :] = jnp.concatenate(outs, axis=1)
        ds = lax.dot_general(kend, v, (((0,), (0,)), ((), ())), preferred_element_type=F32)
        dec = lax.dot_general(la, ones, (((0,), (0,)), ((), ())), precision=HI, preferred_element_type=F32)
        dec = jnp.exp(jnp.concatenate([dec, dec], axis=1))
        s_ref[...] = dec * s_old + ds


def _gla_scan(p3, wd_pad, bd, reverse):
    nb, t, _ = p3.shape
    nblk = t // TM
    blk = lambda i: _scan_block(i, nblk, reverse)
    return pl.pallas_call(
        functools.partial(_gla_kernel, reverse=reverse),
        grid=(nb, 2, nblk),
        in_specs=[pl.BlockSpec((None, TM, LANES), lambda b, p, i: (b, blk(i), 6 + p)),
                  pl.BlockSpec((None, TM, LANES), lambda b, p, i: (b, blk(i), 8 + p)),
                  pl.BlockSpec((None, TM, 2 * GLA_DV), lambda b, p, i: (b, blk(i), 6 + p)),
                  pl.BlockSpec((None, TM, LANES), lambda b, p, i: (b, blk(i), 10)),
                  pl.BlockSpec((LANES, LANES), lambda b, p, i: (0, p)),
                  pl.BlockSpec((1, LANES), lambda b, p, i: (0, p))],
        out_specs=pl.BlockSpec((None, TM, 2 * GLA_DV), lambda b, p, i: (b, blk(i), p)),
        out_shape=jax.ShapeDtypeStruct((nb, t, GLA_V), F32),
        scratch_shapes=[pltpu.VMEM((2 * GLA_DK, 2 * GLA_DV), F32)],
        compiler_params=_cparams(("arbitrary", "arbitrary", "arbitrary")),
        name="gla_bwd" if reverse else "gla_fwd",
    )(p3, p3, p3, p3, wd_pad, bd)


def _seg_ones(width, seg):
    r = lax.broadcasted_iota(jnp.int32, (width, width), 0) // seg
    c = lax.broadcasted_iota(jnp.int32, (width, width), 1) // seg
    return jnp.where(r == c, 1.0 / seg, 0.0).astype(F32)


def _norm_rope(x, g, cos, sin, bd):
    ms = jnp.dot(x * x, bd, precision=HI, preferred_element_type=F32)
    xn = x * lax.rsqrt(ms + EPS) * g
    lane = lax.broadcasted_iota(jnp.int32, xn.shape, 1)
    quarter = GQA_HD // 4
    up = pltpu.roll(xn, LANES - quarter, 1)
    dn = pltpu.roll(xn, quarter, 1)
    rot = jnp.where(lane % (2 * quarter) < quarter, -up, dn)
    return xn * cos + rot * sin


def _qkprep_kernel(q_ref, k_ref, cos_ref, sin_ref, gq_ref, gk_ref, qo_ref, ko_ref):
    bd = _seg_ones(LANES, GQA_HD)
    cos = cos_ref[...]
    sin = sin_ref[...]
    for j in range(GQA_Q // LANES):
        x = q_ref[:, j * LANES:(j + 1) * LANES].astype(F32)
        y = _norm_rope(x, gq_ref[...], cos, sin, bd) * (GQA_HD ** -0.5)
        qo_ref[:, j * LANES:(j + 1) * LANES] = y.astype(BF16)
    ko_ref[...] = _norm_rope(k_ref[...].astype(F32), gk_ref[...], cos, sin, bd).astype(BF16)


def _qkprep(p2, cos2, sin2, gq2, gk2, tps):
    ntok = p2.shape[0]
    return pl.pallas_call(
        _qkprep_kernel,
        grid=(ntok // TM,),
        in_specs=[pl.BlockSpec((TM, GQA_Q), lambda i: (i, 0)),
                  pl.BlockSpec((TM, LANES), lambda i: (i, 4)),
                  pl.BlockSpec((TM, LANES), lambda i: (i % tps, 0)),
                  pl.BlockSpec((TM, LANES), lambda i: (i % tps, 0)),
                  pl.BlockSpec((1, LANES), lambda i: (0, 0)),
                  pl.BlockSpec((1, LANES), lambda i: (0, 0))],
        out_specs=[pl.BlockSpec((TM, GQA_Q), lambda i: (i, 0)),
                   pl.BlockSpec((TM, LANES), lambda i: (i, 0))],
        out_shape=[jax.ShapeDtypeStruct((ntok, GQA_Q), BF16),
                   jax.ShapeDtypeStruct((ntok, LANES), BF16)],
        compiler_params=_cparams(("arbitrary",)),
        name="qkprep",
    )(p2, p2, cos2, sin2, gq2, gk2)


def _attn_kernel(q_ref, k_ref, v_ref, o_ref, *, nkv):
    i = pl.program_id(1)
    q = q_ref[...]
    lane = lax.broadcasted_iota(jnp.int32, q.shape, 1)
    low = lane < GQA_HD
    zero = jnp.zeros_like(q)
    q2 = jnp.concatenate([jnp.where(low, q, zero), jnp.where(low, zero, q)], axis=0)

    def body(c, carry):
        m, l, acc = carry
        off = pl.multiple_of(c * TM, TM)
        kc = k_ref[pl.ds(off, TM), :]
        vc = v_ref[pl.ds(off, TM), :]
        s = lax.dot_general(q2, kc, (((1,), (1,)), ((), ())), preferred_element_type=F32)
        mn = jnp.maximum(m, jnp.max(s, axis=-1, keepdims=True))
        p = jnp.exp(s - mn)
        a = jnp.exp(m - mn)
        l = a * l + jnp.sum(p, axis=-1, keepdims=True)
        acc = a * acc + jnp.dot(p.astype(BF16), vc, preferred_element_type=F32)
        return mn, l, acc

    init = (jnp.full((2 * TM, 1), -1e30, F32), jnp.zeros((2 * TM, 1), F32), jnp.zeros((2 * TM, LANES), F32))
    _, l, acc = lax.fori_loop(0, jnp.where(i == 0, 1, nkv), body, init)
    out = acc / l
    o_ref[...] = jnp.where(low, out[:TM], out[TM:]).astype(o_ref.dtype)


def _attention(qp3, kp3, p3):
    nb, t, _ = qp3.shape
    nblk = t // TM
    return pl.pallas_call(
        functools.partial(_attn_kernel, nkv=nblk),
        grid=(nb, nblk, GQA_Q // LANES),
        in_specs=[pl.BlockSpec((None, TM, LANES), lambda b, i, j: (b, i, j)),
                  pl.BlockSpec((None, t, LANES), lambda b, i, j: (b, 0, 0)),
                  pl.BlockSpec((None, t, LANES), lambda b, i, j: (b, 0, 5))],
        out_specs=pl.BlockSpec((None, TM, LANES), lambda b, i, j: (b, i, j)),
        out_shape=jax.ShapeDtypeStruct((nb, t, GQA_Q), BF16),
        compiler_params=_cparams(("arbitrary", "arbitrary", "arbitrary")),
        name="gqa_attn",
    )(qp3, kp3, p3)


def _ln(x, g, b):
    mu = jnp.mean(x, axis=-1, keepdims=True)
    xc = x - mu
    var = jnp.mean(xc * xc, axis=-1, keepdims=True)
    return xc * lax.rsqrt(var + EPS) * g + b


def _evout_kernel(of_ref, ob_ref, r_ref, att_ref, g_ref, w1_ref, w2_ref, x_ref, mod_ref, lg_ref, lb_ref, o_ref):
    o = of_ref[...] + ob_ref[...]
    r = r_ref[...].astype(F32)
    parts = []
    for h in range(GLA_HEADS):
        oh = o[:, h * GLA_DV:(h + 1) * GLA_DV]
        ms = jnp.mean(oh * oh, axis=-1, keepdims=True)
        parts.append(oh * lax.rsqrt(ms + EPS))
    gl = (jnp.concatenate(parts, axis=1) * g_ref[...] * _silu(r)).astype(BF16)
    y = jnp.dot(gl, w1_ref[...], preferred_element_type=F32)
    y = y + jnp.dot(att_ref[...], w2_ref[...], preferred_element_type=F32)
    gate = mod_ref[0, 2:3, :]
    o_ref[...] = _ln(ALPHA * x_ref[...] + gate * y, lg_ref[...], lb_ref[...])


def _evout(of2, ob2, p2, att2, g512, w1, w2, x2, mod, lg, lb, tps, nb):
    ntok, d = x2.shape
    row = lambda i: (i, 0)
    const = lambda i: (0, 0)
    return pl.pallas_call(
        _evout_kernel,
        grid=(ntok // TM,),
        in_specs=[pl.BlockSpec((TM, GLA_V), row), pl.BlockSpec((TM, GLA_V), row),
                  pl.BlockSpec((TM, GLA_V), lambda i: (i, 4)),
                  pl.BlockSpec((TM, GQA_Q), row),
                  pl.BlockSpec((1, GLA_V), const),
                  pl.BlockSpec((GLA_V, d), const), pl.BlockSpec((GQA_Q, d), const),
                  pl.BlockSpec((TM, d), row),
                  pl.BlockSpec((1, 6, d), lambda i: (_mod_row(i, tps, nb), 0, 0)),
                  pl.BlockSpec((1, d), const), pl.BlockSpec((1, d), const)],
        out_specs=pl.BlockSpec((TM, d), row),
        out_shape=jax.ShapeDtypeStruct((ntok, d), F32),
        compiler_params=_cparams(("arbitrary",)),
        name="even_out",
    )(of2, ob2, p2, att2, g512, w1, w2, x2, mod, lg, lb)


def _conv_kernel(prev_ref, cur_ref, next_ref, w_ref, b_ref, o_ref, pad_ref, *, tps):
    j = pl.program_id(0) % tps
    has_prev = (j >= 2).astype(F32)
    has_next = jnp.logical_and(j >= 1, j <= tps - 2).astype(F32)
    pad_ref[0:8, :] = prev_ref[...].astype(F32) * has_prev
    pad_ref[8:8 + TM, :] = cur_ref[...].astype(F32)
    pad_ref[8 + TM:16 + TM, :] = next_ref[...].astype(F32) * has_next
    half = (SSD_CONV - 1) // 2
    acc = jnp.zeros((TM, cur_ref.shape[1]), F32) + b_ref[...]
    for tap in range(SSD_CONV):
        acc = acc + pad_ref[pl.ds(8 + tap - half, TM), :] * w_ref[tap:tap + 1, :]
    o_ref[...] = _silu(acc).astype(o_ref.dtype)


def _conv(p2, w8, bias, tps):
    ntok = p2.shape[0]
    cw = 256
    ncb = SSD_CONV_CH // cw
    cb0 = SSD_INNER // cw
    r8 = TM // 8
    nrow8 = ntok // 8
    return pl.pallas_call(
        functools.partial(_conv_kernel, tps=tps),
        grid=(ntok // TM, ncb),
        in_specs=[pl.BlockSpec((8, cw), lambda i, c: (jnp.maximum(i * r8 - 1, 0), cb0 + c)),
                  pl.BlockSpec((TM, cw), lambda i, c: (i, cb0 + c)),
                  pl.BlockSpec((8, cw), lambda i, c: (jnp.minimum((i + 1) * r8, nrow8 - 1), cb0 + c)),
                  pl.BlockSpec((8, cw), lambda i, c: (0, c)),
                  pl.BlockSpec((1, cw), lambda i, c: (0, c))],
        out_specs=pl.BlockSpec((TM, cw), lambda i, c: (i, c)),
        out_shape=jax.ShapeDtypeStruct((ntok, SSD_CONV_CH), BF16),
        scratch_shapes=[pltpu.VMEM((TM + 16, cw), F32)],
        compiler_params=_cparams(("arbitrary", "arbitrary")),
        name="ssd_conv",
    )(p2, p2, p2, w8, bias)


def _ssd_kernel(xbc_ref, dt_ref, bias_ref, a_ref, eexp_ref, esel_ref, o_ref, h_ref, *, reverse):
    L = SSD_CHUNK
    GW = SSD_HPG * SSD_HD

    @pl.when(pl.program_id(1) == 0)
    def _():
        h_ref[...] = jnp.zeros_like(h_ref)

    tri = _tri(L, reverse)
    cmat = tri.astype(F32)
    lane = lax.broadcasted_iota(jnp.int32, (L, LANES), 1)
    low = lane < SSD_HD
    end = 0 if reverse else L - 1
    lane0 = SSD_HEADS if reverse else 0
    nchunk = TM // L
    order = range(nchunk - 1, -1, -1) if reverse else range(nchunk)
    for c in order:
        sl = pl.ds(c * L, L)
        xs = xbc_ref[sl, 0:SSD_INNER].astype(F32)
        bs = xbc_ref[sl, SSD_INNER:SSD_INNER + SSD_BC]
        cs = xbc_ref[sl, SSD_INNER + SSD_BC:SSD_CONV_CH]
        dt = _softplus(dt_ref[sl, :].astype(F32) + bias_ref[...])
        a = dt * a_ref[...]
        cum = jnp.dot(cmat, a, precision=HI, preferred_element_type=F32)
        cum_t = lax.dot_general(esel_ref[...], cum, (((1,), (1,)), ((), ())), precision=HI,
                                preferred_element_type=F32)
        dt_e = jnp.dot(dt, eexp_ref[...], precision=HI, preferred_element_type=F32)
        cum_e = jnp.dot(cum, eexp_ref[...], precision=HI, preferred_element_type=F32)
        cend_e = cum_e[end:end + 1, :]
        xdt = xs * dt_e
        xdt_b = xdt.astype(BF16)
        xend = (xdt * jnp.exp(cend_e - cum_e)).astype(BF16)
        ecum = jnp.exp(cum_e)
        edec = jnp.exp(cend_e)
        ys = []
        for g in range(SSD_GROUPS):
            bg = bs[:, g * SSD_STATE:(g + 1) * SSD_STATE]
            cg = cs[:, g * SSD_STATE:(g + 1) * SSD_STATE]
            cb = lax.dot_general(cg, bg, (((1,), (1,)), ((), ())), preferred_element_type=F32)
            hg = h_ref[g]
            y_inter = jnp.dot(cg, hg.astype(BF16), preferred_element_type=F32) * ecum[:, g * GW:(g + 1) * GW]
            pairs = []
            for pp in range(SSD_HPG // 2):
                yh = []
                for u in range(2):
                    h = g * SSD_HPG + 2 * pp + u
                    seg = cum[:, lane0 + h:lane0 + h + 1] - cum_t[h:h + 1, :]
                    dec = jnp.exp(jnp.where(tri, seg, -1e30))
                    mat = (cb * dec).astype(BF16)
                    col = (h - u) * SSD_HD
                    yh.append(jnp.dot(mat, xdt_b[:, col:col + LANES], preferred_element_type=F32))
                pairs.append(jnp.where(low, yh[0], yh[1]))
            ys.append(jnp.concatenate(pairs, axis=1) + y_inter)
            dh = lax.dot_general(bg, xend[:, g * GW:(g + 1) * GW], (((0,), (0,)), ((), ())),
                                 preferred_element_type=F32)
            h_ref[g] = hg * edec[:, g * GW:(g + 1) * GW] + dh
        o_ref[sl, :] = jnp.concatenate(ys, axis=1)


def _ssd_scan(xbc3, p3, bias_pad, a_pad, eexp, esel, reverse):
    nb, t, _ = xbc3.shape
    nblk = t // TM
    blk = lambda i: _scan_block(i, nblk, reverse)
    const = lambda b, i: (0, 0)
    return pl.pallas_call(
        functools.partial(_ssd_kernel, reverse=reverse),
        grid=(nb, nblk),
        in_specs=[pl.BlockSpec((None, TM, SSD_CONV_CH), lambda b, i: (b, blk(i), 0)),
                  pl.BlockSpec((None, TM, LANES), lambda b, i: (b, blk(i), 16)),
                  pl.BlockSpec((1, LANES), const), pl.BlockSpec((1, LANES), const),
                  pl.BlockSpec((LANES, SSD_INNER), const), pl.BlockSpec((16, LANES), const)],
        out_specs=pl.BlockSpec((None, TM, SSD_INNER), lambda b, i: (b, blk(i), 0)),
        out_shape=jax.ShapeDtypeStruct((nb, t, SSD_INNER), F32),
        scratch_shapes=[pltpu.VMEM((SSD_GROUPS, SSD_STATE, SSD_HPG * SSD_HD), F32)],
        compiler_params=_cparams(("arbitrary", "arbitrary")),
        name="ssd_bwd" if reverse else "ssd_fwd",
    )(xbc3, p3, bias_pad, a_pad, eexp, esel)


def _dft_mats(n):
    k = jnp.arange(n, dtype=jnp.int32)
    ang = ((k[:, None] * k[None, :]) % n).astype(F32) * (2.0 * math.pi / n)
    return jnp.cos(ang), jnp.sin(ang)


def _chan_mats():
    cc, sc = _dft_mats(FNET_GC)
    eye = jnp.eye(FNET_GROUPS, dtype=F32)
    return jnp.kron(eye, cc), jnp.kron(eye, sc)


def _group_rms(x, g, bd):
    ms = jnp.dot(x * x, bd, precision=HI, preferred_element_type=F32)
    return x * lax.rsqrt(ms + EPS) * g


def _mm(a, b):
    return jnp.dot(a, b, precision=HI, preferred_element_type=F32)


def _ffta_kernel(x_ref, g_ref, cc_ref, sc_ref, c1_ref, s1_ref, twc_ref, tws_ref, yr_ref, yi_ref, *, nb2):
    bd = _seg_ones(FNET_W, FNET_GC)
    for j in range(nb2):
        x = x_ref[:, j * FNET_W:(j + 1) * FNET_W].astype(F32)
        xn = _group_rms(x, g_ref[...], bd)
        vr = _mm(xn, cc_ref[...])
        vi = -_mm(xn, sc_ref[...])
        c1 = c1_ref[...]
        s1 = s1_ref[...]
        yr = _mm(c1, vr) + _mm(s1, vi)
        yi = _mm(c1, vi) - _mm(s1, vr)
        tc = twc_ref[j]
        ts = tws_ref[j]
        yr_ref[j] = yr * tc + yi * ts
        yi_ref[j] = yi * tc - yr * ts


def _fftb_kernel(yr_ref, yi_ref, c2_ref, s2_ref, o_ref, *, scale):
    o_ref[...] = (_mm(c2_ref[...], yr_ref[...]) + _mm(s2_ref[...], yi_ref[...])) * scale


def _fftc_kernel(x_ref, g_ref, cc_ref, sc_ref, ct_ref, st_ref, o_ref, *, scale):
    bd = _seg_ones(FNET_W, FNET_GC)
    xn = _group_rms(x_ref[...].astype(F32), g_ref[...], bd)
    a = _mm(xn, cc_ref[...])
    b = _mm(xn, sc_ref[...])
    o_ref[...] = (_mm(ct_ref[...], a) - _mm(st_ref[...], b)) * scale


def _fourier_latent(f_lat, g256):
    nb, s, _ = f_lat.shape
    n2 = 64
    n1 = s // n2
    nb2 = 8
    ccm, scm = _chan_mats()
    c1, s1 = _dft_mats(n1)
    c2, s2 = _dft_mats(n2)
    t2 = jnp.arange(n2, dtype=jnp.int32)[:, None]
    k1 = jnp.arange(n1, dtype=jnp.int32)[None, :]
    tw = ((t2 * k1) % s).astype(F32) * (2.0 * math.pi / s)
    twc = jnp.cos(tw)[:, :, None]
    tws = jnp.sin(tw)[:, :, None]
    x2 = f_lat.reshape(nb, n1, n2 * FNET_W)
    const2 = lambda b, j: (0, 0)
    yr, yi = pl.pallas_call(
        functools.partial(_ffta_kernel, nb2=nb2),
        grid=(nb, n2 // nb2),
        in_specs=[pl.BlockSpec((None, n1, nb2 * FNET_W), lambda b, j: (b, 0, j)),
                  pl.BlockSpec((1, FNET_W), const2),
                  pl.BlockSpec((FNET_W, FNET_W), const2), pl.BlockSpec((FNET_W, FNET_W), const2),
                  pl.BlockSpec((n1, n1), const2), pl.BlockSpec((n1, n1), const2),
                  pl.BlockSpec((nb2, n1, 1), lambda b, j: (j, 0, 0)),
                  pl.BlockSpec((nb2, n1, 1), lambda b, j: (j, 0, 0))],
        out_specs=[pl.BlockSpec((None, nb2, n1, FNET_W), lambda b, j: (b, j, 0, 0)),
                   pl.BlockSpec((None, nb2, n1, FNET_W), lambda b, j: (b, j, 0, 0))],
        out_shape=[jax.ShapeDtypeStruct((nb, n2, n1, FNET_W), F32),
                   jax.ShapeDtypeStruct((nb, n2, n1, FNET_W), F32)],
        compiler_params=_cparams(("arbitrary", "arbitrary")),
        name="fft_a",
    )(x2, g256, ccm, scm, c1, s1, twc, tws)
    ncol = n1 * FNET_W
    tn = min(2048, ncol)
    out = pl.pallas_call(
        functools.partial(_fftb_kernel, scale=1.0 / math.sqrt(s * FNET_GC)),
        grid=(nb, ncol // tn),
        in_specs=[pl.BlockSpec((None, n2, tn), lambda b, j: (b, 0, j)),
                  pl.BlockSpec((None, n2, tn), lambda b, j: (b, 0, j)),
                  pl.BlockSpec((n2, n2), const2), pl.BlockSpec((n2, n2), const2)],
        out_specs=pl.BlockSpec((None, n2, tn), lambda b, j: (b, 0, j)),
        out_shape=jax.ShapeDtypeStruct((nb, n2, ncol), F32),
        compiler_params=_cparams(("arbitrary", "arbitrary")),
        name="fft_b",
    )(yr.reshape(nb, n2, ncol), yi.reshape(nb, n2, ncol), c2, s2)
    return out.reshape(nb, s, FNET_W)


def _fourier_ctx(f_ctx, g256):
    nb, tc, _ = f_ctx.shape
    ccm, scm = _chan_mats()
    ct, st = _dft_mats(tc)
    const = lambda b: (0, 0)
    return pl.pallas_call(
        functools.partial(_fftc_kernel, scale=1.0 / math.sqrt(tc * FNET_GC)),
        grid=(nb,),
        in_specs=[pl.BlockSpec((None, tc, FNET_W), lambda b: (b, 0, 0)),
                  pl.BlockSpec((1, FNET_W), const),
                  pl.BlockSpec((FNET_W, FNET_W), const), pl.BlockSpec((FNET_W, FNET_W), const),
                  pl.BlockSpec((tc, tc), const), pl.BlockSpec((tc, tc), const)],
        out_specs=pl.BlockSpec((None, tc, FNET_W), lambda b: (b, 0, 0)),
        out_shape=jax.ShapeDtypeStruct((nb, tc, FNET_W), F32),
        compiler_params=_cparams(("arbitrary",)),
        name="fft_ctx",
    )(f_ctx, g256, ccm, scm, ct, st)


def _odout_kernel(yf_ref, yb_ref, xs_ref, z_ref, f_ref, dsk_ref, g_ref, w1_ref, w2_ref, x_ref, mod_ref,
                  lg_ref, lb_ref, o_ref):
    GW = SSD_HPG * SSD_HD
    y = yf_ref[...] + yb_ref[...] + xs_ref[...].astype(F32) * dsk_ref[...]
    y = y * _silu(z_ref[...].astype(F32))
    parts = []
    for g in range(SSD_GROUPS):
        yg = y[:, g * GW:(g + 1) * GW]
        ms = jnp.mean(yg * yg, axis=-1, keepdims=True)
        parts.append(yg * lax.rsqrt(ms + EPS))
    yn = (jnp.concatenate(parts, axis=1) * g_ref[...]).astype(BF16)
    o = jnp.dot(yn, w1_ref[...], preferred_element_type=F32)
    o = o + jnp.dot(f_ref[...].astype(BF16), w2_ref[...], preferred_element_type=F32)
    gate = mod_ref[0, 2:3, :]
    o_ref[...] = _ln(ALPHA * x_ref[...] + gate * o, lg_ref[...], lb_ref[...])


def _odout(yf2, yb2, xbc2, p2, f2, dsk, g768, w1, w2, x2, mod, lg, lb, tps, nb):
    ntok, d = x2.shape
    row = lambda i: (i, 0)
    const = lambda i: (0, 0)
    return pl.pallas_call(
        _odout_kernel,
        grid=(ntok // TM,),
        in_specs=[pl.BlockSpec((TM, SSD_INNER), row), pl.BlockSpec((TM, SSD_INNER), row),
                  pl.BlockSpec((TM, SSD_INNER), row), pl.BlockSpec((TM, SSD_INNER), row),
                  pl.BlockSpec((TM, FNET_W), row),
                  pl.BlockSpec((1, SSD_INNER), const), pl.BlockSpec((1, SSD_INNER), const),
                  pl.BlockSpec((SSD_INNER, d), const), pl.BlockSpec((FNET_W, d), const),
                  pl.BlockSpec((TM, d), row),
                  pl.BlockSpec((1, 6, d), lambda i: (_mod_row(i, tps, nb), 0, 0)),
                  pl.BlockSpec((1, d), const), pl.BlockSpec((1, d), const)],
        out_specs=pl.BlockSpec((TM, d), row),
        out_shape=jax.ShapeDtypeStruct((ntok, d), F32),
        compiler_params=_cparams(("arbitrary",)),
        name="odd_out",
    )(yf2, yb2, xbc2, p2, f2, dsk, g768, w1, w2, x2, mod, lg, lb)


def _router_kernel(x_ref, mod_ref, rw_ref, rb_ref, v_ref, idx_ref, gw_ref, cnt_ref):
    @pl.when(pl.program_id(0) == 0)
    def _():
        cnt_ref[...] = jnp.zeros_like(cnt_ref)

    shift = mod_ref[0, 3:4, :]
    scale = mod_ref[0, 4:5, :]
    v = x_ref[...] * (1.0 + scale) + shift
    v_ref[...] = v.astype(v_ref.dtype)
    logits = lax.dot_general(rw_ref[...], v, (((1,), (1,)), ((), ())), precision=HI,
                             preferred_element_type=F32)
    s = 1.0 / (1.0 + jnp.exp(-logits))
    sel = s + rb_ref[:, 0:1]
    izero = jnp.zeros((1, TM), jnp.int32)
    best = None
    for g in range(N_EXPERT_GROUPS):
        a = [sel[g * EXPERTS_PER_GROUP + j:g * EXPERTS_PER_GROUP + j + 1, :] for j in range(EXPERTS_PER_GROUP)]
        sv = [s[g * EXPERTS_PER_GROUP + j:g * EXPERTS_PER_GROUP + j + 1, :] for j in range(EXPERTS_PER_GROUP)]
        m1, i1, s1 = a[0], izero, sv[0]
        for j in range(1, EXPERTS_PER_GROUP):
            gt = a[j] > m1
            m1 = jnp.where(gt, a[j], m1)
            i1 = jnp.where(gt, j, i1)
            s1 = jnp.where(gt, sv[j], s1)
        m2 = jnp.full((1, TM), -jnp.inf, F32)
        i2, s2 = izero, jnp.zeros((1, TM), F32)
        for j in range(EXPERTS_PER_GROUP):
            gt = jnp.logical_and(i1 != j, a[j] > m2)
            m2 = jnp.where(gt, a[j], m2)
            i2 = jnp.where(gt, j, i2)
            s2 = jnp.where(gt, sv[j], s2)
        cand = (m1 + m2, i1 + g * EXPERTS_PER_GROUP, i2 + g * EXPERTS_PER_GROUP, s1, s2)
        if best is None:
            best = cand
        else:
            gt = cand[0] > best[0]
            best = tuple(jnp.where(gt, cn, bs) for cn, bs in zip(cand, best))
    _, e1, e2, w1, w2 = best
    tot = w1 + w2
    idx_ref[0] = jnp.concatenate([e1, e2], axis=0)
    gw_ref[0] = jnp.concatenate([w1 / tot, w2 / tot], axis=0)
    eio = lax.broadcasted_iota(jnp.int32, (N_EXPERTS, TM), 0)
    oh = jnp.logical_or(eio == e1, eio == e2).astype(F32)
    cnt_ref[...] += jnp.sum(oh, axis=1, keepdims=True)


def _router(x2, mod, rw_t, rb, tps, nb):
    ntok, d = x2.shape
    nt = ntok // TM
    return pl.pallas_call(
        _router_kernel,
        grid=(nt,),
        in_specs=[pl.BlockSpec((TM, d), lambda i: (i, 0)),
                  pl.BlockSpec((1, 6, d), lambda i: (_mod_row(i, tps, nb), 0, 0)),
                  pl.BlockSpec((N_EXPERTS, d), lambda i: (0, 0)),
                  pl.BlockSpec((N_EXPERTS, LANES), lambda i: (0, 0))],
        out_specs=[pl.BlockSpec((TM, d), lambda i: (i, 0)),
                   pl.BlockSpec((1, TOP_K, TM), lambda i: (i, 0, 0)),
                   pl.BlockSpec((1, TOP_K, TM), lambda i: (i, 0, 0)),
                   pl.BlockSpec((N_EXPERTS, LANES), lambda i: (0, 0))],
        out_shape=[jax.ShapeDtypeStruct((ntok, d), BF16),
                   jax.ShapeDtypeStruct((nt, TOP_K, TM), jnp.int32),
                   jax.ShapeDtypeStruct((nt, TOP_K, TM), F32),
                   jax.ShapeDtypeStruct((N_EXPERTS, LANES), F32)],
        compiler_params=_cparams(("arbitrary",)),
        name="router",
    )(x2, mod, rw_t, rb)


def _dispatch_kernel(idx_ref, start_ref, dest_ref, run_ref):
    @pl.when(pl.program_id(0) == 0)
    def _():
        run_ref[...] = jnp.zeros_like(run_ref)

    e1 = idx_ref[0, 0:1, :]
    e2 = idx_ref[0, 1:2, :]
    eio = lax.broadcasted_iota(jnp.int32, (N_EXPERTS, TM), 0)
    oh1 = eio == e1
    oh2 = eio == e2
    oh = jnp.logical_or(oh1, oh2)
    r = lax.broadcasted_iota(jnp.int32, (TM, TM), 0)
    c = lax.broadcasted_iota(jnp.int32, (TM, TM), 1)
    before = (r < c).astype(BF16)
    rank = jnp.dot(oh.astype(BF16), before, preferred_element_type=F32)
    pos = rank + run_ref[:, 0:1] + start_ref[:, 0:1]
    d1 = jnp.sum(jnp.where(oh1, pos, 0.0), axis=0, keepdims=True)
    d2 = jnp.sum(jnp.where(oh2, pos, 0.0), axis=0, keepdims=True)
    dest_ref[0] = jnp.concatenate([d1, d2], axis=0).astype(jnp.int32)
    run_ref[...] += jnp.sum(oh.astype(F32), axis=1, keepdims=True)


def _dispatch(idx_t, start):
    nt = idx_t.shape[0]
    return pl.pallas_call(
        _dispatch_kernel,
        grid=(nt,),
        in_specs=[pl.BlockSpec((1, TOP_K, TM), lambda i: (i, 0, 0)),
                  pl.BlockSpec((N_EXPERTS, LANES), lambda i: (0, 0))],
        out_specs=pl.BlockSpec((1, TOP_K, TM), lambda i: (i, 0, 0)),
        out_shape=jax.ShapeDtypeStruct((nt, TOP_K, TM), jnp.int32),
        scratch_shapes=[pltpu.VMEM((N_EXPERTS, LANES), F32)],
        compiler_params=_cparams(("arbitrary",)),
        name="dispatch",
    )(idx_t, start)


def _moe_kernel(be_ref, x_ref, wg_ref, wu_ref, wd_ref, gw_ref, o_ref):
    del be_ref
    x = x_ref[...]
    g = jnp.dot(x, wg_ref[...], preferred_element_type=F32)
    u = jnp.dot(x, wu_ref[...], preferred_element_type=F32)
    h = (_silu(g) * u).astype(BF16)
    y = jnp.dot(h, wd_ref[...], preferred_element_type=F32)
    o_ref[...] = (y * gw_ref[...]).astype(o_ref.dtype)


def _moe_ffn(block_e, buf, wg, wu, wd, wrow):
    nrow, d = buf.shape
    nblk = nrow // TM
    de = wg.shape[2]
    grid_spec = pltpu.PrefetchScalarGridSpec(
        num_scalar_prefetch=1,
        grid=(nblk,),
        in_specs=[pl.BlockSpec((TM, d), lambda i, be: (i, 0)),
                  pl.BlockSpec((None, d, de), lambda i, be: (be[i], 0, 0)),
                  pl.BlockSpec((None, d, de), lambda i, be: (be[i], 0, 0)),
                  pl.BlockSpec((None, de, d), lambda i, be: (be[i], 0, 0)),
                  pl.BlockSpec((TM, 1), lambda i, be: (i, 0))],
        out_specs=pl.BlockSpec((TM, d), lambda i, be: (i, 0)),
    )
    return pl.pallas_call(
        _moe_kernel,
        grid_spec=grid_spec,
        out_shape=jax.ShapeDtypeStruct((nrow, d), BF16),
        compiler_params=_cparams(("arbitrary",)),
        name="moe_ffn",
    )(block_e, buf, wg, wu, wd, wrow)


def _ln2_kernel(x_ref, y0_ref, y1_ref, mod_ref, lg_ref, lb_ref, o_ref):
    y = y0_ref[...].astype(F32) + y1_ref[...].astype(F32)
    gate = mod_ref[0, 5:6, :]
    o_ref[...] = _ln(ALPHA * x_ref[...] + gate * y, lg_ref[...], lb_ref[...])


def _ln2(x2, y0, y1, mod, lg, lb, tps, nb):
    ntok, d = x2.shape
    row = lambda i: (i, 0)
    const = lambda i: (0, 0)
    return pl.pallas_call(
        _ln2_kernel,
        grid=(ntok // TM,),
        in_specs=[pl.BlockSpec((TM, d), row), pl.BlockSpec((TM, d), row), pl.BlockSpec((TM, d), row),
                  pl.BlockSpec((1, 6, d), lambda i: (_mod_row(i, tps, nb), 0, 0)),
                  pl.BlockSpec((1, d), const), pl.BlockSpec((1, d), const)],
        out_specs=pl.BlockSpec((TM, d), row),
        out_shape=jax.ShapeDtypeStruct((ntok, d), F32),
        compiler_params=_cparams(("arbitrary",)),
        name="post_moe_ln",
    )(x2, y0, y1, mod, lg, lb)


def _even_w_in(w):
    o = np.cumsum((0, GLA_QK, GLA_QK, GLA_V, GLA_V, GLA_RANK, GLA_RANK, GQA_Q, GQA_KV, GQA_KV))
    q, k, v, r = w[:, o[0]:o[1]], w[:, o[1]:o[2]], w[:, o[2]:o[3]], w[:, o[3]:o[4]]
    lr = w[:, o[4]:o[6]]
    gq, gk, gv = w[:, o[6]:o[7]], w[:, o[7]:o[8]], w[:, o[8]:o[9]]
    half = GQA_HEADS // 2
    gq_pairs = []
    for j in range(half):
        gq_pairs += [gq[:, j * GQA_HD:(j + 1) * GQA_HD], gq[:, (j + half) * GQA_HD:(j + half + 1) * GQA_HD]]
    d = w.shape[0]
    zeros = lambda n: jnp.zeros((d, n), w.dtype)
    out = jnp.concatenate(gq_pairs + [gk, gv, q, k, lr, zeros(LANES - 2 * GLA_RANK), zeros(LANES), v, r], axis=1)
    assert out.shape[1] == EVEN_W
    return out.astype(BF16)


def _even_w_o(w):
    w1 = w[:GLA_V]
    wa = w[GLA_V:]
    half = GQA_HEADS // 2
    rows = []
    for j in range(half):
        rows += [wa[j * GQA_HD:(j + 1) * GQA_HD], wa[(j + half) * GQA_HD:(j + half + 1) * GQA_HD]]
    return w1.astype(BF16), jnp.concatenate(rows, axis=0).astype(BF16)


def _odd_w_in(w):
    o = np.cumsum((0, SSD_INNER, SSD_INNER, SSD_BC, SSD_BC, SSD_HEADS, SSD_HEADS, FNET_W))
    d = w.shape[0]
    out = jnp.concatenate([w[:, :o[4]], w[:, o[4]:o[6]], jnp.zeros((d, LANES - 2 * SSD_HEADS), w.dtype),
                           w[:, o[6]:o[7]]], axis=1)
    assert out.shape[1] == ODD_W
    return out.astype(BF16)


def _rope_tables(ctx_len, seq):
    rows = seq // GRID_W
    r = jnp.repeat(jnp.arange(rows, dtype=F32), GRID_W)
    col = jnp.tile(jnp.arange(GRID_W, dtype=F32), rows)
    half = GQA_HD // 2
    inv = ROPE_THETA ** (-jnp.arange(0, half, 2, dtype=F32) / half)
    ar = r[:, None] * inv
    ac = col[:, None] * inv
    ang = jnp.concatenate([ar, ar, ac, ac], -1)
    cos = jnp.concatenate([jnp.ones((ctx_len, GQA_HD), F32), jnp.cos(ang)], 0)
    sin = jnp.concatenate([jnp.zeros((ctx_len, GQA_HD), F32), jnp.sin(ang)], 0)
    return jnp.tile(cos, (1, 2)), jnp.tile(sin, (1, 2))


def _even_layer(x2, mod, nb, t, tps, w_in, w_o, w_dec, b_dec, gla_g, qn_g, kn_g, cos2, sin2, lg, lb):
    p2 = _inproj(x2, mod, _even_w_in(w_in), tps, nb)
    p3 = p2.reshape(nb, t, EVEN_W)
    outs = []
    for di in range(2):
        wd_pad = jnp.zeros((LANES, GLA_QK), F32).at[di * GLA_RANK:(di + 1) * GLA_RANK].set(w_dec[di])
        outs.append(_gla_scan(p3, wd_pad, b_dec[di][None, :], reverse=(di == 1)))
    qp2, kp2 = _qkprep(p2, cos2, sin2, jnp.tile(qn_g, 2)[None, :], jnp.tile(kn_g, 2)[None, :], tps)
    att = _attention(qp2.reshape(nb, t, GQA_Q), kp2.reshape(nb, t, LANES), p3)
    w1, w2 = _even_w_o(w_o)
    ntok = nb * t
    return _evout(outs[0].reshape(ntok, GLA_V), outs[1].reshape(ntok, GLA_V), p2, att.reshape(ntok, GQA_Q),
                  jnp.tile(gla_g, GLA_HEADS)[None, :], w1, w2, x2, mod, lg, lb, tps, nb)


def _odd_layer(x2, mod, nb, t, tps, ctx_len, w_in, w_o, conv_w, conv_b, dt_bias, a_log, d_skip, ssd_g, fnet_g,
               lg, lb):
    p2 = _inproj(x2, mod, _odd_w_in(w_in), tps, nb)
    p3 = p2.reshape(nb, t, ODD_W)
    w8 = jnp.zeros((8, SSD_CONV_CH), F32).at[:SSD_CONV].set(conv_w)
    xbc2 = _conv(p2, w8, conv_b[None, :], tps)
    xbc3 = xbc2.reshape(nb, t, SSD_CONV_CH)
    ys = []
    heads = jnp.arange(SSD_HEADS)
    for di in range(2):
        lanes = di * SSD_HEADS + heads
        bias_pad = jnp.zeros((1, LANES), F32).at[0, lanes].set(dt_bias[di])
        a_pad = jnp.zeros((1, LANES), F32).at[0, lanes].set(-jnp.exp(a_log[di]))
        eexp = jnp.zeros((LANES, SSD_INNER), F32).at[jnp.repeat(lanes, SSD_HD), jnp.arange(SSD_INNER)].set(1.0)
        esel = jnp.zeros((16, LANES), F32).at[heads, lanes].set(1.0)
        ys.append(_ssd_scan(xbc3, p3, bias_pad, a_pad, eexp, esel, reverse=(di == 1)))
    f3 = p3[:, :, ODD_W - FNET_W:]
    g256 = fnet_g[None, :]
    fmix = jnp.concatenate([_fourier_ctx(f3[:, :ctx_len], g256), _fourier_latent(f3[:, ctx_len:], g256)], axis=1)
    ntok = nb * t
    return _odout(ys[0].reshape(ntok, SSD_INNER), ys[1].reshape(ntok, SSD_INNER), xbc2, p2,
                  fmix.reshape(ntok, FNET_W), jnp.repeat(d_skip, SSD_HD)[None, :], ssd_g[None, :],
                  w_o[:SSD_INNER].astype(BF16), w_o[SSD_INNER:].astype(BF16), x2, mod, lg, lb, tps, nb)


def _moe_layer(x2, mod, nb, tps, rw_t, rb, wg, wu, wd, lg, lb):
    ntok, d = x2.shape
    v, idx_t, gw_t, cnt = _router(x2, mod, rw_t, rb, tps, nb)
    counts = cnt[:, 0].astype(jnp.int32)
    padded = (counts + TM - 1) // TM * TM
    pad_end = jnp.cumsum(padded)
    pad_start = pad_end - padded
    nblk = ntok * TOP_K // TM + N_EXPERTS
    block_e = jnp.minimum(jnp.searchsorted(pad_end, jnp.arange(nblk, dtype=jnp.int32) * TM, side="right"),
                          N_EXPERTS - 1).astype(jnp.int32)
    start = jnp.broadcast_to(pad_start.astype(F32)[:, None], (N_EXPERTS, LANES))
    dest_t = _dispatch(idx_t, start)
    dest0 = dest_t[:, 0, :].reshape(ntok)
    dest1 = dest_t[:, 1, :].reshape(ntok)
    tok = jnp.arange(ntok, dtype=jnp.int32)
    src = jnp.zeros((nblk * TM,), jnp.int32).at[dest0].set(tok).at[dest1].set(tok)
    wrow = jnp.zeros((nblk * TM,), F32).at[dest0].set(gw_t[:, 0, :].reshape(ntok)).at[dest1].set(
        gw_t[:, 1, :].reshape(ntok))
    buf = v[src]
    y = _moe_ffn(block_e, buf, wg, wu, wd, wrow[:, None])
    return _ln2(x2, y[dest0], y[dest1], mod, lg, lb, tps, nb)


def kernel(x, c, ctx, c_ctx, ada_w, ada_b, ln_g, ln_b, ev_w_in, ev_w_o, gla_w_decay, gla_b_decay, gla_norm_g,
           gqa_q_norm_g, gqa_k_norm_g, od_w_in, od_w_o, ssd_conv_w, ssd_conv_b, ssd_dt_bias, ssd_a_log, ssd_d,
           ssd_norm_g, fnet_norm_g, router_w, router_b, exp_w_gate, exp_w_up, exp_w_down):
    nb, seq, d = x.shape
    ctx_len = ctx.shape[1]
    assert ctx_len == TM and seq % TM == 0 and d == D_MODEL and nb <= 7
    t = ctx_len + seq
    tps = t // TM
    ntok = nb * t
    cc = jnp.zeros((8, d), F32).at[:nb].set(c).at[nb].set(c_ctx)
    mod_all = _ada_all(cc, ada_w, ada_b).reshape(DEPTH, 8, 6, d)
    cos2, sin2 = _rope_tables(ctx_len, seq)
    rw_t = router_w.T
    rb = jnp.broadcast_to(router_b[:, None], (N_EXPERTS, LANES))
    x2 = jnp.concatenate([ctx, x], axis=1).reshape(ntok, d)
    for layer in range(DEPTH):
        mod = mod_all[layer]
        i = layer // 2
        lg0, lb0 = ln_g[layer, 0][None, :], ln_b[layer, 0][None, :]
        lg1, lb1 = ln_g[layer, 1][None, :], ln_b[layer, 1][None, :]
        if layer % 2 == 0:
            x2 = _even_layer(x2, mod, nb, t, tps, ev_w_in[i], ev_w_o[i], gla_w_decay[i], gla_b_decay[i],
                             gla_norm_g[i], gqa_q_norm_g[i], gqa_k_norm_g[i], cos2, sin2, lg0, lb0)
        else:
            x2 = _odd_layer(x2, mod, nb, t, tps, ctx_len, od_w_in[i], od_w_o[i], ssd_conv_w[i], ssd_conv_b[i],
                            ssd_dt_bias[i], ssd_a_log[i], ssd_d[i], ssd_norm_g[i], fnet_norm_g[i], lg0, lb0)
        x2 = _moe_layer(x2, mod, nb, tps, rw_t, rb, exp_w_gate[layer].astype(BF16), exp_w_up[layer].astype(BF16),
                        exp_w_down[layer].astype(BF16), lg1, lb1)
    return x2.reshape(nb, t, d)[:, ctx_len:]
```

```python
import functools
import math

import jax
import jax.numpy as jnp
import numpy as np
from jax import lax
from jax.experimental import pallas as pl
from jax.experimental.pallas import tpu as pltpu

F32 = jnp.float32
BF16 = jnp.bfloat16
HI = lax.Precision.HIGHEST

D_MODEL = 1024
DEPTH = 4
GRID_W = 64
GLA_HEADS, GLA_DK, GLA_DV, GLA_RANK, GLA_TAU, GLA_CHUNK = 4, 64, 128, 16, 16.0, 64
GQA_HEADS, GQA_KV_HEADS, GQA_HD = 8, 2, 64
ROPE_THETA = 10000.0
SSD_HEADS, SSD_HD, SSD_GROUPS, SSD_STATE, SSD_CONV, SSD_CHUNK = 12, 64, 2, 128, 5, 64
SSD_HPG = SSD_HEADS // SSD_GROUPS
SSD_INNER = SSD_HEADS * SSD_HD
SSD_BC = SSD_GROUPS * SSD_STATE
SSD_CONV_CH = SSD_INNER + 2 * SSD_BC
FNET_GROUPS, FNET_GC = 4, 64
FNET_W = FNET_GROUPS * FNET_GC
N_EXPERTS, N_EXPERT_GROUPS, TOP_K, D_EXPERT = 16, 4, 2, 768
EXPERTS_PER_GROUP = N_EXPERTS // N_EXPERT_GROUPS
GLA_QK = GLA_HEADS * GLA_DK
GLA_V = GLA_HEADS * GLA_DV
GQA_Q = GQA_HEADS * GQA_HD
GQA_KV = GQA_KV_HEADS * GQA_HD
EPS = 1e-6
ALPHA = (2.0 * DEPTH) ** 0.25

LANES = 128
TM = 256
EVEN_W = 2560
ODD_W = 2432
VMEM_LIMIT = 48 * 1024 * 1024
LOG2E = 1.4426950408889634
ATT_TQ = 128
ATT_HEADROOM = 64.0
ATT_MIN_DENOM = 2.0 ** -60


def _cparams(sem):
    return pltpu.CompilerParams(dimension_semantics=sem, vmem_limit_bytes=VMEM_LIMIT)


def _silu(x):
    return x * (1.0 / (1.0 + jnp.exp(-x)))


def _softplus(x):
    return jnp.maximum(x, 0.0) + jnp.log1p(jnp.exp(-jnp.abs(x)))


def _mod_row(i, tiles_per_seq, n_batch):
    return jnp.where(i % tiles_per_seq == 0, n_batch, i // tiles_per_seq)


def _ada_kernel(c_ref, w_ref, b_ref, o_ref):
    s = _silu(c_ref[...])
    o_ref[0] = jnp.dot(s, w_ref[0], precision=HI, preferred_element_type=F32) + b_ref[0]


def _ada_all(cc, ada_w, ada_b):
    depth, d, n = ada_w.shape
    tn = 512
    return pl.pallas_call(
        _ada_kernel,
        grid=(depth, n // tn),
        in_specs=[pl.BlockSpec((8, d), lambda l, j: (0, 0)),
                  pl.BlockSpec((1, d, tn), lambda l, j: (l, 0, j)),
                  pl.BlockSpec((1, 1, tn), lambda l, j: (l, 0, j))],
        out_specs=pl.BlockSpec((1, 8, tn), lambda l, j: (l, 0, j)),
        out_shape=jax.ShapeDtypeStruct((depth, 8, n), F32),
        compiler_params=_cparams(("arbitrary", "arbitrary")),
        name="adaln",
    )(cc, ada_w, ada_b.reshape(depth, 1, n))


def _inproj_kernel(x_ref, mod_ref, w_ref, o_ref):
    shift = mod_ref[0, 0:1, :]
    scale = mod_ref[0, 1:2, :]
    u = (x_ref[...] * (1.0 + scale) + shift).astype(BF16)
    o_ref[...] = jnp.dot(u, w_ref[...], preferred_element_type=F32).astype(o_ref.dtype)


def _inproj(x2, mod, w, tps, nb):
    ntok, d = x2.shape
    nw = w.shape[1]
    return pl.pallas_call(
        _inproj_kernel,
        grid=(ntok // TM,),
        in_specs=[pl.BlockSpec((TM, d), lambda i: (i, 0)),
                  pl.BlockSpec((1, 6, d), lambda i: (_mod_row(i, tps, nb), 0, 0)),
                  pl.BlockSpec((d, nw), lambda i: (0, 0))],
        out_specs=pl.BlockSpec((TM, nw), lambda i: (i, 0)),
        out_shape=jax.ShapeDtypeStruct((ntok, nw), BF16),
        compiler_params=_cparams(("arbitrary",)),
        name="inproj",
    )(x2, mod, w)


def _scan_block(i, nblk, reverse):
    if not reverse:
        return i
    return jnp.where(i == 0, 0, nblk - i)


def _tri(n, reverse):
    r = lax.broadcasted_iota(jnp.int32, (n, n), 0)
    c = lax.broadcasted_iota(jnp.int32, (n, n), 1)
    return (c >= r) if reverse else (c <= r)


def _gla_kernel(q_ref, k_ref, v_ref, lr_ref, wd_ref, bd_ref, o_ref, s_ref, *, reverse):
    L = GLA_CHUNK

    @pl.when(pl.program_id(2) == 0)
    def _():
        s_ref[...] = jnp.zeros_like(s_ref)

    tri = _tri(L, reverse)
    cmat = tri.astype(F32)
    lane = lax.broadcasted_iota(jnp.int32, (L, LANES), 1)
    ones = jnp.ones((L, LANES), F32)
    end = 0 if reverse else L - 1
    mid = L // 2 if reverse else L // 2 - 1
    nchunk = TM // L
    order = range(nchunk - 1, -1, -1) if reverse else range(nchunk)
    for c in order:
        sl = pl.ds(c * L, L)
        q = q_ref[sl, :].astype(F32) * (GLA_DK ** -0.5)
        k = k_ref[sl, :].astype(F32)
        v = v_ref[sl, :]
        z = jnp.dot(lr_ref[sl, :].astype(F32), wd_ref[...], precision=HI, preferred_element_type=F32) + bd_ref[...]
        la = -_softplus(-z) * (1.0 / GLA_TAU)
        b = jnp.dot(cmat, la, precision=HI, preferred_element_type=F32)
        bmid = b[mid:mid + 1, :]
        bend = b[end:end + 1, :]
        qs = q * jnp.exp(b - bmid)
        ks = (k * jnp.exp(bmid - b)).astype(BF16)
        qi = q * jnp.exp(b)
        kend = (k * jnp.exp(bend - b)).astype(BF16)
        s_old = s_ref[...]
        s_bf = s_old.astype(BF16)
        outs = []
        for h in range(2):
            m = (lane < GLA_DK) if h == 0 else (lane >= GLA_DK)
            qh = jnp.where(m, qs, 0.0).astype(BF16)
            att = lax.dot_general(qh, ks, (((1,), (1,)), ((), ())), preferred_element_type=F32)
            att = jnp.where(tri, att, 0.0).astype(BF16)
            vh = v[:, h * GLA_DV:(h + 1) * GLA_DV]
            oh = jnp.dot(att, vh, preferred_element_type=F32)
            oh = oh + jnp.dot(jnp.where(m, qi, 0.0).astype(BF16), s_bf[:, h * GLA_DV:(h + 1) * GLA_DV],
                              preferred_element_type=F32)
            outs.append(oh)
        o_ref[sl, :] = jnp.concatenate(outs, axis=1)
        ds = lax.dot_general(kend, v, (((0,), (0,)), ((), ())), preferred_element_type=F32)
        dec = lax.dot_general(la, ones, (((0,), (0,)), ((), ())), precision=HI, preferred_element_type=F32)
        dec = jnp.exp(jnp.concatenate([dec, dec], axis=1))
        s_ref[...] = dec * s_old + ds


def _gla_scan(p3, wd_pad, bd, reverse):
    nb, t, _ = p3.shape
    nblk = t // TM
    blk = lambda i: _scan_block(i, nblk, reverse)
    return pl.pallas_call(
        functools.partial(_gla_kernel, reverse=reverse),
        grid=(nb, 2, nblk),
        in_specs=[pl.BlockSpec((None, TM, LANES), lambda b, p, i: (b, blk(i), 6 + p)),
                  pl.BlockSpec((None, TM, LANES), lambda b, p, i: (b, blk(i), 8 + p)),
                  pl.BlockSpec((None, TM, 2 * GLA_DV), lambda b, p, i: (b, blk(i), 6 + p)),
                  pl.BlockSpec((None, TM, LANES), lambda b, p, i: (b, blk(i), 10)),
                  pl.BlockSpec((LANES, LANES), lambda b, p, i: (0, p)),
                  pl.BlockSpec((1, LANES), lambda b, p, i: (0, p))],
        out_specs=pl.BlockSpec((None, TM, 2 * GLA_DV), lambda b, p, i: (b, blk(i), p)),
        out_shape=jax.ShapeDtypeStruct((nb, t, GLA_V), F32),
        scratch_shapes=[pltpu.VMEM((2 * GLA_DK, 2 * GLA_DV), F32)],
        compiler_params=_cparams(("arbitrary", "arbitrary", "arbitrary")),
        name="gla_bwd" if reverse else "gla_fwd",
    )(p3, p3, p3, p3, wd_pad, bd)


def _seg_ones(width, seg):
    r = lax.broadcasted_iota(jnp.int32, (width, width), 0) // seg
    c = lax.broadcasted_iota(jnp.int32, (width, width), 1) // seg
    return jnp.where(r == c, 1.0 / seg, 0.0).astype(F32)


def _norm_rope(x, g, cos, sin, bd):
    ms = jnp.dot(x * x, bd, precision=HI, preferred_element_type=F32)
    xn = x * lax.rsqrt(ms + EPS) * g
    lane = lax.broadcasted_iota(jnp.int32, xn.shape, 1)
    quarter = GQA_HD // 4
    up = pltpu.roll(xn, LANES - quarter, 1)
    dn = pltpu.roll(xn, quarter, 1)
    rot = jnp.where(lane % (2 * quarter) < quarter, -up, dn)
    return xn * cos + rot * sin


def _qkprep_kernel(q_ref, k_ref, v_ref, cos_ref, sin_ref, gq_ref, gk_ref, qo_ref, ko_ref, vo_ref):
    bd = _seg_ones(LANES, GQA_HD)
    cos = cos_ref[...]
    sin = sin_ref[...]
    lane = lax.broadcasted_iota(jnp.int32, (TM, LANES), 1)
    low = lane < GQA_HD
    extra = lane == GQA_HD
    one_hot = jnp.where(extra, 1.0, 0.0)
    k_bound = math.sqrt(GQA_HD) * jnp.max(jnp.abs(gk_ref[...]), axis=-1, keepdims=True) * 1.02
    for j in range(GQA_Q // LANES):
        x = q_ref[:, j * LANES:(j + 1) * LANES].astype(F32)
        y = _norm_rope(x, gq_ref[...], cos, sin, bd) * (GQA_HD ** -0.5 * LOG2E)
        y = y.astype(BF16).astype(F32)
        norm = jnp.sqrt(jnp.dot(y * y, bd, precision=HI, preferred_element_type=F32) * GQA_HD)
        shift = ATT_HEADROOM - norm * k_bound
        lo = jnp.where(low, y, jnp.where(extra, pltpu.roll(shift, GQA_HD, 1), 0.0))
        hi = jnp.where(low, pltpu.roll(y, GQA_HD, 1), jnp.where(extra, shift, 0.0))
        qo_ref[:, (2 * j) * LANES:(2 * j + 1) * LANES] = lo.astype(BF16)
        qo_ref[:, (2 * j + 1) * LANES:(2 * j + 2) * LANES] = hi.astype(BF16)
    k = _norm_rope(k_ref[...].astype(F32), gk_ref[...], cos, sin, bd)
    v = v_ref[...].astype(F32)
    for src, dst in ((k, ko_ref), (v, vo_ref)):
        dst[:, 0:LANES] = jnp.where(low, src, one_hot).astype(BF16)
        dst[:, LANES:2 * LANES] = jnp.where(low, pltpu.roll(src, GQA_HD, 1), one_hot).astype(BF16)


def _qkprep(p2, cos2, sin2, gq2, gk2, tps):
    ntok = p2.shape[0]
    return pl.pallas_call(
        _qkprep_kernel,
        grid=(ntok // TM,),
        in_specs=[pl.BlockSpec((TM, GQA_Q), lambda i: (i, 0)),
                  pl.BlockSpec((TM, LANES), lambda i: (i, 4)),
                  pl.BlockSpec((TM, LANES), lambda i: (i, 5)),
                  pl.BlockSpec((TM, LANES), lambda i: (i % tps, 0)),
                  pl.BlockSpec((TM, LANES), lambda i: (i % tps, 0)),
                  pl.BlockSpec((1, LANES), lambda i: (0, 0)),
                  pl.BlockSpec((1, LANES), lambda i: (0, 0))],
        out_specs=[pl.BlockSpec((TM, GQA_HEADS * LANES), lambda i: (i, 0)),
                   pl.BlockSpec((TM, GQA_KV_HEADS * LANES), lambda i: (i, 0)),
                   pl.BlockSpec((TM, GQA_KV_HEADS * LANES), lambda i: (i, 0))],
        out_shape=[jax.ShapeDtypeStruct((ntok, GQA_HEADS * LANES), BF16),
                   jax.ShapeDtypeStruct((ntok, GQA_KV_HEADS * LANES), BF16),
                   jax.ShapeDtypeStruct((ntok, GQA_KV_HEADS * LANES), BF16)],
        compiler_params=_cparams(("arbitrary",)),
        name="qkprep",
    )(p2, p2, p2, cos2, sin2, gq2, gk2)


def _attn_kernel(q_ref, k_ref, v_ref, o_ref, *, ctx_len, n_lat_chunks, ck, n_blocks):
    i = pl.program_id(2)
    rep = GQA_HEADS // GQA_KV_HEADS
    is_ctx = i < ctx_len // ATT_TQ
    q4 = jnp.concatenate([q_ref[:, r * LANES:(r + 1) * LANES] for r in range(rep)], axis=0)
    nt = (((1,), (1,)), ((), ()))

    def chunk(acc, kc, vc):
        s = lax.dot_general(q4, kc, nt, preferred_element_type=F32)
        return acc + jnp.dot(jnp.exp2(s).astype(BF16), vc, preferred_element_type=F32)

    def store(acc):
        out = acc * (1.0 / acc[:, GQA_HD:GQA_HD + 1])
        low = lax.broadcasted_iota(jnp.int32, (ATT_TQ, LANES), 1) < GQA_HD
        hs = [out[r * ATT_TQ:(r + 1) * ATT_TQ] for r in range(rep)]
        for u in range(rep // 2):
            o_ref[:, u * LANES:(u + 1) * LANES] = jnp.where(
                low, hs[2 * u], pltpu.roll(hs[2 * u + 1], GQA_HD, 1)).astype(o_ref.dtype)

    acc = chunk(jnp.zeros((rep * ATT_TQ, LANES), F32), k_ref[0:ctx_len, :], v_ref[0:ctx_len, :])

    def body(c, acc):
        off = pl.multiple_of(ctx_len + c * ck, TM)
        return chunk(acc, k_ref[pl.ds(off, ck), :], v_ref[pl.ds(off, ck), :])

    acc = lax.fori_loop(0, jnp.where(is_ctx, 0, n_lat_chunks), body, acc)
    healthy = jnp.min(acc[:, GQA_HD:GQA_HD + 1]) >= ATT_MIN_DENOM

    @pl.when(healthy)
    def _():
        store(acc)

    @pl.when(jnp.logical_not(healthy))
    def _():
        def online(c, carry):
            m, acc = carry
            off = pl.multiple_of(c * TM, TM)
            s = lax.dot_general(q4, k_ref[pl.ds(off, TM), :], nt, preferred_element_type=F32)
            mn = jnp.maximum(m, jnp.max(s, axis=-1, keepdims=True))
            p = jnp.exp2(s - mn).astype(BF16)
            acc = jnp.exp2(m - mn) * acc + jnp.dot(p, v_ref[pl.ds(off, TM), :], preferred_element_type=F32)
            return mn, acc

        init = (jnp.full((rep * ATT_TQ, 1), -1e30, F32), jnp.zeros((rep * ATT_TQ, LANES), F32))
        _, acc2 = lax.fori_loop(0, jnp.where(is_ctx, ctx_len // TM, n_blocks), online, init)
        store(acc2)


def _attention(qh3, kg3, vg3, ctx_len):
    nb, t, _ = qh3.shape
    seq = t - ctx_len
    ck = next(c for c in (2048, 1024, 512, 256) if seq % c == 0)
    rep = GQA_HEADS // GQA_KV_HEADS
    return pl.pallas_call(
        functools.partial(_attn_kernel, ctx_len=ctx_len, n_lat_chunks=seq // ck, ck=ck, n_blocks=t // TM),
        grid=(nb, GQA_KV_HEADS, t // ATT_TQ),
        in_specs=[pl.BlockSpec((None, ATT_TQ, rep * LANES), lambda b, g, i: (b, i, g)),
                  pl.BlockSpec((None, t, LANES), lambda b, g, i: (b, 0, g)),
                  pl.BlockSpec((None, t, LANES), lambda b, g, i: (b, 0, g))],
        out_specs=pl.BlockSpec((None, ATT_TQ, rep * GQA_HD), lambda b, g, i: (b, i, g)),
        out_shape=jax.ShapeDtypeStruct((nb, t, GQA_Q), BF16),
        compiler_params=_cparams(("arbitrary", "arbitrary", "arbitrary")),
        name="gqa_attn",
    )(qh3, kg3, vg3)


def _ln(x, g, b):
    mu = jnp.mean(x, axis=-1, keepdims=True)
    xc = x - mu
    var = jnp.mean(xc * xc, axis=-1, keepdims=True)
    return xc * lax.rsqrt(var + EPS) * g + b


def _evout_kernel(of_ref, ob_ref, r_ref, att_ref, g_ref, w1_ref, w2_ref, x_ref, mod_ref, lg_ref, lb_ref, o_ref):
    o = of_ref[...] + ob_ref[...]
    r = r_ref[...].astype(F32)
    parts = []
    for h in range(GLA_HEADS):
        oh = o[:, h * GLA_DV:(h + 1) * GLA_DV]
        ms = jnp.mean(oh * oh, axis=-1, keepdims=True)
        parts.append(oh * lax.rsqrt(ms + EPS))
    gl = (jnp.concatenate(parts, axis=1) * g_ref[...] * _silu(r)).astype(BF16)
    y = jnp.dot(gl, w1_ref[...], preferred_element_type=F32)
    y = y + jnp.dot(att_ref[...], w2_ref[...], preferred_element_type=F32)
    gate = mod_ref[0, 2:3, :]
    o_ref[...] = _ln(ALPHA * x_ref[...] + gate * y, lg_ref[...], lb_ref[...])


def _evout(of2, ob2, p2, att2, g512, w1, w2, x2, mod, lg, lb, tps, nb):
    ntok, d = x2.shape
    row = lambda i: (i, 0)
    const = lambda i: (0, 0)
    return pl.pallas_call(
        _evout_kernel,
        grid=(ntok // TM,),
        in_specs=[pl.BlockSpec((TM, GLA_V), row), pl.BlockSpec((TM, GLA_V), row),
                  pl.BlockSpec((TM, GLA_V), lambda i: (i, 4)),
                  pl.BlockSpec((TM, GQA_Q), row),
                  pl.BlockSpec((1, GLA_V), const),
                  pl.BlockSpec((GLA_V, d), const), pl.BlockSpec((GQA_Q, d), const),
                  pl.BlockSpec((TM, d), row),
                  pl.BlockSpec((1, 6, d), lambda i: (_mod_row(i, tps, nb), 0, 0)),
                  pl.BlockSpec((1, d), const), pl.BlockSpec((1, d), const)],
        out_specs=pl.BlockSpec((TM, d), row),
        out_shape=jax.ShapeDtypeStruct((ntok, d), F32),
        compiler_params=_cparams(("arbitrary",)),
        name="even_out",
    )(of2, ob2, p2, att2, g512, w1, w2, x2, mod, lg, lb)


def _conv_kernel(prev_ref, cur_ref, next_ref, w_ref, b_ref, o_ref, pad_ref, *, tps):
    j = pl.program_id(0) % tps
    has_prev = (j >= 2).astype(F32)
    has_next = jnp.logical_and(j >= 1, j <= tps - 2).astype(F32)
    pad_ref[0:8, :] = prev_ref[...].astype(F32) * has_prev
    pad_ref[8:8 + TM, :] = cur_ref[...].astype(F32)
    pad_ref[8 + TM:16 + TM, :] = next_ref[...].astype(F32) * has_next
    half = (SSD_CONV - 1) // 2
    acc = jnp.zeros((TM, cur_ref.shape[1]), F32) + b_ref[...]
    for tap in range(SSD_CONV):
        acc = acc + pad_ref[pl.ds(8 + tap - half, TM), :] * w_ref[tap:tap + 1, :]
    o_ref[...] = _silu(acc).astype(o_ref.dtype)


def _conv(p2, w8, bias, tps):
    ntok = p2.shape[0]
    cw = 256
    ncb = SSD_CONV_CH // cw
    cb0 = SSD_INNER // cw
    r8 = TM // 8
    nrow8 = ntok // 8
    return pl.pallas_call(
        functools.partial(_conv_kernel, tps=tps),
        grid=(ntok // TM, ncb),
        in_specs=[pl.BlockSpec((8, cw), lambda i, c: (jnp.maximum(i * r8 - 1, 0), cb0 + c)),
                  pl.BlockSpec((TM, cw), lambda i, c: (i, cb0 + c)),
                  pl.BlockSpec((8, cw), lambda i, c: (jnp.minimum((i + 1) * r8, nrow8 - 1), cb0 + c)),
                  pl.BlockSpec((8, cw), lambda i, c: (0, c)),
                  pl.BlockSpec((1, cw), lambda i, c: (0, c))],
        out_specs=pl.BlockSpec((TM, cw), lambda i, c: (i, c)),
        out_shape=jax.ShapeDtypeStruct((ntok, SSD_CONV_CH), BF16),
        scratch_shapes=[pltpu.VMEM((TM + 16, cw), F32)],
        compiler_params=_cparams(("arbitrary", "arbitrary")),
        name="ssd_conv",
    )(p2, p2, p2, w8, bias)


def _ssd_kernel(xbc_ref, dt_ref, bias_ref, a_ref, eexp_ref, esel_ref, o_ref, h_ref, *, reverse):
    L = SSD_CHUNK
    GW = SSD_HPG * SSD_HD

    @pl.when(pl.program_id(1) == 0)
    def _():
        h_ref[...] = jnp.zeros_like(h_ref)

    tri = _tri(L, reverse)
    cmat = tri.astype(F32)
    lane = lax.broadcasted_iota(jnp.int32, (L, LANES), 1)
    low = lane < SSD_HD
    end = 0 if reverse else L - 1
    lane0 = SSD_HEADS if reverse else 0
    nchunk = TM // L
    order = range(nchunk - 1, -1, -1) if reverse else range(nchunk)
    for c in order:
        sl = pl.ds(c * L, L)
        xs = xbc_ref[sl, 0:SSD_INNER].astype(F32)
        bs = xbc_ref[sl, SSD_INNER:SSD_INNER + SSD_BC]
        cs = xbc_ref[sl, SSD_INNER + SSD_BC:SSD_CONV_CH]
        dt = _softplus(dt_ref[sl, :].astype(F32) + bias_ref[...])
        a = dt * a_ref[...]
        cum = jnp.dot(cmat, a, precision=HI, preferred_element_type=F32)
        cum_t = lax.dot_general(esel_ref[...], cum, (((1,), (1,)), ((), ())), precision=HI,
                                preferred_element_type=F32)
        dt_e = jnp.dot(dt, eexp_ref[...], precision=HI, preferred_element_type=F32)
        cum_e = jnp.dot(cum, eexp_ref[...], precision=HI, preferred_element_type=F32)
        cend_e = cum_e[end:end + 1, :]
        xdt = xs * dt_e
        xdt_b = xdt.astype(BF16)
        xend = (xdt * jnp.exp(cend_e - cum_e)).astype(BF16)
        ecum = jnp.exp(cum_e)
        edec = jnp.exp(cend_e)
        ys = []
        for g in range(SSD_GROUPS):
            bg = bs[:, g * SSD_STATE:(g + 1) * SSD_STATE]
            cg = cs[:, g * SSD_STATE:(g + 1) * SSD_STATE]
            cb = lax.dot_general(cg, bg, (((1,), (1,)), ((), ())), preferred_element_type=F32)
            hg = h_ref[g]
            y_inter = jnp.dot(cg, hg.astype(BF16), preferred_element_type=F32) * ecum[:, g * GW:(g + 1) * GW]
            pairs = []
            for pp in range(SSD_HPG // 2):
                yh = []
                for u in range(2):
                    h = g * SSD_HPG + 2 * pp + u
                    seg = cum[:, lane0 + h:lane0 + h + 1] - cum_t[h:h + 1, :]
                    dec = jnp.exp(jnp.where(tri, seg, -1e30))
                    mat = (cb * dec).astype(BF16)
                    col = (h - u) * SSD_HD
                    yh.append(jnp.dot(mat, xdt_b[:, col:col + LANES], preferred_element_type=F32))
                pairs.append(jnp.where(low, yh[0], yh[1]))
            ys.append(jnp.concatenate(pairs, axis=1) + y_inter)
            dh = lax.dot_general(bg, xend[:, g * GW:(g + 1) * GW], (((0,), (0,)), ((), ())),
                                 preferred_element_type=F32)
            h_ref[g] = hg * edec[:, g * GW:(g + 1) * GW] + dh
        o_ref[sl, :] = jnp.concatenate(ys, axis=1)


def _ssd_scan(xbc3, p3, bias_pad, a_pad, eexp, esel, reverse):
    nb, t, _ = xbc3.shape
    nblk = t // TM
    blk = lambda i: _scan_block(i, nblk, reverse)
    const = lambda b, i: (0, 0)
    return pl.pallas_call(
        functools.partial(_ssd_kernel, reverse=reverse),
        grid=(nb, nblk),
        in_specs=[pl.BlockSpec((None, TM, SSD_CONV_CH), lambda b, i: (b, blk(i), 0)),
                  pl.BlockSpec((None, TM, LANES), lambda b, i: (b, blk(i), 16)),
                  pl.BlockSpec((1, LANES), const), pl.BlockSpec((1, LANES), const),
                  pl.BlockSpec((LANES, SSD_INNER), const), pl.BlockSpec((16, LANES), const)],
        out_specs=pl.BlockSpec((None, TM, SSD_INNER), lambda b, i: (b, blk(i), 0)),
        out_shape=jax.ShapeDtypeStruct((nb, t, SSD_INNER), F32),
        scratch_shapes=[pltpu.VMEM((SSD_GROUPS, SSD_STATE, SSD_HPG * SSD_HD), F32)],
        compiler_params=_cparams(("arbitrary", "arbitrary")),
        name="ssd_bwd" if reverse else "ssd_fwd",
    )(xbc3, p3, bias_pad, a_pad, eexp, esel)


def _dft_mats(n):
    k = jnp.arange(n, dtype=jnp.int32)
    ang = ((k[:, None] * k[None, :]) % n).astype(F32) * (2.0 * math.pi / n)
    return jnp.cos(ang), jnp.sin(ang)


def _chan_mats():
    cc, sc = _dft_mats(FNET_GC)
    eye = jnp.eye(FNET_GROUPS, dtype=F32)
    return jnp.kron(eye, cc), jnp.kron(eye, sc)


def _group_rms(x, g, bd):
    ms = jnp.dot(x * x, bd, precision=HI, preferred_element_type=F32)
    return x * lax.rsqrt(ms + EPS) * g


def _mm(a, b):
    return jnp.dot(a, b, precision=HI, preferred_element_type=F32)


def _ffta_kernel(x_ref, g_ref, cc_ref, sc_ref, c1_ref, s1_ref, twc_ref, tws_ref, yr_ref, yi_ref, *, nb2):
    bd = _seg_ones(FNET_W, FNET_GC)
    for j in range(nb2):
        x = x_ref[:, j * FNET_W:(j + 1) * FNET_W].astype(F32)
        xn = _group_rms(x, g_ref[...], bd)
        vr = _mm(xn, cc_ref[...])
        vi = -_mm(xn, sc_ref[...])
        c1 = c1_ref[...]
        s1 = s1_ref[...]
        yr = _mm(c1, vr) + _mm(s1, vi)
        yi = _mm(c1, vi) - _mm(s1, vr)
        tc = twc_ref[j]
        ts = tws_ref[j]
        yr_ref[j] = yr * tc + yi * ts
        yi_ref[j] = yi * tc - yr * ts


def _fftb_kernel(yr_ref, yi_ref, c2_ref, s2_ref, o_ref, *, scale):
    o_ref[...] = (_mm(c2_ref[...], yr_ref[...]) + _mm(s2_ref[...], yi_ref[...])) * scale


def _fftc_kernel(x_ref, g_ref, cc_ref, sc_ref, ct_ref, st_ref, o_ref, *, scale):
    bd = _seg_ones(FNET_W, FNET_GC)
    xn = _group_rms(x_ref[...].astype(F32), g_ref[...], bd)
    a = _mm(xn, cc_ref[...])
    b = _mm(xn, sc_ref[...])
    o_ref[...] = (_mm(ct_ref[...], a) - _mm(st_ref[...], b)) * scale


def _fourier_latent(f_lat, g256):
    nb, s, _ = f_lat.shape
    n2 = 64
    n1 = s // n2
    nb2 = 8
    ccm, scm = _chan_mats()
    c1, s1 = _dft_mats(n1)
    c2, s2 = _dft_mats(n2)
    t2 = jnp.arange(n2, dtype=jnp.int32)[:, None]
    k1 = jnp.arange(n1, dtype=jnp.int32)[None, :]
    tw = ((t2 * k1) % s).astype(F32) * (2.0 * math.pi / s)
    twc = jnp.cos(tw)[:, :, None]
    tws = jnp.sin(tw)[:, :, None]
    x2 = f_lat.reshape(nb, n1, n2 * FNET_W)
    const2 = lambda b, j: (0, 0)
    yr, yi = pl.pallas_call(
        functools.partial(_ffta_kernel, nb2=nb2),
        grid=(nb, n2 // nb2),
        in_specs=[pl.BlockSpec((None, n1, nb2 * FNET_W), lambda b, j: (b, 0, j)),
                  pl.BlockSpec((1, FNET_W), const2),
                  pl.BlockSpec((FNET_W, FNET_W), const2), pl.BlockSpec((FNET_W, FNET_W), const2),
                  pl.BlockSpec((n1, n1), const2), pl.BlockSpec((n1, n1), const2),
                  pl.BlockSpec((nb2, n1, 1), lambda b, j: (j, 0, 0)),
                  pl.BlockSpec((nb2, n1, 1), lambda b, j: (j, 0, 0))],
        out_specs=[pl.BlockSpec((None, nb2, n1, FNET_W), lambda b, j: (b, j, 0, 0)),
                   pl.BlockSpec((None, nb2, n1, FNET_W), lambda b, j: (b, j, 0, 0))],
        out_shape=[jax.ShapeDtypeStruct((nb, n2, n1, FNET_W), F32),
                   jax.ShapeDtypeStruct((nb, n2, n1, FNET_W), F32)],
        compiler_params=_cparams(("arbitrary", "arbitrary")),
        name="fft_a",
    )(x2, g256, ccm, scm, c1, s1, twc, tws)
    ncol = n1 * FNET_W
    tn = min(2048, ncol)
    out = pl.pallas_call(
        functools.partial(_fftb_kernel, scale=1.0 / math.sqrt(s * FNET_GC)),
        grid=(nb, ncol // tn),
        in_specs=[pl.BlockSpec((None, n2, tn), lambda b, j: (b, 0, j)),
                  pl.BlockSpec((None, n2, tn), lambda b, j: (b, 0, j)),
                  pl.BlockSpec((n2, n2), const2), pl.BlockSpec((n2, n2), const2)],
        out_specs=pl.BlockSpec((None, n2, tn), lambda b, j: (b, 0, j)),
        out_shape=jax.ShapeDtypeStruct((nb, n2, ncol), F32),
        compiler_params=_cparams(("arbitrary", "arbitrary")),
        name="fft_b",
    )(yr.reshape(nb, n2, ncol), yi.reshape(nb, n2, ncol), c2, s2)
    return out.reshape(nb, s, FNET_W)


def _fourier_ctx(f_ctx, g256):
    nb, tc, _ = f_ctx.shape
    ccm, scm = _chan_mats()
    ct, st = _dft_mats(tc)
    const = lambda b: (0, 0)
    return pl.pallas_call(
        functools.partial(_fftc_kernel, scale=1.0 / math.sqrt(tc * FNET_GC)),
        grid=(nb,),
        in_specs=[pl.BlockSpec((None, tc, FNET_W), lambda b: (b, 0, 0)),
                  pl.BlockSpec((1, FNET_W), const),
                  pl.BlockSpec((FNET_W, FNET_W), const), pl.BlockSpec((FNET_W, FNET_W), const),
                  pl.BlockSpec((tc, tc), const), pl.BlockSpec((tc, tc), const)],
        out_specs=pl.BlockSpec((None, tc, FNET_W), lambda b: (b, 0, 0)),
        out_shape=jax.ShapeDtypeStruct((nb, tc, FNET_W), F32),
        compiler_params=_cparams(("arbitrary",)),
        name="fft_ctx",
    )(f_ctx, g256, ccm, scm, ct, st)


def _odout_kernel(yf_ref, yb_ref, xs_ref, z_ref, f_ref, dsk_ref, g_ref, w1_ref, w2_ref, x_ref, mod_ref,
                  lg_ref, lb_ref, o_ref):
    GW = SSD_HPG * SSD_HD
    y = yf_ref[...] + yb_ref[...] + xs_ref[...].astype(F32) * dsk_ref[...]
    y = y * _silu(z_ref[...].astype(F32))
    parts = []
    for g in range(SSD_GROUPS):
        yg = y[:, g * GW:(g + 1) * GW]
        ms = jnp.mean(yg * yg, axis=-1, keepdims=True)
        parts.append(yg * lax.rsqrt(ms + EPS))
    yn = (jnp.concatenate(parts, axis=1) * g_ref[...]).astype(BF16)
    o = jnp.dot(yn, w1_ref[...], preferred_element_type=F32)
    o = o + jnp.dot(f_ref[...].astype(BF16), w2_ref[...], preferred_element_type=F32)
    gate = mod_ref[0, 2:3, :]
    o_ref[...] = _ln(ALPHA * x_ref[...] + gate * o, lg_ref[...], lb_ref[...])


def _odout(yf2, yb2, xbc2, p2, f2, dsk, g768, w1, w2, x2, mod, lg, lb, tps, nb):
    ntok, d = x2.shape
    row = lambda i: (i, 0)
    const = lambda i: (0, 0)
    return pl.pallas_call(
        _odout_kernel,
        grid=(ntok // TM,),
        in_specs=[pl.BlockSpec((TM, SSD_INNER), row), pl.BlockSpec((TM, SSD_INNER), row),
                  pl.BlockSpec((TM, SSD_INNER), row), pl.BlockSpec((TM, SSD_INNER), row),
                  pl.BlockSpec((TM, FNET_W), row),
                  pl.BlockSpec((1, SSD_INNER), const), pl.BlockSpec((1, SSD_INNER), const),
                  pl.BlockSpec((SSD_INNER, d), const), pl.BlockSpec((FNET_W, d), const),
                  pl.BlockSpec((TM, d), row),
                  pl.BlockSpec((1, 6, d), lambda i: (_mod_row(i, tps, nb), 0, 0)),
                  pl.BlockSpec((1, d), const), pl.BlockSpec((1, d), const)],
        out_specs=pl.BlockSpec((TM, d), row),
        out_shape=jax.ShapeDtypeStruct((ntok, d), F32),
        compiler_params=_cparams(("arbitrary",)),
        name="odd_out",
    )(yf2, yb2, xbc2, p2, f2, dsk, g768, w1, w2, x2, mod, lg, lb)


def _router_kernel(x_ref, mod_ref, rw_ref, rb_ref, v_ref, idx_ref, gw_ref, cnt_ref):
    @pl.when(pl.program_id(0) == 0)
    def _():
        cnt_ref[...] = jnp.zeros_like(cnt_ref)

    shift = mod_ref[0, 3:4, :]
    scale = mod_ref[0, 4:5, :]
    v = x_ref[...] * (1.0 + scale) + shift
    v_ref[...] = v.astype(v_ref.dtype)
    logits = lax.dot_general(rw_ref[...], v, (((1,), (1,)), ((), ())), precision=HI,
                             preferred_element_type=F32)
    s = 1.0 / (1.0 + jnp.exp(-logits))
    sel = s + rb_ref[:, 0:1]
    izero = jnp.zeros((1, TM), jnp.int32)
    best = None
    for g in range(N_EXPERT_GROUPS):
        a = [sel[g * EXPERTS_PER_GROUP + j:g * EXPERTS_PER_GROUP + j + 1, :] for j in range(EXPERTS_PER_GROUP)]
        sv = [s[g * EXPERTS_PER_GROUP + j:g * EXPERTS_PER_GROUP + j + 1, :] for j in range(EXPERTS_PER_GROUP)]
        m1, i1, s1 = a[0], izero, sv[0]
        for j in range(1, EXPERTS_PER_GROUP):
            gt = a[j] > m1
            m1 = jnp.where(gt, a[j], m1)
            i1 = jnp.where(gt, j, i1)
            s1 = jnp.where(gt, sv[j], s1)
        m2 = jnp.full((1, TM), -jnp.inf, F32)
        i2, s2 = izero, jnp.zeros((1, TM), F32)
        for j in range(EXPERTS_PER_GROUP):
            gt = jnp.logical_and(i1 != j, a[j] > m2)
            m2 = jnp.where(gt, a[j], m2)
            i2 = jnp.where(gt, j, i2)
            s2 = jnp.where(gt, sv[j], s2)
        cand = (m1 + m2, i1 + g * EXPERTS_PER_GROUP, i2 + g * EXPERTS_PER_GROUP, s1, s2)
        if best is None:
            best = cand
        else:
            gt = cand[0] > best[0]
            best = tuple(jnp.where(gt, cn, bs) for cn, bs in zip(cand, best))
    _, e1, e2, w1, w2 = best
    tot = w1 + w2
    idx_ref[0] = jnp.concatenate([e1, e2], axis=0)
    gw_ref[0] = jnp.concatenate([w1 / tot, w2 / tot], axis=0)
    eio = lax.broadcasted_iota(jnp.int32, (N_EXPERTS, TM), 0)
    oh = jnp.logical_or(eio == e1, eio == e2).astype(F32)
    cnt_ref[...] += jnp.sum(oh, axis=1, keepdims=True)


def _router(x2, mod, rw_t, rb, tps, nb):
    ntok, d = x2.shape
    nt = ntok // TM
    return pl.pallas_call(
        _router_kernel,
        grid=(nt,),
        in_specs=[pl.BlockSpec((TM, d), lambda i: (i, 0)),
                  pl.BlockSpec((1, 6, d), lambda i: (_mod_row(i, tps, nb), 0, 0)),
                  pl.BlockSpec((N_EXPERTS, d), lambda i: (0, 0)),
                  pl.BlockSpec((N_EXPERTS, LANES), lambda i: (0, 0))],
        out_specs=[pl.BlockSpec((TM, d), lambda i: (i, 0)),
                   pl.BlockSpec((1, TOP_K, TM), lambda i: (i, 0, 0)),
                   pl.BlockSpec((1, TOP_K, TM), lambda i: (i, 0, 0)),
                   pl.BlockSpec((N_EXPERTS, LANES), lambda i: (0, 0))],
        out_shape=[jax.ShapeDtypeStruct((ntok, d), BF16),
                   jax.ShapeDtypeStruct((nt, TOP_K, TM), jnp.int32),
                   jax.ShapeDtypeStruct((nt, TOP_K, TM), F32),
                   jax.ShapeDtypeStruct((N_EXPERTS, LANES), F32)],
        compiler_params=_cparams(("arbitrary",)),
        name="router",
    )(x2, mod, rw_t, rb)


def _dispatch_kernel(idx_ref, start_ref, dest_ref, run_ref):
    @pl.when(pl.program_id(0) == 0)
    def _():
        run_ref[...] = jnp.zeros_like(run_ref)

    e1 = idx_ref[0, 0:1, :]
    e2 = idx_ref[0, 1:2, :]
    eio = lax.broadcasted_iota(jnp.int32, (N_EXPERTS, TM), 0)
    oh1 = eio == e1
    oh2 = eio == e2
    oh = jnp.logical_or(oh1, oh2)
    r = lax.broadcasted_iota(jnp.int32, (TM, TM), 0)
    c = lax.broadcasted_iota(jnp.int32, (TM, TM), 1)
    before = (r < c).astype(BF16)
    rank = jnp.dot(oh.astype(BF16), before, preferred_element_type=F32)
    pos = rank + run_ref[:, 0:1] + start_ref[:, 0:1]
    d1 = jnp.sum(jnp.where(oh1, pos, 0.0), axis=0, keepdims=True)
    d2 = jnp.sum(jnp.where(oh2, pos, 0.0), axis=0, keepdims=True)
    dest_ref[0] = jnp.concatenate([d1, d2], axis=0).astype(jnp.int32)
    run_ref[...] += jnp.sum(oh.astype(F32), axis=1, keepdims=True)


def _dispatch(idx_t, start):
    nt = idx_t.shape[0]
    return pl.pallas_call(
        _dispatch_kernel,
        grid=(nt,),
        in_specs=[pl.BlockSpec((1, TOP_K, TM), lambda i: (i, 0, 0)),
                  pl.BlockSpec((N_EXPERTS, LANES), lambda i: (0, 0))],
        out_specs=pl.BlockSpec((1, TOP_K, TM), lambda i: (i, 0, 0)),
        out_shape=jax.ShapeDtypeStruct((nt, TOP_K, TM), jnp.int32),
        scratch_shapes=[pltpu.VMEM((N_EXPERTS, LANES), F32)],
        compiler_params=_cparams(("arbitrary",)),
        name="dispatch",
    )(idx_t, start)


def _moe_kernel(be_ref, x_ref, wg_ref, wu_ref, wd_ref, gw_ref, o_ref):
    del be_ref
    x = x_ref[...]
    g = jnp.dot(x, wg_ref[...], preferred_element_type=F32)
    u = jnp.dot(x, wu_ref[...], preferred_element_type=F32)
    h = (_silu(g) * u).astype(BF16)
    y = jnp.dot(h, wd_ref[...], preferred_element_type=F32)
    o_ref[...] = (y * gw_ref[...]).astype(o_ref.dtype)


def _moe_ffn(block_e, buf, wg, wu, wd, wrow):
    nrow, d = buf.shape
    nblk = nrow // TM
    de = wg.shape[2]
    grid_spec = pltpu.PrefetchScalarGridSpec(
        num_scalar_prefetch=1,
        grid=(nblk,),
        in_specs=[pl.BlockSpec((TM, d), lambda i, be: (i, 0)),
                  pl.BlockSpec((None, d, de), lambda i, be: (be[i], 0, 0)),
                  pl.BlockSpec((None, d, de), lambda i, be: (be[i], 0, 0)),
                  pl.BlockSpec((None, de, d), lambda i, be: (be[i], 0, 0)),
                  pl.BlockSpec((TM, 1), lambda i, be: (i, 0))],
        out_specs=pl.BlockSpec((TM, d), lambda i, be: (i, 0)),
    )
    return pl.pallas_call(
        _moe_kernel,
        grid_spec=grid_spec,
        out_shape=jax.ShapeDtypeStruct((nrow, d), BF16),
        compiler_params=_cparams(("arbitrary",)),
        name="moe_ffn",
    )(block_e, buf, wg, wu, wd, wrow)


def _ln2_kernel(x_ref, y0_ref, y1_ref, mod_ref, lg_ref, lb_ref, o_ref):
    y = y0_ref[...].astype(F32) + y1_ref[...].astype(F32)
    gate = mod_ref[0, 5:6, :]
    o_ref[...] = _ln(ALPHA * x_ref[...] + gate * y, lg_ref[...], lb_ref[...])


def _ln2(x2, y0, y1, mod, lg, lb, tps, nb):
    ntok, d = x2.shape
    row = lambda i: (i, 0)
    const = lambda i: (0, 0)
    return pl.pallas_call(
        _ln2_kernel,
        grid=(ntok // TM,),
        in_specs=[pl.BlockSpec((TM, d), row), pl.BlockSpec((TM, d), row), pl.BlockSpec((TM, d), row),
                  pl.BlockSpec((1, 6, d), lambda i: (_mod_row(i, tps, nb), 0, 0)),
                  pl.BlockSpec((1, d), const), pl.BlockSpec((1, d), const)],
        out_specs=pl.BlockSpec((TM, d), row),
        out_shape=jax.ShapeDtypeStruct((ntok, d), F32),
        compiler_params=_cparams(("arbitrary",)),
        name="post_moe_ln",
    )(x2, y0, y1, mod, lg, lb)


def _even_w_in(w):
    o = np.cumsum((0, GLA_QK, GLA_QK, GLA_V, GLA_V, GLA_RANK, GLA_RANK, GQA_Q, GQA_KV, GQA_KV))
    q, k, v, r = w[:, o[0]:o[1]], w[:, o[1]:o[2]], w[:, o[2]:o[3]], w[:, o[3]:o[4]]
    lr = w[:, o[4]:o[6]]
    gq, gk, gv = w[:, o[6]:o[7]], w[:, o[7]:o[8]], w[:, o[8]:o[9]]
    d = w.shape[0]
    zeros = lambda n: jnp.zeros((d, n), w.dtype)
    out = jnp.concatenate([gq, gk, gv, q, k, lr, zeros(LANES - 2 * GLA_RANK), zeros(LANES), v, r], axis=1)
    assert out.shape[1] == EVEN_W
    return out.astype(BF16)


def _odd_w_in(w):
    o = np.cumsum((0, SSD_INNER, SSD_INNER, SSD_BC, SSD_BC, SSD_HEADS, SSD_HEADS, FNET_W))
    d = w.shape[0]
    out = jnp.concatenate([w[:, :o[4]], w[:, o[4]:o[6]], jnp.zeros((d, LANES - 2 * SSD_HEADS), w.dtype),
                           w[:, o[6]:o[7]]], axis=1)
    assert out.shape[1] == ODD_W
    return out.astype(BF16)


def _rope_tables(ctx_len, seq):
    rows = seq // GRID_W
    r = jnp.repeat(jnp.arange(rows, dtype=F32), GRID_W)
    col = jnp.tile(jnp.arange(GRID_W, dtype=F32), rows)
    half = GQA_HD // 2
    inv = ROPE_THETA ** (-jnp.arange(0, half, 2, dtype=F32) / half)
    ar = r[:, None] * inv
    ac = col[:, None] * inv
    ang = jnp.concatenate([ar, ar, ac, ac], -1)
    cos = jnp.concatenate([jnp.ones((ctx_len, GQA_HD), F32), jnp.cos(ang)], 0)
    sin = jnp.concatenate([jnp.zeros((ctx_len, GQA_HD), F32), jnp.sin(ang)], 0)
    return jnp.tile(cos, (1, 2)), jnp.tile(sin, (1, 2))


def _even_layer(x2, mod, nb, t, tps, ctx_len, w_in, w_o, w_dec, b_dec, gla_g, qn_g, kn_g, cos2, sin2, lg, lb):
    p2 = _inproj(x2, mod, _even_w_in(w_in), tps, nb)
    p3 = p2.reshape(nb, t, EVEN_W)
    outs = []
    for di in range(2):
        wd_pad = jnp.zeros((LANES, GLA_QK), F32).at[di * GLA_RANK:(di + 1) * GLA_RANK].set(w_dec[di])
        outs.append(_gla_scan(p3, wd_pad, b_dec[di][None, :], reverse=(di == 1)))
    qh2, kg2, vg2 = _qkprep(p2, cos2, sin2, jnp.tile(qn_g, 2)[None, :], jnp.tile(kn_g, 2)[None, :], tps)
    att = _attention(qh2.reshape(nb, t, GQA_HEADS * LANES), kg2.reshape(nb, t, GQA_KV_HEADS * LANES),
                     vg2.reshape(nb, t, GQA_KV_HEADS * LANES), ctx_len)
    w1, w2 = w_o[:GLA_V].astype(BF16), w_o[GLA_V:].astype(BF16)
    ntok = nb * t
    return _evout(outs[0].reshape(ntok, GLA_V), outs[1].reshape(ntok, GLA_V), p2, att.reshape(ntok, GQA_Q),
                  jnp.tile(gla_g, GLA_HEADS)[None, :], w1, w2, x2, mod, lg, lb, tps, nb)


def _odd_layer(x2, mod, nb, t, tps, ctx_len, w_in, w_o, conv_w, conv_b, dt_bias, a_log, d_skip, ssd_g, fnet_g,
               lg, lb):
    p2 = _inproj(x2, mod, _odd_w_in(w_in), tps, nb)
    p3 = p2.reshape(nb, t, ODD_W)
    w8 = jnp.zeros((8, SSD_CONV_CH), F32).at[:SSD_CONV].set(conv_w)
    xbc2 = _conv(p2, w8, conv_b[None, :], tps)
    xbc3 = xbc2.reshape(nb, t, SSD_CONV_CH)
    ys = []
    heads = jnp.arange(SSD_HEADS)
    for di in range(2):
        lanes = di * SSD_HEADS + heads
        bias_pad = jnp.zeros((1, LANES), F32).at[0, lanes].set(dt_bias[di])
        a_pad = jnp.zeros((1, LANES), F32).at[0, lanes].set(-jnp.exp(a_log[di]))
        eexp = jnp.zeros((LANES, SSD_INNER), F32).at[jnp.repeat(lanes, SSD_HD), jnp.arange(SSD_INNER)].set(1.0)
        esel = jnp.zeros((16, LANES), F32).at[heads, lanes].set(1.0)
        ys.append(_ssd_scan(xbc3, p3, bias_pad, a_pad, eexp, esel, reverse=(di == 1)))
    f3 = p3[:, :, ODD_W - FNET_W:]
    g256 = fnet_g[None, :]
    fmix = jnp.concatenate([_fourier_ctx(f3[:, :ctx_len], g256), _fourier_latent(f3[:, ctx_len:], g256)], axis=1)
    ntok = nb * t
    return _odout(ys[0].reshape(ntok, SSD_INNER), ys[1].reshape(ntok, SSD_INNER), xbc2, p2,
                  fmix.reshape(ntok, FNET_W), jnp.repeat(d_skip, SSD_HD)[None, :], ssd_g[None, :],
                  w_o[:SSD_INNER].astype(BF16), w_o[SSD_INNER:].astype(BF16), x2, mod, lg, lb, tps, nb)


def _moe_layer(x2, mod, nb, tps, rw_t, rb, wg, wu, wd, lg, lb):
    ntok, d = x2.shape
    v, idx_t, gw_t, cnt = _router(x2, mod, rw_t, rb, tps, nb)
    counts = cnt[:, 0].astype(jnp.int32)
    padded = (counts + TM - 1) // TM * TM
    pad_end = jnp.cumsum(padded)
    pad_start = pad_end - padded
    nblk = ntok * TOP_K // TM + N_EXPERTS
    block_e = jnp.minimum(jnp.searchsorted(pad_end, jnp.arange(nblk, dtype=jnp.int32) * TM, side="right"),
                          N_EXPERTS - 1).astype(jnp.int32)
    start = jnp.broadcast_to(pad_start.astype(F32)[:, None], (N_EXPERTS, LANES))
    dest_t = _dispatch(idx_t, start)
    dest0 = dest_t[:, 0, :].reshape(ntok)
    dest1 = dest_t[:, 1, :].reshape(ntok)
    tok = jnp.arange(ntok, dtype=jnp.int32)
    src = jnp.zeros((nblk * TM,), jnp.int32).at[dest0].set(tok).at[dest1].set(tok)
    wrow = jnp.zeros((nblk * TM,), F32).at[dest0].set(gw_t[:, 0, :].reshape(ntok)).at[dest1].set(
        gw_t[:, 1, :].reshape(ntok))
    buf = v[src]
    y = _moe_ffn(block_e, buf, wg, wu, wd, wrow[:, None])
    return _ln2(x2, y[dest0], y[dest1], mod, lg, lb, tps, nb)


def kernel(x, c, ctx, c_ctx, ada_w, ada_b, ln_g, ln_b, ev_w_in, ev_w_o, gla_w_decay, gla_b_decay, gla_norm_g,
           gqa_q_norm_g, gqa_k_norm_g, od_w_in, od_w_o, ssd_conv_w, ssd_conv_b, ssd_dt_bias, ssd_a_log, ssd_d,
           ssd_norm_g, fnet_norm_g, router_w, router_b, exp_w_gate, exp_w_up, exp_w_down):
    nb, seq, d = x.shape
    ctx_len = ctx.shape[1]
    assert ctx_len == TM and seq % TM == 0 and d == D_MODEL and nb <= 7
    t = ctx_len + seq
    tps = t // TM
    ntok = nb * t
    cc = jnp.zeros((8, d), F32).at[:nb].set(c).at[nb].set(c_ctx)
    mod_all = _ada_all(cc, ada_w, ada_b).reshape(DEPTH, 8, 6, d)
    cos2, sin2 = _rope_tables(ctx_len, seq)
    rw_t = router_w.T
    rb = jnp.broadcast_to(router_b[:, None], (N_EXPERTS, LANES))
    x2 = jnp.concatenate([ctx, x], axis=1).reshape(ntok, d)
    for layer in range(DEPTH):
        mod = mod_all[layer]
        i = layer // 2
        lg0, lb0 = ln_g[layer, 0][None, :], ln_b[layer, 0][None, :]
        lg1, lb1 = ln_g[layer, 1][None, :], ln_b[layer, 1][None, :]
        if layer % 2 == 0:
            x2 = _even_layer(x2, mod, nb, t, tps, ctx_len, ev_w_in[i], ev_w_o[i], gla_w_decay[i], gla_b_decay[i],
                             gla_norm_g[i], gqa_q_norm_g[i], gqa_k_norm_g[i], cos2, sin2, lg0, lb0)
        else:
            x2 = _odd_layer(x2, mod, nb, t, tps, ctx_len, od_w_in[i], od_w_o[i], ssd_conv_w[i], ssd_conv_b[i],
                            ssd_dt_bias[i], ssd_a_log[i], ssd_d[i], ssd_norm_g[i], fnet_norm_g[i], lg0, lb0)
        x2 = _moe_layer(x2, mod, nb, tps, rw_t, rb, exp_w_gate[layer].astype(BF16), exp_w_up[layer].astype(BF16),
                        exp_w_down[layer].astype(BF16), lg1, lb1)
    return x2.reshape(nb, t, d)[:, ctx_len:]
```

```python
import functools
import math

import jax
import jax.numpy as jnp
import numpy as np
from jax import lax
from jax.experimental import pallas as pl
from jax.experimental.pallas import tpu as pltpu

F32 = jnp.float32
BF16 = jnp.bfloat16
HI = lax.Precision.HIGHEST

D_MODEL = 1024
DEPTH = 4
GRID_W = 64
GLA_HEADS, GLA_DK, GLA_DV, GLA_RANK, GLA_TAU, GLA_CHUNK = 4, 64, 128, 16, 16.0, 64
GQA_HEADS, GQA_KV_HEADS, GQA_HD = 8, 2, 64
ROPE_THETA = 10000.0
SSD_HEADS, SSD_HD, SSD_GROUPS, SSD_STATE, SSD_CONV, SSD_CHUNK = 12, 64, 2, 128, 5, 64
SSD_HPG = SSD_HEADS // SSD_GROUPS
SSD_INNER = SSD_HEADS * SSD_HD
SSD_BC = SSD_GROUPS * SSD_STATE
SSD_CONV_CH = SSD_INNER + 2 * SSD_BC
FNET_GROUPS, FNET_GC = 4, 64
FNET_W = FNET_GROUPS * FNET_GC
N_EXPERTS, N_EXPERT_GROUPS, TOP_K, D_EXPERT = 16, 4, 2, 768
EXPERTS_PER_GROUP = N_EXPERTS // N_EXPERT_GROUPS
GLA_QK = GLA_HEADS * GLA_DK
GLA_V = GLA_HEADS * GLA_DV
GQA_Q = GQA_HEADS * GQA_HD
GQA_KV = GQA_KV_HEADS * GQA_HD
EPS = 1e-6
ALPHA = (2.0 * DEPTH) ** 0.25

LANES = 128
TM = 256
EVEN_W = 2560
ODD_W = 2432
VMEM_LIMIT = 48 * 1024 * 1024
LOG2E = 1.4426950408889634
ATT_TQ = 128
ATT_HEADROOM = 64.0
ATT_MIN_DENOM = 2.0 ** -60
ROWG = 8
SORT_ROWS = TOP_K * TM + N_EXPERTS * (ROWG - 1)


def _cparams(sem):
    return pltpu.CompilerParams(dimension_semantics=sem, vmem_limit_bytes=VMEM_LIMIT)


def _silu(x):
    return x * (1.0 / (1.0 + jnp.exp(-x)))


def _softplus(x):
    return jnp.maximum(x, 0.0) + jnp.log1p(jnp.exp(-jnp.abs(x)))


def _mod_row(i, tiles_per_seq, n_batch):
    return jnp.where(i % tiles_per_seq == 0, n_batch, i // tiles_per_seq)


def _ada_kernel(c_ref, w_ref, b_ref, o_ref):
    s = _silu(c_ref[...])
    o_ref[0] = jnp.dot(s, w_ref[0], precision=HI, preferred_element_type=F32) + b_ref[0]


def _ada_all(cc, ada_w, ada_b):
    depth, d, n = ada_w.shape
    tn = 512
    return pl.pallas_call(
        _ada_kernel,
        grid=(depth, n // tn),
        in_specs=[pl.BlockSpec((8, d), lambda l, j: (0, 0)),
                  pl.BlockSpec((1, d, tn), lambda l, j: (l, 0, j)),
                  pl.BlockSpec((1, 1, tn), lambda l, j: (l, 0, j))],
        out_specs=pl.BlockSpec((1, 8, tn), lambda l, j: (l, 0, j)),
        out_shape=jax.ShapeDtypeStruct((depth, 8, n), F32),
        compiler_params=_cparams(("arbitrary", "arbitrary")),
        name="adaln",
    )(cc, ada_w, ada_b.reshape(depth, 1, n))


def _inproj_kernel(x_ref, mod_ref, w_ref, o_ref):
    shift = mod_ref[0, 0:1, :]
    scale = mod_ref[0, 1:2, :]
    u = (x_ref[...] * (1.0 + scale) + shift).astype(BF16)
    o_ref[...] = jnp.dot(u, w_ref[...], preferred_element_type=F32).astype(o_ref.dtype)


def _inproj(x2, mod, w, tps, nb):
    ntok, d = x2.shape
    nw = w.shape[1]
    return pl.pallas_call(
        _inproj_kernel,
        grid=(ntok // TM,),
        in_specs=[pl.BlockSpec((TM, d), lambda i: (i, 0)),
                  pl.BlockSpec((1, 6, d), lambda i: (_mod_row(i, tps, nb), 0, 0)),
                  pl.BlockSpec((d, nw), lambda i: (0, 0))],
        out_specs=pl.BlockSpec((TM, nw), lambda i: (i, 0)),
        out_shape=jax.ShapeDtypeStruct((ntok, nw), BF16),
        compiler_params=_cparams(("arbitrary",)),
        name="inproj",
    )(x2, mod, w)


def _scan_block(i, nblk, reverse):
    if not reverse:
        return i
    return jnp.where(i == 0, 0, nblk - i)


def _tri(n, reverse):
    r = lax.broadcasted_iota(jnp.int32, (n, n), 0)
    c = lax.broadcasted_iota(jnp.int32, (n, n), 1)
    return (c >= r) if reverse else (c <= r)


def _chunk_sum_mat(chunk, reverse):
    l = np.arange(TM)[:, None]
    m = np.arange(TM)[None, :]
    same = (l // chunk) == (m // chunk)
    return (same & ((m >= l) if reverse else (m <= l))).astype(np.float32)


def _split3(x):
    hi = x.astype(BF16)
    r1 = x - hi.astype(F32)
    mid = r1.astype(BF16)
    return hi, mid, (r1 - mid.astype(F32)).astype(BF16)


def _chunk_rows(x, chunk, idx):
    return jnp.concatenate([jnp.broadcast_to(x[c * chunk + idx:c * chunk + idx + 1, :], (chunk, x.shape[1]))
                            for c in range(x.shape[0] // chunk)], axis=0)


def _gla_kernel(q_ref, k_ref, v_ref, lr_ref, wd_ref, bd_ref, cm_ref, o_ref, s_ref, *, reverse):
    L = GLA_CHUNK
    nchunk = TM // L
    nt = (((1,), (1,)), ((), ()))
    tn = (((0,), (0,)), ((), ()))
    end = 0 if reverse else L - 1
    mid = L // 2 if reverse else L // 2 - 1

    @pl.when(pl.program_id(2) == 0)
    def _():
        s_ref[...] = jnp.zeros_like(s_ref)

    q = q_ref[...].astype(F32) * (GLA_DK ** -0.5)
    k = k_ref[...].astype(F32)
    v = v_ref[...]
    lr = lr_ref[...]
    z = bd_ref[...] + sum(jnp.dot(lr, wd_ref[j], preferred_element_type=F32) for j in range(3))
    la = -_softplus(-z) * (1.0 / GLA_TAU)
    cm = cm_ref[...]
    b = sum(jnp.dot(cm, part, preferred_element_type=F32) for part in _split3(la))
    bmid = _chunk_rows(b, L, mid)
    bend = _chunk_rows(b, L, end)
    qs = q * jnp.exp(b - bmid)
    ks = (k * jnp.exp(bmid - b)).astype(BF16)
    qi = q * jnp.exp(b)
    kend = (k * jnp.exp(bend - b)).astype(BF16)
    r = lax.broadcasted_iota(jnp.int32, (TM, TM), 0)
    c = lax.broadcasted_iota(jnp.int32, (TM, TM), 1)
    mask = jnp.logical_and(r // L == c // L, (c >= r) if reverse else (c <= r))
    lane = lax.broadcasted_iota(jnp.int32, (TM, LANES), 1)
    heads = (lane < GLA_DK, lane >= GLA_DK)
    intra = []
    for h in range(2):
        att = lax.dot_general(jnp.where(heads[h], qs, 0.0).astype(BF16), ks, nt, preferred_element_type=F32)
        att = jnp.where(mask, att, 0.0).astype(BF16)
        intra.append(jnp.dot(att, v[:, h * GLA_DV:(h + 1) * GLA_DV], preferred_element_type=F32))
    qi_h = [jnp.where(heads[h], qi, 0.0).astype(BF16) for h in range(2)]
    tot = jnp.concatenate([b[ch * L + end:ch * L + end + 1, :] for ch in range(nchunk)]
                          + [jnp.zeros((ROWG - nchunk, LANES), F32)], axis=0).T
    ds, dec = [], []
    for ch in range(nchunk):
        rows = slice(ch * L, (ch + 1) * L)
        ds.append(lax.dot_general(kend[rows], v[rows], tn, preferred_element_type=F32))
        dec.append(jnp.exp(jnp.broadcast_to(tot[:, ch:ch + 1], (2 * GLA_DK, 2 * GLA_DV))))
    s = s_ref[...]
    inter = [None] * nchunk
    for ch in (range(nchunk - 1, -1, -1) if reverse else range(nchunk)):
        rows = slice(ch * L, (ch + 1) * L)
        s_bf = s.astype(BF16)
        inter[ch] = jnp.concatenate(
            [jnp.dot(qi_h[h][rows], s_bf[:, h * GLA_DV:(h + 1) * GLA_DV], preferred_element_type=F32)
             for h in range(2)], axis=1)
        s = dec[ch] * s + ds[ch]
    s_ref[...] = s
    o_ref[...] = jnp.concatenate(intra, axis=1) + jnp.concatenate(inter, axis=0)


def _gla_scan(p3, wd_pad, bd, reverse):
    nb, t, _ = p3.shape
    nblk = t // TM
    blk = lambda i: _scan_block(i, nblk, reverse)
    cm = jnp.asarray(_chunk_sum_mat(GLA_CHUNK, reverse), BF16)
    wd3 = jnp.stack(_split3(wd_pad))
    return pl.pallas_call(
        functools.partial(_gla_kernel, reverse=reverse),
        grid=(nb, 2, nblk),
        in_specs=[pl.BlockSpec((None, TM, LANES), lambda b, p, i: (b, blk(i), 6 + p)),
                  pl.BlockSpec((None, TM, LANES), lambda b, p, i: (b, blk(i), 8 + p)),
                  pl.BlockSpec((None, TM, 2 * GLA_DV), lambda b, p, i: (b, blk(i), 6 + p)),
                  pl.BlockSpec((None, TM, LANES), lambda b, p, i: (b, blk(i), 10)),
                  pl.BlockSpec((3, LANES, LANES), lambda b, p, i: (0, 0, p)),
                  pl.BlockSpec((1, LANES), lambda b, p, i: (0, p)),
                  pl.BlockSpec((TM, TM), lambda b, p, i: (0, 0))],
        out_specs=pl.BlockSpec((None, TM, 2 * GLA_DV), lambda b, p, i: (b, blk(i), p)),
        out_shape=jax.ShapeDtypeStruct((nb, t, GLA_V), F32),
        scratch_shapes=[pltpu.VMEM((2 * GLA_DK, 2 * GLA_DV), F32)],
        compiler_params=_cparams(("arbitrary", "arbitrary", "arbitrary")),
        name="gla_bwd" if reverse else "gla_fwd",
    )(p3, p3, p3, p3, wd3, bd, cm)


def _seg_ones(width, seg):
    r = lax.broadcasted_iota(jnp.int32, (width, width), 0) // seg
    c = lax.broadcasted_iota(jnp.int32, (width, width), 1) // seg
    return jnp.where(r == c, 1.0 / seg, 0.0).astype(F32)


def _norm_rope(x, g, cos, sin, bd):
    ms = jnp.dot(x * x, bd, precision=HI, preferred_element_type=F32)
    xn = x * lax.rsqrt(ms + EPS) * g
    lane = lax.broadcasted_iota(jnp.int32, xn.shape, 1)
    quarter = GQA_HD // 4
    up = pltpu.roll(xn, LANES - quarter, 1)
    dn = pltpu.roll(xn, quarter, 1)
    rot = jnp.where(lane % (2 * quarter) < quarter, -up, dn)
    return xn * cos + rot * sin


def _qkprep_kernel(q_ref, k_ref, v_ref, cos_ref, sin_ref, gq_ref, gk_ref, qo_ref, ko_ref, vo_ref):
    bd = _seg_ones(LANES, GQA_HD)
    cos = cos_ref[...]
    sin = sin_ref[...]
    lane = lax.broadcasted_iota(jnp.int32, (TM, LANES), 1)
    low = lane < GQA_HD
    extra = lane == GQA_HD
    one_hot = jnp.where(extra, 1.0, 0.0)
    k_bound = math.sqrt(GQA_HD) * jnp.max(jnp.abs(gk_ref[...]), axis=-1, keepdims=True) * 1.02
    for j in range(GQA_Q // LANES):
        x = q_ref[:, j * LANES:(j + 1) * LANES].astype(F32)
        y = _norm_rope(x, gq_ref[...], cos, sin, bd) * (GQA_HD ** -0.5 * LOG2E)
        y = y.astype(BF16).astype(F32)
        norm = jnp.sqrt(jnp.dot(y * y, bd, precision=HI, preferred_element_type=F32) * GQA_HD)
        shift = ATT_HEADROOM - norm * k_bound
        lo = jnp.where(low, y, jnp.where(extra, pltpu.roll(shift, GQA_HD, 1), 0.0))
        hi = jnp.where(low, pltpu.roll(y, GQA_HD, 1), jnp.where(extra, shift, 0.0))
        qo_ref[:, (2 * j) * LANES:(2 * j + 1) * LANES] = lo.astype(BF16)
        qo_ref[:, (2 * j + 1) * LANES:(2 * j + 2) * LANES] = hi.astype(BF16)
    k = _norm_rope(k_ref[...].astype(F32), gk_ref[...], cos, sin, bd)
    v = v_ref[...].astype(F32)
    for src, dst in ((k, ko_ref), (v, vo_ref)):
        dst[:, 0:LANES] = jnp.where(low, src, one_hot).astype(BF16)
        dst[:, LANES:2 * LANES] = jnp.where(low, pltpu.roll(src, GQA_HD, 1), one_hot).astype(BF16)


def _qkprep(p2, cos2, sin2, gq2, gk2, tps):
    ntok = p2.shape[0]
    return pl.pallas_call(
        _qkprep_kernel,
        grid=(ntok // TM,),
        in_specs=[pl.BlockSpec((TM, GQA_Q), lambda i: (i, 0)),
                  pl.BlockSpec((TM, LANES), lambda i: (i, 4)),
                  pl.BlockSpec((TM, LANES), lambda i: (i, 5)),
                  pl.BlockSpec((TM, LANES), lambda i: (i % tps, 0)),
                  pl.BlockSpec((TM, LANES), lambda i: (i % tps, 0)),
                  pl.BlockSpec((1, LANES), lambda i: (0, 0)),
                  pl.BlockSpec((1, LANES), lambda i: (0, 0))],
        out_specs=[pl.BlockSpec((TM, GQA_HEADS * LANES), lambda i: (i, 0)),
                   pl.BlockSpec((TM, GQA_KV_HEADS * LANES), lambda i: (i, 0)),
                   pl.BlockSpec((TM, GQA_KV_HEADS * LANES), lambda i: (i, 0))],
        out_shape=[jax.ShapeDtypeStruct((ntok, GQA_HEADS * LANES), BF16),
                   jax.ShapeDtypeStruct((ntok, GQA_KV_HEADS * LANES), BF16),
                   jax.ShapeDtypeStruct((ntok, GQA_KV_HEADS * LANES), BF16)],
        compiler_params=_cparams(("arbitrary",)),
        name="qkprep",
    )(p2, p2, p2, cos2, sin2, gq2, gk2)


def _attn_kernel(q_ref, k_ref, v_ref, o_ref, *, ctx_len, n_lat_chunks, ck, n_blocks):
    i = pl.program_id(2)
    rep = GQA_HEADS // GQA_KV_HEADS
    is_ctx = i < ctx_len // ATT_TQ
    q4 = jnp.concatenate([q_ref[:, r * LANES:(r + 1) * LANES] for r in range(rep)], axis=0)
    nt = (((1,), (1,)), ((), ()))

    def chunk(acc, kc, vc):
        s = lax.dot_general(q4, kc, nt, preferred_element_type=F32)
        return acc + jnp.dot(jnp.exp2(s).astype(BF16), vc, preferred_element_type=F32)

    def store(acc):
        out = acc * (1.0 / acc[:, GQA_HD:GQA_HD + 1])
        low = lax.broadcasted_iota(jnp.int32, (ATT_TQ, LANES), 1) < GQA_HD
        hs = [out[r * ATT_TQ:(r + 1) * ATT_TQ] for r in range(rep)]
        for u in range(rep // 2):
            o_ref[:, u * LANES:(u + 1) * LANES] = jnp.where(
                low, hs[2 * u], pltpu.roll(hs[2 * u + 1], GQA_HD, 1)).astype(o_ref.dtype)

    acc = chunk(jnp.zeros((rep * ATT_TQ, LANES), F32), k_ref[0:ctx_len, :], v_ref[0:ctx_len, :])

    def body(c, acc):
        off = pl.multiple_of(ctx_len + c * ck, TM)
        return chunk(acc, k_ref[pl.ds(off, ck), :], v_ref[pl.ds(off, ck), :])

    acc = lax.fori_loop(0, jnp.where(is_ctx, 0, n_lat_chunks), body, acc)
    healthy = jnp.min(acc[:, GQA_HD:GQA_HD + 1]) >= ATT_MIN_DENOM

    @pl.when(healthy)
    def _():
        store(acc)

    @pl.when(jnp.logical_not(healthy))
    def _():
        def online(c, carry):
            m, acc = carry
            off = pl.multiple_of(c * TM, TM)
            s = lax.dot_general(q4, k_ref[pl.ds(off, TM), :], nt, preferred_element_type=F32)
            mn = jnp.maximum(m, jnp.max(s, axis=-1, keepdims=True))
            p = jnp.exp2(s - mn).astype(BF16)
            acc = jnp.exp2(m - mn) * acc + jnp.dot(p, v_ref[pl.ds(off, TM), :], preferred_element_type=F32)
            return mn, acc

        init = (jnp.full((rep * ATT_TQ, 1), -1e30, F32), jnp.zeros((rep * ATT_TQ, LANES), F32))
        _, acc2 = lax.fori_loop(0, jnp.where(is_ctx, ctx_len // TM, n_blocks), online, init)
        store(acc2)


def _attention(qh3, kg3, vg3, ctx_len):
    nb, t, _ = qh3.shape
    seq = t - ctx_len
    ck = next(c for c in (2048, 1024, 512, 256) if seq % c == 0)
    rep = GQA_HEADS // GQA_KV_HEADS
    return pl.pallas_call(
        functools.partial(_attn_kernel, ctx_len=ctx_len, n_lat_chunks=seq // ck, ck=ck, n_blocks=t // TM),
        grid=(nb, GQA_KV_HEADS, t // ATT_TQ),
        in_specs=[pl.BlockSpec((None, ATT_TQ, rep * LANES), lambda b, g, i: (b, i, g)),
                  pl.BlockSpec((None, t, LANES), lambda b, g, i: (b, 0, g)),
                  pl.BlockSpec((None, t, LANES), lambda b, g, i: (b, 0, g))],
        out_specs=pl.BlockSpec((None, ATT_TQ, rep * GQA_HD), lambda b, g, i: (b, i, g)),
        out_shape=jax.ShapeDtypeStruct((nb, t, GQA_Q), BF16),
        compiler_params=_cparams(("arbitrary", "arbitrary", "arbitrary")),
        name="gqa_attn",
    )(qh3, kg3, vg3)


def _ln(x, g, b):
    mu = jnp.mean(x, axis=-1, keepdims=True)
    xc = x - mu
    var = jnp.mean(xc * xc, axis=-1, keepdims=True)
    return xc * lax.rsqrt(var + EPS) * g + b


def _evout_kernel(of_ref, ob_ref, r_ref, att_ref, g_ref, w1_ref, w2_ref, x_ref, mod_ref, lg_ref, lb_ref, o_ref):
    o = of_ref[...] + ob_ref[...]
    r = r_ref[...].astype(F32)
    parts = []
    for h in range(GLA_HEADS):
        oh = o[:, h * GLA_DV:(h + 1) * GLA_DV]
        ms = jnp.mean(oh * oh, axis=-1, keepdims=True)
        parts.append(oh * lax.rsqrt(ms + EPS))
    gl = (jnp.concatenate(parts, axis=1) * g_ref[...] * _silu(r)).astype(BF16)
    y = jnp.dot(gl, w1_ref[...], preferred_element_type=F32)
    y = y + jnp.dot(att_ref[...], w2_ref[...], preferred_element_type=F32)
    gate = mod_ref[0, 2:3, :]
    o_ref[...] = _ln(ALPHA * x_ref[...] + gate * y, lg_ref[...], lb_ref[...])


def _evout(of2, ob2, p2, att2, g512, w1, w2, x2, mod, lg, lb, tps, nb):
    ntok, d = x2.shape
    row = lambda i: (i, 0)
    const = lambda i: (0, 0)
    return pl.pallas_call(
        _evout_kernel,
        grid=(ntok // TM,),
        in_specs=[pl.BlockSpec((TM, GLA_V), row), pl.BlockSpec((TM, GLA_V), row),
                  pl.BlockSpec((TM, GLA_V), lambda i: (i, 4)),
                  pl.BlockSpec((TM, GQA_Q), row),
                  pl.BlockSpec((1, GLA_V), const),
                  pl.BlockSpec((GLA_V, d), const), pl.BlockSpec((GQA_Q, d), const),
                  pl.BlockSpec((TM, d), row),
                  pl.BlockSpec((1, 6, d), lambda i: (_mod_row(i, tps, nb), 0, 0)),
                  pl.BlockSpec((1, d), const), pl.BlockSpec((1, d), const)],
        out_specs=pl.BlockSpec((TM, d), row),
        out_shape=jax.ShapeDtypeStruct((ntok, d), F32),
        compiler_params=_cparams(("arbitrary",)),
        name="even_out",
    )(of2, ob2, p2, att2, g512, w1, w2, x2, mod, lg, lb)


def _conv_kernel(prev_ref, cur_ref, next_ref, w_ref, b_ref, o_ref, pad_ref, *, tps):
    j = pl.program_id(0) % tps
    has_prev = (j >= 2).astype(F32)
    has_next = jnp.logical_and(j >= 1, j <= tps - 2).astype(F32)
    pad_ref[0:8, :] = prev_ref[...].astype(F32) * has_prev
    pad_ref[8:8 + TM, :] = cur_ref[...].astype(F32)
    pad_ref[8 + TM:16 + TM, :] = next_ref[...].astype(F32) * has_next
    half = (SSD_CONV - 1) // 2
    acc = jnp.zeros((TM, cur_ref.shape[1]), F32) + b_ref[...]
    for tap in range(SSD_CONV):
        acc = acc + pad_ref[pl.ds(8 + tap - half, TM), :] * w_ref[tap:tap + 1, :]
    o_ref[...] = _silu(acc).astype(o_ref.dtype)


def _conv(p2, w8, bias, tps):
    ntok = p2.shape[0]
    cw = 256
    ncb = SSD_CONV_CH // cw
    cb0 = SSD_INNER // cw
    r8 = TM // 8
    nrow8 = ntok // 8
    return pl.pallas_call(
        functools.partial(_conv_kernel, tps=tps),
        grid=(ntok // TM, ncb),
        in_specs=[pl.BlockSpec((8, cw), lambda i, c: (jnp.maximum(i * r8 - 1, 0), cb0 + c)),
                  pl.BlockSpec((TM, cw), lambda i, c: (i, cb0 + c)),
                  pl.BlockSpec((8, cw), lambda i, c: (jnp.minimum((i + 1) * r8, nrow8 - 1), cb0 + c)),
                  pl.BlockSpec((8, cw), lambda i, c: (0, c)),
                  pl.BlockSpec((1, cw), lambda i, c: (0, c))],
        out_specs=pl.BlockSpec((TM, cw), lambda i, c: (i, c)),
        out_shape=jax.ShapeDtypeStruct((ntok, SSD_CONV_CH), BF16),
        scratch_shapes=[pltpu.VMEM((TM + 16, cw), F32)],
        compiler_params=_cparams(("arbitrary", "arbitrary")),
        name="ssd_conv",
    )(p2, p2, p2, w8, bias)


def _ssd_kernel(xbc_ref, dt_ref, bias_ref, a_ref, eexp_ref, esel_ref, o_ref, h_ref, *, reverse):
    L = SSD_CHUNK
    GW = SSD_HPG * SSD_HD

    @pl.when(pl.program_id(1) == 0)
    def _():
        h_ref[...] = jnp.zeros_like(h_ref)

    tri = _tri(L, reverse)
    cmat = tri.astype(F32)
    lane = lax.broadcasted_iota(jnp.int32, (L, LANES), 1)
    low = lane < SSD_HD
    end = 0 if reverse else L - 1
    lane0 = SSD_HEADS if reverse else 0
    nchunk = TM // L
    order = range(nchunk - 1, -1, -1) if reverse else range(nchunk)
    for c in order:
        sl = pl.ds(c * L, L)
        xs = xbc_ref[sl, 0:SSD_INNER].astype(F32)
        bs = xbc_ref[sl, SSD_INNER:SSD_INNER + SSD_BC]
        cs = xbc_ref[sl, SSD_INNER + SSD_BC:SSD_CONV_CH]
        dt = _softplus(dt_ref[sl, :].astype(F32) + bias_ref[...])
        a = dt * a_ref[...]
        cum = jnp.dot(cmat, a, precision=HI, preferred_element_type=F32)
        cum_t = lax.dot_general(esel_ref[...], cum, (((1,), (1,)), ((), ())), precision=HI,
                                preferred_element_type=F32)
        dt_e = jnp.dot(dt, eexp_ref[...], precision=HI, preferred_element_type=F32)
        cum_e = jnp.dot(cum, eexp_ref[...], precision=HI, preferred_element_type=F32)
        cend_e = cum_e[end:end + 1, :]
        xdt = xs * dt_e
        xdt_b = xdt.astype(BF16)
        xend = (xdt * jnp.exp(cend_e - cum_e)).astype(BF16)
        ecum = jnp.exp(cum_e)
        edec = jnp.exp(cend_e)
        ys = []
        for g in range(SSD_GROUPS):
            bg = bs[:, g * SSD_STATE:(g + 1) * SSD_STATE]
            cg = cs[:, g * SSD_STATE:(g + 1) * SSD_STATE]
            cb = lax.dot_general(cg, bg, (((1,), (1,)), ((), ())), preferred_element_type=F32)
            hg = h_ref[g]
            y_inter = jnp.dot(cg, hg.astype(BF16), preferred_element_type=F32) * ecum[:, g * GW:(g + 1) * GW]
            pairs = []
            for pp in range(SSD_HPG // 2):
                yh = []
                for u in range(2):
                    h = g * SSD_HPG + 2 * pp + u
                    seg = cum[:, lane0 + h:lane0 + h + 1] - cum_t[h:h + 1, :]
                    dec = jnp.exp(jnp.where(tri, seg, -1e30))
                    mat = (cb * dec).astype(BF16)
                    col = (h - u) * SSD_HD
                    yh.append(jnp.dot(mat, xdt_b[:, col:col + LANES], preferred_element_type=F32))
                pairs.append(jnp.where(low, yh[0], yh[1]))
            ys.append(jnp.concatenate(pairs, axis=1) + y_inter)
            dh = lax.dot_general(bg, xend[:, g * GW:(g + 1) * GW], (((0,), (0,)), ((), ())),
                                 preferred_element_type=F32)
            h_ref[g] = hg * edec[:, g * GW:(g + 1) * GW] + dh
        o_ref[sl, :] = jnp.concatenate(ys, axis=1)


def _ssd_scan(xbc3, p3, bias_pad, a_pad, eexp, esel, reverse):
    nb, t, _ = xbc3.shape
    nblk = t // TM
    blk = lambda i: _scan_block(i, nblk, reverse)
    const = lambda b, i: (0, 0)
    return pl.pallas_call(
        functools.partial(_ssd_kernel, reverse=reverse),
        grid=(nb, nblk),
        in_specs=[pl.BlockSpec((None, TM, SSD_CONV_CH), lambda b, i: (b, blk(i), 0)),
                  pl.BlockSpec((None, TM, LANES), lambda b, i: (b, blk(i), 16)),
                  pl.BlockSpec((1, LANES), const), pl.BlockSpec((1, LANES), const),
                  pl.BlockSpec((LANES, SSD_INNER), const), pl.BlockSpec((16, LANES), const)],
        out_specs=pl.BlockSpec((None, TM, SSD_INNER), lambda b, i: (b, blk(i), 0)),
        out_shape=jax.ShapeDtypeStruct((nb, t, SSD_INNER), F32),
        scratch_shapes=[pltpu.VMEM((SSD_GROUPS, SSD_STATE, SSD_HPG * SSD_HD), F32)],
        compiler_params=_cparams(("arbitrary", "arbitrary")),
        name="ssd_bwd" if reverse else "ssd_fwd",
    )(xbc3, p3, bias_pad, a_pad, eexp, esel)


def _dft_mats(n):
    k = jnp.arange(n, dtype=jnp.int32)
    ang = ((k[:, None] * k[None, :]) % n).astype(F32) * (2.0 * math.pi / n)
    return jnp.cos(ang), jnp.sin(ang)


def _chan_mats():
    cc, sc = _dft_mats(FNET_GC)
    eye = jnp.eye(FNET_GROUPS, dtype=F32)
    return jnp.kron(eye, cc), jnp.kron(eye, sc)


def _group_rms(x, g, bd):
    ms = jnp.dot(x * x, bd, precision=HI, preferred_element_type=F32)
    return x * lax.rsqrt(ms + EPS) * g


def _mm(a, b):
    return jnp.dot(a, b, precision=HI, preferred_element_type=F32)


def _ffta_kernel(x_ref, g_ref, cc_ref, sc_ref, c1_ref, s1_ref, twc_ref, tws_ref, yr_ref, yi_ref, *, nb2):
    bd = _seg_ones(FNET_W, FNET_GC)
    for j in range(nb2):
        x = x_ref[:, j * FNET_W:(j + 1) * FNET_W].astype(F32)
        xn = _group_rms(x, g_ref[...], bd)
        vr = _mm(xn, cc_ref[...])
        vi = -_mm(xn, sc_ref[...])
        c1 = c1_ref[...]
        s1 = s1_ref[...]
        yr = _mm(c1, vr) + _mm(s1, vi)
        yi = _mm(c1, vi) - _mm(s1, vr)
        tc = twc_ref[j]
        ts = tws_ref[j]
        yr_ref[j] = yr * tc + yi * ts
        yi_ref[j] = yi * tc - yr * ts


def _fftb_kernel(yr_ref, yi_ref, c2_ref, s2_ref, o_ref, *, scale):
    o_ref[...] = (_mm(c2_ref[...], yr_ref[...]) + _mm(s2_ref[...], yi_ref[...])) * scale


def _fftc_kernel(x_ref, g_ref, cc_ref, sc_ref, ct_ref, st_ref, o_ref, *, scale):
    bd = _seg_ones(FNET_W, FNET_GC)
    xn = _group_rms(x_ref[...].astype(F32), g_ref[...], bd)
    a = _mm(xn, cc_ref[...])
    b = _mm(xn, sc_ref[...])
    o_ref[...] = (_mm(ct_ref[...], a) - _mm(st_ref[...], b)) * scale


def _fourier_latent(f_lat, g256):
    nb, s, _ = f_lat.shape
    n2 = 64
    n1 = s // n2
    nb2 = 8
    ccm, scm = _chan_mats()
    c1, s1 = _dft_mats(n1)
    c2, s2 = _dft_mats(n2)
    t2 = jnp.arange(n2, dtype=jnp.int32)[:, None]
    k1 = jnp.arange(n1, dtype=jnp.int32)[None, :]
    tw = ((t2 * k1) % s).astype(F32) * (2.0 * math.pi / s)
    twc = jnp.cos(tw)[:, :, None]
    tws = jnp.sin(tw)[:, :, None]
    x2 = f_lat.reshape(nb, n1, n2 * FNET_W)
    const2 = lambda b, j: (0, 0)
    yr, yi = pl.pallas_call(
        functools.partial(_ffta_kernel, nb2=nb2),
        grid=(nb, n2 // nb2),
        in_specs=[pl.BlockSpec((None, n1, nb2 * FNET_W), lambda b, j: (b, 0, j)),
                  pl.BlockSpec((1, FNET_W), const2),
                  pl.BlockSpec((FNET_W, FNET_W), const2), pl.BlockSpec((FNET_W, FNET_W), const2),
                  pl.BlockSpec((n1, n1), const2), pl.BlockSpec((n1, n1), const2),
                  pl.BlockSpec((nb2, n1, 1), lambda b, j: (j, 0, 0)),
                  pl.BlockSpec((nb2, n1, 1), lambda b, j: (j, 0, 0))],
        out_specs=[pl.BlockSpec((None, nb2, n1, FNET_W), lambda b, j: (b, j, 0, 0)),
                   pl.BlockSpec((None, nb2, n1, FNET_W), lambda b, j: (b, j, 0, 0))],
        out_shape=[jax.ShapeDtypeStruct((nb, n2, n1, FNET_W), F32),
                   jax.ShapeDtypeStruct((nb, n2, n1, FNET_W), F32)],
        compiler_params=_cparams(("arbitrary", "arbitrary")),
        name="fft_a",
    )(x2, g256, ccm, scm, c1, s1, twc, tws)
    ncol = n1 * FNET_W
    tn = min(2048, ncol)
    out = pl.pallas_call(
        functools.partial(_fftb_kernel, scale=1.0 / math.sqrt(s * FNET_GC)),
        grid=(nb, ncol // tn),
        in_specs=[pl.BlockSpec((None, n2, tn), lambda b, j: (b, 0, j)),
                  pl.BlockSpec((None, n2, tn), lambda b, j: (b, 0, j)),
                  pl.BlockSpec((n2, n2), const2), pl.BlockSpec((n2, n2), const2)],
        out_specs=pl.BlockSpec((None, n2, tn), lambda b, j: (b, 0, j)),
        out_shape=jax.ShapeDtypeStruct((nb, n2, ncol), F32),
        compiler_params=_cparams(("arbitrary", "arbitrary")),
        name="fft_b",
    )(yr.reshape(nb, n2, ncol), yi.reshape(nb, n2, ncol), c2, s2)
    return out.reshape(nb, s, FNET_W)


def _fourier_ctx(f_ctx, g256):
    nb, tc, _ = f_ctx.shape
    ccm, scm = _chan_mats()
    ct, st = _dft_mats(tc)
    const = lambda b: (0, 0)
    return pl.pallas_call(
        functools.partial(_fftc_kernel, scale=1.0 / math.sqrt(tc * FNET_GC)),
        grid=(nb,),
        in_specs=[pl.BlockSpec((None, tc, FNET_W), lambda b: (b, 0, 0)),
                  pl.BlockSpec((1, FNET_W), const),
                  pl.BlockSpec((FNET_W, FNET_W), const), pl.BlockSpec((FNET_W, FNET_W), const),
                  pl.BlockSpec((tc, tc), const), pl.BlockSpec((tc, tc), const)],
        out_specs=pl.BlockSpec((None, tc, FNET_W), lambda b: (b, 0, 0)),
        out_shape=jax.ShapeDtypeStruct((nb, tc, FNET_W), F32),
        compiler_params=_cparams(("arbitrary",)),
        name="fft_ctx",
    )(f_ctx, g256, ccm, scm, ct, st)


def _odout_kernel(yf_ref, yb_ref, xs_ref, z_ref, f_ref, dsk_ref, g_ref, w1_ref, w2_ref, x_ref, mod_ref,
                  lg_ref, lb_ref, o_ref):
    GW = SSD_HPG * SSD_HD
    y = yf_ref[...] + yb_ref[...] + xs_ref[...].astype(F32) * dsk_ref[...]
    y = y * _silu(z_ref[...].astype(F32))
    parts = []
    for g in range(SSD_GROUPS):
        yg = y[:, g * GW:(g + 1) * GW]
        ms = jnp.mean(yg * yg, axis=-1, keepdims=True)
        parts.append(yg * lax.rsqrt(ms + EPS))
    yn = (jnp.concatenate(parts, axis=1) * g_ref[...]).astype(BF16)
    o = jnp.dot(yn, w1_ref[...], preferred_element_type=F32)
    o = o + jnp.dot(f_ref[...].astype(BF16), w2_ref[...], preferred_element_type=F32)
    gate = mod_ref[0, 2:3, :]
    o_ref[...] = _ln(ALPHA * x_ref[...] + gate * o, lg_ref[...], lb_ref[...])


def _odout(yf2, yb2, xbc2, p2, f2, dsk, g768, w1, w2, x2, mod, lg, lb, tps, nb):
    ntok, d = x2.shape
    row = lambda i: (i, 0)
    const = lambda i: (0, 0)
    return pl.pallas_call(
        _odout_kernel,
        grid=(ntok // TM,),
        in_specs=[pl.BlockSpec((TM, SSD_INNER), row), pl.BlockSpec((TM, SSD_INNER), row),
                  pl.BlockSpec((TM, SSD_INNER), row), pl.BlockSpec((TM, SSD_INNER), row),
                  pl.BlockSpec((TM, FNET_W), row),
                  pl.BlockSpec((1, SSD_INNER), const), pl.BlockSpec((1, SSD_INNER), const),
                  pl.BlockSpec((SSD_INNER, d), const), pl.BlockSpec((FNET_W, d), const),
                  pl.BlockSpec((TM, d), row),
                  pl.BlockSpec((1, 6, d), lambda i: (_mod_row(i, tps, nb), 0, 0)),
                  pl.BlockSpec((1, d), const), pl.BlockSpec((1, d), const)],
        out_specs=pl.BlockSpec((TM, d), row),
        out_shape=jax.ShapeDtypeStruct((ntok, d), F32),
        compiler_params=_cparams(("arbitrary",)),
        name="odd_out",
    )(yf2, yb2, xbc2, p2, f2, dsk, g768, w1, w2, x2, mod, lg, lb)


def _router_kernel(x_ref, mod_ref, rw_ref, rb_ref, v_ref, idx_ref, gw_ref, cnt_ref):
    @pl.when(pl.program_id(0) == 0)
    def _():
        cnt_ref[...] = jnp.zeros_like(cnt_ref)

    shift = mod_ref[0, 3:4, :]
    scale = mod_ref[0, 4:5, :]
    v = x_ref[...] * (1.0 + scale) + shift
    v_ref[...] = v.astype(v_ref.dtype)
    logits = lax.dot_general(rw_ref[...], v, (((1,), (1,)), ((), ())), precision=HI,
                             preferred_element_type=F32)
    s = 1.0 / (1.0 + jnp.exp(-logits))
    sel = s + rb_ref[:, 0:1]
    izero = jnp.zeros((1, TM), jnp.int32)
    best = None
    for g in range(N_EXPERT_GROUPS):
        a = [sel[g * EXPERTS_PER_GROUP + j:g * EXPERTS_PER_GROUP + j + 1, :] for j in range(EXPERTS_PER_GROUP)]
        sv = [s[g * EXPERTS_PER_GROUP + j:g * EXPERTS_PER_GROUP + j + 1, :] for j in range(EXPERTS_PER_GROUP)]
        m1, i1, s1 = a[0], izero, sv[0]
        for j in range(1, EXPERTS_PER_GROUP):
            gt = a[j] > m1
            m1 = jnp.where(gt, a[j], m1)
            i1 = jnp.where(gt, j, i1)
            s1 = jnp.where(gt, sv[j], s1)
        m2 = jnp.full((1, TM), -jnp.inf, F32)
        i2, s2 = izero, jnp.zeros((1, TM), F32)
        for j in range(EXPERTS_PER_GROUP):
            gt = jnp.logical_and(i1 != j, a[j] > m2)
            m2 = jnp.where(gt, a[j], m2)
            i2 = jnp.where(gt, j, i2)
            s2 = jnp.where(gt, sv[j], s2)
        cand = (m1 + m2, i1 + g * EXPERTS_PER_GROUP, i2 + g * EXPERTS_PER_GROUP, s1, s2)
        if best is None:
            best = cand
        else:
            gt = cand[0] > best[0]
            best = tuple(jnp.where(gt, cn, bs) for cn, bs in zip(cand, best))
    _, e1, e2, w1, w2 = best
    tot = w1 + w2
    idx_ref[0] = jnp.concatenate([e1, e2], axis=0)
    gw_ref[0] = jnp.concatenate([w1 / tot, w2 / tot], axis=0)
    eio = lax.broadcasted_iota(jnp.int32, (N_EXPERTS, TM), 0)
    oh = jnp.logical_or(eio == e1, eio == e2).astype(F32)
    cnt_ref[...] += _ceil_rows(jnp.sum(oh, axis=1, keepdims=True))


def _router(x2, mod, rw_t, rb, tps, nb):
    ntok, d = x2.shape
    nt = ntok // TM
    return pl.pallas_call(
        _router_kernel,
        grid=(nt,),
        in_specs=[pl.BlockSpec((TM, d), lambda i: (i, 0)),
                  pl.BlockSpec((1, 6, d), lambda i: (_mod_row(i, tps, nb), 0, 0)),
                  pl.BlockSpec((N_EXPERTS, d), lambda i: (0, 0)),
                  pl.BlockSpec((N_EXPERTS, LANES), lambda i: (0, 0))],
        out_specs=[pl.BlockSpec((TM, d), lambda i: (i, 0)),
                   pl.BlockSpec((1, TOP_K, TM), lambda i: (i, 0, 0)),
                   pl.BlockSpec((1, TOP_K, TM), lambda i: (i, 0, 0)),
                   pl.BlockSpec((N_EXPERTS, LANES), lambda i: (0, 0))],
        out_shape=[jax.ShapeDtypeStruct((ntok, d), BF16),
                   jax.ShapeDtypeStruct((nt, TOP_K, TM), jnp.int32),
                   jax.ShapeDtypeStruct((nt, TOP_K, TM), F32),
                   jax.ShapeDtypeStruct((N_EXPERTS, LANES), F32)],
        compiler_params=_cparams(("arbitrary",)),
        name="router",
    )(x2, mod, rw_t, rb)


def _ceil_rows(c):
    return jnp.ceil(c * (1.0 / ROWG)) * ROWG


def _meta_kernel(idx_ref, start_ref, qpos_ref, meta_ref, run_ref):
    @pl.when(pl.program_id(0) == 0)
    def _():
        run_ref[...] = jnp.zeros_like(run_ref)

    e1 = idx_ref[0, 0:1, :]
    e2 = idx_ref[0, 1:2, :]
    eio = lax.broadcasted_iota(jnp.int32, (N_EXPERTS, TM), 0)
    oh1 = eio == e1
    oh2 = eio == e2
    oh = jnp.logical_or(oh1, oh2)
    r = lax.broadcasted_iota(jnp.int32, (TM, TM), 0)
    c = lax.broadcasted_iota(jnp.int32, (TM, TM), 1)
    before = (r < c).astype(BF16)
    rank = jnp.dot(oh.astype(BF16), before, preferred_element_type=F32)
    run_len = jnp.broadcast_to(_ceil_rows(jnp.sum(oh.astype(F32), axis=1, keepdims=True)), (N_EXPERTS, LANES))
    er = lax.broadcasted_iota(jnp.int32, (N_EXPERTS, N_EXPERTS), 0)
    ec = lax.broadcasted_iota(jnp.int32, (N_EXPERTS, N_EXPERTS), 1)
    off = jnp.dot((ec < er).astype(F32), run_len, precision=HI, preferred_element_type=F32)
    pos = rank + off[:, 0:1]
    q1 = jnp.sum(jnp.where(oh1, pos, 0.0), axis=0, keepdims=True)
    q2 = jnp.sum(jnp.where(oh2, pos, 0.0), axis=0, keepdims=True)
    qpos_ref[0] = jnp.concatenate([q1, q2], axis=0).astype(jnp.int32)
    lane = lax.broadcasted_iota(jnp.int32, (N_EXPERTS, LANES), 1)
    meta = jnp.where(lane == 0, start_ref[...] + run_ref[...], jnp.where(lane == 1, off, run_len))
    meta_ref[0] = (meta * (1.0 / ROWG)).astype(jnp.int32)
    run_ref[...] += run_len


def _meta(idx_t, start):
    nt = idx_t.shape[0]
    return pl.pallas_call(
        _meta_kernel,
        grid=(nt,),
        in_specs=[pl.BlockSpec((1, TOP_K, TM), lambda i: (i, 0, 0)),
                  pl.BlockSpec((N_EXPERTS, LANES), lambda i: (0, 0))],
        out_specs=[pl.BlockSpec((1, TOP_K, TM), lambda i: (i, 0, 0)),
                   pl.BlockSpec((1, N_EXPERTS, LANES), lambda i: (i, 0, 0))],
        out_shape=[jax.ShapeDtypeStruct((nt, TOP_K, TM), jnp.int32),
                   jax.ShapeDtypeStruct((nt, N_EXPERTS, LANES), jnp.int32)],
        scratch_shapes=[pltpu.VMEM((N_EXPERTS, LANES), F32)],
        compiler_params=_cparams(("arbitrary",)),
        name="moe_meta",
    )(idx_t, start)


def _sort_select(qpos_ref):
    r = lax.broadcasted_iota(jnp.int32, (SORT_ROWS, TM), 0)
    return r == qpos_ref[0, 0:1, :], r == qpos_ref[0, 1:2, :]


def _run_copies(i, gb_ref, ob_ref, nb_ref, make_copy):
    total = 0
    for e in range(N_EXPERTS):
        n = nb_ref[i * N_EXPERTS + e]
        buf_row = gb_ref[i * N_EXPERTS + e] * ROWG
        tile_row = ob_ref[i * N_EXPERTS + e] * ROWG

        def body(k, carry, buf_row=buf_row, tile_row=tile_row):
            make_copy(pl.multiple_of(tile_row + k * ROWG, ROWG), pl.multiple_of(buf_row + k * ROWG, ROWG)).start()
            return carry

        lax.fori_loop(0, n, body, 0)
        total = total + n
    return total


def _wait_copies(n, make_copy):
    def body(k, carry):
        make_copy(0, 0).wait()
        return carry

    lax.fori_loop(0, n, body, 0)


def _scatter_kernel(gb_ref, ob_ref, nb_ref, ts_ref, tn_ref, nu_ref, v_ref, qpos_ref, gw_ref, buf_ref, xs_ref, z_ref,
                    sem):
    i = pl.program_id(0)
    d = v_ref.shape[1]
    m0, m1 = _sort_select(qpos_ref)
    sel = jnp.logical_or(m0, m1).astype(BF16)
    xs_ref[:, 0:d] = jnp.dot(sel, v_ref[...], preferred_element_type=F32)
    gate = jnp.sum(jnp.where(m0, gw_ref[0, 0:1, :], 0.0) + jnp.where(m1, gw_ref[0, 1:2, :], 0.0),
                   axis=1, keepdims=True)
    xs_ref[:, d:d + LANES] = jnp.broadcast_to(gate, (SORT_ROWS, LANES))

    def copy(tile_row, buf_row):
        return pltpu.make_async_copy(xs_ref.at[pl.ds(tile_row, ROWG)], buf_ref.at[pl.ds(buf_row, ROWG)], sem)

    _wait_copies(_run_copies(i, gb_ref, ob_ref, nb_ref, copy), copy)

    @pl.when(i == pl.num_programs(0) - 1)
    def _():
        z_ref[...] = jnp.zeros_like(z_ref)

        def zcopy(buf_row):
            return pltpu.make_async_copy(z_ref.at[pl.ds(0, ROWG)], buf_ref.at[pl.ds(buf_row, ROWG)], sem)

        total = 0
        for e in range(N_EXPERTS):
            n = tn_ref[e]
            row0 = ts_ref[e] * ROWG

            def body(k, carry, row0=row0):
                zcopy(pl.multiple_of(row0 + k * ROWG, ROWG)).start()
                return carry

            lax.fori_loop(0, n, body, 0)
            total = total + n

        def wbody(k, carry):
            zcopy(0).wait()
            return carry

        lax.fori_loop(0, total, wbody, 0)

        def zblock(blk):
            return pltpu.make_async_copy(z_ref, buf_ref.at[pl.ds(pl.multiple_of(blk * TM, TM), TM)], sem)

        def bbody(blk, carry):
            zblock(blk).start()
            return carry

        def bwait(blk, carry):
            zblock(blk).wait()
            return carry

        nblk = buf_ref.shape[0] // TM
        lax.fori_loop(nu_ref[0], nblk, bbody, 0)
        lax.fori_loop(nu_ref[0], nblk, bwait, 0)


def _scatter(gb, ob, nb8, tail_s, tail_n, n_used, v, qpos, gw_t, nrow):
    ntok, d = v.shape
    grid_spec = pltpu.PrefetchScalarGridSpec(
        num_scalar_prefetch=6,
        grid=(ntok // TM,),
        in_specs=[pl.BlockSpec((TM, d), lambda i, *_: (i, 0)),
                  pl.BlockSpec((1, TOP_K, TM), lambda i, *_: (i, 0, 0)),
                  pl.BlockSpec((1, TOP_K, TM), lambda i, *_: (i, 0, 0))],
        out_specs=pl.BlockSpec(memory_space=pl.ANY),
        scratch_shapes=[pltpu.VMEM((SORT_ROWS, d + LANES), F32), pltpu.VMEM((TM, d + LANES), F32),
                        pltpu.SemaphoreType.DMA(())],
    )
    return pl.pallas_call(
        _scatter_kernel,
        grid_spec=grid_spec,
        out_shape=jax.ShapeDtypeStruct((nrow, d + LANES), F32),
        compiler_params=_cparams(("arbitrary",)),
        name="moe_scatter",
    )(gb, ob, nb8, tail_s, tail_n, n_used, v, qpos, gw_t)


def _moe_kernel(be_ref, nu_ref, x_ref, wg_ref, wu_ref, wd_ref, o_ref, wg_bf, wu_bf, wd_bf):
    i = pl.program_id(0)
    d = o_ref.shape[1]

    @pl.when(jnp.logical_or(i == 0, be_ref[i] != be_ref[jnp.maximum(i - 1, 0)]))
    def _():
        wg_bf[...] = wg_ref[...].astype(BF16)
        wu_bf[...] = wu_ref[...].astype(BF16)
        wd_bf[...] = wd_ref[...].astype(BF16)

    @pl.when(i < nu_ref[0])
    def _():
        xb = x_ref[...]
        x = xb[:, 0:d].astype(BF16)
        g = jnp.dot(x, wg_bf[...], preferred_element_type=F32)
        u = jnp.dot(x, wu_bf[...], preferred_element_type=F32)
        h = (_silu(g) * u).astype(BF16)
        y = jnp.dot(h, wd_bf[...], preferred_element_type=F32)
        o_ref[...] = y * xb[:, d:d + 1]

    @pl.when(pl.program_id(0) >= nu_ref[0])
    def _():
        o_ref[...] = jnp.zeros_like(o_ref)


def _moe_ffn(block_e, n_used, buf, layer, wg, wu, wd):
    nrow, dw = buf.shape
    d = dw - LANES
    nblk = nrow // TM
    de = wg.shape[3]
    used = lambda i, nu: jnp.minimum(i, nu[0] - 1)
    grid_spec = pltpu.PrefetchScalarGridSpec(
        num_scalar_prefetch=2,
        grid=(nblk,),
        in_specs=[pl.BlockSpec((TM, dw), lambda i, be, nu: (used(i, nu), 0)),
                  pl.BlockSpec((None, None, d, de), lambda i, be, nu: (layer, be[i], 0, 0)),
                  pl.BlockSpec((None, None, d, de), lambda i, be, nu: (layer, be[i], 0, 0)),
                  pl.BlockSpec((None, None, de, d), lambda i, be, nu: (layer, be[i], 0, 0))],
        out_specs=pl.BlockSpec((TM, d), lambda i, be, nu: (i, 0)),
        scratch_shapes=[pltpu.VMEM((d, de), BF16), pltpu.VMEM((d, de), BF16), pltpu.VMEM((de, d), BF16)],
    )
    return pl.pallas_call(
        _moe_kernel,
        grid_spec=grid_spec,
        out_shape=jax.ShapeDtypeStruct((nrow, d), F32),
        compiler_params=_cparams(("arbitrary",)),
        name="moe_ffn",
    )(block_e, n_used, buf, wg, wu, wd)


def _ln2_kernel(gb_ref, ob_ref, nb_ref, x_ref, qpos_ref, mod_ref, lg_ref, lb_ref, y_ref, o_ref, ys_ref, sem):
    i = pl.program_id(0)

    @pl.when(i == 0)
    def _():
        ys_ref[...] = jnp.zeros_like(ys_ref)

    def copy(tile_row, buf_row):
        return pltpu.make_async_copy(y_ref.at[pl.ds(buf_row, ROWG)], ys_ref.at[pl.ds(tile_row, ROWG)], sem)

    _wait_copies(_run_copies(i, gb_ref, ob_ref, nb_ref, copy), copy)
    m0, m1 = _sort_select(qpos_ref)
    sel = jnp.logical_or(m0, m1).astype(BF16)
    y = lax.dot_general(sel, ys_ref[...].astype(BF16), (((0,), (0,)), ((), ())), preferred_element_type=F32)
    gate = mod_ref[0, 5:6, :]
    o_ref[...] = _ln(ALPHA * x_ref[...] + gate * y, lg_ref[...], lb_ref[...])


def _ln2(gb, ob, nb8, x2, qpos, mod, lg, lb, ybuf, tps, nb):
    ntok, d = x2.shape
    grid_spec = pltpu.PrefetchScalarGridSpec(
        num_scalar_prefetch=3,
        grid=(ntok // TM,),
        in_specs=[pl.BlockSpec((TM, d), lambda i, *_: (i, 0)),
                  pl.BlockSpec((1, TOP_K, TM), lambda i, *_: (i, 0, 0)),
                  pl.BlockSpec((1, 6, d), lambda i, *_: (_mod_row(i, tps, nb), 0, 0)),
                  pl.BlockSpec((1, d), lambda i, *_: (0, 0)),
                  pl.BlockSpec((1, d), lambda i, *_: (0, 0)),
                  pl.BlockSpec(memory_space=pl.ANY)],
        out_specs=pl.BlockSpec((TM, d), lambda i, *_: (i, 0)),
        scratch_shapes=[pltpu.VMEM((SORT_ROWS, d), F32), pltpu.SemaphoreType.DMA(())],
    )
    return pl.pallas_call(
        _ln2_kernel,
        grid_spec=grid_spec,
        out_shape=jax.ShapeDtypeStruct((ntok, d), F32),
        compiler_params=_cparams(("arbitrary",)),
        name="post_moe_ln",
    )(gb, ob, nb8, x2, qpos, mod, lg, lb, ybuf)


def _even_w_in(w):
    o = np.cumsum((0, GLA_QK, GLA_QK, GLA_V, GLA_V, GLA_RANK, GLA_RANK, GQA_Q, GQA_KV, GQA_KV))
    q, k, v, r = w[:, o[0]:o[1]], w[:, o[1]:o[2]], w[:, o[2]:o[3]], w[:, o[3]:o[4]]
    lr = w[:, o[4]:o[6]]
    gq, gk, gv = w[:, o[6]:o[7]], w[:, o[7]:o[8]], w[:, o[8]:o[9]]
    d = w.shape[0]
    zeros = lambda n: jnp.zeros((d, n), w.dtype)
    out = jnp.concatenate([gq, gk, gv, q, k, lr, zeros(LANES - 2 * GLA_RANK), zeros(LANES), v, r], axis=1)
    assert out.shape[1] == EVEN_W
    return out.astype(BF16)


def _odd_w_in(w):
    o = np.cumsum((0, SSD_INNER, SSD_INNER, SSD_BC, SSD_BC, SSD_HEADS, SSD_HEADS, FNET_W))
    d = w.shape[0]
    out = jnp.concatenate([w[:, :o[4]], w[:, o[4]:o[6]], jnp.zeros((d, LANES - 2 * SSD_HEADS), w.dtype),
                           w[:, o[6]:o[7]]], axis=1)
    assert out.shape[1] == ODD_W
    return out.astype(BF16)


def _rope_tables(ctx_len, seq):
    rows = seq // GRID_W
    r = jnp.repeat(jnp.arange(rows, dtype=F32), GRID_W)
    col = jnp.tile(jnp.arange(GRID_W, dtype=F32), rows)
    half = GQA_HD // 2
    inv = ROPE_THETA ** (-jnp.arange(0, half, 2, dtype=F32) / half)
    ar = r[:, None] * inv
    ac = col[:, None] * inv
    ang = jnp.concatenate([ar, ar, ac, ac], -1)
    cos = jnp.concatenate([jnp.ones((ctx_len, GQA_HD), F32), jnp.cos(ang)], 0)
    sin = jnp.concatenate([jnp.zeros((ctx_len, GQA_HD), F32), jnp.sin(ang)], 0)
    return jnp.tile(cos, (1, 2)), jnp.tile(sin, (1, 2))


def _even_layer(x2, mod, nb, t, tps, ctx_len, w_in, w_o, w_dec, b_dec, gla_g, qn_g, kn_g, cos2, sin2, lg, lb):
    p2 = _inproj(x2, mod, _even_w_in(w_in), tps, nb)
    p3 = p2.reshape(nb, t, EVEN_W)
    outs = []
    for di in range(2):
        wd_pad = jnp.zeros((LANES, GLA_QK), F32).at[di * GLA_RANK:(di + 1) * GLA_RANK].set(w_dec[di])
        outs.append(_gla_scan(p3, wd_pad, b_dec[di][None, :], reverse=(di == 1)))
    qh2, kg2, vg2 = _qkprep(p2, cos2, sin2, jnp.tile(qn_g, 2)[None, :], jnp.tile(kn_g, 2)[None, :], tps)
    att = _attention(qh2.reshape(nb, t, GQA_HEADS * LANES), kg2.reshape(nb, t, GQA_KV_HEADS * LANES),
                     vg2.reshape(nb, t, GQA_KV_HEADS * LANES), ctx_len)
    w1, w2 = w_o[:GLA_V].astype(BF16), w_o[GLA_V:].astype(BF16)
    ntok = nb * t
    return _evout(outs[0].reshape(ntok, GLA_V), outs[1].reshape(ntok, GLA_V), p2, att.reshape(ntok, GQA_Q),
                  jnp.tile(gla_g, GLA_HEADS)[None, :], w1, w2, x2, mod, lg, lb, tps, nb)


def _odd_layer(x2, mod, nb, t, tps, ctx_len, w_in, w_o, conv_w, conv_b, dt_bias, a_log, d_skip, ssd_g, fnet_g,
               lg, lb):
    p2 = _inproj(x2, mod, _odd_w_in(w_in), tps, nb)
    p3 = p2.reshape(nb, t, ODD_W)
    w8 = jnp.zeros((8, SSD_CONV_CH), F32).at[:SSD_CONV].set(conv_w)
    xbc2 = _conv(p2, w8, conv_b[None, :], tps)
    xbc3 = xbc2.reshape(nb, t, SSD_CONV_CH)
    ys = []
    heads = jnp.arange(SSD_HEADS)
    for di in range(2):
        lanes = di * SSD_HEADS + heads
        bias_pad = jnp.zeros((1, LANES), F32).at[0, lanes].set(dt_bias[di])
        a_pad = jnp.zeros((1, LANES), F32).at[0, lanes].set(-jnp.exp(a_log[di]))
        eexp = jnp.zeros((LANES, SSD_INNER), F32).at[jnp.repeat(lanes, SSD_HD), jnp.arange(SSD_INNER)].set(1.0)
        esel = jnp.zeros((16, LANES), F32).at[heads, lanes].set(1.0)
        ys.append(_ssd_scan(xbc3, p3, bias_pad, a_pad, eexp, esel, reverse=(di == 1)))
    f3 = p3[:, :, ODD_W - FNET_W:]
    g256 = fnet_g[None, :]
    fmix = jnp.concatenate([_fourier_ctx(f3[:, :ctx_len], g256), _fourier_latent(f3[:, ctx_len:], g256)], axis=1)
    ntok = nb * t
    return _odout(ys[0].reshape(ntok, SSD_INNER), ys[1].reshape(ntok, SSD_INNER), xbc2, p2,
                  fmix.reshape(ntok, FNET_W), jnp.repeat(d_skip, SSD_HD)[None, :], ssd_g[None, :],
                  w_o[:SSD_INNER].astype(BF16), w_o[SSD_INNER:].astype(BF16), x2, mod, lg, lb, tps, nb)


def _moe_layer(x2, mod, nb, tps, rw_t, rb, layer, wg, wu, wd, lg, lb):
    ntok, d = x2.shape
    v, idx_t, gw_t, cnt = _router(x2, mod, rw_t, rb, tps, nb)
    nt = ntok // TM
    rows = cnt[:, 0].astype(jnp.int32)
    padded = (rows + TM - 1) // TM * TM
    pad_end = jnp.cumsum(padded)
    pad_start = pad_end - padded
    nblk = -(-(ntok * TOP_K + nt * N_EXPERTS * (ROWG - 1)) // TM) + N_EXPERTS
    block_e = jnp.minimum(jnp.searchsorted(pad_end, jnp.arange(nblk, dtype=jnp.int32) * TM, side="right"),
                          N_EXPERTS - 1).astype(jnp.int32)
    n_used = (pad_end[-1:] // TM).astype(jnp.int32)
    start = jnp.broadcast_to(pad_start.astype(F32)[:, None], (N_EXPERTS, LANES))
    qpos, meta = _meta(idx_t, start)
    gb, ob, nb8 = (meta[:, :, j].reshape(nt * N_EXPERTS) for j in range(3))
    tail_s = ((pad_start + rows) // ROWG).astype(jnp.int32)
    tail_n = ((padded - rows) // ROWG).astype(jnp.int32)
    buf = _scatter(gb, ob, nb8, tail_s, tail_n, n_used, v, qpos, gw_t, nblk * TM)
    y = _moe_ffn(block_e, n_used, buf, layer, wg, wu, wd)
    return _ln2(gb, ob, nb8, x2, qpos, mod, lg, lb, y, tps, nb)


def kernel(x, c, ctx, c_ctx, ada_w, ada_b, ln_g, ln_b, ev_w_in, ev_w_o, gla_w_decay, gla_b_decay, gla_norm_g,
           gqa_q_norm_g, gqa_k_norm_g, od_w_in, od_w_o, ssd_conv_w, ssd_conv_b, ssd_dt_bias, ssd_a_log, ssd_d,
           ssd_norm_g, fnet_norm_g, router_w, router_b, exp_w_gate, exp_w_up, exp_w_down):
    nb, seq, d = x.shape
    ctx_len = ctx.shape[1]
    assert ctx_len == TM and seq % TM == 0 and d == D_MODEL and nb <= 7
    t = ctx_len + seq
    tps = t // TM
    ntok = nb * t
    cc = jnp.zeros((8, d), F32).at[:nb].set(c).at[nb].set(c_ctx)
    mod_all = _ada_all(cc, ada_w, ada_b).reshape(DEPTH, 8, 6, d)
    cos2, sin2 = _rope_tables(ctx_len, seq)
    rw_t = router_w.T
    rb = jnp.broadcast_to(router_b[:, None], (N_EXPERTS, LANES))
    x2 = jnp.concatenate([ctx, x], axis=1).reshape(ntok, d)
    for layer in range(DEPTH):
        mod = mod_all[layer]
        i = layer // 2
        lg0, lb0 = ln_g[layer, 0][None, :], ln_b[layer, 0][None, :]
        lg1, lb1 = ln_g[layer, 1][None, :], ln_b[layer, 1][None, :]
        if layer % 2 == 0:
            x2 = _even_layer(x2, mod, nb, t, tps, ctx_len, ev_w_in[i], ev_w_o[i], gla_w_decay[i], gla_b_decay[i],
                             gla_norm_g[i], gqa_q_norm_g[i], gqa_k_norm_g[i], cos2, sin2, lg0, lb0)
        else:
            x2 = _odd_layer(x2, mod, nb, t, tps, ctx_len, od_w_in[i], od_w_o[i], ssd_conv_w[i], ssd_conv_b[i],
                            ssd_dt_bias[i], ssd_a_log[i], ssd_d[i], ssd_norm_g[i], fnet_norm_g[i], lg0, lb0)
        x2 = _moe_layer(x2, mod, nb, tps, rw_t, rb, layer, exp_w_gate, exp_w_up, exp_w_down, lg1, lb1)
    return x2.reshape(nb, t, d)[:, ctx_len:]
```

```python
import functools
import math

import jax
import jax.numpy as jnp
import numpy as np
from jax import lax
from jax.experimental import pallas as pl
from jax.experimental.pallas import tpu as pltpu

F32 = jnp.float32
BF16 = jnp.bfloat16
HI = lax.Precision.HIGHEST

D_MODEL = 1024
DEPTH = 4
GRID_W = 64
GLA_HEADS, GLA_DK, GLA_DV, GLA_RANK, GLA_TAU, GLA_CHUNK = 4, 64, 128, 16, 16.0, 64
GQA_HEADS, GQA_KV_HEADS, GQA_HD = 8, 2, 64
ROPE_THETA = 10000.0
SSD_HEADS, SSD_HD, SSD_GROUPS, SSD_STATE, SSD_CONV, SSD_CHUNK = 12, 64, 2, 128, 5, 64
SSD_HPG = SSD_HEADS // SSD_GROUPS
SSD_INNER = SSD_HEADS * SSD_HD
SSD_BC = SSD_GROUPS * SSD_STATE
SSD_CONV_CH = SSD_INNER + 2 * SSD_BC
FNET_GROUPS, FNET_GC = 4, 64
FNET_W = FNET_GROUPS * FNET_GC
N_EXPERTS, N_EXPERT_GROUPS, TOP_K, D_EXPERT = 16, 4, 2, 768
EXPERTS_PER_GROUP = N_EXPERTS // N_EXPERT_GROUPS
GLA_QK = GLA_HEADS * GLA_DK
GLA_V = GLA_HEADS * GLA_DV
GQA_Q = GQA_HEADS * GQA_HD
GQA_KV = GQA_KV_HEADS * GQA_HD
EPS = 1e-6
ALPHA = (2.0 * DEPTH) ** 0.25

LANES = 128
TM = 256
EVEN_W = 2560
ODD_W = 2432
VMEM_LIMIT = 48 * 1024 * 1024
LOG2E = 1.4426950408889634
ATT_TQ = 256
ATT_HEADROOM = 64.0
ATT_MIN_DENOM = 2.0 ** -60
ROWG = 8
SORT_ROWS = TOP_K * TM + N_EXPERTS * (ROWG - 1)
FFN_ROWS = 512
FFN_VMEM_LIMIT = 56 * 1024 * 1024


def _cparams(sem):
    return pltpu.CompilerParams(dimension_semantics=sem, vmem_limit_bytes=VMEM_LIMIT)


def _silu(x):
    return x * (1.0 / (1.0 + jnp.exp(-x)))


def _softplus(x):
    return jnp.maximum(x, 0.0) + jnp.log1p(jnp.exp(-jnp.abs(x)))


def _mod_row(i, tiles_per_seq, n_batch):
    return jnp.where(i % tiles_per_seq == 0, n_batch, i // tiles_per_seq)


def _ada_kernel(c_ref, w_ref, b_ref, o_ref):
    s = _silu(c_ref[...])
    o_ref[0] = jnp.dot(s, w_ref[0], precision=HI, preferred_element_type=F32) + b_ref[0]


def _ada_all(cc, ada_w, ada_b):
    depth, d, n = ada_w.shape
    tn = 512
    return pl.pallas_call(
        _ada_kernel,
        grid=(depth, n // tn),
        in_specs=[pl.BlockSpec((8, d), lambda l, j: (0, 0)),
                  pl.BlockSpec((1, d, tn), lambda l, j: (l, 0, j)),
                  pl.BlockSpec((1, 1, tn), lambda l, j: (l, 0, j))],
        out_specs=pl.BlockSpec((1, 8, tn), lambda l, j: (l, 0, j)),
        out_shape=jax.ShapeDtypeStruct((depth, 8, n), F32),
        compiler_params=_cparams(("arbitrary", "arbitrary")),
        name="adaln",
    )(cc, ada_w, ada_b.reshape(depth, 1, n))


def _inproj_kernel(x_ref, mod_ref, w_ref, o_ref):
    shift = mod_ref[0, 0:1, :]
    scale = mod_ref[0, 1:2, :]
    u = (x_ref[...] * (1.0 + scale) + shift).astype(BF16)
    o_ref[...] = jnp.dot(u, w_ref[...], preferred_element_type=F32).astype(o_ref.dtype)


def _inproj(x2, mod, w, tps, nb):
    ntok, d = x2.shape
    nw = w.shape[1]
    return pl.pallas_call(
        _inproj_kernel,
        grid=(ntok // TM,),
        in_specs=[pl.BlockSpec((TM, d), lambda i: (i, 0)),
                  pl.BlockSpec((1, 6, d), lambda i: (_mod_row(i, tps, nb), 0, 0)),
                  pl.BlockSpec((d, nw), lambda i: (0, 0))],
        out_specs=pl.BlockSpec((TM, nw), lambda i: (i, 0)),
        out_shape=jax.ShapeDtypeStruct((ntok, nw), BF16),
        compiler_params=_cparams(("arbitrary",)),
        name="inproj",
    )(x2, mod, w)


def _scan_block(i, nblk, reverse):
    if not reverse:
        return i
    return jnp.where(i == 0, 0, nblk - i)


def _tri(n, reverse):
    r = lax.broadcasted_iota(jnp.int32, (n, n), 0)
    c = lax.broadcasted_iota(jnp.int32, (n, n), 1)
    return (c >= r) if reverse else (c <= r)


def _chunk_sum_mat(chunk, reverse):
    l = np.arange(TM)[:, None]
    m = np.arange(TM)[None, :]
    same = (l // chunk) == (m // chunk)
    return (same & ((m >= l) if reverse else (m <= l))).astype(np.float32)


def _split3(x):
    hi = x.astype(BF16)
    r1 = x - hi.astype(F32)
    mid = r1.astype(BF16)
    return hi, mid, (r1 - mid.astype(F32)).astype(BF16)


def _chunk_rows(x, chunk, idx):
    return jnp.concatenate([jnp.broadcast_to(x[c * chunk + idx:c * chunk + idx + 1, :], (chunk, x.shape[1]))
                            for c in range(x.shape[0] // chunk)], axis=0)


def _gla_kernel(q_ref, k_ref, v_ref, lr_ref, wd_ref, bd_ref, cm_ref, o_ref, s_ref, *, reverse):
    L = GLA_CHUNK
    nchunk = TM // L
    nt = (((1,), (1,)), ((), ()))
    tn = (((0,), (0,)), ((), ()))
    end = 0 if reverse else L - 1
    mid = L // 2 if reverse else L // 2 - 1

    @pl.when(pl.program_id(2) == 0)
    def _():
        s_ref[...] = jnp.zeros_like(s_ref)

    q = q_ref[...].astype(F32) * (GLA_DK ** -0.5)
    k = k_ref[...].astype(F32)
    v = v_ref[...]
    lr = lr_ref[...]
    z = bd_ref[...] + sum(jnp.dot(lr, wd_ref[j], preferred_element_type=F32) for j in range(3))
    la = -_softplus(-z) * (1.0 / GLA_TAU)
    cm = cm_ref[...]
    b = sum(jnp.dot(cm, part, preferred_element_type=F32) for part in _split3(la))
    bmid = _chunk_rows(b, L, mid)
    bend = _chunk_rows(b, L, end)
    qs = q * jnp.exp(b - bmid)
    ks = (k * jnp.exp(bmid - b)).astype(BF16)
    qi = q * jnp.exp(b)
    kend = (k * jnp.exp(bend - b)).astype(BF16)
    r = lax.broadcasted_iota(jnp.int32, (TM, TM), 0)
    c = lax.broadcasted_iota(jnp.int32, (TM, TM), 1)
    mask = jnp.logical_and(r // L == c // L, (c >= r) if reverse else (c <= r))
    lane = lax.broadcasted_iota(jnp.int32, (TM, LANES), 1)
    heads = (lane < GLA_DK, lane >= GLA_DK)
    intra = []
    for h in range(2):
        att = lax.dot_general(jnp.where(heads[h], qs, 0.0).astype(BF16), ks, nt, preferred_element_type=F32)
        att = jnp.where(mask, att, 0.0).astype(BF16)
        intra.append(jnp.dot(att, v[:, h * GLA_DV:(h + 1) * GLA_DV], preferred_element_type=F32))
    qi_h = [jnp.where(heads[h], qi, 0.0).astype(BF16) for h in range(2)]
    tot = jnp.concatenate([b[ch * L + end:ch * L + end + 1, :] for ch in range(nchunk)]
                          + [jnp.zeros((ROWG - nchunk, LANES), F32)], axis=0).T
    ds, dec = [], []
    for ch in range(nchunk):
        rows = slice(ch * L, (ch + 1) * L)
        ds.append(lax.dot_general(kend[rows], v[rows], tn, preferred_element_type=F32))
        dec.append(jnp.exp(jnp.broadcast_to(tot[:, ch:ch + 1], (2 * GLA_DK, 2 * GLA_DV))))
    s = s_ref[...]
    inter = [None] * nchunk
    for ch in (range(nchunk - 1, -1, -1) if reverse else range(nchunk)):
        rows = slice(ch * L, (ch + 1) * L)
        s_bf = s.astype(BF16)
        inter[ch] = jnp.concatenate(
            [jnp.dot(qi_h[h][rows], s_bf[:, h * GLA_DV:(h + 1) * GLA_DV], preferred_element_type=F32)
             for h in range(2)], axis=1)
        s = dec[ch] * s + ds[ch]
    s_ref[...] = s
    o_ref[...] = jnp.concatenate(intra, axis=1) + jnp.concatenate(inter, axis=0)


def _gla_scan(p3, wd_pad, bd, reverse):
    nb, t, _ = p3.shape
    nblk = t // TM
    blk = lambda i: _scan_block(i, nblk, reverse)
    cm = jnp.asarray(_chunk_sum_mat(GLA_CHUNK, reverse), BF16)
    wd3 = jnp.stack(_split3(wd_pad))
    return pl.pallas_call(
        functools.partial(_gla_kernel, reverse=reverse),
        grid=(nb, 2, nblk),
        in_specs=[pl.BlockSpec((None, TM, LANES), lambda b, p, i: (b, blk(i), 6 + p)),
                  pl.BlockSpec((None, TM, LANES), lambda b, p, i: (b, blk(i), 8 + p)),
                  pl.BlockSpec((None, TM, 2 * GLA_DV), lambda b, p, i: (b, blk(i), 6 + p)),
                  pl.BlockSpec((None, TM, LANES), lambda b, p, i: (b, blk(i), 10)),
                  pl.BlockSpec((3, LANES, LANES), lambda b, p, i: (0, 0, p)),
                  pl.BlockSpec((1, LANES), lambda b, p, i: (0, p)),
                  pl.BlockSpec((TM, TM), lambda b, p, i: (0, 0))],
        out_specs=pl.BlockSpec((None, TM, 2 * GLA_DV), lambda b, p, i: (b, blk(i), p)),
        out_shape=jax.ShapeDtypeStruct((nb, t, GLA_V), F32),
        scratch_shapes=[pltpu.VMEM((2 * GLA_DK, 2 * GLA_DV), F32)],
        compiler_params=_cparams(("arbitrary", "arbitrary", "arbitrary")),
        name="gla_bwd" if reverse else "gla_fwd",
    )(p3, p3, p3, p3, wd3, bd, cm)


def _seg_ones(width, seg):
    r = lax.broadcasted_iota(jnp.int32, (width, width), 0) // seg
    c = lax.broadcasted_iota(jnp.int32, (width, width), 1) // seg
    return jnp.where(r == c, 1.0 / seg, 0.0).astype(F32)


def _norm_rope(x, g, cos, sin, bd):
    ms = jnp.dot(x * x, bd, precision=HI, preferred_element_type=F32)
    xn = x * lax.rsqrt(ms + EPS) * g
    lane = lax.broadcasted_iota(jnp.int32, xn.shape, 1)
    quarter = GQA_HD // 4
    up = pltpu.roll(xn, LANES - quarter, 1)
    dn = pltpu.roll(xn, quarter, 1)
    rot = jnp.where(lane % (2 * quarter) < quarter, -up, dn)
    return xn * cos + rot * sin


def _qkprep_kernel(q_ref, k_ref, v_ref, cos_ref, sin_ref, gq_ref, gk_ref, qo_ref, ko_ref, vo_ref):
    bd = _seg_ones(LANES, GQA_HD)
    cos = cos_ref[...]
    sin = sin_ref[...]
    lane = lax.broadcasted_iota(jnp.int32, (TM, LANES), 1)
    low = lane < GQA_HD
    extra = lane == GQA_HD
    one_hot = jnp.where(extra, 1.0, 0.0)
    k_bound = math.sqrt(GQA_HD) * jnp.max(jnp.abs(gk_ref[...]), axis=-1, keepdims=True) * 1.02
    for j in range(GQA_Q // LANES):
        x = q_ref[:, j * LANES:(j + 1) * LANES].astype(F32)
        y = _norm_rope(x, gq_ref[...], cos, sin, bd) * (GQA_HD ** -0.5 * LOG2E)
        y = y.astype(BF16).astype(F32)
        norm = jnp.sqrt(jnp.dot(y * y, bd, precision=HI, preferred_element_type=F32) * GQA_HD)
        shift = ATT_HEADROOM - norm * k_bound
        lo = jnp.where(low, y, jnp.where(extra, pltpu.roll(shift, GQA_HD, 1), 0.0))
        hi = jnp.where(low, pltpu.roll(y, GQA_HD, 1), jnp.where(extra, shift, 0.0))
        qo_ref[:, (2 * j) * LANES:(2 * j + 1) * LANES] = lo.astype(BF16)
        qo_ref[:, (2 * j + 1) * LANES:(2 * j + 2) * LANES] = hi.astype(BF16)
    k = _norm_rope(k_ref[...].astype(F32), gk_ref[...], cos, sin, bd)
    v = v_ref[...].astype(F32)
    for src, dst in ((k, ko_ref), (v, vo_ref)):
        dst[:, 0:LANES] = jnp.where(low, src, one_hot).astype(BF16)
        dst[:, LANES:2 * LANES] = jnp.where(low, pltpu.roll(src, GQA_HD, 1), one_hot).astype(BF16)


def _qkprep(p2, cos2, sin2, gq2, gk2, tps):
    ntok = p2.shape[0]
    return pl.pallas_call(
        _qkprep_kernel,
        grid=(ntok // TM,),
        in_specs=[pl.BlockSpec((TM, GQA_Q), lambda i: (i, 0)),
                  pl.BlockSpec((TM, LANES), lambda i: (i, 4)),
                  pl.BlockSpec((TM, LANES), lambda i: (i, 5)),
                  pl.BlockSpec((TM, LANES), lambda i: (i % tps, 0)),
                  pl.BlockSpec((TM, LANES), lambda i: (i % tps, 0)),
                  pl.BlockSpec((1, LANES), lambda i: (0, 0)),
                  pl.BlockSpec((1, LANES), lambda i: (0, 0))],
        out_specs=[pl.BlockSpec((TM, GQA_HEADS * LANES), lambda i: (i, 0)),
                   pl.BlockSpec((TM, GQA_KV_HEADS * LANES), lambda i: (i, 0)),
                   pl.BlockSpec((TM, GQA_KV_HEADS * LANES), lambda i: (i, 0))],
        out_shape=[jax.ShapeDtypeStruct((ntok, GQA_HEADS * LANES), BF16),
                   jax.ShapeDtypeStruct((ntok, GQA_KV_HEADS * LANES), BF16),
                   jax.ShapeDtypeStruct((ntok, GQA_KV_HEADS * LANES), BF16)],
        compiler_params=_cparams(("arbitrary",)),
        name="qkprep",
    )(p2, p2, p2, cos2, sin2, gq2, gk2)


def _attn_kernel(q_ref, k_ref, v_ref, o_ref, *, ctx_len, n_lat_chunks, ck, n_blocks):
    i = pl.program_id(2)
    rep = GQA_HEADS // GQA_KV_HEADS
    is_ctx = i < ctx_len // ATT_TQ
    q4 = jnp.concatenate([q_ref[:, r * LANES:(r + 1) * LANES] for r in range(rep)], axis=0)
    nt = (((1,), (1,)), ((), ()))

    def chunk(acc, kc, vc):
        s = lax.dot_general(q4, kc, nt, preferred_element_type=F32)
        return acc + jnp.dot(jnp.exp2(s).astype(BF16), vc, preferred_element_type=F32)

    def store(acc):
        out = acc * (1.0 / acc[:, GQA_HD:GQA_HD + 1])
        low = lax.broadcasted_iota(jnp.int32, (ATT_TQ, LANES), 1) < GQA_HD
        hs = [out[r * ATT_TQ:(r + 1) * ATT_TQ] for r in range(rep)]
        for u in range(rep // 2):
            o_ref[:, u * LANES:(u + 1) * LANES] = jnp.where(
                low, hs[2 * u], pltpu.roll(hs[2 * u + 1], GQA_HD, 1)).astype(o_ref.dtype)

    acc = chunk(jnp.zeros((rep * ATT_TQ, LANES), F32), k_ref[0:ctx_len, :], v_ref[0:ctx_len, :])

    def body(c, acc):
        off = pl.multiple_of(ctx_len + c * ck, TM)
        return chunk(acc, k_ref[pl.ds(off, ck), :], v_ref[pl.ds(off, ck), :])

    acc = lax.fori_loop(0, jnp.where(is_ctx, 0, n_lat_chunks), body, acc)
    healthy = jnp.min(acc[:, GQA_HD:GQA_HD + 1]) >= ATT_MIN_DENOM

    @pl.when(healthy)
    def _():
        store(acc)

    @pl.when(jnp.logical_not(healthy))
    def _():
        def online(c, carry):
            m, acc = carry
            off = pl.multiple_of(c * TM, TM)
            s = lax.dot_general(q4, k_ref[pl.ds(off, TM), :], nt, preferred_element_type=F32)
            mn = jnp.maximum(m, jnp.max(s, axis=-1, keepdims=True))
            p = jnp.exp2(s - mn).astype(BF16)
            acc = jnp.exp2(m - mn) * acc + jnp.dot(p, v_ref[pl.ds(off, TM), :], preferred_element_type=F32)
            return mn, acc

        init = (jnp.full((rep * ATT_TQ, 1), -1e30, F32), jnp.zeros((rep * ATT_TQ, LANES), F32))
        _, acc2 = lax.fori_loop(0, jnp.where(is_ctx, ctx_len // TM, n_blocks), online, init)
        store(acc2)


def _attention(qh3, kg3, vg3, ctx_len):
    nb, t, _ = qh3.shape
    seq = t - ctx_len
    ck = next(c for c in (2048, 1024, 512, 256) if seq % c == 0)
    rep = GQA_HEADS // GQA_KV_HEADS
    return pl.pallas_call(
        functools.partial(_attn_kernel, ctx_len=ctx_len, n_lat_chunks=seq // ck, ck=ck, n_blocks=t // TM),
        grid=(nb, GQA_KV_HEADS, t // ATT_TQ),
        in_specs=[pl.BlockSpec((None, ATT_TQ, rep * LANES), lambda b, g, i: (b, i, g)),
                  pl.BlockSpec((None, t, LANES), lambda b, g, i: (b, 0, g)),
                  pl.BlockSpec((None, t, LANES), lambda b, g, i: (b, 0, g))],
        out_specs=pl.BlockSpec((None, ATT_TQ, rep * GQA_HD), lambda b, g, i: (b, i, g)),
        out_shape=jax.ShapeDtypeStruct((nb, t, GQA_Q), BF16),
        compiler_params=_cparams(("arbitrary", "arbitrary", "arbitrary")),
        name="gqa_attn",
    )(qh3, kg3, vg3)


def _ln(x, g, b):
    mu = jnp.mean(x, axis=-1, keepdims=True)
    xc = x - mu
    var = jnp.mean(xc * xc, axis=-1, keepdims=True)
    return xc * lax.rsqrt(var + EPS) * g + b


def _evout_kernel(of_ref, ob_ref, r_ref, att_ref, g_ref, w1_ref, w2_ref, x_ref, mod_ref, lg_ref, lb_ref, o_ref):
    o = of_ref[...] + ob_ref[...]
    r = r_ref[...].astype(F32)
    parts = []
    for h in range(GLA_HEADS):
        oh = o[:, h * GLA_DV:(h + 1) * GLA_DV]
        ms = jnp.mean(oh * oh, axis=-1, keepdims=True)
        parts.append(oh * lax.rsqrt(ms + EPS))
    gl = (jnp.concatenate(parts, axis=1) * g_ref[...] * _silu(r)).astype(BF16)
    y = jnp.dot(gl, w1_ref[...], preferred_element_type=F32)
    y = y + jnp.dot(att_ref[...], w2_ref[...], preferred_element_type=F32)
    gate = mod_ref[0, 2:3, :]
    o_ref[...] = _ln(ALPHA * x_ref[...] + gate * y, lg_ref[...], lb_ref[...])


def _evout(of2, ob2, p2, att2, g512, w1, w2, x2, mod, lg, lb, tps, nb):
    ntok, d = x2.shape
    row = lambda i: (i, 0)
    const = lambda i: (0, 0)
    return pl.pallas_call(
        _evout_kernel,
        grid=(ntok // TM,),
        in_specs=[pl.BlockSpec((TM, GLA_V), row), pl.BlockSpec((TM, GLA_V), row),
                  pl.BlockSpec((TM, GLA_V), lambda i: (i, 4)),
                  pl.BlockSpec((TM, GQA_Q), row),
                  pl.BlockSpec((1, GLA_V), const),
                  pl.BlockSpec((GLA_V, d), const), pl.BlockSpec((GQA_Q, d), const),
                  pl.BlockSpec((TM, d), row),
                  pl.BlockSpec((1, 6, d), lambda i: (_mod_row(i, tps, nb), 0, 0)),
                  pl.BlockSpec((1, d), const), pl.BlockSpec((1, d), const)],
        out_specs=pl.BlockSpec((TM, d), row),
        out_shape=jax.ShapeDtypeStruct((ntok, d), F32),
        compiler_params=_cparams(("arbitrary",)),
        name="even_out",
    )(of2, ob2, p2, att2, g512, w1, w2, x2, mod, lg, lb)


def _conv_kernel(prev_ref, cur_ref, next_ref, w_ref, b_ref, o_ref, pad_ref, *, tps):
    j = pl.program_id(0) % tps
    has_prev = (j >= 2).astype(F32)
    has_next = jnp.logical_and(j >= 1, j <= tps - 2).astype(F32)
    pad_ref[0:8, :] = prev_ref[...].astype(F32) * has_prev
    pad_ref[8:8 + TM, :] = cur_ref[...].astype(F32)
    pad_ref[8 + TM:16 + TM, :] = next_ref[...].astype(F32) * has_next
    half = (SSD_CONV - 1) // 2
    acc = jnp.zeros((TM, cur_ref.shape[1]), F32) + b_ref[...]
    for tap in range(SSD_CONV):
        acc = acc + pad_ref[pl.ds(8 + tap - half, TM), :] * w_ref[tap:tap + 1, :]
    o_ref[...] = _silu(acc).astype(o_ref.dtype)


def _conv(p2, w8, bias, tps):
    ntok = p2.shape[0]
    cw = 256
    ncb = SSD_CONV_CH // cw
    cb0 = SSD_INNER // cw
    r8 = TM // 8
    nrow8 = ntok // 8
    return pl.pallas_call(
        functools.partial(_conv_kernel, tps=tps),
        grid=(ntok // TM, ncb),
        in_specs=[pl.BlockSpec((8, cw), lambda i, c: (jnp.maximum(i * r8 - 1, 0), cb0 + c)),
                  pl.BlockSpec((TM, cw), lambda i, c: (i, cb0 + c)),
                  pl.BlockSpec((8, cw), lambda i, c: (jnp.minimum((i + 1) * r8, nrow8 - 1), cb0 + c)),
                  pl.BlockSpec((8, cw), lambda i, c: (0, c)),
                  pl.BlockSpec((1, cw), lambda i, c: (0, c))],
        out_specs=pl.BlockSpec((TM, cw), lambda i, c: (i, c)),
        out_shape=jax.ShapeDtypeStruct((ntok, SSD_CONV_CH), BF16),
        scratch_shapes=[pltpu.VMEM((TM + 16, cw), F32)],
        compiler_params=_cparams(("arbitrary", "arbitrary")),
        name="ssd_conv",
    )(p2, p2, p2, w8, bias)


def _ssd_kernel(xbc_ref, dt_ref, bias_ref, a_ref, eexp_ref, esel_ref, cm_ref, o_ref, h_ref, *, reverse):
    L = SSD_CHUNK
    GW = SSD_HPG * SSD_HD
    nchunk = TM // L
    nt = (((1,), (1,)), ((), ()))
    tn = (((0,), (0,)), ((), ()))
    end = 0 if reverse else L - 1
    lane0 = SSD_HEADS if reverse else 0

    @pl.when(pl.program_id(1) == 0)
    def _():
        h_ref[...] = jnp.zeros_like(h_ref)

    xs = xbc_ref[:, 0:SSD_INNER].astype(F32)
    bs = xbc_ref[:, SSD_INNER:SSD_INNER + SSD_BC]
    cs = xbc_ref[:, SSD_INNER + SSD_BC:SSD_CONV_CH]
    dt = _softplus(dt_ref[...].astype(F32) + bias_ref[...])
    a = dt * a_ref[...]
    cm = cm_ref[...]
    cum = sum(jnp.dot(cm, part, preferred_element_type=F32) for part in _split3(a))
    cum_t = sum(lax.dot_general(esel_ref[...], part, nt, preferred_element_type=F32)
                for part in _split3(cum))
    cend = _chunk_rows(cum, L, end)
    eexp = eexp_ref[...]

    def expand(m):
        return jnp.dot(m.astype(BF16), eexp, preferred_element_type=F32)

    xdt = xs * expand(dt)
    xdt_b = xdt.astype(BF16)
    xend = (xdt * expand(jnp.exp(cend - cum))).astype(BF16)
    ecum = expand(jnp.exp(cum))
    tot = jnp.concatenate([cum[ch * L + end:ch * L + end + 1, :] for ch in range(nchunk)]
                          + [jnp.zeros((ROWG - nchunk, LANES), F32)], axis=0)
    edec = jnp.exp(sum(jnp.dot(part, eexp, preferred_element_type=F32) for part in _split3(tot)))

    tri = _tri(L, reverse)
    low = lax.broadcasted_iota(jnp.int32, (L, LANES), 1) < SSD_HD
    intra = [[None] * SSD_GROUPS for _ in range(nchunk)]
    dh = [[None] * SSD_GROUPS for _ in range(nchunk)]
    for ch in range(nchunk):
        rows = slice(ch * L, (ch + 1) * L)
        for g in range(SSD_GROUPS):
            bg = bs[rows, g * SSD_STATE:(g + 1) * SSD_STATE]
            cg = cs[rows, g * SSD_STATE:(g + 1) * SSD_STATE]
            cb = lax.dot_general(cg, bg, nt, preferred_element_type=F32)
            pairs = []
            for pp in range(SSD_HPG // 2):
                yh = []
                for u in range(2):
                    h = g * SSD_HPG + 2 * pp + u
                    seg = cum[rows, lane0 + h:lane0 + h + 1] - cum_t[h:h + 1, rows]
                    dec = jnp.exp(jnp.where(tri, seg, -1e30))
                    mat = (cb * dec).astype(BF16)
                    col = (h - u) * SSD_HD
                    yh.append(jnp.dot(mat, xdt_b[rows, col:col + LANES], preferred_element_type=F32))
                pairs.append(jnp.where(low, yh[0], yh[1]))
            intra[ch][g] = jnp.concatenate(pairs, axis=1)
            dh[ch][g] = lax.dot_general(bg, xend[rows, g * GW:(g + 1) * GW], tn, preferred_element_type=F32)

    hs = [h_ref[g] for g in range(SSD_GROUPS)]
    out = [None] * nchunk
    for ch in (range(nchunk - 1, -1, -1) if reverse else range(nchunk)):
        rows = slice(ch * L, (ch + 1) * L)
        ys = []
        for g in range(SSD_GROUPS):
            cg = cs[rows, g * SSD_STATE:(g + 1) * SSD_STATE]
            y_inter = jnp.dot(cg, hs[g].astype(BF16), preferred_element_type=F32) * ecum[rows, g * GW:(g + 1) * GW]
            ys.append(intra[ch][g] + y_inter)
            hs[g] = hs[g] * edec[ch:ch + 1, g * GW:(g + 1) * GW] + dh[ch][g]
        out[ch] = jnp.concatenate(ys, axis=1)
    for g in range(SSD_GROUPS):
        h_ref[g] = hs[g]
    o_ref[...] = jnp.concatenate(out, axis=0)


def _ssd_scan(xbc3, p3, bias_pad, a_pad, eexp, esel, reverse):
    nb, t, _ = xbc3.shape
    nblk = t // TM
    blk = lambda i: _scan_block(i, nblk, reverse)
    const = lambda b, i: (0, 0)
    cm = jnp.asarray(_chunk_sum_mat(SSD_CHUNK, reverse), BF16)
    return pl.pallas_call(
        functools.partial(_ssd_kernel, reverse=reverse),
        grid=(nb, nblk),
        in_specs=[pl.BlockSpec((None, TM, SSD_CONV_CH), lambda b, i: (b, blk(i), 0)),
                  pl.BlockSpec((None, TM, LANES), lambda b, i: (b, blk(i), 16)),
                  pl.BlockSpec((1, LANES), const), pl.BlockSpec((1, LANES), const),
                  pl.BlockSpec((LANES, SSD_INNER), const), pl.BlockSpec((16, LANES), const),
                  pl.BlockSpec((TM, TM), const)],
        out_specs=pl.BlockSpec((None, TM, SSD_INNER), lambda b, i: (b, blk(i), 0)),
        out_shape=jax.ShapeDtypeStruct((nb, t, SSD_INNER), F32),
        scratch_shapes=[pltpu.VMEM((SSD_GROUPS, SSD_STATE, SSD_HPG * SSD_HD), F32)],
        compiler_params=_cparams(("arbitrary", "arbitrary")),
        name="ssd_bwd" if reverse else "ssd_fwd",
    )(xbc3, p3, bias_pad, a_pad, eexp.astype(BF16), esel.astype(BF16), cm)


def _dft_mats(n):
    k = jnp.arange(n, dtype=jnp.int32)
    ang = ((k[:, None] * k[None, :]) % n).astype(F32) * (2.0 * math.pi / n)
    return jnp.cos(ang), jnp.sin(ang)


def _chan_mats():
    cc, sc = _dft_mats(FNET_GC)
    eye = jnp.eye(FNET_GROUPS, dtype=F32)
    return jnp.kron(eye, cc), jnp.kron(eye, sc)


def _group_rms(x, g, bd):
    ms = jnp.dot(x * x, bd, precision=HI, preferred_element_type=F32)
    return x * lax.rsqrt(ms + EPS) * g


def _mm(a, b):
    return jnp.dot(a, b, precision=HI, preferred_element_type=F32)


def _ffta_kernel(x_ref, g_ref, cc_ref, sc_ref, c1_ref, s1_ref, twc_ref, tws_ref, yr_ref, yi_ref, *, nb2):
    bd = _seg_ones(FNET_W, FNET_GC)
    for j in range(nb2):
        x = x_ref[:, j * FNET_W:(j + 1) * FNET_W].astype(F32)
        xn = _group_rms(x, g_ref[...], bd)
        vr = _mm(xn, cc_ref[...])
        vi = -_mm(xn, sc_ref[...])
        c1 = c1_ref[...]
        s1 = s1_ref[...]
        yr = _mm(c1, vr) + _mm(s1, vi)
        yi = _mm(c1, vi) - _mm(s1, vr)
        tc = twc_ref[j]
        ts = tws_ref[j]
        yr_ref[j] = yr * tc + yi * ts
        yi_ref[j] = yi * tc - yr * ts


def _fftb_kernel(yr_ref, yi_ref, c2_ref, s2_ref, o_ref, *, scale):
    o_ref[...] = (_mm(c2_ref[...], yr_ref[...]) + _mm(s2_ref[...], yi_ref[...])) * scale


def _fftc_kernel(x_ref, g_ref, cc_ref, sc_ref, ct_ref, st_ref, o_ref, *, scale):
    bd = _seg_ones(FNET_W, FNET_GC)
    xn = _group_rms(x_ref[...].astype(F32), g_ref[...], bd)
    a = _mm(xn, cc_ref[...])
    b = _mm(xn, sc_ref[...])
    o_ref[...] = (_mm(ct_ref[...], a) - _mm(st_ref[...], b)) * scale


def _fourier_latent(f_lat, g256):
    nb, s, _ = f_lat.shape
    n2 = 64
    n1 = s // n2
    nb2 = 8
    ccm, scm = _chan_mats()
    c1, s1 = _dft_mats(n1)
    c2, s2 = _dft_mats(n2)
    t2 = jnp.arange(n2, dtype=jnp.int32)[:, None]
    k1 = jnp.arange(n1, dtype=jnp.int32)[None, :]
    tw = ((t2 * k1) % s).astype(F32) * (2.0 * math.pi / s)
    twc = jnp.cos(tw)[:, :, None]
    tws = jnp.sin(tw)[:, :, None]
    x2 = f_lat.reshape(nb, n1, n2 * FNET_W)
    const2 = lambda b, j: (0, 0)
    yr, yi = pl.pallas_call(
        functools.partial(_ffta_kernel, nb2=nb2),
        grid=(nb, n2 // nb2),
        in_specs=[pl.BlockSpec((None, n1, nb2 * FNET_W), lambda b, j: (b, 0, j)),
                  pl.BlockSpec((1, FNET_W), const2),
                  pl.BlockSpec((FNET_W, FNET_W), const2), pl.BlockSpec((FNET_W, FNET_W), const2),
                  pl.BlockSpec((n1, n1), const2), pl.BlockSpec((n1, n1), const2),
                  pl.BlockSpec((nb2, n1, 1), lambda b, j: (j, 0, 0)),
                  pl.BlockSpec((nb2, n1, 1), lambda b, j: (j, 0, 0))],
        out_specs=[pl.BlockSpec((None, nb2, n1, FNET_W), lambda b, j: (b, j, 0, 0)),
                   pl.BlockSpec((None, nb2, n1, FNET_W), lambda b, j: (b, j, 0, 0))],
        out_shape=[jax.ShapeDtypeStruct((nb, n2, n1, FNET_W), F32),
                   jax.ShapeDtypeStruct((nb, n2, n1, FNET_W), F32)],
        compiler_params=_cparams(("arbitrary", "arbitrary")),
        name="fft_a",
    )(x2, g256, ccm, scm, c1, s1, twc, tws)
    ncol = n1 * FNET_W
    tn = min(2048, ncol)
    out = pl.pallas_call(
        functools.partial(_fftb_kernel, scale=1.0 / math.sqrt(s * FNET_GC)),
        grid=(nb, ncol // tn),
        in_specs=[pl.BlockSpec((None, n2, tn), lambda b, j: (b, 0, j)),
                  pl.BlockSpec((None, n2, tn), lambda b, j: (b, 0, j)),
                  pl.BlockSpec((n2, n2), const2), pl.BlockSpec((n2, n2), const2)],
        out_specs=pl.BlockSpec((None, n2, tn), lambda b, j: (b, 0, j)),
        out_shape=jax.ShapeDtypeStruct((nb, n2, ncol), F32),
        compiler_params=_cparams(("arbitrary", "arbitrary")),
        name="fft_b",
    )(yr.reshape(nb, n2, ncol), yi.reshape(nb, n2, ncol), c2, s2)
    return out.reshape(nb, s, FNET_W)


def _fourier_ctx(f_ctx, g256):
    nb, tc, _ = f_ctx.shape
    ccm, scm = _chan_mats()
    ct, st = _dft_mats(tc)
    const = lambda b: (0, 0)
    return pl.pallas_call(
        functools.partial(_fftc_kernel, scale=1.0 / math.sqrt(tc * FNET_GC)),
        grid=(nb,),
        in_specs=[pl.BlockSpec((None, tc, FNET_W), lambda b: (b, 0, 0)),
                  pl.BlockSpec((1, FNET_W), const),
                  pl.BlockSpec((FNET_W, FNET_W), const), pl.BlockSpec((FNET_W, FNET_W), const),
                  pl.BlockSpec((tc, tc), const), pl.BlockSpec((tc, tc), const)],
        out_specs=pl.BlockSpec((None, tc, FNET_W), lambda b: (b, 0, 0)),
        out_shape=jax.ShapeDtypeStruct((nb, tc, FNET_W), F32),
        compiler_params=_cparams(("arbitrary",)),
        name="fft_ctx",
    )(f_ctx, g256, ccm, scm, ct, st)


def _odout_kernel(yf_ref, yb_ref, xs_ref, z_ref, f_ref, dsk_ref, g_ref, w1_ref, w2_ref, x_ref, mod_ref,
                  lg_ref, lb_ref, o_ref):
    GW = SSD_HPG * SSD_HD
    y = yf_ref[...] + yb_ref[...] + xs_ref[...].astype(F32) * dsk_ref[...]
    y = y * _silu(z_ref[...].astype(F32))
    parts = []
    for g in range(SSD_GROUPS):
        yg = y[:, g * GW:(g + 1) * GW]
        ms = jnp.mean(yg * yg, axis=-1, keepdims=True)
        parts.append(yg * lax.rsqrt(ms + EPS))
    yn = (jnp.concatenate(parts, axis=1) * g_ref[...]).astype(BF16)
    o = jnp.dot(yn, w1_ref[...], preferred_element_type=F32)
    o = o + jnp.dot(f_ref[...].astype(BF16), w2_ref[...], preferred_element_type=F32)
    gate = mod_ref[0, 2:3, :]
    o_ref[...] = _ln(ALPHA * x_ref[...] + gate * o, lg_ref[...], lb_ref[...])


def _odout(yf2, yb2, xbc2, p2, f2, dsk, g768, w1, w2, x2, mod, lg, lb, tps, nb):
    ntok, d = x2.shape
    row = lambda i: (i, 0)
    const = lambda i: (0, 0)
    return pl.pallas_call(
        _odout_kernel,
        grid=(ntok // TM,),
        in_specs=[pl.BlockSpec((TM, SSD_INNER), row), pl.BlockSpec((TM, SSD_INNER), row),
                  pl.BlockSpec((TM, SSD_INNER), row), pl.BlockSpec((TM, SSD_INNER), row),
                  pl.BlockSpec((TM, FNET_W), row),
                  pl.BlockSpec((1, SSD_INNER), const), pl.BlockSpec((1, SSD_INNER), const),
                  pl.BlockSpec((SSD_INNER, d), const), pl.BlockSpec((FNET_W, d), const),
                  pl.BlockSpec((TM, d), row),
                  pl.BlockSpec((1, 6, d), lambda i: (_mod_row(i, tps, nb), 0, 0)),
                  pl.BlockSpec((1, d), const), pl.BlockSpec((1, d), const)],
        out_specs=pl.BlockSpec((TM, d), row),
        out_shape=jax.ShapeDtypeStruct((ntok, d), F32),
        compiler_params=_cparams(("arbitrary",)),
        name="odd_out",
    )(yf2, yb2, xbc2, p2, f2, dsk, g768, w1, w2, x2, mod, lg, lb)


def _router_kernel(x_ref, mod_ref, rw_ref, rb_ref, v_ref, idx_ref, gw_ref, cnt_ref):
    @pl.when(pl.program_id(0) == 0)
    def _():
        cnt_ref[...] = jnp.zeros_like(cnt_ref)

    shift = mod_ref[0, 3:4, :]
    scale = mod_ref[0, 4:5, :]
    v = x_ref[...] * (1.0 + scale) + shift
    v_ref[...] = v.astype(v_ref.dtype)
    logits = lax.dot_general(rw_ref[...], v, (((1,), (1,)), ((), ())), precision=HI,
                             preferred_element_type=F32)
    s = 1.0 / (1.0 + jnp.exp(-logits))
    sel = s + rb_ref[:, 0:1]
    izero = jnp.zeros((1, TM), jnp.int32)
    best = None
    for g in range(N_EXPERT_GROUPS):
        a = [sel[g * EXPERTS_PER_GROUP + j:g * EXPERTS_PER_GROUP + j + 1, :] for j in range(EXPERTS_PER_GROUP)]
        sv = [s[g * EXPERTS_PER_GROUP + j:g * EXPERTS_PER_GROUP + j + 1, :] for j in range(EXPERTS_PER_GROUP)]
        m1, i1, s1 = a[0], izero, sv[0]
        for j in range(1, EXPERTS_PER_GROUP):
            gt = a[j] > m1
            m1 = jnp.where(gt, a[j], m1)
            i1 = jnp.where(gt, j, i1)
            s1 = jnp.where(gt, sv[j], s1)
        m2 = jnp.full((1, TM), -jnp.inf, F32)
        i2, s2 = izero, jnp.zeros((1, TM), F32)
        for j in range(EXPERTS_PER_GROUP):
            gt = jnp.logical_and(i1 != j, a[j] > m2)
            m2 = jnp.where(gt, a[j], m2)
            i2 = jnp.where(gt, j, i2)
            s2 = jnp.where(gt, sv[j], s2)
        cand = (m1 + m2, i1 + g * EXPERTS_PER_GROUP, i2 + g * EXPERTS_PER_GROUP, s1, s2)
        if best is None:
            best = cand
        else:
            gt = cand[0] > best[0]
            best = tuple(jnp.where(gt, cn, bs) for cn, bs in zip(cand, best))
    _, e1, e2, w1, w2 = best
    tot = w1 + w2
    idx_ref[0] = jnp.concatenate([e1, e2], axis=0)
    gw_ref[0] = jnp.concatenate([w1 / tot, w2 / tot], axis=0)
    eio = lax.broadcasted_iota(jnp.int32, (N_EXPERTS, TM), 0)
    oh = jnp.logical_or(eio == e1, eio == e2).astype(F32)
    cnt_ref[...] += _ceil_rows(jnp.sum(oh, axis=1, keepdims=True))


def _router(x2, mod, rw_t, rb, tps, nb):
    ntok, d = x2.shape
    nt = ntok // TM
    return pl.pallas_call(
        _router_kernel,
        grid=(nt,),
        in_specs=[pl.BlockSpec((TM, d), lambda i: (i, 0)),
                  pl.BlockSpec((1, 6, d), lambda i: (_mod_row(i, tps, nb), 0, 0)),
                  pl.BlockSpec((N_EXPERTS, d), lambda i: (0, 0)),
                  pl.BlockSpec((N_EXPERTS, LANES), lambda i: (0, 0))],
        out_specs=[pl.BlockSpec((TM, d), lambda i: (i, 0)),
                   pl.BlockSpec((1, TOP_K, TM), lambda i: (i, 0, 0)),
                   pl.BlockSpec((1, TOP_K, TM), lambda i: (i, 0, 0)),
                   pl.BlockSpec((N_EXPERTS, LANES), lambda i: (0, 0))],
        out_shape=[jax.ShapeDtypeStruct((ntok, d), BF16),
                   jax.ShapeDtypeStruct((nt, TOP_K, TM), jnp.int32),
                   jax.ShapeDtypeStruct((nt, TOP_K, TM), F32),
                   jax.ShapeDtypeStruct((N_EXPERTS, LANES), F32)],
        compiler_params=_cparams(("arbitrary",)),
        name="router",
    )(x2, mod, rw_t, rb)


def _ceil_rows(c):
    return jnp.ceil(c * (1.0 / ROWG)) * ROWG


def _meta_kernel(idx_ref, start_ref, qpos_ref, meta_ref, run_ref):
    @pl.when(pl.program_id(0) == 0)
    def _():
        run_ref[...] = jnp.zeros_like(run_ref)

    e1 = idx_ref[0, 0:1, :]
    e2 = idx_ref[0, 1:2, :]
    eio = lax.broadcasted_iota(jnp.int32, (N_EXPERTS, TM), 0)
    oh1 = eio == e1
    oh2 = eio == e2
    oh = jnp.logical_or(oh1, oh2)
    r = lax.broadcasted_iota(jnp.int32, (TM, TM), 0)
    c = lax.broadcasted_iota(jnp.int32, (TM, TM), 1)
    before = (r < c).astype(BF16)
    rank = jnp.dot(oh.astype(BF16), before, preferred_element_type=F32)
    run_len = jnp.broadcast_to(_ceil_rows(jnp.sum(oh.astype(F32), axis=1, keepdims=True)), (N_EXPERTS, LANES))
    er = lax.broadcasted_iota(jnp.int32, (N_EXPERTS, N_EXPERTS), 0)
    ec = lax.broadcasted_iota(jnp.int32, (N_EXPERTS, N_EXPERTS), 1)
    off = jnp.dot((ec < er).astype(F32), run_len, precision=HI, preferred_element_type=F32)
    pos = rank + off[:, 0:1]
    q1 = jnp.sum(jnp.where(oh1, pos, 0.0), axis=0, keepdims=True)
    q2 = jnp.sum(jnp.where(oh2, pos, 0.0), axis=0, keepdims=True)
    qpos_ref[0] = jnp.concatenate([q1, q2], axis=0).astype(jnp.int32)
    lane = lax.broadcasted_iota(jnp.int32, (N_EXPERTS, LANES), 1)
    meta = jnp.where(lane == 0, start_ref[...] + run_ref[...], jnp.where(lane == 1, off, run_len))
    meta_ref[0] = (meta * (1.0 / ROWG)).astype(jnp.int32)
    run_ref[...] += run_len


def _meta(idx_t, start):
    nt = idx_t.shape[0]
    return pl.pallas_call(
        _meta_kernel,
        grid=(nt,),
        in_specs=[pl.BlockSpec((1, TOP_K, TM), lambda i: (i, 0, 0)),
                  pl.BlockSpec((N_EXPERTS, LANES), lambda i: (0, 0))],
        out_specs=[pl.BlockSpec((1, TOP_K, TM), lambda i: (i, 0, 0)),
                   pl.BlockSpec((1, N_EXPERTS, LANES), lambda i: (i, 0, 0))],
        out_shape=[jax.ShapeDtypeStruct((nt, TOP_K, TM), jnp.int32),
                   jax.ShapeDtypeStruct((nt, N_EXPERTS, LANES), jnp.int32)],
        scratch_shapes=[pltpu.VMEM((N_EXPERTS, LANES), F32)],
        compiler_params=_cparams(("arbitrary",)),
        name="moe_meta",
    )(idx_t, start)


def _sort_select(qpos_ref):
    r = lax.broadcasted_iota(jnp.int32, (SORT_ROWS, TM), 0)
    return r == qpos_ref[0, 0:1, :], r == qpos_ref[0, 1:2, :]


def _start_run_copies(i, gb_ref, ob_ref, nb_ref, make_copy):
    for e in range(N_EXPERTS):
        n = nb_ref[i * N_EXPERTS + e]
        buf_row = gb_ref[i * N_EXPERTS + e] * ROWG
        tile_row = ob_ref[i * N_EXPERTS + e] * ROWG

        def body(k, carry, buf_row=buf_row, tile_row=tile_row):
            make_copy(pl.multiple_of(tile_row + k * ROWG, ROWG), pl.multiple_of(buf_row + k * ROWG, ROWG)).start()
            return carry

        lax.fori_loop(0, n, body, 0)


def _wait_run_copies(i, nb_ref, make_copy):
    total = 0
    for e in range(N_EXPERTS):
        total = total + nb_ref[i * N_EXPERTS + e]

    def body(k, carry):
        make_copy(0, 0).wait()
        return carry

    lax.fori_loop(0, total, body, 0)


def _scatter_kernel(gb_ref, ob_ref, nb_ref, ts_ref, tn_ref, nu_ref, v_ref, qpos_ref, gw_ref, buf_ref, xs_ref, z_ref,
                    sem, zsem):
    i = pl.program_id(0)
    last = pl.num_programs(0) - 1
    slot = i % 2
    d = v_ref.shape[1]
    m0, m1 = _sort_select(qpos_ref)
    sel = jnp.logical_or(m0, m1).astype(BF16)
    xs_ref[slot, :, 0:d] = jnp.dot(sel, v_ref[...], preferred_element_type=F32)
    gate = jnp.sum(jnp.where(m0, gw_ref[0, 0:1, :], 0.0) + jnp.where(m1, gw_ref[0, 1:2, :], 0.0),
                   axis=1, keepdims=True)
    xs_ref[slot, :, d:d + LANES] = jnp.broadcast_to(gate, (SORT_ROWS, LANES))

    def copy_from(s):
        def copy(tile_row, buf_row):
            return pltpu.make_async_copy(xs_ref.at[s, pl.ds(tile_row, ROWG)], buf_ref.at[pl.ds(buf_row, ROWG)],
                                         sem.at[s])
        return copy

    _start_run_copies(i, gb_ref, ob_ref, nb_ref, copy_from(slot))

    @pl.when(i > 0)
    def _():
        _wait_run_copies(i - 1, nb_ref, copy_from(1 - slot))

    @pl.when(i == last)
    def _():
        _wait_run_copies(i, nb_ref, copy_from(slot))

    @pl.when(i == last)
    def _():
        z_ref[...] = jnp.zeros_like(z_ref)

        def zcopy(buf_row):
            return pltpu.make_async_copy(z_ref.at[pl.ds(0, ROWG)], buf_ref.at[pl.ds(buf_row, ROWG)], zsem)

        total = 0
        for e in range(N_EXPERTS):
            n = tn_ref[e]
            row0 = ts_ref[e] * ROWG

            def body(k, carry, row0=row0):
                zcopy(pl.multiple_of(row0 + k * ROWG, ROWG)).start()
                return carry

            lax.fori_loop(0, n, body, 0)
            total = total + n

        def wbody(k, carry):
            zcopy(0).wait()
            return carry

        lax.fori_loop(0, total, wbody, 0)

        def zblock(blk):
            return pltpu.make_async_copy(z_ref, buf_ref.at[pl.ds(pl.multiple_of(blk * FFN_ROWS, FFN_ROWS), FFN_ROWS)],
                                         zsem)

        def bbody(blk, carry):
            zblock(blk).start()
            return carry

        def bwait(blk, carry):
            zblock(blk).wait()
            return carry

        nblk = buf_ref.shape[0] // FFN_ROWS
        lax.fori_loop(nu_ref[0], nblk, bbody, 0)
        lax.fori_loop(nu_ref[0], nblk, bwait, 0)


def _scatter(gb, ob, nb8, tail_s, tail_n, n_used, v, qpos, gw_t, nrow):
    ntok, d = v.shape
    grid_spec = pltpu.PrefetchScalarGridSpec(
        num_scalar_prefetch=6,
        grid=(ntok // TM,),
        in_specs=[pl.BlockSpec((TM, d), lambda i, *_: (i, 0)),
                  pl.BlockSpec((1, TOP_K, TM), lambda i, *_: (i, 0, 0)),
                  pl.BlockSpec((1, TOP_K, TM), lambda i, *_: (i, 0, 0))],
        out_specs=pl.BlockSpec(memory_space=pl.ANY),
        scratch_shapes=[pltpu.VMEM((2, SORT_ROWS, d + LANES), F32), pltpu.VMEM((FFN_ROWS, d + LANES), F32),
                        pltpu.SemaphoreType.DMA((2,)), pltpu.SemaphoreType.DMA(())],
    )
    return pl.pallas_call(
        _scatter_kernel,
        grid_spec=grid_spec,
        out_shape=jax.ShapeDtypeStruct((nrow, d + LANES), F32),
        compiler_params=_cparams(("arbitrary",)),
        name="moe_scatter",
    )(gb, ob, nb8, tail_s, tail_n, n_used, v, qpos, gw_t)


def _moe_kernel(be_ref, nu_ref, x_ref, wg_ref, wu_ref, wd_ref, o_ref, wg_bf, wu_bf, wd_bf):
    i = pl.program_id(0)
    d = o_ref.shape[1]

    @pl.when(jnp.logical_or(i == 0, be_ref[i] != be_ref[jnp.maximum(i - 1, 0)]))
    def _():
        wg_bf[...] = wg_ref[...].astype(BF16)
        wu_bf[...] = wu_ref[...].astype(BF16)
        wd_bf[...] = wd_ref[...].astype(BF16)

    @pl.when(i < nu_ref[0])
    def _():
        xb = x_ref[...]
        x = xb[:, 0:d].astype(BF16)
        g = jnp.dot(x, wg_bf[...], preferred_element_type=F32)
        u = jnp.dot(x, wu_bf[...], preferred_element_type=F32)
        h = (_silu(g) * u).astype(BF16)
        y = jnp.dot(h, wd_bf[...], preferred_element_type=F32)
        o_ref[...] = y * xb[:, d:d + 1]

    @pl.when(pl.program_id(0) >= nu_ref[0])
    def _():
        o_ref[...] = jnp.zeros_like(o_ref)


def _moe_ffn(block_e, n_used, buf, layer, wg, wu, wd):
    nrow, dw = buf.shape
    d = dw - LANES
    nblk = nrow // FFN_ROWS
    de = wg.shape[3]
    used = lambda i, nu: jnp.minimum(i, nu[0] - 1)
    grid_spec = pltpu.PrefetchScalarGridSpec(
        num_scalar_prefetch=2,
        grid=(nblk,),
        in_specs=[pl.BlockSpec((FFN_ROWS, dw), lambda i, be, nu: (used(i, nu), 0)),
                  pl.BlockSpec((None, None, d, de), lambda i, be, nu: (layer, be[i], 0, 0)),
                  pl.BlockSpec((None, None, d, de), lambda i, be, nu: (layer, be[i], 0, 0)),
                  pl.BlockSpec((None, None, de, d), lambda i, be, nu: (layer, be[i], 0, 0))],
        out_specs=pl.BlockSpec((FFN_ROWS, d), lambda i, be, nu: (i, 0)),
        scratch_shapes=[pltpu.VMEM((d, de), BF16), pltpu.VMEM((d, de), BF16), pltpu.VMEM((de, d), BF16)],
    )
    return pl.pallas_call(
        _moe_kernel,
        grid_spec=grid_spec,
        out_shape=jax.ShapeDtypeStruct((nrow, d), F32),
        compiler_params=pltpu.CompilerParams(dimension_semantics=("arbitrary",), vmem_limit_bytes=FFN_VMEM_LIMIT),
        name="moe_ffn",
    )(block_e, n_used, buf, wg, wu, wd)


def _ln2_kernel(gb_ref, ob_ref, nb_ref, x_ref, qpos_ref, mod_ref, lg_ref, lb_ref, y_ref, o_ref, ys_ref, sem):
    i = pl.program_id(0)
    slot = i % 2

    def copy_to(s):
        def copy(tile_row, buf_row):
            return pltpu.make_async_copy(y_ref.at[pl.ds(buf_row, ROWG)], ys_ref.at[s, pl.ds(tile_row, ROWG)],
                                         sem.at[s])
        return copy

    @pl.when(i == 0)
    def _():
        ys_ref[...] = jnp.zeros_like(ys_ref)
        _start_run_copies(0, gb_ref, ob_ref, nb_ref, copy_to(0))

    @pl.when(i + 1 < pl.num_programs(0))
    def _():
        _start_run_copies(i + 1, gb_ref, ob_ref, nb_ref, copy_to(1 - slot))

    _wait_run_copies(i, nb_ref, copy_to(slot))
    m0, m1 = _sort_select(qpos_ref)
    sel = jnp.logical_or(m0, m1).astype(BF16)
    y = lax.dot_general(sel, ys_ref[slot].astype(BF16), (((0,), (0,)), ((), ())), preferred_element_type=F32)
    gate = mod_ref[0, 5:6, :]
    o_ref[...] = _ln(ALPHA * x_ref[...] + gate * y, lg_ref[...], lb_ref[...])


def _ln2(gb, ob, nb8, x2, qpos, mod, lg, lb, ybuf, tps, nb):
    ntok, d = x2.shape
    grid_spec = pltpu.PrefetchScalarGridSpec(
        num_scalar_prefetch=3,
        grid=(ntok // TM,),
        in_specs=[pl.BlockSpec((TM, d), lambda i, *_: (i, 0)),
                  pl.BlockSpec((1, TOP_K, TM), lambda i, *_: (i, 0, 0)),
                  pl.BlockSpec((1, 6, d), lambda i, *_: (_mod_row(i, tps, nb), 0, 0)),
                  pl.BlockSpec((1, d), lambda i, *_: (0, 0)),
                  pl.BlockSpec((1, d), lambda i, *_: (0, 0)),
                  pl.BlockSpec(memory_space=pl.ANY)],
        out_specs=pl.BlockSpec((TM, d), lambda i, *_: (i, 0)),
        scratch_shapes=[pltpu.VMEM((2, SORT_ROWS, d), F32), pltpu.SemaphoreType.DMA((2,))],
    )
    return pl.pallas_call(
        _ln2_kernel,
        grid_spec=grid_spec,
        out_shape=jax.ShapeDtypeStruct((ntok, d), F32),
        compiler_params=_cparams(("arbitrary",)),
        name="post_moe_ln",
    )(gb, ob, nb8, x2, qpos, mod, lg, lb, ybuf)


def _even_w_in(w):
    o = np.cumsum((0, GLA_QK, GLA_QK, GLA_V, GLA_V, GLA_RANK, GLA_RANK, GQA_Q, GQA_KV, GQA_KV))
    q, k, v, r = w[:, o[0]:o[1]], w[:, o[1]:o[2]], w[:, o[2]:o[3]], w[:, o[3]:o[4]]
    lr = w[:, o[4]:o[6]]
    gq, gk, gv = w[:, o[6]:o[7]], w[:, o[7]:o[8]], w[:, o[8]:o[9]]
    d = w.shape[0]
    zeros = lambda n: jnp.zeros((d, n), w.dtype)
    out = jnp.concatenate([gq, gk, gv, q, k, lr, zeros(LANES - 2 * GLA_RANK), zeros(LANES), v, r], axis=1)
    assert out.shape[1] == EVEN_W
    return out.astype(BF16)


def _odd_w_in(w):
    o = np.cumsum((0, SSD_INNER, SSD_INNER, SSD_BC, SSD_BC, SSD_HEADS, SSD_HEADS, FNET_W))
    d = w.shape[0]
    out = jnp.concatenate([w[:, :o[4]], w[:, o[4]:o[6]], jnp.zeros((d, LANES - 2 * SSD_HEADS), w.dtype),
                           w[:, o[6]:o[7]]], axis=1)
    assert out.shape[1] == ODD_W
    return out.astype(BF16)


def _rope_tables(ctx_len, seq):
    rows = seq // GRID_W
    r = jnp.repeat(jnp.arange(rows, dtype=F32), GRID_W)
    col = jnp.tile(jnp.arange(GRID_W, dtype=F32), rows)
    half = GQA_HD // 2
    inv = ROPE_THETA ** (-jnp.arange(0, half, 2, dtype=F32) / half)
    ar = r[:, None] * inv
    ac = col[:, None] * inv
    ang = jnp.concatenate([ar, ar, ac, ac], -1)
    cos = jnp.concatenate([jnp.ones((ctx_len, GQA_HD), F32), jnp.cos(ang)], 0)
    sin = jnp.concatenate([jnp.zeros((ctx_len, GQA_HD), F32), jnp.sin(ang)], 0)
    return jnp.tile(cos, (1, 2)), jnp.tile(sin, (1, 2))


def _even_layer(x2, mod, nb, t, tps, ctx_len, w_in, w_o, w_dec, b_dec, gla_g, qn_g, kn_g, cos2, sin2, lg, lb):
    p2 = _inproj(x2, mod, _even_w_in(w_in), tps, nb)
    p3 = p2.reshape(nb, t, EVEN_W)
    outs = []
    for di in range(2):
        wd_pad = jnp.zeros((LANES, GLA_QK), F32).at[di * GLA_RANK:(di + 1) * GLA_RANK].set(w_dec[di])
        outs.append(_gla_scan(p3, wd_pad, b_dec[di][None, :], reverse=(di == 1)))
    qh2, kg2, vg2 = _qkprep(p2, cos2, sin2, jnp.tile(qn_g, 2)[None, :], jnp.tile(kn_g, 2)[None, :], tps)
    att = _attention(qh2.reshape(nb, t, GQA_HEADS * LANES), kg2.reshape(nb, t, GQA_KV_HEADS * LANES),
                     vg2.reshape(nb, t, GQA_KV_HEADS * LANES), ctx_len)
    w1, w2 = w_o[:GLA_V].astype(BF16), w_o[GLA_V:].astype(BF16)
    ntok = nb * t
    return _evout(outs[0].reshape(ntok, GLA_V), outs[1].reshape(ntok, GLA_V), p2, att.reshape(ntok, GQA_Q),
                  jnp.tile(gla_g, GLA_HEADS)[None, :], w1, w2, x2, mod, lg, lb, tps, nb)


def _odd_layer(x2, mod, nb, t, tps, ctx_len, w_in, w_o, conv_w, conv_b, dt_bias, a_log, d_skip, ssd_g, fnet_g,
               lg, lb):
    p2 = _inproj(x2, mod, _odd_w_in(w_in), tps, nb)
    p3 = p2.reshape(nb, t, ODD_W)
    w8 = jnp.zeros((8, SSD_CONV_CH), F32).at[:SSD_CONV].set(conv_w)
    xbc2 = _conv(p2, w8, conv_b[None, :], tps)
    xbc3 = xbc2.reshape(nb, t, SSD_CONV_CH)
    ys = []
    heads = jnp.arange(SSD_HEADS)
    for di in range(2):
        lanes = di * SSD_HEADS + heads
        bias_pad = jnp.zeros((1, LANES), F32).at[0, lanes].set(dt_bias[di])
        a_pad = jnp.zeros((1, LANES), F32).at[0, lanes].set(-jnp.exp(a_log[di]))
        eexp = jnp.zeros((LANES, SSD_INNER), F32).at[jnp.repeat(lanes, SSD_HD), jnp.arange(SSD_INNER)].set(1.0)
        esel = jnp.zeros((16, LANES), F32).at[heads, lanes].set(1.0)
        ys.append(_ssd_scan(xbc3, p3, bias_pad, a_pad, eexp, esel, reverse=(di == 1)))
    f3 = p3[:, :, ODD_W - FNET_W:]
    g256 = fnet_g[None, :]
    fmix = jnp.concatenate([_fourier_ctx(f3[:, :ctx_len], g256), _fourier_latent(f3[:, ctx_len:], g256)], axis=1)
    ntok = nb * t
    return _odout(ys[0].reshape(ntok, SSD_INNER), ys[1].reshape(ntok, SSD_INNER), xbc2, p2,
                  fmix.reshape(ntok, FNET_W), jnp.repeat(d_skip, SSD_HD)[None, :], ssd_g[None, :],
                  w_o[:SSD_INNER].astype(BF16), w_o[SSD_INNER:].astype(BF16), x2, mod, lg, lb, tps, nb)


def _moe_layer(x2, mod, nb, tps, rw_t, rb, layer, wg, wu, wd, lg, lb):
    ntok, d = x2.shape
    v, idx_t, gw_t, cnt = _router(x2, mod, rw_t, rb, tps, nb)
    nt = ntok // TM
    rows = cnt[:, 0].astype(jnp.int32)
    padded = (rows + FFN_ROWS - 1) // FFN_ROWS * FFN_ROWS
    pad_end = jnp.cumsum(padded)
    pad_start = pad_end - padded
    nblk = -(-(ntok * TOP_K + nt * N_EXPERTS * (ROWG - 1)) // FFN_ROWS) + N_EXPERTS
    blk_row = jnp.arange(nblk, dtype=jnp.int32)[:, None] * FFN_ROWS
    block_e = jnp.minimum(jnp.sum((blk_row >= pad_end[None, :]).astype(jnp.int32), axis=1), N_EXPERTS - 1)
    n_used = (pad_end[-1:] // FFN_ROWS).astype(jnp.int32)
    start = jnp.broadcast_to(pad_start.astype(F32)[:, None], (N_EXPERTS, LANES))
    qpos, meta = _meta(idx_t, start)
    gb, ob, nb8 = (meta[:, :, j].reshape(nt * N_EXPERTS) for j in range(3))
    tail_s = ((pad_start + rows) // ROWG).astype(jnp.int32)
    tail_n = ((padded - rows) // ROWG).astype(jnp.int32)
    buf = _scatter(gb, ob, nb8, tail_s, tail_n, n_used, v, qpos, gw_t, nblk * FFN_ROWS)
    y = _moe_ffn(block_e, n_used, buf, layer, wg, wu, wd)
    return _ln2(gb, ob, nb8, x2, qpos, mod, lg, lb, y, tps, nb)


def kernel(x, c, ctx, c_ctx, ada_w, ada_b, ln_g, ln_b, ev_w_in, ev_w_o, gla_w_decay, gla_b_decay, gla_norm_g,
           gqa_q_norm_g, gqa_k_norm_g, od_w_in, od_w_o, ssd_conv_w, ssd_conv_b, ssd_dt_bias, ssd_a_log, ssd_d,
           ssd_norm_g, fnet_norm_g, router_w, router_b, exp_w_gate, exp_w_up, exp_w_down):
    nb, seq, d = x.shape
    ctx_len = ctx.shape[1]
    assert ctx_len == TM and seq % TM == 0 and d == D_MODEL and nb <= 7
    t = ctx_len + seq
    tps = t // TM
    ntok = nb * t
    cc = jnp.zeros((8, d), F32).at[:nb].set(c).at[nb].set(c_ctx)
    mod_all = _ada_all(cc, ada_w, ada_b).reshape(DEPTH, 8, 6, d)
    cos2, sin2 = _rope_tables(ctx_len, seq)
    rw_t = router_w.T
    rb = jnp.broadcast_to(router_b[:, None], (N_EXPERTS, LANES))
    x2 = jnp.concatenate([ctx, x], axis=1).reshape(ntok, d)
    for layer in range(DEPTH):
        mod = mod_all[layer]
        i = layer // 2
        lg0, lb0 = ln_g[layer, 0][None, :], ln_b[layer, 0][None, :]
        lg1, lb1 = ln_g[layer, 1][None, :], ln_b[layer, 1][None, :]
        if layer % 2 == 0:
            x2 = _even_layer(x2, mod, nb, t, tps, ctx_len, ev_w_in[i], ev_w_o[i], gla_w_decay[i], gla_b_decay[i],
                             gla_norm_g[i], gqa_q_norm_g[i], gqa_k_norm_g[i], cos2, sin2, lg0, lb0)
        else:
            x2 = _odd_layer(x2, mod, nb, t, tps, ctx_len, od_w_in[i], od_w_o[i], ssd_conv_w[i], ssd_conv_b[i],
                            ssd_dt_bias[i], ssd_a_log[i], ssd_d[i], ssd_norm_g[i], fnet_norm_g[i], lg0, lb0)
        x2 = _moe_layer(x2, mod, nb, tps, rw_t, rb, layer, exp_w_gate, exp_w_up, exp_w_down, lg1, lb1)
    return x2.reshape(nb, t, d)[:, ctx_len:]
```

```python
import functools
import itertools
import math

import jax
import jax.numpy as jnp
import numpy as np
from jax import lax
from jax.experimental import pallas as pl
from jax.experimental.pallas import tpu as pltpu

F32 = jnp.float32
BF16 = jnp.bfloat16
HI = lax.Precision.HIGHEST

D_MODEL = 1024
DEPTH = 4
GRID_W = 64
GLA_HEADS, GLA_DK, GLA_DV, GLA_RANK, GLA_TAU, GLA_CHUNK = 4, 64, 128, 16, 16.0, 64
GQA_HEADS, GQA_KV_HEADS, GQA_HD = 8, 2, 64
ROPE_THETA = 10000.0
SSD_HEADS, SSD_HD, SSD_GROUPS, SSD_STATE, SSD_CONV, SSD_CHUNK = 12, 64, 2, 128, 5, 64
SSD_HPG = SSD_HEADS // SSD_GROUPS
SSD_INNER = SSD_HEADS * SSD_HD
SSD_BC = SSD_GROUPS * SSD_STATE
SSD_CONV_CH = SSD_INNER + 2 * SSD_BC
FNET_GROUPS, FNET_GC = 4, 64
FNET_W = FNET_GROUPS * FNET_GC
N_EXPERTS, N_EXPERT_GROUPS, TOP_K, D_EXPERT = 16, 4, 2, 768
EXPERTS_PER_GROUP = N_EXPERTS // N_EXPERT_GROUPS
GLA_QK = GLA_HEADS * GLA_DK
GLA_V = GLA_HEADS * GLA_DV
GQA_Q = GQA_HEADS * GQA_HD
GQA_KV = GQA_KV_HEADS * GQA_HD
EPS = 1e-6
ALPHA = (2.0 * DEPTH) ** 0.25

LANES = 128
TM = 256
EVEN_W = 2560
ODD_W = 2432
VMEM_LIMIT = 48 * 1024 * 1024
LOG2E = 1.4426950408889634
ATT_TQ = 256
ATT_HEADROOM = 64.0
ATT_MIN_DENOM = 2.0 ** -60
ROWG = 8
SORT_ROWS = TOP_K * TM + N_EXPERTS * (ROWG - 1)
COPY_SHIFT = 2
COPY_GROUPS = 1 << COPY_SHIFT
FFN_ROWS = 512
FFN_VMEM_LIMIT = 56 * 1024 * 1024


def _cparams(sem):
    return pltpu.CompilerParams(dimension_semantics=sem, vmem_limit_bytes=VMEM_LIMIT)


def _silu(x):
    return x * (1.0 / (1.0 + jnp.exp(-x)))


def _softplus(x):
    return jnp.maximum(x, 0.0) + jnp.log1p(jnp.exp(-jnp.abs(x)))


def _mod_row(i, tiles_per_seq, n_batch):
    return jnp.where(i % tiles_per_seq == 0, n_batch, i // tiles_per_seq)


def _ada_kernel(c_ref, w_ref, b_ref, o_ref):
    s = _silu(c_ref[...])
    o_ref[0] = jnp.dot(s, w_ref[0], precision=HI, preferred_element_type=F32) + b_ref[0]


def _ada_all(cc, ada_w, ada_b):
    depth, d, n = ada_w.shape
    tn = 512
    return pl.pallas_call(
        _ada_kernel,
        grid=(depth, n // tn),
        in_specs=[pl.BlockSpec((8, d), lambda l, j: (0, 0)),
                  pl.BlockSpec((1, d, tn), lambda l, j: (l, 0, j)),
                  pl.BlockSpec((1, 1, tn), lambda l, j: (l, 0, j))],
        out_specs=pl.BlockSpec((1, 8, tn), lambda l, j: (l, 0, j)),
        out_shape=jax.ShapeDtypeStruct((depth, 8, n), F32),
        compiler_params=_cparams(("arbitrary", "arbitrary")),
        name="adaln",
    )(cc, ada_w, ada_b.reshape(depth, 1, n))


def _inproj_kernel(x_ref, mod_ref, w_ref, o_ref):
    shift = mod_ref[0, 0:1, :]
    scale = mod_ref[0, 1:2, :]
    u = (x_ref[...] * (1.0 + scale) + shift).astype(BF16)
    o_ref[...] = jnp.dot(u, w_ref[...], preferred_element_type=F32).astype(o_ref.dtype)


def _inproj(x2, mod, w, tps, nb):
    ntok, d = x2.shape
    nw = w.shape[1]
    return pl.pallas_call(
        _inproj_kernel,
        grid=(ntok // TM,),
        in_specs=[pl.BlockSpec((TM, d), lambda i: (i, 0)),
                  pl.BlockSpec((1, 6, d), lambda i: (_mod_row(i, tps, nb), 0, 0)),
                  pl.BlockSpec((d, nw), lambda i: (0, 0))],
        out_specs=pl.BlockSpec((TM, nw), lambda i: (i, 0)),
        out_shape=jax.ShapeDtypeStruct((ntok, nw), BF16),
        compiler_params=_cparams(("arbitrary",)),
        name="inproj",
    )(x2, mod, w)


def _scan_block(i, nblk, reverse):
    if not reverse:
        return i
    return jnp.where(i == 0, 0, nblk - i)


def _tri(n, reverse):
    r = lax.broadcasted_iota(jnp.int32, (n, n), 0)
    c = lax.broadcasted_iota(jnp.int32, (n, n), 1)
    return (c >= r) if reverse else (c <= r)


def _chunk_sum_mat(chunk, reverse):
    l = np.arange(TM)[:, None]
    m = np.arange(TM)[None, :]
    same = (l // chunk) == (m // chunk)
    return (same & ((m >= l) if reverse else (m <= l))).astype(np.float32)


def _split3(x):
    hi = x.astype(BF16)
    r1 = x - hi.astype(F32)
    mid = r1.astype(BF16)
    return hi, mid, (r1 - mid.astype(F32)).astype(BF16)


def _chunk_rows(x, chunk, idx):
    return jnp.concatenate([jnp.broadcast_to(x[c * chunk + idx:c * chunk + idx + 1, :], (chunk, x.shape[1]))
                            for c in range(x.shape[0] // chunk)], axis=0)


def _gla_dir(q_ref, k_ref, v_ref, lr_ref, wd_ref, bd_ref, cm_ref, o_ref, s_ref, *, reverse):
    L = GLA_CHUNK
    nchunk = TM // L
    nt = (((1,), (1,)), ((), ()))
    tn = (((0,), (0,)), ((), ()))
    end = 0 if reverse else L - 1
    mid = L // 2 if reverse else L // 2 - 1

    @pl.when(pl.program_id(2) == 0)
    def _():
        s_ref[...] = jnp.zeros_like(s_ref)

    q = q_ref[...].astype(F32) * (GLA_DK ** -0.5)
    k = k_ref[...].astype(F32)
    v = v_ref[...]
    lr = lr_ref[...]
    z = bd_ref[...] + sum(jnp.dot(lr, wd_ref[j], preferred_element_type=F32) for j in range(3))
    yield
    la = -_softplus(-z) * (1.0 / GLA_TAU)
    cm = cm_ref[...]
    b = sum(jnp.dot(cm, part, preferred_element_type=F32) for part in _split3(la))
    yield
    bmid = _chunk_rows(b, L, mid)
    bend = _chunk_rows(b, L, end)
    qs = q * jnp.exp(b - bmid)
    ks = (k * jnp.exp(bmid - b)).astype(BF16)
    qi = q * jnp.exp(b)
    kend = (k * jnp.exp(bend - b)).astype(BF16)
    r = lax.broadcasted_iota(jnp.int32, (TM, TM), 0)
    c = lax.broadcasted_iota(jnp.int32, (TM, TM), 1)
    mask = jnp.logical_and(r // L == c // L, (c >= r) if reverse else (c <= r))
    lane = lax.broadcasted_iota(jnp.int32, (TM, LANES), 1)
    heads = (lane < GLA_DK, lane >= GLA_DK)
    yield
    att = [lax.dot_general(jnp.where(heads[h], qs, 0.0).astype(BF16), ks, nt, preferred_element_type=F32)
           for h in range(2)]
    yield
    intra = []
    for h in range(2):
        att_h = jnp.where(mask, att[h], 0.0).astype(BF16)
        intra.append(jnp.dot(att_h, v[:, h * GLA_DV:(h + 1) * GLA_DV], preferred_element_type=F32))
    yield
    qi_h = [jnp.where(heads[h], qi, 0.0).astype(BF16) for h in range(2)]
    tot = jnp.concatenate([b[ch * L + end:ch * L + end + 1, :] for ch in range(nchunk)]
                          + [jnp.zeros((ROWG - nchunk, LANES), F32)], axis=0).T
    ds, dec = [], []
    for ch in range(nchunk):
        rows = slice(ch * L, (ch + 1) * L)
        ds.append(lax.dot_general(kend[rows], v[rows], tn, preferred_element_type=F32))
        dec.append(jnp.exp(jnp.broadcast_to(tot[:, ch:ch + 1], (2 * GLA_DK, 2 * GLA_DV))))
    yield
    s = s_ref[...]
    inter = [None] * nchunk
    for ch in (range(nchunk - 1, -1, -1) if reverse else range(nchunk)):
        rows = slice(ch * L, (ch + 1) * L)
        s_bf = s.astype(BF16)
        inter[ch] = jnp.concatenate(
            [jnp.dot(qi_h[h][rows], s_bf[:, h * GLA_DV:(h + 1) * GLA_DV], preferred_element_type=F32)
             for h in range(2)], axis=1)
        s = dec[ch] * s + ds[ch]
        yield
    s_ref[...] = s
    o_ref[...] = (jnp.concatenate(intra, axis=1) + jnp.concatenate(inter, axis=0)).astype(o_ref.dtype)


def _alternate(*stage_generators):
    for _ in itertools.zip_longest(*stage_generators):
        pass


def _gla_kernel(*refs):
    n_in = (len(refs) - 4) // 2
    o_f, o_b, s_f, s_b = refs[2 * n_in:]
    _alternate(_gla_dir(*refs[:n_in], o_f, s_f, reverse=False),
               _gla_dir(*refs[n_in:2 * n_in], o_b, s_b, reverse=True))


def _gla_scan(p3, wd_pads, bds):
    nb, t, _ = p3.shape
    nblk = t // TM
    in_specs, args, out_specs = [], [], []
    for reverse in (False, True):
        blk = functools.partial(_scan_block, nblk=nblk, reverse=reverse)
        in_specs += [pl.BlockSpec((None, TM, LANES), lambda b, p, i, blk=blk: (b, blk(i), 6 + p)),
                     pl.BlockSpec((None, TM, LANES), lambda b, p, i, blk=blk: (b, blk(i), 8 + p)),
                     pl.BlockSpec((None, TM, 2 * GLA_DV), lambda b, p, i, blk=blk: (b, blk(i), 6 + p)),
                     pl.BlockSpec((None, TM, LANES), lambda b, p, i, blk=blk: (b, blk(i), 10)),
                     pl.BlockSpec((3, LANES, LANES), lambda b, p, i: (0, 0, p)),
                     pl.BlockSpec((1, LANES), lambda b, p, i: (0, p)),
                     pl.BlockSpec((TM, TM), lambda b, p, i: (0, 0))]
        args += [p3, p3, p3, p3, jnp.stack(_split3(wd_pads[int(reverse)])), bds[int(reverse)],
                 jnp.asarray(_chunk_sum_mat(GLA_CHUNK, reverse), BF16)]
        out_specs.append(pl.BlockSpec((None, TM, 2 * GLA_DV), lambda b, p, i, blk=blk: (b, blk(i), p)))
    state = pltpu.VMEM((2 * GLA_DK, 2 * GLA_DV), F32)
    return pl.pallas_call(
        _gla_kernel,
        grid=(nb, 2, nblk),
        in_specs=in_specs,
        out_specs=out_specs,
        out_shape=[jax.ShapeDtypeStruct((nb, t, GLA_V), BF16)] * 2,
        scratch_shapes=[state, state],
        compiler_params=_cparams(("arbitrary", "arbitrary", "arbitrary")),
        name="gla_scan",
    )(*args)


def _seg_ones(width, seg):
    r = lax.broadcasted_iota(jnp.int32, (width, width), 0) // seg
    c = lax.broadcasted_iota(jnp.int32, (width, width), 1) // seg
    return jnp.where(r == c, 1.0 / seg, 0.0).astype(F32)


def _norm_rope(x, g, cos, sin, bd):
    ms = jnp.dot(x * x, bd, precision=HI, preferred_element_type=F32)
    xn = x * lax.rsqrt(ms + EPS) * g
    lane = lax.broadcasted_iota(jnp.int32, xn.shape, 1)
    quarter = GQA_HD // 4
    up = pltpu.roll(xn, LANES - quarter, 1)
    dn = pltpu.roll(xn, quarter, 1)
    rot = jnp.where(lane % (2 * quarter) < quarter, -up, dn)
    return xn * cos + rot * sin


def _qkprep_kernel(q_ref, k_ref, v_ref, cos_ref, sin_ref, gq_ref, gk_ref, qo_ref, ko_ref, vo_ref):
    bd = _seg_ones(LANES, GQA_HD)
    cos = cos_ref[...]
    sin = sin_ref[...]
    lane = lax.broadcasted_iota(jnp.int32, (TM, LANES), 1)
    low = lane < GQA_HD
    extra = lane == GQA_HD
    one_hot = jnp.where(extra, 1.0, 0.0)
    k_bound = math.sqrt(GQA_HD) * jnp.max(jnp.abs(gk_ref[...]), axis=-1, keepdims=True) * 1.02
    for j in range(GQA_Q // LANES):
        x = q_ref[:, j * LANES:(j + 1) * LANES].astype(F32)
        y = _norm_rope(x, gq_ref[...], cos, sin, bd) * (GQA_HD ** -0.5 * LOG2E)
        y = y.astype(BF16).astype(F32)
        norm = jnp.sqrt(jnp.dot(y * y, bd, precision=HI, preferred_element_type=F32) * GQA_HD)
        shift = ATT_HEADROOM - norm * k_bound
        lo = jnp.where(low, y, jnp.where(extra, pltpu.roll(shift, GQA_HD, 1), 0.0))
        hi = jnp.where(low, pltpu.roll(y, GQA_HD, 1), jnp.where(extra, shift, 0.0))
        qo_ref[:, (2 * j) * LANES:(2 * j + 1) * LANES] = lo.astype(BF16)
        qo_ref[:, (2 * j + 1) * LANES:(2 * j + 2) * LANES] = hi.astype(BF16)
    k = _norm_rope(k_ref[...].astype(F32), gk_ref[...], cos, sin, bd)
    v = v_ref[...].astype(F32)
    for src, dst in ((k, ko_ref), (v, vo_ref)):
        dst[:, 0:LANES] = jnp.where(low, src, one_hot).astype(BF16)
        dst[:, LANES:2 * LANES] = jnp.where(low, pltpu.roll(src, GQA_HD, 1), one_hot).astype(BF16)


def _qkprep(p2, cos2, sin2, gq2, gk2, tps):
    ntok = p2.shape[0]
    return pl.pallas_call(
        _qkprep_kernel,
        grid=(ntok // TM,),
        in_specs=[pl.BlockSpec((TM, GQA_Q), lambda i: (i, 0)),
                  pl.BlockSpec((TM, LANES), lambda i: (i, 4)),
                  pl.BlockSpec((TM, LANES), lambda i: (i, 5)),
                  pl.BlockSpec((TM, LANES), lambda i: (i % tps, 0)),
                  pl.BlockSpec((TM, LANES), lambda i: (i % tps, 0)),
                  pl.BlockSpec((1, LANES), lambda i: (0, 0)),
                  pl.BlockSpec((1, LANES), lambda i: (0, 0))],
        out_specs=[pl.BlockSpec((TM, GQA_HEADS * LANES), lambda i: (i, 0)),
                   pl.BlockSpec((TM, GQA_KV_HEADS * LANES), lambda i: (i, 0)),
                   pl.BlockSpec((TM, GQA_KV_HEADS * LANES), lambda i: (i, 0))],
        out_shape=[jax.ShapeDtypeStruct((ntok, GQA_HEADS * LANES), BF16),
                   jax.ShapeDtypeStruct((ntok, GQA_KV_HEADS * LANES), BF16),
                   jax.ShapeDtypeStruct((ntok, GQA_KV_HEADS * LANES), BF16)],
        compiler_params=_cparams(("arbitrary",)),
        name="qkprep",
    )(p2, p2, p2, cos2, sin2, gq2, gk2)


def _attn_kernel(q_ref, k_ref, v_ref, o_ref, *, ctx_len, n_lat_chunks, ck, n_blocks):
    i = pl.program_id(2)
    rep = GQA_HEADS // GQA_KV_HEADS
    is_ctx = i < ctx_len // ATT_TQ
    q4 = jnp.concatenate([q_ref[:, r * LANES:(r + 1) * LANES] for r in range(rep)], axis=0)
    nt = (((1,), (1,)), ((), ()))

    def chunk(acc, kc, vc):
        s = lax.dot_general(q4, kc, nt, preferred_element_type=F32)
        return acc + jnp.dot(jnp.exp2(s).astype(BF16), vc, preferred_element_type=F32)

    def store(acc):
        out = acc * (1.0 / acc[:, GQA_HD:GQA_HD + 1])
        low = lax.broadcasted_iota(jnp.int32, (ATT_TQ, LANES), 1) < GQA_HD
        hs = [out[r * ATT_TQ:(r + 1) * ATT_TQ] for r in range(rep)]
        for u in range(rep // 2):
            o_ref[:, u * LANES:(u + 1) * LANES] = jnp.where(
                low, hs[2 * u], pltpu.roll(hs[2 * u + 1], GQA_HD, 1)).astype(o_ref.dtype)

    acc = chunk(jnp.zeros((rep * ATT_TQ, LANES), F32), k_ref[0:ctx_len, :], v_ref[0:ctx_len, :])

    def body(c, acc):
        off = pl.multiple_of(ctx_len + c * ck, TM)
        return chunk(acc, k_ref[pl.ds(off, ck), :], v_ref[pl.ds(off, ck), :])

    acc = lax.fori_loop(0, jnp.where(is_ctx, 0, n_lat_chunks), body, acc)
    healthy = jnp.min(acc[:, GQA_HD:GQA_HD + 1]) >= ATT_MIN_DENOM

    @pl.when(healthy)
    def _():
        store(acc)

    @pl.when(jnp.logical_not(healthy))
    def _():
        def online(c, carry):
            m, acc = carry
            off = pl.multiple_of(c * TM, TM)
            s = lax.dot_general(q4, k_ref[pl.ds(off, TM), :], nt, preferred_element_type=F32)
            mn = jnp.maximum(m, jnp.max(s, axis=-1, keepdims=True))
            p = jnp.exp2(s - mn).astype(BF16)
            acc = jnp.exp2(m - mn) * acc + jnp.dot(p, v_ref[pl.ds(off, TM), :], preferred_element_type=F32)
            return mn, acc

        init = (jnp.full((rep * ATT_TQ, 1), -1e30, F32), jnp.zeros((rep * ATT_TQ, LANES), F32))
        _, acc2 = lax.fori_loop(0, jnp.where(is_ctx, ctx_len // TM, n_blocks), online, init)
        store(acc2)


def _attention(qh3, kg3, vg3, ctx_len):
    nb, t, _ = qh3.shape
    seq = t - ctx_len
    ck = next(c for c in (2048, 1024, 512, 256) if seq % c == 0)
    rep = GQA_HEADS // GQA_KV_HEADS
    return pl.pallas_call(
        functools.partial(_attn_kernel, ctx_len=ctx_len, n_lat_chunks=seq // ck, ck=ck, n_blocks=t // TM),
        grid=(nb, GQA_KV_HEADS, t // ATT_TQ),
        in_specs=[pl.BlockSpec((None, ATT_TQ, rep * LANES), lambda b, g, i: (b, i, g)),
                  pl.BlockSpec((None, t, LANES), lambda b, g, i: (b, 0, g)),
                  pl.BlockSpec((None, t, LANES), lambda b, g, i: (b, 0, g))],
        out_specs=pl.BlockSpec((None, ATT_TQ, rep * GQA_HD), lambda b, g, i: (b, i, g)),
        out_shape=jax.ShapeDtypeStruct((nb, t, GQA_Q), BF16),
        compiler_params=_cparams(("arbitrary", "arbitrary", "arbitrary")),
        name="gqa_attn",
    )(qh3, kg3, vg3)


def _ln(x, g, b):
    mu = jnp.mean(x, axis=-1, keepdims=True)
    xc = x - mu
    var = jnp.mean(xc * xc, axis=-1, keepdims=True)
    return xc * lax.rsqrt(var + EPS) * g + b


def _evout_kernel(of_ref, ob_ref, r_ref, att_ref, g_ref, w1_ref, w2_ref, x_ref, mod_ref, lg_ref, lb_ref, o_ref):
    o = of_ref[...].astype(F32) + ob_ref[...].astype(F32)
    r = r_ref[...].astype(F32)
    parts = []
    for h in range(GLA_HEADS):
        oh = o[:, h * GLA_DV:(h + 1) * GLA_DV]
        ms = jnp.mean(oh * oh, axis=-1, keepdims=True)
        parts.append(oh * lax.rsqrt(ms + EPS))
    gl = (jnp.concatenate(parts, axis=1) * g_ref[...] * _silu(r)).astype(BF16)
    y = jnp.dot(gl, w1_ref[...], preferred_element_type=F32)
    y = y + jnp.dot(att_ref[...], w2_ref[...], preferred_element_type=F32)
    gate = mod_ref[0, 2:3, :]
    o_ref[...] = _ln(ALPHA * x_ref[...] + gate * y, lg_ref[...], lb_ref[...])


def _evout(of2, ob2, p2, att2, g512, w1, w2, x2, mod, lg, lb, tps, nb):
    ntok, d = x2.shape
    row = lambda i: (i, 0)
    const = lambda i: (0, 0)
    return pl.pallas_call(
        _evout_kernel,
        grid=(ntok // TM,),
        in_specs=[pl.BlockSpec((TM, GLA_V), row), pl.BlockSpec((TM, GLA_V), row),
                  pl.BlockSpec((TM, GLA_V), lambda i: (i, 4)),
                  pl.BlockSpec((TM, GQA_Q), row),
                  pl.BlockSpec((1, GLA_V), const),
                  pl.BlockSpec((GLA_V, d), const), pl.BlockSpec((GQA_Q, d), const),
                  pl.BlockSpec((TM, d), row),
                  pl.BlockSpec((1, 6, d), lambda i: (_mod_row(i, tps, nb), 0, 0)),
                  pl.BlockSpec((1, d), const), pl.BlockSpec((1, d), const)],
        out_specs=pl.BlockSpec((TM, d), row),
        out_shape=jax.ShapeDtypeStruct((ntok, d), F32),
        compiler_params=_cparams(("arbitrary",)),
        name="even_out",
    )(of2, ob2, p2, att2, g512, w1, w2, x2, mod, lg, lb)


def _conv_kernel(prev_ref, cur_ref, next_ref, w_ref, b_ref, o_ref, pad_ref, *, tps):
    j = pl.program_id(0) % tps
    has_prev = (j >= 2).astype(F32)
    has_next = jnp.logical_and(j >= 1, j <= tps - 2).astype(F32)
    pad_ref[0:8, :] = prev_ref[...].astype(F32) * has_prev
    pad_ref[8:8 + TM, :] = cur_ref[...].astype(F32)
    pad_ref[8 + TM:16 + TM, :] = next_ref[...].astype(F32) * has_next
    half = (SSD_CONV - 1) // 2
    acc = jnp.zeros((TM, cur_ref.shape[1]), F32) + b_ref[...]
    for tap in range(SSD_CONV):
        acc = acc + pad_ref[pl.ds(8 + tap - half, TM), :] * w_ref[tap:tap + 1, :]
    o_ref[...] = _silu(acc).astype(o_ref.dtype)


def _conv(p2, w8, bias, tps):
    ntok = p2.shape[0]
    cw = 256
    ncb = SSD_CONV_CH // cw
    cb0 = SSD_INNER // cw
    r8 = TM // 8
    nrow8 = ntok // 8
    return pl.pallas_call(
        functools.partial(_conv_kernel, tps=tps),
        grid=(ntok // TM, ncb),
        in_specs=[pl.BlockSpec((8, cw), lambda i, c: (jnp.maximum(i * r8 - 1, 0), cb0 + c)),
                  pl.BlockSpec((TM, cw), lambda i, c: (i, cb0 + c)),
                  pl.BlockSpec((8, cw), lambda i, c: (jnp.minimum((i + 1) * r8, nrow8 - 1), cb0 + c)),
                  pl.BlockSpec((8, cw), lambda i, c: (0, c)),
                  pl.BlockSpec((1, cw), lambda i, c: (0, c))],
        out_specs=pl.BlockSpec((TM, cw), lambda i, c: (i, c)),
        out_shape=jax.ShapeDtypeStruct((ntok, SSD_CONV_CH), BF16),
        scratch_shapes=[pltpu.VMEM((TM + 16, cw), F32)],
        compiler_params=_cparams(("arbitrary", "arbitrary")),
        name="ssd_conv",
    )(p2, p2, p2, w8, bias)


def _ssd_dir(xbc_ref, dt_ref, bias_ref, a_ref, eexp_ref, esel_ref, cm_ref, o_ref, h_ref, *, reverse):
    L = SSD_CHUNK
    GW = SSD_HPG * SSD_HD
    nchunk = TM // L
    nt = (((1,), (1,)), ((), ()))
    tn = (((0,), (0,)), ((), ()))
    end = 0 if reverse else L - 1
    lane0 = SSD_HEADS if reverse else 0

    @pl.when(pl.program_id(1) == 0)
    def _():
        h_ref[...] = jnp.zeros_like(h_ref)

    xs = xbc_ref[:, 0:SSD_INNER].astype(F32)
    bs = xbc_ref[:, SSD_INNER:SSD_INNER + SSD_BC]
    cs = xbc_ref[:, SSD_INNER + SSD_BC:SSD_CONV_CH]
    dt = _softplus(dt_ref[...].astype(F32) + bias_ref[...])
    a = dt * a_ref[...]
    cm = cm_ref[...]
    cum = sum(jnp.dot(cm, part, preferred_element_type=F32) for part in _split3(a))
    yield
    cum_t = sum(lax.dot_general(esel_ref[...], part, nt, preferred_element_type=F32)
                for part in _split3(cum))
    cend = _chunk_rows(cum, L, end)
    eexp = eexp_ref[...]

    def expand(m):
        return jnp.dot(m.astype(BF16), eexp, preferred_element_type=F32)

    xdt = xs * expand(dt)
    xdt_b = xdt.astype(BF16)
    yield
    xend = (xdt * expand(jnp.exp(cend - cum))).astype(BF16)
    yield
    ecum = expand(jnp.exp(cum))
    tot = jnp.concatenate([cum[ch * L + end:ch * L + end + 1, :] for ch in range(nchunk)]
                          + [jnp.zeros((ROWG - nchunk, LANES), F32)], axis=0)
    edec = jnp.exp(sum(jnp.dot(part, eexp, preferred_element_type=F32) for part in _split3(tot)))

    tri = _tri(L, reverse)
    low = lax.broadcasted_iota(jnp.int32, (L, LANES), 1) < SSD_HD
    intra = [[None] * SSD_GROUPS for _ in range(nchunk)]
    dh = [[None] * SSD_GROUPS for _ in range(nchunk)]
    for ch in range(nchunk):
        rows = slice(ch * L, (ch + 1) * L)
        for g in range(SSD_GROUPS):
            bg = bs[rows, g * SSD_STATE:(g + 1) * SSD_STATE]
            cg = cs[rows, g * SSD_STATE:(g + 1) * SSD_STATE]
            cb = lax.dot_general(cg, bg, nt, preferred_element_type=F32)
            pairs = []
            for pp in range(SSD_HPG // 2):
                yh = []
                for u in range(2):
                    h = g * SSD_HPG + 2 * pp + u
                    seg = cum[rows, lane0 + h:lane0 + h + 1] - cum_t[h:h + 1, rows]
                    dec = jnp.exp(jnp.where(tri, seg, -1e30))
                    mat = (cb * dec).astype(BF16)
                    col = (h - u) * SSD_HD
                    yh.append(jnp.dot(mat, xdt_b[rows, col:col + LANES], preferred_element_type=F32))
                pairs.append(jnp.where(low, yh[0], yh[1]))
                yield
            intra[ch][g] = jnp.concatenate(pairs, axis=1)
            dh[ch][g] = lax.dot_general(bg, xend[rows, g * GW:(g + 1) * GW], tn, preferred_element_type=F32)

    hs = [h_ref[g] for g in range(SSD_GROUPS)]
    out = [None] * nchunk
    for ch in (range(nchunk - 1, -1, -1) if reverse else range(nchunk)):
        rows = slice(ch * L, (ch + 1) * L)
        ys = []
        for g in range(SSD_GROUPS):
            cg = cs[rows, g * SSD_STATE:(g + 1) * SSD_STATE]
            y_inter = jnp.dot(cg, hs[g].astype(BF16), preferred_element_type=F32) * ecum[rows, g * GW:(g + 1) * GW]
            ys.append(intra[ch][g] + y_inter)
            hs[g] = hs[g] * edec[ch:ch + 1, g * GW:(g + 1) * GW] + dh[ch][g]
        out[ch] = jnp.concatenate(ys, axis=1)
        yield
    for g in range(SSD_GROUPS):
        h_ref[g] = hs[g]
    o_ref[...] = jnp.concatenate(out, axis=0).astype(o_ref.dtype)


def _ssd_kernel(*refs):
    n_in = (len(refs) - 4) // 2
    o_f, o_b, h_f, h_b = refs[2 * n_in:]
    _alternate(_ssd_dir(*refs[:n_in], o_f, h_f, reverse=False),
               _ssd_dir(*refs[n_in:2 * n_in], o_b, h_b, reverse=True))


def _ssd_scan(xbc3, p3, params):
    nb, t, _ = xbc3.shape
    nblk = t // TM
    const = lambda b, i: (0, 0)
    in_specs, args, out_specs = [], [], []
    for reverse in (False, True):
        blk = functools.partial(_scan_block, nblk=nblk, reverse=reverse)
        bias_pad, a_pad, eexp, esel = params[int(reverse)]
        in_specs += [pl.BlockSpec((None, TM, SSD_CONV_CH), lambda b, i, blk=blk: (b, blk(i), 0)),
                     pl.BlockSpec((None, TM, LANES), lambda b, i, blk=blk: (b, blk(i), 16)),
                     pl.BlockSpec((1, LANES), const), pl.BlockSpec((1, LANES), const),
                     pl.BlockSpec((LANES, SSD_INNER), const), pl.BlockSpec((16, LANES), const),
                     pl.BlockSpec((TM, TM), const)]
        args += [xbc3, p3, bias_pad, a_pad, eexp.astype(BF16), esel.astype(BF16),
                 jnp.asarray(_chunk_sum_mat(SSD_CHUNK, reverse), BF16)]
        out_specs.append(pl.BlockSpec((None, TM, SSD_INNER), lambda b, i, blk=blk: (b, blk(i), 0)))
    state = pltpu.VMEM((SSD_GROUPS, SSD_STATE, SSD_HPG * SSD_HD), F32)
    return pl.pallas_call(
        _ssd_kernel,
        grid=(nb, nblk),
        in_specs=in_specs,
        out_specs=out_specs,
        out_shape=[jax.ShapeDtypeStruct((nb, t, SSD_INNER), BF16)] * 2,
        scratch_shapes=[state, state],
        compiler_params=_cparams(("arbitrary", "arbitrary")),
        name="ssd_scan",
    )(*args)


def _dft_mats(n):
    k = jnp.arange(n, dtype=jnp.int32)
    ang = ((k[:, None] * k[None, :]) % n).astype(F32) * (2.0 * math.pi / n)
    return jnp.cos(ang), jnp.sin(ang)


def _chan_mats():
    cc, sc = _dft_mats(FNET_GC)
    eye = jnp.eye(FNET_GROUPS, dtype=F32)
    return jnp.kron(eye, cc), jnp.kron(eye, sc)


def _group_rms(x, g, bd):
    ms = jnp.dot(x * x, bd, precision=HI, preferred_element_type=F32)
    return x * lax.rsqrt(ms + EPS) * g


def _split2(x):
    hi = x.astype(BF16)
    return hi, (x - hi.astype(F32)).astype(BF16)


def _mm(a, b):
    a_hi, a_lo = _split2(a)
    b_hi, b_lo = _split2(b)
    dot = functools.partial(jnp.dot, preferred_element_type=F32)
    return dot(a_hi, b_hi) + (dot(a_hi, b_lo) + dot(a_lo, b_hi))


def _ffta_kernel(x_ref, g_ref, cc_ref, sc_ref, c1_ref, s1_ref, twc_ref, tws_ref, yr_ref, yi_ref, *, nb2):
    bd = _seg_ones(FNET_W, FNET_GC)
    for j in range(nb2):
        x = x_ref[:, j * FNET_W:(j + 1) * FNET_W].astype(F32)
        xn = _group_rms(x, g_ref[...], bd)
        vr = _mm(xn, cc_ref[...])
        vi = -_mm(xn, sc_ref[...])
        c1 = c1_ref[...]
        s1 = s1_ref[...]
        yr = _mm(c1, vr) + _mm(s1, vi)
        yi = _mm(c1, vi) - _mm(s1, vr)
        tc = twc_ref[j]
        ts = tws_ref[j]
        yr_ref[j] = yr * tc + yi * ts
        yi_ref[j] = yi * tc - yr * ts


def _fftb_kernel(yr_ref, yi_ref, c2_ref, s2_ref, o_ref, *, scale):
    o_ref[...] = (_mm(c2_ref[...], yr_ref[...]) + _mm(s2_ref[...], yi_ref[...])) * scale


def _fftc_kernel(x_ref, g_ref, cc_ref, sc_ref, ct_ref, st_ref, o_ref, *, scale):
    bd = _seg_ones(FNET_W, FNET_GC)
    xn = _group_rms(x_ref[...].astype(F32), g_ref[...], bd)
    a = _mm(xn, cc_ref[...])
    b = _mm(xn, sc_ref[...])
    o_ref[...] = (_mm(ct_ref[...], a) - _mm(st_ref[...], b)) * scale


def _fourier_latent(f_lat, g256):
    nb, s, _ = f_lat.shape
    n2 = 64
    n1 = s // n2
    nb2 = 8
    ccm, scm = _chan_mats()
    c1, s1 = _dft_mats(n1)
    c2, s2 = _dft_mats(n2)
    t2 = jnp.arange(n2, dtype=jnp.int32)[:, None]
    k1 = jnp.arange(n1, dtype=jnp.int32)[None, :]
    tw = ((t2 * k1) % s).astype(F32) * (2.0 * math.pi / s)
    twc = jnp.cos(tw)[:, :, None]
    tws = jnp.sin(tw)[:, :, None]
    x2 = f_lat.reshape(nb, n1, n2 * FNET_W)
    const2 = lambda b, j: (0, 0)
    yr, yi = pl.pallas_call(
        functools.partial(_ffta_kernel, nb2=nb2),
        grid=(nb, n2 // nb2),
        in_specs=[pl.BlockSpec((None, n1, nb2 * FNET_W), lambda b, j: (b, 0, j)),
                  pl.BlockSpec((1, FNET_W), const2),
                  pl.BlockSpec((FNET_W, FNET_W), const2), pl.BlockSpec((FNET_W, FNET_W), const2),
                  pl.BlockSpec((n1, n1), const2), pl.BlockSpec((n1, n1), const2),
                  pl.BlockSpec((nb2, n1, 1), lambda b, j: (j, 0, 0)),
                  pl.BlockSpec((nb2, n1, 1), lambda b, j: (j, 0, 0))],
        out_specs=[pl.BlockSpec((None, nb2, n1, FNET_W), lambda b, j: (b, j, 0, 0)),
                   pl.BlockSpec((None, nb2, n1, FNET_W), lambda b, j: (b, j, 0, 0))],
        out_shape=[jax.ShapeDtypeStruct((nb, n2, n1, FNET_W), F32),
                   jax.ShapeDtypeStruct((nb, n2, n1, FNET_W), F32)],
        compiler_params=_cparams(("arbitrary", "arbitrary")),
        name="fft_a",
    )(x2, g256, ccm, scm, c1, s1, twc, tws)
    ncol = n1 * FNET_W
    tn = min(2048, ncol)
    out = pl.pallas_call(
        functools.partial(_fftb_kernel, scale=1.0 / math.sqrt(s * FNET_GC)),
        grid=(nb, ncol // tn),
        in_specs=[pl.BlockSpec((None, n2, tn), lambda b, j: (b, 0, j)),
                  pl.BlockSpec((None, n2, tn), lambda b, j: (b, 0, j)),
                  pl.BlockSpec((n2, n2), const2), pl.BlockSpec((n2, n2), const2)],
        out_specs=pl.BlockSpec((None, n2, tn), lambda b, j: (b, 0, j)),
        out_shape=jax.ShapeDtypeStruct((nb, n2, ncol), F32),
        compiler_params=_cparams(("arbitrary", "arbitrary")),
        name="fft_b",
    )(yr.reshape(nb, n2, ncol), yi.reshape(nb, n2, ncol), c2, s2)
    return out.reshape(nb, s, FNET_W)


def _fourier_ctx(f_ctx, g256):
    nb, tc, _ = f_ctx.shape
    ccm, scm = _chan_mats()
    ct, st = _dft_mats(tc)
    const = lambda b: (0, 0)
    return pl.pallas_call(
        functools.partial(_fftc_kernel, scale=1.0 / math.sqrt(tc * FNET_GC)),
        grid=(nb,),
        in_specs=[pl.BlockSpec((None, tc, FNET_W), lambda b: (b, 0, 0)),
                  pl.BlockSpec((1, FNET_W), const),
                  pl.BlockSpec((FNET_W, FNET_W), const), pl.BlockSpec((FNET_W, FNET_W), const),
                  pl.BlockSpec((tc, tc), const), pl.BlockSpec((tc, tc), const)],
        out_specs=pl.BlockSpec((None, tc, FNET_W), lambda b: (b, 0, 0)),
        out_shape=jax.ShapeDtypeStruct((nb, tc, FNET_W), F32),
        compiler_params=_cparams(("arbitrary",)),
        name="fft_ctx",
    )(f_ctx, g256, ccm, scm, ct, st)


def _odout_kernel(yf_ref, yb_ref, xs_ref, z_ref, f_ref, dsk_ref, g_ref, w1_ref, w2_ref, x_ref, mod_ref,
                  lg_ref, lb_ref, o_ref):
    GW = SSD_HPG * SSD_HD
    y = yf_ref[...].astype(F32) + yb_ref[...].astype(F32) + xs_ref[...].astype(F32) * dsk_ref[...]
    y = y * _silu(z_ref[...].astype(F32))
    parts = []
    for g in range(SSD_GROUPS):
        yg = y[:, g * GW:(g + 1) * GW]
        ms = jnp.mean(yg * yg, axis=-1, keepdims=True)
        parts.append(yg * lax.rsqrt(ms + EPS))
    yn = (jnp.concatenate(parts, axis=1) * g_ref[...]).astype(BF16)
    o = jnp.dot(yn, w1_ref[...], preferred_element_type=F32)
    o = o + jnp.dot(f_ref[...].astype(BF16), w2_ref[...], preferred_element_type=F32)
    gate = mod_ref[0, 2:3, :]
    o_ref[...] = _ln(ALPHA * x_ref[...] + gate * o, lg_ref[...], lb_ref[...])


def _odout(yf2, yb2, xbc2, p2, f2, dsk, g768, w1, w2, x2, mod, lg, lb, tps, nb):
    ntok, d = x2.shape
    row = lambda i: (i, 0)
    const = lambda i: (0, 0)
    return pl.pallas_call(
        _odout_kernel,
        grid=(ntok // TM,),
        in_specs=[pl.BlockSpec((TM, SSD_INNER), row), pl.BlockSpec((TM, SSD_INNER), row),
                  pl.BlockSpec((TM, SSD_INNER), row), pl.BlockSpec((TM, SSD_INNER), row),
                  pl.BlockSpec((TM, FNET_W), row),
                  pl.BlockSpec((1, SSD_INNER), const), pl.BlockSpec((1, SSD_INNER), const),
                  pl.BlockSpec((SSD_INNER, d), const), pl.BlockSpec((FNET_W, d), const),
                  pl.BlockSpec((TM, d), row),
                  pl.BlockSpec((1, 6, d), lambda i: (_mod_row(i, tps, nb), 0, 0)),
                  pl.BlockSpec((1, d), const), pl.BlockSpec((1, d), const)],
        out_specs=pl.BlockSpec((TM, d), row),
        out_shape=jax.ShapeDtypeStruct((ntok, d), F32),
        compiler_params=_cparams(("arbitrary",)),
        name="odd_out",
    )(yf2, yb2, xbc2, p2, f2, dsk, g768, w1, w2, x2, mod, lg, lb)


def _router_kernel(x_ref, mod_ref, rw_ref, rb_ref, v_ref, idx_ref, gw_ref, cnt_ref):
    @pl.when(pl.program_id(0) == 0)
    def _():
        cnt_ref[...] = jnp.zeros_like(cnt_ref)

    shift = mod_ref[0, 3:4, :]
    scale = mod_ref[0, 4:5, :]
    v = x_ref[...] * (1.0 + scale) + shift
    v_ref[...] = v.astype(v_ref.dtype)
    logits = lax.dot_general(rw_ref[...], v, (((1,), (1,)), ((), ())), precision=HI,
                             preferred_element_type=F32)
    s = 1.0 / (1.0 + jnp.exp(-logits))
    sel = s + rb_ref[:, 0:1]
    izero = jnp.zeros((1, TM), jnp.int32)
    best = None
    for g in range(N_EXPERT_GROUPS):
        a = [sel[g * EXPERTS_PER_GROUP + j:g * EXPERTS_PER_GROUP + j + 1, :] for j in range(EXPERTS_PER_GROUP)]
        sv = [s[g * EXPERTS_PER_GROUP + j:g * EXPERTS_PER_GROUP + j + 1, :] for j in range(EXPERTS_PER_GROUP)]
        m1, i1, s1 = a[0], izero, sv[0]
        for j in range(1, EXPERTS_PER_GROUP):
            gt = a[j] > m1
            m1 = jnp.where(gt, a[j], m1)
            i1 = jnp.where(gt, j, i1)
            s1 = jnp.where(gt, sv[j], s1)
        m2 = jnp.full((1, TM), -jnp.inf, F32)
        i2, s2 = izero, jnp.zeros((1, TM), F32)
        for j in range(EXPERTS_PER_GROUP):
            gt = jnp.logical_and(i1 != j, a[j] > m2)
            m2 = jnp.where(gt, a[j], m2)
            i2 = jnp.where(gt, j, i2)
            s2 = jnp.where(gt, sv[j], s2)
        cand = (m1 + m2, i1 + g * EXPERTS_PER_GROUP, i2 + g * EXPERTS_PER_GROUP, s1, s2)
        if best is None:
            best = cand
        else:
            gt = cand[0] > best[0]
            best = tuple(jnp.where(gt, cn, bs) for cn, bs in zip(cand, best))
    _, e1, e2, w1, w2 = best
    tot = w1 + w2
    idx_ref[0] = jnp.concatenate([e1, e2], axis=0)
    gw_ref[0] = jnp.concatenate([w1 / tot, w2 / tot], axis=0)
    eio = lax.broadcasted_iota(jnp.int32, (N_EXPERTS, TM), 0)
    oh = jnp.logical_or(eio == e1, eio == e2).astype(F32)
    cnt_ref[...] += _ceil_rows(jnp.sum(oh, axis=1, keepdims=True))


def _router(x2, mod, rw_t, rb, tps, nb):
    ntok, d = x2.shape
    nt = ntok // TM
    return pl.pallas_call(
        _router_kernel,
        grid=(nt,),
        in_specs=[pl.BlockSpec((TM, d), lambda i: (i, 0)),
                  pl.BlockSpec((1, 6, d), lambda i: (_mod_row(i, tps, nb), 0, 0)),
                  pl.BlockSpec((N_EXPERTS, d), lambda i: (0, 0)),
                  pl.BlockSpec((N_EXPERTS, LANES), lambda i: (0, 0))],
        out_specs=[pl.BlockSpec((TM, d), lambda i: (i, 0)),
                   pl.BlockSpec((1, TOP_K, TM), lambda i: (i, 0, 0)),
                   pl.BlockSpec((1, TOP_K, TM), lambda i: (i, 0, 0)),
                   pl.BlockSpec((N_EXPERTS, LANES), lambda i: (0, 0))],
        out_shape=[jax.ShapeDtypeStruct((ntok, d), BF16),
                   jax.ShapeDtypeStruct((nt, TOP_K, TM), jnp.int32),
                   jax.ShapeDtypeStruct((nt, TOP_K, TM), F32),
                   jax.ShapeDtypeStruct((N_EXPERTS, LANES), F32)],
        compiler_params=_cparams(("arbitrary",)),
        name="router",
    )(x2, mod, rw_t, rb)


def _ceil_rows(c):
    return jnp.ceil(c * (1.0 / ROWG)) * ROWG


def _meta_kernel(idx_ref, start_ref, qpos_ref, meta_ref, run_ref):
    @pl.when(pl.program_id(0) == 0)
    def _():
        run_ref[...] = jnp.zeros_like(run_ref)

    e1 = idx_ref[0, 0:1, :]
    e2 = idx_ref[0, 1:2, :]
    eio = lax.broadcasted_iota(jnp.int32, (N_EXPERTS, TM), 0)
    oh1 = eio == e1
    oh2 = eio == e2
    oh = jnp.logical_or(oh1, oh2)
    r = lax.broadcasted_iota(jnp.int32, (TM, TM), 0)
    c = lax.broadcasted_iota(jnp.int32, (TM, TM), 1)
    before = (r < c).astype(BF16)
    rank = jnp.dot(oh.astype(BF16), before, preferred_element_type=F32)
    run_len = jnp.broadcast_to(_ceil_rows(jnp.sum(oh.astype(F32), axis=1, keepdims=True)), (N_EXPERTS, LANES))
    er = lax.broadcasted_iota(jnp.int32, (N_EXPERTS, N_EXPERTS), 0)
    ec = lax.broadcasted_iota(jnp.int32, (N_EXPERTS, N_EXPERTS), 1)
    off = jnp.dot((ec < er).astype(F32), run_len, precision=HI, preferred_element_type=F32)
    pos = rank + off[:, 0:1]
    q1 = jnp.sum(jnp.where(oh1, pos, 0.0), axis=0, keepdims=True)
    q2 = jnp.sum(jnp.where(oh2, pos, 0.0), axis=0, keepdims=True)
    qpos_ref[0] = jnp.concatenate([q1, q2], axis=0).astype(jnp.int32)
    lane = lax.broadcasted_iota(jnp.int32, (N_EXPERTS, LANES), 1)
    meta = jnp.where(lane == 0, start_ref[...] + run_ref[...], jnp.where(lane == 1, off, run_len))
    meta_ref[0] = (meta * (1.0 / ROWG)).astype(jnp.int32)
    run_ref[...] += run_len


def _meta(idx_t, start):
    nt = idx_t.shape[0]
    return pl.pallas_call(
        _meta_kernel,
        grid=(nt,),
        in_specs=[pl.BlockSpec((1, TOP_K, TM), lambda i: (i, 0, 0)),
                  pl.BlockSpec((N_EXPERTS, LANES), lambda i: (0, 0))],
        out_specs=[pl.BlockSpec((1, TOP_K, TM), lambda i: (i, 0, 0)),
                   pl.BlockSpec((1, N_EXPERTS, LANES), lambda i: (i, 0, 0))],
        out_shape=[jax.ShapeDtypeStruct((nt, TOP_K, TM), jnp.int32),
                   jax.ShapeDtypeStruct((nt, N_EXPERTS, LANES), jnp.int32)],
        scratch_shapes=[pltpu.VMEM((N_EXPERTS, LANES), F32)],
        compiler_params=_cparams(("arbitrary",)),
        name="moe_meta",
    )(idx_t, start)


def _sort_select(qpos_ref):
    r = lax.broadcasted_iota(jnp.int32, (SORT_ROWS, TM), 0)
    return r == qpos_ref[0, 0:1, :], r == qpos_ref[0, 1:2, :]


def _start_run_copies(i, gb_ref, ob_ref, nb_ref, make_copy):
    big = COPY_GROUPS * ROWG
    for e in range(N_EXPERTS):
        n = nb_ref[i * N_EXPERTS + e]
        buf_row = gb_ref[i * N_EXPERTS + e] * ROWG
        tile_row = ob_ref[i * N_EXPERTS + e] * ROWG
        n_big = lax.shift_right_logical(n, COPY_SHIFT)

        def body_big(k, carry, buf_row=buf_row, tile_row=tile_row):
            make_copy(pl.multiple_of(tile_row + k * big, ROWG), pl.multiple_of(buf_row + k * big, ROWG), big).start()
            return carry

        def body_one(k, carry, buf_row=buf_row, tile_row=tile_row):
            make_copy(pl.multiple_of(tile_row + k * ROWG, ROWG), pl.multiple_of(buf_row + k * ROWG, ROWG),
                      ROWG).start()
            return carry

        lax.fori_loop(0, n_big, body_big, 0)
        lax.fori_loop(lax.shift_left(n_big, COPY_SHIFT), n, body_one, 0)


def _wait_run_copies(i, nb_ref, make_copy):
    n_big = 0
    n_one = 0
    for e in range(N_EXPERTS):
        n = nb_ref[i * N_EXPERTS + e]
        n_big = n_big + lax.shift_right_logical(n, COPY_SHIFT)
        n_one = n_one + lax.bitwise_and(n, COPY_GROUPS - 1)

    def wait_big(k, carry):
        make_copy(0, 0, COPY_GROUPS * ROWG).wait()
        return carry

    def wait_one(k, carry):
        make_copy(0, 0, ROWG).wait()
        return carry

    lax.fori_loop(0, n_big, wait_big, 0)
    lax.fori_loop(0, n_one, wait_one, 0)


def _scatter_kernel(gb_ref, ob_ref, nb_ref, ts_ref, tn_ref, nu_ref, v_ref, qpos_ref, gw_ref, buf_ref, xs_ref, z_ref,
                    sem, zsem):
    i = pl.program_id(0)
    last = pl.num_programs(0) - 1
    slot = i % 2
    d = v_ref.shape[1]
    m0, m1 = _sort_select(qpos_ref)
    sel = jnp.logical_or(m0, m1).astype(BF16)
    xs_ref[slot, :, 0:d] = jnp.dot(sel, v_ref[...], preferred_element_type=F32)
    gate = jnp.sum(jnp.where(m0, gw_ref[0, 0:1, :], 0.0) + jnp.where(m1, gw_ref[0, 1:2, :], 0.0),
                   axis=1, keepdims=True)
    xs_ref[slot, :, d:d + LANES] = jnp.broadcast_to(gate, (SORT_ROWS, LANES))

    def copy_from(s):
        def copy(tile_row, buf_row, rows):
            return pltpu.make_async_copy(xs_ref.at[s, pl.ds(tile_row, rows)], buf_ref.at[pl.ds(buf_row, rows)],
                                         sem.at[s])
        return copy

    _start_run_copies(i, gb_ref, ob_ref, nb_ref, copy_from(slot))

    @pl.when(i > 0)
    def _():
        _wait_run_copies(i - 1, nb_ref, copy_from(1 - slot))

    @pl.when(i == last)
    def _():
        _wait_run_copies(i, nb_ref, copy_from(slot))

    @pl.when(i == last)
    def _():
        z_ref[...] = jnp.zeros_like(z_ref)

        def zcopy(buf_row):
            return pltpu.make_async_copy(z_ref.at[pl.ds(0, ROWG)], buf_ref.at[pl.ds(buf_row, ROWG)], zsem)

        total = 0
        for e in range(N_EXPERTS):
            n = tn_ref[e]
            row0 = ts_ref[e] * ROWG

            def body(k, carry, row0=row0):
                zcopy(pl.multiple_of(row0 + k * ROWG, ROWG)).start()
                return carry

            lax.fori_loop(0, n, body, 0)
            total = total + n

        def wbody(k, carry):
            zcopy(0).wait()
            return carry

        lax.fori_loop(0, total, wbody, 0)

        def zblock(blk):
            return pltpu.make_async_copy(z_ref, buf_ref.at[pl.ds(pl.multiple_of(blk * FFN_ROWS, FFN_ROWS), FFN_ROWS)],
                                         zsem)

        def bbody(blk, carry):
            zblock(blk).start()
            return carry

        def bwait(blk, carry):
            zblock(blk).wait()
            return carry

        nblk = buf_ref.shape[0] // FFN_ROWS
        lax.fori_loop(nu_ref[0], nblk, bbody, 0)
        lax.fori_loop(nu_ref[0], nblk, bwait, 0)


def _scatter(gb, ob, nb8, tail_s, tail_n, n_used, v, qpos, gw_t, nrow):
    ntok, d = v.shape
    grid_spec = pltpu.PrefetchScalarGridSpec(
        num_scalar_prefetch=6,
        grid=(ntok // TM,),
        in_specs=[pl.BlockSpec((TM, d), lambda i, *_: (i, 0)),
                  pl.BlockSpec((1, TOP_K, TM), lambda i, *_: (i, 0, 0)),
                  pl.BlockSpec((1, TOP_K, TM), lambda i, *_: (i, 0, 0))],
        out_specs=pl.BlockSpec(memory_space=pl.ANY),
        scratch_shapes=[pltpu.VMEM((2, SORT_ROWS, d + LANES), F32), pltpu.VMEM((FFN_ROWS, d + LANES), F32),
                        pltpu.SemaphoreType.DMA((2,)), pltpu.SemaphoreType.DMA(())],
    )
    return pl.pallas_call(
        _scatter_kernel,
        grid_spec=grid_spec,
        out_shape=jax.ShapeDtypeStruct((nrow, d + LANES), F32),
        compiler_params=_cparams(("arbitrary",)),
        name="moe_scatter",
    )(gb, ob, nb8, tail_s, tail_n, n_used, v, qpos, gw_t)


def _moe_kernel(be_ref, nu_ref, x_ref, wg_ref, wu_ref, wd_ref, o_ref, wg_bf, wu_bf, wd_bf):
    i = pl.program_id(0)
    d = o_ref.shape[1]

    @pl.when(jnp.logical_or(i == 0, be_ref[i] != be_ref[jnp.maximum(i - 1, 0)]))
    def _():
        wg_bf[...] = wg_ref[...].astype(BF16)
        wu_bf[...] = wu_ref[...].astype(BF16)
        wd_bf[...] = wd_ref[...].astype(BF16)

    @pl.when(i < nu_ref[0])
    def _():
        xb = x_ref[...]
        x = xb[:, 0:d].astype(BF16)
        g = jnp.dot(x, wg_bf[...], preferred_element_type=F32)
        u = jnp.dot(x, wu_bf[...], preferred_element_type=F32)
        h = (_silu(g) * u).astype(BF16)
        y = jnp.dot(h, wd_bf[...], preferred_element_type=F32)
        o_ref[...] = y * xb[:, d:d + 1]

    @pl.when(pl.program_id(0) >= nu_ref[0])
    def _():
        o_ref[...] = jnp.zeros_like(o_ref)


def _moe_ffn(block_e, n_used, buf, layer, wg, wu, wd):
    nrow, dw = buf.shape
    d = dw - LANES
    nblk = nrow // FFN_ROWS
    de = wg.shape[3]
    used = lambda i, nu: jnp.minimum(i, nu[0] - 1)
    grid_spec = pltpu.PrefetchScalarGridSpec(
        num_scalar_prefetch=2,
        grid=(nblk,),
        in_specs=[pl.BlockSpec((FFN_ROWS, dw), lambda i, be, nu: (used(i, nu), 0)),
                  pl.BlockSpec((None, None, d, de), lambda i, be, nu: (layer, be[i], 0, 0)),
                  pl.BlockSpec((None, None, d, de), lambda i, be, nu: (layer, be[i], 0, 0)),
                  pl.BlockSpec((None, None, de, d), lambda i, be, nu: (layer, be[i], 0, 0))],
        out_specs=pl.BlockSpec((FFN_ROWS, d), lambda i, be, nu: (i, 0)),
        scratch_shapes=[pltpu.VMEM((d, de), BF16), pltpu.VMEM((d, de), BF16), pltpu.VMEM((de, d), BF16)],
    )
    return pl.pallas_call(
        _moe_kernel,
        grid_spec=grid_spec,
        out_shape=jax.ShapeDtypeStruct((nrow, d), F32),
        compiler_params=pltpu.CompilerParams(dimension_semantics=("arbitrary",), vmem_limit_bytes=FFN_VMEM_LIMIT),
        name="moe_ffn",
    )(block_e, n_used, buf, wg, wu, wd)


def _ln2_kernel(gb_ref, ob_ref, nb_ref, x_ref, qpos_ref, mod_ref, lg_ref, lb_ref, y_ref, o_ref, ys_ref, sem):
    i = pl.program_id(0)
    slot = i % 2

    def copy_to(s):
        def copy(tile_row, buf_row, rows):
            return pltpu.make_async_copy(y_ref.at[pl.ds(buf_row, rows)], ys_ref.at[s, pl.ds(tile_row, rows)],
                                         sem.at[s])
        return copy

    @pl.when(i == 0)
    def _():
        ys_ref[...] = jnp.zeros_like(ys_ref)
        _start_run_copies(0, gb_ref, ob_ref, nb_ref, copy_to(0))

    @pl.when(i + 1 < pl.num_programs(0))
    def _():
        _start_run_copies(i + 1, gb_ref, ob_ref, nb_ref, copy_to(1 - slot))

    _wait_run_copies(i, nb_ref, copy_to(slot))
    m0, m1 = _sort_select(qpos_ref)
    sel = jnp.logical_or(m0, m1).astype(BF16)
    y = lax.dot_general(sel, ys_ref[slot].astype(BF16), (((0,), (0,)), ((), ())), preferred_element_type=F32)
    gate = mod_ref[0, 5:6, :]
    o_ref[...] = _ln(ALPHA * x_ref[...] + gate * y, lg_ref[...], lb_ref[...])


def _ln2(gb, ob, nb8, x2, qpos, mod, lg, lb, ybuf, tps, nb, latent_only):
    ntok, d = x2.shape
    if latent_only:
        out_map = lambda i, *_: ((i // tps) * (tps - 1) + jnp.maximum(i % tps - 1, 0), 0)
        out_rows = ntok - nb * TM
    else:
        out_map = lambda i, *_: (i, 0)
        out_rows = ntok
    grid_spec = pltpu.PrefetchScalarGridSpec(
        num_scalar_prefetch=3,
        grid=(ntok // TM,),
        in_specs=[pl.BlockSpec((TM, d), lambda i, *_: (i, 0)),
                  pl.BlockSpec((1, TOP_K, TM), lambda i, *_: (i, 0, 0)),
                  pl.BlockSpec((1, 6, d), lambda i, *_: (_mod_row(i, tps, nb), 0, 0)),
                  pl.BlockSpec((1, d), lambda i, *_: (0, 0)),
                  pl.BlockSpec((1, d), lambda i, *_: (0, 0)),
                  pl.BlockSpec(memory_space=pl.ANY)],
        out_specs=pl.BlockSpec((TM, d), out_map),
        scratch_shapes=[pltpu.VMEM((2, SORT_ROWS, d), F32), pltpu.SemaphoreType.DMA((2,))],
    )
    return pl.pallas_call(
        _ln2_kernel,
        grid_spec=grid_spec,
        out_shape=jax.ShapeDtypeStruct((out_rows, d), F32),
        compiler_params=_cparams(("arbitrary",)),
        name="post_moe_ln",
    )(gb, ob, nb8, x2, qpos, mod, lg, lb, ybuf)


def _even_w_in(w):
    o = np.cumsum((0, GLA_QK, GLA_QK, GLA_V, GLA_V, GLA_RANK, GLA_RANK, GQA_Q, GQA_KV, GQA_KV))
    q, k, v, r = w[:, o[0]:o[1]], w[:, o[1]:o[2]], w[:, o[2]:o[3]], w[:, o[3]:o[4]]
    lr = w[:, o[4]:o[6]]
    gq, gk, gv = w[:, o[6]:o[7]], w[:, o[7]:o[8]], w[:, o[8]:o[9]]
    d = w.shape[0]
    zeros = lambda n: jnp.zeros((d, n), w.dtype)
    out = jnp.concatenate([gq, gk, gv, q, k, lr, zeros(LANES - 2 * GLA_RANK), zeros(LANES), v, r], axis=1)
    assert out.shape[1] == EVEN_W
    return out.astype(BF16)


def _odd_w_in(w):
    o = np.cumsum((0, SSD_INNER, SSD_INNER, SSD_BC, SSD_BC, SSD_HEADS, SSD_HEADS, FNET_W))
    d = w.shape[0]
    out = jnp.concatenate([w[:, :o[4]], w[:, o[4]:o[6]], jnp.zeros((d, LANES - 2 * SSD_HEADS), w.dtype),
                           w[:, o[6]:o[7]]], axis=1)
    assert out.shape[1] == ODD_W
    return out.astype(BF16)


def _rope_tables(ctx_len, seq):
    rows = seq // GRID_W
    r = jnp.repeat(jnp.arange(rows, dtype=F32), GRID_W)
    col = jnp.tile(jnp.arange(GRID_W, dtype=F32), rows)
    half = GQA_HD // 2
    inv = ROPE_THETA ** (-jnp.arange(0, half, 2, dtype=F32) / half)
    ar = r[:, None] * inv
    ac = col[:, None] * inv
    ang = jnp.concatenate([ar, ar, ac, ac], -1)
    cos = jnp.concatenate([jnp.ones((ctx_len, GQA_HD), F32), jnp.cos(ang)], 0)
    sin = jnp.concatenate([jnp.zeros((ctx_len, GQA_HD), F32), jnp.sin(ang)], 0)
    return jnp.tile(cos, (1, 2)), jnp.tile(sin, (1, 2))


def _even_layer(x2, mod, nb, t, tps, ctx_len, w_in, w_o, w_dec, b_dec, gla_g, qn_g, kn_g, cos2, sin2, lg, lb):
    p2 = _inproj(x2, mod, _even_w_in(w_in), tps, nb)
    p3 = p2.reshape(nb, t, EVEN_W)
    wd_pads = [jnp.zeros((LANES, GLA_QK), F32).at[di * GLA_RANK:(di + 1) * GLA_RANK].set(w_dec[di])
               for di in range(2)]
    outs = _gla_scan(p3, wd_pads, [b_dec[di][None, :] for di in range(2)])
    qh2, kg2, vg2 = _qkprep(p2, cos2, sin2, jnp.tile(qn_g, 2)[None, :], jnp.tile(kn_g, 2)[None, :], tps)
    att = _attention(qh2.reshape(nb, t, GQA_HEADS * LANES), kg2.reshape(nb, t, GQA_KV_HEADS * LANES),
                     vg2.reshape(nb, t, GQA_KV_HEADS * LANES), ctx_len)
    w1, w2 = w_o[:GLA_V].astype(BF16), w_o[GLA_V:].astype(BF16)
    ntok = nb * t
    return _evout(outs[0].reshape(ntok, GLA_V), outs[1].reshape(ntok, GLA_V), p2, att.reshape(ntok, GQA_Q),
                  jnp.tile(gla_g, GLA_HEADS)[None, :], w1, w2, x2, mod, lg, lb, tps, nb)


def _odd_layer(x2, mod, nb, t, tps, ctx_len, w_in, w_o, conv_w, conv_b, dt_bias, a_log, d_skip, ssd_g, fnet_g,
               lg, lb):
    p2 = _inproj(x2, mod, _odd_w_in(w_in), tps, nb)
    p3 = p2.reshape(nb, t, ODD_W)
    w8 = jnp.zeros((8, SSD_CONV_CH), F32).at[:SSD_CONV].set(conv_w)
    xbc2 = _conv(p2, w8, conv_b[None, :], tps)
    xbc3 = xbc2.reshape(nb, t, SSD_CONV_CH)
    params = []
    heads = jnp.arange(SSD_HEADS)
    for di in range(2):
        lanes = di * SSD_HEADS + heads
        bias_pad = jnp.zeros((1, LANES), F32).at[0, lanes].set(dt_bias[di])
        a_pad = jnp.zeros((1, LANES), F32).at[0, lanes].set(-jnp.exp(a_log[di]))
        eexp = jnp.zeros((LANES, SSD_INNER), F32).at[jnp.repeat(lanes, SSD_HD), jnp.arange(SSD_INNER)].set(1.0)
        esel = jnp.zeros((16, LANES), F32).at[heads, lanes].set(1.0)
        params.append((bias_pad, a_pad, eexp, esel))
    ys = _ssd_scan(xbc3, p3, params)
    f3 = p3[:, :, ODD_W - FNET_W:]
    g256 = fnet_g[None, :]
    fmix = jnp.concatenate([_fourier_ctx(f3[:, :ctx_len], g256), _fourier_latent(f3[:, ctx_len:], g256)], axis=1)
    ntok = nb * t
    return _odout(ys[0].reshape(ntok, SSD_INNER), ys[1].reshape(ntok, SSD_INNER), xbc2, p2,
                  fmix.reshape(ntok, FNET_W), jnp.repeat(d_skip, SSD_HD)[None, :], ssd_g[None, :],
                  w_o[:SSD_INNER].astype(BF16), w_o[SSD_INNER:].astype(BF16), x2, mod, lg, lb, tps, nb)


def _moe_layer(x2, mod, nb, tps, rw_t, rb, layer, wg, wu, wd, lg, lb):
    ntok, d = x2.shape
    v, idx_t, gw_t, cnt = _router(x2, mod, rw_t, rb, tps, nb)
    nt = ntok // TM
    rows = cnt[:, 0].astype(jnp.int32)
    padded = (rows + FFN_ROWS - 1) // FFN_ROWS * FFN_ROWS
    pad_end = jnp.cumsum(padded)
    pad_start = pad_end - padded
    nblk = -(-(ntok * TOP_K + nt * N_EXPERTS * (ROWG - 1)) // FFN_ROWS) + N_EXPERTS
    blk_row = jnp.arange(nblk, dtype=jnp.int32)[:, None] * FFN_ROWS
    block_e = jnp.minimum(jnp.sum((blk_row >= pad_end[None, :]).astype(jnp.int32), axis=1), N_EXPERTS - 1)
    n_used = (pad_end[-1:] // FFN_ROWS).astype(jnp.int32)
    start = jnp.broadcast_to(pad_start.astype(F32)[:, None], (N_EXPERTS, LANES))
    qpos, meta = _meta(idx_t, start)
    gb, ob, nb8 = (meta[:, :, j].reshape(nt * N_EXPERTS) for j in range(3))
    tail_s = ((pad_start + rows) // ROWG).astype(jnp.int32)
    tail_n = ((padded - rows) // ROWG).astype(jnp.int32)
    buf = _scatter(gb, ob, nb8, tail_s, tail_n, n_used, v, qpos, gw_t, nblk * FFN_ROWS)
    y = _moe_ffn(block_e, n_used, buf, layer, wg, wu, wd)
    return _ln2(gb, ob, nb8, x2, qpos, mod, lg, lb, y, tps, nb, latent_only=(layer == DEPTH - 1))


def kernel(x, c, ctx, c_ctx, ada_w, ada_b, ln_g, ln_b, ev_w_in, ev_w_o, gla_w_decay, gla_b_decay, gla_norm_g,
           gqa_q_norm_g, gqa_k_norm_g, od_w_in, od_w_o, ssd_conv_w, ssd_conv_b, ssd_dt_bias, ssd_a_log, ssd_d,
           ssd_norm_g, fnet_norm_g, router_w, router_b, exp_w_gate, exp_w_up, exp_w_down):
    nb, seq, d = x.shape
    ctx_len = ctx.shape[1]
    assert ctx_len == TM and seq % TM == 0 and d == D_MODEL and nb <= 7
    t = ctx_len + seq
    tps = t // TM
    ntok = nb * t
    cc = jnp.zeros((8, d), F32).at[:nb].set(c).at[nb].set(c_ctx)
    mod_all = _ada_all(cc, ada_w, ada_b).reshape(DEPTH, 8, 6, d)
    cos2, sin2 = _rope_tables(ctx_len, seq)
    rw_t = router_w.T
    rb = jnp.broadcast_to(router_b[:, None], (N_EXPERTS, LANES))
    x2 = jnp.concatenate([ctx, x], axis=1).reshape(ntok, d)
    for layer in range(DEPTH):
        mod = mod_all[layer]
        i = layer // 2
        lg0, lb0 = ln_g[layer, 0][None, :], ln_b[layer, 0][None, :]
        lg1, lb1 = ln_g[layer, 1][None, :], ln_b[layer, 1][None, :]
        if layer % 2 == 0:
            x2 = _even_layer(x2, mod, nb, t, tps, ctx_len, ev_w_in[i], ev_w_o[i], gla_w_decay[i], gla_b_decay[i],
                             gla_norm_g[i], gqa_q_norm_g[i], gqa_k_norm_g[i], cos2, sin2, lg0, lb0)
        else:
            x2 = _odd_layer(x2, mod, nb, t, tps, ctx_len, od_w_in[i], od_w_o[i], ssd_conv_w[i], ssd_conv_b[i],
                            ssd_dt_bias[i], ssd_a_log[i], ssd_d[i], ssd_norm_g[i], fnet_norm_g[i], lg0, lb0)
        x2 = _moe_layer(x2, mod, nb, tps, rw_t, rb, layer, exp_w_gate, exp_w_up, exp_w_down, lg1, lb1)
    return x2.reshape(nb, seq, d)
```

```python
import functools
import itertools
import math

import jax
import jax.numpy as jnp
import numpy as np
from jax import lax
from jax.experimental import pallas as pl
from jax.experimental.pallas import tpu as pltpu

F32 = jnp.float32
BF16 = jnp.bfloat16
HI = lax.Precision.HIGHEST

D_MODEL = 1024
DEPTH = 4
GRID_W = 64
GLA_HEADS, GLA_DK, GLA_DV, GLA_RANK, GLA_TAU, GLA_CHUNK = 4, 64, 128, 16, 16.0, 64
GQA_HEADS, GQA_KV_HEADS, GQA_HD = 8, 2, 64
ROPE_THETA = 10000.0
SSD_HEADS, SSD_HD, SSD_GROUPS, SSD_STATE, SSD_CONV, SSD_CHUNK = 12, 64, 2, 128, 5, 64
SSD_HPG = SSD_HEADS // SSD_GROUPS
SSD_INNER = SSD_HEADS * SSD_HD
SSD_BC = SSD_GROUPS * SSD_STATE
SSD_CONV_CH = SSD_INNER + 2 * SSD_BC
FNET_GROUPS, FNET_GC = 4, 64
FNET_W = FNET_GROUPS * FNET_GC
N_EXPERTS, N_EXPERT_GROUPS, TOP_K, D_EXPERT = 16, 4, 2, 768
EXPERTS_PER_GROUP = N_EXPERTS // N_EXPERT_GROUPS
GLA_QK = GLA_HEADS * GLA_DK
GLA_V = GLA_HEADS * GLA_DV
GQA_Q = GQA_HEADS * GQA_HD
GQA_KV = GQA_KV_HEADS * GQA_HD
EPS = 1e-6
ALPHA = (2.0 * DEPTH) ** 0.25

LANES = 128
TM = 256
EVEN_W = 2560
ODD_W = 2432
VMEM_LIMIT = 48 * 1024 * 1024
LOG2E = 1.4426950408889634
ATT_TQ = 256
ATT_HEADROOM = 64.0
ATT_MIN_DENOM = 2.0 ** -60
ROWG = 8
SORT_ROWS = TOP_K * TM + N_EXPERTS * (ROWG - 1)
COPY_SHIFT = 2
COPY_GROUPS = 1 << COPY_SHIFT
FFN_ROWS = 512
FFN_VMEM_LIMIT = 56 * 1024 * 1024


def _cparams(sem):
    return pltpu.CompilerParams(dimension_semantics=sem, vmem_limit_bytes=VMEM_LIMIT)


def _silu(x):
    return x * (1.0 / (1.0 + jnp.exp(-x)))


def _softplus(x):
    return jnp.maximum(x, 0.0) + jnp.log1p(jnp.exp(-jnp.abs(x)))


def _mod_row(i, tiles_per_seq, n_batch):
    return jnp.where(i % tiles_per_seq == 0, n_batch, i // tiles_per_seq)


def _ada_kernel(c_ref, w_ref, b_ref, o_ref):
    s = _silu(c_ref[...])
    o_ref[0] = jnp.dot(s, w_ref[0], precision=HI, preferred_element_type=F32) + b_ref[0]


def _ada_all(cc, ada_w, ada_b):
    depth, d, n = ada_w.shape
    tn = 512
    return pl.pallas_call(
        _ada_kernel,
        grid=(depth, n // tn),
        in_specs=[pl.BlockSpec((8, d), lambda l, j: (0, 0)),
                  pl.BlockSpec((1, d, tn), lambda l, j: (l, 0, j)),
                  pl.BlockSpec((1, 1, tn), lambda l, j: (l, 0, j))],
        out_specs=pl.BlockSpec((1, 8, tn), lambda l, j: (l, 0, j)),
        out_shape=jax.ShapeDtypeStruct((depth, 8, n), F32),
        compiler_params=_cparams(("arbitrary", "arbitrary")),
        name="adaln",
    )(cc, ada_w, ada_b.reshape(depth, 1, n))


def _inproj_kernel(x_ref, mod_ref, w_ref, o_ref):
    shift = mod_ref[0, 0:1, :]
    scale = mod_ref[0, 1:2, :]
    u = (x_ref[...] * (1.0 + scale) + shift).astype(BF16)
    o_ref[...] = jnp.dot(u, w_ref[...], preferred_element_type=F32).astype(o_ref.dtype)


def _inproj(x2, mod, w, tps, nb):
    ntok, d = x2.shape
    nw = w.shape[1]
    return pl.pallas_call(
        _inproj_kernel,
        grid=(ntok // TM,),
        in_specs=[pl.BlockSpec((TM, d), lambda i: (i, 0)),
                  pl.BlockSpec((1, 6, d), lambda i: (_mod_row(i, tps, nb), 0, 0)),
                  pl.BlockSpec((d, nw), lambda i: (0, 0))],
        out_specs=pl.BlockSpec((TM, nw), lambda i: (i, 0)),
        out_shape=jax.ShapeDtypeStruct((ntok, nw), BF16),
        compiler_params=_cparams(("arbitrary",)),
        name="inproj",
    )(x2, mod, w)


def _scan_block(i, nblk, reverse):
    if not reverse:
        return i
    return jnp.where(i == 0, 0, nblk - i)


def _tri(n, reverse):
    r = lax.broadcasted_iota(jnp.int32, (n, n), 0)
    c = lax.broadcasted_iota(jnp.int32, (n, n), 1)
    return (c >= r) if reverse else (c <= r)


def _chunk_sum_mat(chunk, reverse):
    l = np.arange(TM)[:, None]
    m = np.arange(TM)[None, :]
    same = (l // chunk) == (m // chunk)
    return (same & ((m >= l) if reverse else (m <= l))).astype(np.float32)


def _split3(x):
    hi = x.astype(BF16)
    r1 = x - hi.astype(F32)
    mid = r1.astype(BF16)
    return hi, mid, (r1 - mid.astype(F32)).astype(BF16)


def _chunk_rows(x, chunk, idx):
    return jnp.concatenate([jnp.broadcast_to(x[c * chunk + idx:c * chunk + idx + 1, :], (chunk, x.shape[1]))
                            for c in range(x.shape[0] // chunk)], axis=0)


def _gla_dir(q_ref, k_ref, v_ref, lr_ref, wd_ref, bd_ref, cm_ref, o_ref, s_ref, *, reverse):
    L = GLA_CHUNK
    nchunk = TM // L
    nt = (((1,), (1,)), ((), ()))
    tn = (((0,), (0,)), ((), ()))
    end = 0 if reverse else L - 1
    mid = L // 2 if reverse else L // 2 - 1

    @pl.when(pl.program_id(2) == 0)
    def _():
        s_ref[...] = jnp.zeros_like(s_ref)

    q = q_ref[...].astype(F32) * (GLA_DK ** -0.5)
    k = k_ref[...].astype(F32)
    v = v_ref[...]
    lr = lr_ref[...]
    z = bd_ref[...] + sum(jnp.dot(lr, wd_ref[j], preferred_element_type=F32) for j in range(3))
    yield
    la = -_softplus(-z) * (1.0 / GLA_TAU)
    cm = cm_ref[...]
    b = sum(jnp.dot(cm, part, preferred_element_type=F32) for part in _split3(la))
    yield
    bmid = _chunk_rows(b, L, mid)
    bend = _chunk_rows(b, L, end)
    qs = q * jnp.exp(b - bmid)
    ks = (k * jnp.exp(bmid - b)).astype(BF16)
    qi = q * jnp.exp(b)
    kend = (k * jnp.exp(bend - b)).astype(BF16)
    r = lax.broadcasted_iota(jnp.int32, (TM, TM), 0)
    c = lax.broadcasted_iota(jnp.int32, (TM, TM), 1)
    mask = jnp.logical_and(r // L == c // L, (c >= r) if reverse else (c <= r))
    lane = lax.broadcasted_iota(jnp.int32, (TM, LANES), 1)
    heads = (lane < GLA_DK, lane >= GLA_DK)
    yield
    att = [lax.dot_general(jnp.where(heads[h], qs, 0.0).astype(BF16), ks, nt, preferred_element_type=F32)
           for h in range(2)]
    yield
    intra = []
    for h in range(2):
        att_h = jnp.where(mask, att[h], 0.0).astype(BF16)
        intra.append(jnp.dot(att_h, v[:, h * GLA_DV:(h + 1) * GLA_DV], preferred_element_type=F32))
    yield
    qi_h = [jnp.where(heads[h], qi, 0.0).astype(BF16) for h in range(2)]
    tot = jnp.concatenate([b[ch * L + end:ch * L + end + 1, :] for ch in range(nchunk)]
                          + [jnp.zeros((ROWG - nchunk, LANES), F32)], axis=0).T
    ds, dec = [], []
    for ch in range(nchunk):
        rows = slice(ch * L, (ch + 1) * L)
        ds.append(lax.dot_general(kend[rows], v[rows], tn, preferred_element_type=F32))
        dec.append(jnp.exp(jnp.broadcast_to(tot[:, ch:ch + 1], (2 * GLA_DK, 2 * GLA_DV))))
    yield
    s = s_ref[...]
    inter = [None] * nchunk
    for ch in (range(nchunk - 1, -1, -1) if reverse else range(nchunk)):
        rows = slice(ch * L, (ch + 1) * L)
        s_bf = s.astype(BF16)
        inter[ch] = jnp.concatenate(
            [jnp.dot(qi_h[h][rows], s_bf[:, h * GLA_DV:(h + 1) * GLA_DV], preferred_element_type=F32)
             for h in range(2)], axis=1)
        s = dec[ch] * s + ds[ch]
        yield
    s_ref[...] = s
    o_ref[...] = (jnp.concatenate(intra, axis=1) + jnp.concatenate(inter, axis=0)).astype(o_ref.dtype)


def _alternate(*stage_generators):
    for _ in itertools.zip_longest(*stage_generators):
        pass


def _gla_kernel(*refs):
    n_in = (len(refs) - 4) // 2
    o_f, o_b, s_f, s_b = refs[2 * n_in:]
    _alternate(_gla_dir(*refs[:n_in], o_f, s_f, reverse=False),
               _gla_dir(*refs[n_in:2 * n_in], o_b, s_b, reverse=True))


def _gla_scan(p3, wd_pads, bds):
    nb, t, _ = p3.shape
    nblk = t // TM
    in_specs, args, out_specs = [], [], []
    for reverse in (False, True):
        blk = functools.partial(_scan_block, nblk=nblk, reverse=reverse)
        in_specs += [pl.BlockSpec((None, TM, LANES), lambda b, p, i, blk=blk: (b, blk(i), 6 + p)),
                     pl.BlockSpec((None, TM, LANES), lambda b, p, i, blk=blk: (b, blk(i), 8 + p)),
                     pl.BlockSpec((None, TM, 2 * GLA_DV), lambda b, p, i, blk=blk: (b, blk(i), 6 + p)),
                     pl.BlockSpec((None, TM, LANES), lambda b, p, i, blk=blk: (b, blk(i), 10)),
                     pl.BlockSpec((3, LANES, LANES), lambda b, p, i: (0, 0, p)),
                     pl.BlockSpec((1, LANES), lambda b, p, i: (0, p)),
                     pl.BlockSpec((TM, TM), lambda b, p, i: (0, 0))]
        args += [p3, p3, p3, p3, jnp.stack(_split3(wd_pads[int(reverse)])), bds[int(reverse)],
                 jnp.asarray(_chunk_sum_mat(GLA_CHUNK, reverse), BF16)]
        out_specs.append(pl.BlockSpec((None, TM, 2 * GLA_DV), lambda b, p, i, blk=blk: (b, blk(i), p)))
    state = pltpu.VMEM((2 * GLA_DK, 2 * GLA_DV), F32)
    return pl.pallas_call(
        _gla_kernel,
        grid=(nb, 2, nblk),
        in_specs=in_specs,
        out_specs=out_specs,
        out_shape=[jax.ShapeDtypeStruct((nb, t, GLA_V), BF16)] * 2,
        scratch_shapes=[state, state],
        compiler_params=_cparams(("arbitrary", "arbitrary", "arbitrary")),
        name="gla_scan",
    )(*args)


def _seg_ones(width, seg):
    r = lax.broadcasted_iota(jnp.int32, (width, width), 0) // seg
    c = lax.broadcasted_iota(jnp.int32, (width, width), 1) // seg
    return jnp.where(r == c, 1.0 / seg, 0.0).astype(F32)


def _norm_rope(x, g, cos, sin, bd):
    ms = jnp.dot(x * x, bd, precision=HI, preferred_element_type=F32)
    xn = x * lax.rsqrt(ms + EPS) * g
    lane = lax.broadcasted_iota(jnp.int32, xn.shape, 1)
    quarter = GQA_HD // 4
    up = pltpu.roll(xn, LANES - quarter, 1)
    dn = pltpu.roll(xn, quarter, 1)
    rot = jnp.where(lane % (2 * quarter) < quarter, -up, dn)
    return xn * cos + rot * sin


def _qkprep_kernel(q_ref, k_ref, v_ref, cos_ref, sin_ref, gq_ref, gk_ref, qo_ref, ko_ref, vo_ref):
    bd = _seg_ones(LANES, GQA_HD)
    cos = cos_ref[...]
    sin = sin_ref[...]
    lane = lax.broadcasted_iota(jnp.int32, (TM, LANES), 1)
    low = lane < GQA_HD
    extra = lane == GQA_HD
    one_hot = jnp.where(extra, 1.0, 0.0)
    k_bound = math.sqrt(GQA_HD) * jnp.max(jnp.abs(gk_ref[...]), axis=-1, keepdims=True) * 1.02
    for j in range(GQA_Q // LANES):
        x = q_ref[:, j * LANES:(j + 1) * LANES].astype(F32)
        y = _norm_rope(x, gq_ref[...], cos, sin, bd) * (GQA_HD ** -0.5 * LOG2E)
        y = y.astype(BF16).astype(F32)
        norm = jnp.sqrt(jnp.dot(y * y, bd, precision=HI, preferred_element_type=F32) * GQA_HD)
        shift = ATT_HEADROOM - norm * k_bound
        lo = jnp.where(low, y, jnp.where(extra, pltpu.roll(shift, GQA_HD, 1), 0.0))
        hi = jnp.where(low, pltpu.roll(y, GQA_HD, 1), jnp.where(extra, shift, 0.0))
        qo_ref[:, (2 * j) * LANES:(2 * j + 1) * LANES] = lo.astype(BF16)
        qo_ref[:, (2 * j + 1) * LANES:(2 * j + 2) * LANES] = hi.astype(BF16)
    k = _norm_rope(k_ref[...].astype(F32), gk_ref[...], cos, sin, bd)
    ko_ref[:, 0:LANES] = jnp.where(low, k, one_hot).astype(BF16)
    ko_ref[:, LANES:2 * LANES] = jnp.where(low, pltpu.roll(k, GQA_HD, 1), one_hot).astype(BF16)
    v = v_ref[...].astype(F32)
    vo_ref[0] = jnp.where(low, v, one_hot).T.astype(BF16)
    vo_ref[1] = jnp.where(low, pltpu.roll(v, GQA_HD, 1), one_hot).T.astype(BF16)


def _qkprep(p2, cos2, sin2, gq2, gk2, tps):
    ntok = p2.shape[0]
    nb = ntok // (tps * TM)
    return pl.pallas_call(
        _qkprep_kernel,
        grid=(ntok // TM,),
        in_specs=[pl.BlockSpec((TM, GQA_Q), lambda i: (i, 0)),
                  pl.BlockSpec((TM, LANES), lambda i: (i, 4)),
                  pl.BlockSpec((TM, LANES), lambda i: (i, 5)),
                  pl.BlockSpec((TM, LANES), lambda i: (i % tps, 0)),
                  pl.BlockSpec((TM, LANES), lambda i: (i % tps, 0)),
                  pl.BlockSpec((1, LANES), lambda i: (0, 0)),
                  pl.BlockSpec((1, LANES), lambda i: (0, 0))],
        out_specs=[pl.BlockSpec((TM, GQA_HEADS * LANES), lambda i: (i, 0)),
                   pl.BlockSpec((TM, GQA_KV_HEADS * LANES), lambda i: (i, 0)),
                   pl.BlockSpec((None, GQA_KV_HEADS, LANES, TM), lambda i: (i // tps, 0, 0, i % tps))],
        out_shape=[jax.ShapeDtypeStruct((ntok, GQA_HEADS * LANES), BF16),
                   jax.ShapeDtypeStruct((ntok, GQA_KV_HEADS * LANES), BF16),
                   jax.ShapeDtypeStruct((nb, GQA_KV_HEADS, LANES, tps * TM), BF16)],
        compiler_params=_cparams(("arbitrary",)),
        name="qkprep",
    )(p2, p2, p2, cos2, sin2, gq2, gk2)


def _attn_kernel(q_ref, k_ref, vt_ref, o_ref, *, ctx_len, n_lat_chunks, ck, n_blocks):
    i = pl.program_id(2)
    rep = GQA_HEADS // GQA_KV_HEADS
    is_ctx = i < ctx_len // ATT_TQ
    q4 = jnp.concatenate([q_ref[:, r * LANES:(r + 1) * LANES] for r in range(rep)], axis=0)
    nt = (((1,), (1,)), ((), ()))

    def chunk(acc, kc, vtc):
        st = lax.dot_general(kc, q4, nt, preferred_element_type=F32)
        return acc + jnp.dot(vtc, jnp.exp2(st).astype(BF16), preferred_element_type=F32)

    def store(acc):
        out = (acc * (1.0 / acc[GQA_HD:GQA_HD + 1, :])).T
        low = lax.broadcasted_iota(jnp.int32, (ATT_TQ, LANES), 1) < GQA_HD
        hs = [out[r * ATT_TQ:(r + 1) * ATT_TQ] for r in range(rep)]
        for u in range(rep // 2):
            o_ref[:, u * LANES:(u + 1) * LANES] = jnp.where(
                low, hs[2 * u], pltpu.roll(hs[2 * u + 1], GQA_HD, 1)).astype(o_ref.dtype)

    acc = chunk(jnp.zeros((LANES, rep * ATT_TQ), F32), k_ref[0:ctx_len, :], vt_ref[:, 0:ctx_len])

    def body(c, acc):
        off = pl.multiple_of(ctx_len + c * ck, TM)
        return chunk(acc, k_ref[pl.ds(off, ck), :], vt_ref[:, pl.ds(off, ck)])

    acc = lax.fori_loop(0, jnp.where(is_ctx, 0, n_lat_chunks), body, acc)
    healthy = jnp.min(acc[GQA_HD:GQA_HD + 1, :]) >= ATT_MIN_DENOM

    @pl.when(healthy)
    def _():
        store(acc)

    @pl.when(jnp.logical_not(healthy))
    def _():
        def online(c, carry):
            m, acc = carry
            off = pl.multiple_of(c * TM, TM)
            st = lax.dot_general(k_ref[pl.ds(off, TM), :], q4, nt, preferred_element_type=F32)
            mn = jnp.maximum(m, jnp.max(st, axis=0, keepdims=True))
            pt = jnp.exp2(st - mn).astype(BF16)
            acc = jnp.exp2(m - mn) * acc + jnp.dot(vt_ref[:, pl.ds(off, TM)], pt, preferred_element_type=F32)
            return mn, acc

        init = (jnp.full((1, rep * ATT_TQ), -1e30, F32), jnp.zeros((LANES, rep * ATT_TQ), F32))
        _, acc2 = lax.fori_loop(0, jnp.where(is_ctx, ctx_len // TM, n_blocks), online, init)
        store(acc2)


def _attention(qh3, kg3, vt4, ctx_len):
    nb, t, _ = qh3.shape
    seq = t - ctx_len
    ck = next(c for c in (2048, 1024, 512, 256) if seq % c == 0)
    rep = GQA_HEADS // GQA_KV_HEADS
    return pl.pallas_call(
        functools.partial(_attn_kernel, ctx_len=ctx_len, n_lat_chunks=seq // ck, ck=ck, n_blocks=t // TM),
        grid=(nb, GQA_KV_HEADS, t // ATT_TQ),
        in_specs=[pl.BlockSpec((None, ATT_TQ, rep * LANES), lambda b, g, i: (b, i, g)),
                  pl.BlockSpec((None, t, LANES), lambda b, g, i: (b, 0, g)),
                  pl.BlockSpec((None, None, LANES, t), lambda b, g, i: (b, g, 0, 0))],
        out_specs=pl.BlockSpec((None, ATT_TQ, rep * GQA_HD), lambda b, g, i: (b, i, g)),
        out_shape=jax.ShapeDtypeStruct((nb, t, GQA_Q), BF16),
        compiler_params=_cparams(("arbitrary", "arbitrary", "arbitrary")),
        name="gqa_attn",
    )(qh3, kg3, vt4)


def _ln(x, g, b):
    mu = jnp.mean(x, axis=-1, keepdims=True)
    xc = x - mu
    var = jnp.mean(xc * xc, axis=-1, keepdims=True)
    return xc * lax.rsqrt(var + EPS) * g + b


def _evout_kernel(of_ref, ob_ref, r_ref, att_ref, g_ref, w1_ref, w2_ref, x_ref, mod_ref, lg_ref, lb_ref, o_ref):
    o = of_ref[...].astype(F32) + ob_ref[...].astype(F32)
    r = r_ref[...].astype(F32)
    parts = []
    for h in range(GLA_HEADS):
        oh = o[:, h * GLA_DV:(h + 1) * GLA_DV]
        ms = jnp.mean(oh * oh, axis=-1, keepdims=True)
        parts.append(oh * lax.rsqrt(ms + EPS))
    gl = (jnp.concatenate(parts, axis=1) * g_ref[...] * _silu(r)).astype(BF16)
    y = jnp.dot(gl, w1_ref[...], preferred_element_type=F32)
    y = y + jnp.dot(att_ref[...], w2_ref[...], preferred_element_type=F32)
    gate = mod_ref[0, 2:3, :]
    o_ref[...] = _ln(ALPHA * x_ref[...] + gate * y, lg_ref[...], lb_ref[...])


def _evout(of2, ob2, p2, att2, g512, w1, w2, x2, mod, lg, lb, tps, nb):
    ntok, d = x2.shape
    row = lambda i: (i, 0)
    const = lambda i: (0, 0)
    return pl.pallas_call(
        _evout_kernel,
        grid=(ntok // TM,),
        in_specs=[pl.BlockSpec((TM, GLA_V), row), pl.BlockSpec((TM, GLA_V), row),
                  pl.BlockSpec((TM, GLA_V), lambda i: (i, 4)),
                  pl.BlockSpec((TM, GQA_Q), row),
                  pl.BlockSpec((1, GLA_V), const),
                  pl.BlockSpec((GLA_V, d), const), pl.BlockSpec((GQA_Q, d), const),
                  pl.BlockSpec((TM, d), row),
                  pl.BlockSpec((1, 6, d), lambda i: (_mod_row(i, tps, nb), 0, 0)),
                  pl.BlockSpec((1, d), const), pl.BlockSpec((1, d), const)],
        out_specs=pl.BlockSpec((TM, d), row),
        out_shape=jax.ShapeDtypeStruct((ntok, d), F32),
        compiler_params=_cparams(("arbitrary",)),
        name="even_out",
    )(of2, ob2, p2, att2, g512, w1, w2, x2, mod, lg, lb)


def _conv_kernel(prev_ref, cur_ref, next_ref, w_ref, b_ref, o_ref, pad_ref, *, tps):
    j = pl.program_id(0) % tps
    has_prev = (j >= 2).astype(F32)
    has_next = jnp.logical_and(j >= 1, j <= tps - 2).astype(F32)
    pad_ref[0:8, :] = prev_ref[...].astype(F32) * has_prev
    pad_ref[8:8 + TM, :] = cur_ref[...].astype(F32)
    pad_ref[8 + TM:16 + TM, :] = next_ref[...].astype(F32) * has_next
    half = (SSD_CONV - 1) // 2
    acc = jnp.zeros((TM, cur_ref.shape[1]), F32) + b_ref[...]
    for tap in range(SSD_CONV):
        acc = acc + pad_ref[pl.ds(8 + tap - half, TM), :] * w_ref[tap:tap + 1, :]
    o_ref[...] = _silu(acc).astype(o_ref.dtype)


def _conv(p2, w8, bias, tps):
    ntok = p2.shape[0]
    cw = 256
    ncb = SSD_CONV_CH // cw
    cb0 = SSD_INNER // cw
    r8 = TM // 8
    nrow8 = ntok // 8
    return pl.pallas_call(
        functools.partial(_conv_kernel, tps=tps),
        grid=(ntok // TM, ncb),
        in_specs=[pl.BlockSpec((8, cw), lambda i, c: (jnp.maximum(i * r8 - 1, 0), cb0 + c)),
                  pl.BlockSpec((TM, cw), lambda i, c: (i, cb0 + c)),
                  pl.BlockSpec((8, cw), lambda i, c: (jnp.minimum((i + 1) * r8, nrow8 - 1), cb0 + c)),
                  pl.BlockSpec((8, cw), lambda i, c: (0, c)),
                  pl.BlockSpec((1, cw), lambda i, c: (0, c))],
        out_specs=pl.BlockSpec((TM, cw), lambda i, c: (i, c)),
        out_shape=jax.ShapeDtypeStruct((ntok, SSD_CONV_CH), BF16),
        scratch_shapes=[pltpu.VMEM((TM + 16, cw), F32)],
        compiler_params=_cparams(("arbitrary", "arbitrary")),
        name="ssd_conv",
    )(p2, p2, p2, w8, bias)


def _ssd_dir(xbc_ref, dt_ref, bias_ref, a_ref, eexp_ref, esel_ref, cm_ref, o_ref, h_ref, *, reverse):
    L = SSD_CHUNK
    GW = SSD_HPG * SSD_HD
    nchunk = TM // L
    nt = (((1,), (1,)), ((), ()))
    tn = (((0,), (0,)), ((), ()))
    end = 0 if reverse else L - 1
    lane0 = SSD_HEADS if reverse else 0

    @pl.when(pl.program_id(1) == 0)
    def _():
        h_ref[...] = jnp.zeros_like(h_ref)

    xs = xbc_ref[:, 0:SSD_INNER].astype(F32)
    bs = xbc_ref[:, SSD_INNER:SSD_INNER + SSD_BC]
    cs = xbc_ref[:, SSD_INNER + SSD_BC:SSD_CONV_CH]
    dt = _softplus(dt_ref[...].astype(F32) + bias_ref[...])
    a = dt * a_ref[...]
    cm = cm_ref[...]
    cum = sum(jnp.dot(cm, part, preferred_element_type=F32) for part in _split3(a))
    yield
    cum_t = sum(lax.dot_general(esel_ref[...], part, nt, preferred_element_type=F32)
                for part in _split3(cum))
    cend = _chunk_rows(cum, L, end)
    eexp = eexp_ref[...]

    def expand(m):
        return jnp.dot(m.astype(BF16), eexp, preferred_element_type=F32)

    xdt = xs * expand(dt)
    xdt_b = xdt.astype(BF16)
    yield
    xend = (xdt * expand(jnp.exp(cend - cum))).astype(BF16)
    yield
    ecum = expand(jnp.exp(cum))
    tot = jnp.concatenate([cum[ch * L + end:ch * L + end + 1, :] for ch in range(nchunk)]
                          + [jnp.zeros((ROWG - nchunk, LANES), F32)], axis=0)
    edec = jnp.exp(sum(jnp.dot(part, eexp, preferred_element_type=F32) for part in _split3(tot)))

    tri = _tri(L, reverse)
    low = lax.broadcasted_iota(jnp.int32, (L, LANES), 1) < SSD_HD
    intra = [[None] * SSD_GROUPS for _ in range(nchunk)]
    dh = [[None] * SSD_GROUPS for _ in range(nchunk)]
    for ch in range(nchunk):
        rows = slice(ch * L, (ch + 1) * L)
        for g in range(SSD_GROUPS):
            bg = bs[rows, g * SSD_STATE:(g + 1) * SSD_STATE]
            cg = cs[rows, g * SSD_STATE:(g + 1) * SSD_STATE]
            cb = lax.dot_general(cg, bg, nt, preferred_element_type=F32)
            pairs = []
            for pp in range(SSD_HPG // 2):
                yh = []
                for u in range(2):
                    h = g * SSD_HPG + 2 * pp + u
                    seg = cum[rows, lane0 + h:lane0 + h + 1] - cum_t[h:h + 1, rows]
                    dec = jnp.exp(jnp.where(tri, seg, -1e30))
                    mat = (cb * dec).astype(BF16)
                    col = (h - u) * SSD_HD
                    yh.append(jnp.dot(mat, xdt_b[rows, col:col + LANES], preferred_element_type=F32))
                pairs.append(jnp.where(low, yh[0], yh[1]))
                yield
            intra[ch][g] = jnp.concatenate(pairs, axis=1)
            dh[ch][g] = lax.dot_general(bg, xend[rows, g * GW:(g + 1) * GW], tn, preferred_element_type=F32)

    hs = [h_ref[g] for g in range(SSD_GROUPS)]
    out = [None] * nchunk
    for ch in (range(nchunk - 1, -1, -1) if reverse else range(nchunk)):
        rows = slice(ch * L, (ch + 1) * L)
        ys = []
        for g in range(SSD_GROUPS):
            cg = cs[rows, g * SSD_STATE:(g + 1) * SSD_STATE]
            y_inter = jnp.dot(cg, hs[g].astype(BF16), preferred_element_type=F32) * ecum[rows, g * GW:(g + 1) * GW]
            ys.append(intra[ch][g] + y_inter)
            hs[g] = hs[g] * edec[ch:ch + 1, g * GW:(g + 1) * GW] + dh[ch][g]
        out[ch] = jnp.concatenate(ys, axis=1)
        yield
    for g in range(SSD_GROUPS):
        h_ref[g] = hs[g]
    o_ref[...] = jnp.concatenate(out, axis=0).astype(o_ref.dtype)


def _ssd_kernel(*refs):
    n_in = (len(refs) - 4) // 2
    o_f, o_b, h_f, h_b = refs[2 * n_in:]
    _alternate(_ssd_dir(*refs[:n_in], o_f, h_f, reverse=False),
               _ssd_dir(*refs[n_in:2 * n_in], o_b, h_b, reverse=True))


def _ssd_scan(xbc3, p3, params):
    nb, t, _ = xbc3.shape
    nblk = t // TM
    const = lambda b, i: (0, 0)
    in_specs, args, out_specs = [], [], []
    for reverse in (False, True):
        blk = functools.partial(_scan_block, nblk=nblk, reverse=reverse)
        bias_pad, a_pad, eexp, esel = params[int(reverse)]
        in_specs += [pl.BlockSpec((None, TM, SSD_CONV_CH), lambda b, i, blk=blk: (b, blk(i), 0)),
                     pl.BlockSpec((None, TM, LANES), lambda b, i, blk=blk: (b, blk(i), 16)),
                     pl.BlockSpec((1, LANES), const), pl.BlockSpec((1, LANES), const),
                     pl.BlockSpec((LANES, SSD_INNER), const), pl.BlockSpec((16, LANES), const),
                     pl.BlockSpec((TM, TM), const)]
        args += [xbc3, p3, bias_pad, a_pad, eexp.astype(BF16), esel.astype(BF16),
                 jnp.asarray(_chunk_sum_mat(SSD_CHUNK, reverse), BF16)]
        out_specs.append(pl.BlockSpec((None, TM, SSD_INNER), lambda b, i, blk=blk: (b, blk(i), 0)))
    state = pltpu.VMEM((SSD_GROUPS, SSD_STATE, SSD_HPG * SSD_HD), F32)
    return pl.pallas_call(
        _ssd_kernel,
        grid=(nb, nblk),
        in_specs=in_specs,
        out_specs=out_specs,
        out_shape=[jax.ShapeDtypeStruct((nb, t, SSD_INNER), BF16)] * 2,
        scratch_shapes=[state, state],
        compiler_params=_cparams(("arbitrary", "arbitrary")),
        name="ssd_scan",
    )(*args)


def _dft_mats(n):
    k = jnp.arange(n, dtype=jnp.int32)
    ang = ((k[:, None] * k[None, :]) % n).astype(F32) * (2.0 * math.pi / n)
    return jnp.cos(ang), jnp.sin(ang)


def _chan_mats():
    cc, sc = _dft_mats(FNET_GC)
    eye = jnp.eye(FNET_GROUPS, dtype=F32)
    return jnp.kron(eye, cc), jnp.kron(eye, sc)


def _group_rms(x, g, bd):
    ms = jnp.dot(x * x, bd, precision=HI, preferred_element_type=F32)
    return x * lax.rsqrt(ms + EPS) * g


def _split2(x):
    hi = x.astype(BF16)
    return hi, (x - hi.astype(F32)).astype(BF16)


def _mm(a, b):
    a_hi, a_lo = _split2(a)
    b_hi, b_lo = _split2(b)
    dot = functools.partial(jnp.dot, preferred_element_type=F32)
    return dot(a_hi, b_hi) + (dot(a_hi, b_lo) + dot(a_lo, b_hi))


def _ffta_kernel(x_ref, g_ref, cc_ref, sc_ref, c1_ref, s1_ref, twc_ref, tws_ref, yr_ref, yi_ref, *, nb2):
    bd = _seg_ones(FNET_W, FNET_GC)
    for j in range(nb2):
        x = x_ref[:, j * FNET_W:(j + 1) * FNET_W].astype(F32)
        xn = _group_rms(x, g_ref[...], bd)
        vr = _mm(xn, cc_ref[...])
        vi = -_mm(xn, sc_ref[...])
        c1 = c1_ref[...]
        s1 = s1_ref[...]
        yr = _mm(c1, vr) + _mm(s1, vi)
        yi = _mm(c1, vi) - _mm(s1, vr)
        tc = twc_ref[j]
        ts = tws_ref[j]
        yr_ref[j] = yr * tc + yi * ts
        yi_ref[j] = yi * tc - yr * ts


def _fftb_kernel(yr_ref, yi_ref, c2_ref, s2_ref, o_ref, *, scale):
    o_ref[...] = (_mm(c2_ref[...], yr_ref[...]) + _mm(s2_ref[...], yi_ref[...])) * scale


def _fftc_kernel(x_ref, g_ref, cc_ref, sc_ref, ct_ref, st_ref, o_ref, *, scale):
    bd = _seg_ones(FNET_W, FNET_GC)
    xn = _group_rms(x_ref[...].astype(F32), g_ref[...], bd)
    a = _mm(xn, cc_ref[...])
    b = _mm(xn, sc_ref[...])
    o_ref[...] = (_mm(ct_ref[...], a) - _mm(st_ref[...], b)) * scale


def _fourier_latent(f_lat, g256):
    nb, s, _ = f_lat.shape
    n2 = 64
    n1 = s // n2
    nb2 = 8
    ccm, scm = _chan_mats()
    c1, s1 = _dft_mats(n1)
    c2, s2 = _dft_mats(n2)
    t2 = jnp.arange(n2, dtype=jnp.int32)[:, None]
    k1 = jnp.arange(n1, dtype=jnp.int32)[None, :]
    tw = ((t2 * k1) % s).astype(F32) * (2.0 * math.pi / s)
    twc = jnp.cos(tw)[:, :, None]
    tws = jnp.sin(tw)[:, :, None]
    x2 = f_lat.reshape(nb, n1, n2 * FNET_W)
    const2 = lambda b, j: (0, 0)
    yr, yi = pl.pallas_call(
        functools.partial(_ffta_kernel, nb2=nb2),
        grid=(nb, n2 // nb2),
        in_specs=[pl.BlockSpec((None, n1, nb2 * FNET_W), lambda b, j: (b, 0, j)),
                  pl.BlockSpec((1, FNET_W), const2),
                  pl.BlockSpec((FNET_W, FNET_W), const2), pl.BlockSpec((FNET_W, FNET_W), const2),
                  pl.BlockSpec((n1, n1), const2), pl.BlockSpec((n1, n1), const2),
                  pl.BlockSpec((nb2, n1, 1), lambda b, j: (j, 0, 0)),
                  pl.BlockSpec((nb2, n1, 1), lambda b, j: (j, 0, 0))],
        out_specs=[pl.BlockSpec((None, nb2, n1, FNET_W), lambda b, j: (b, j, 0, 0)),
                   pl.BlockSpec((None, nb2, n1, FNET_W), lambda b, j: (b, j, 0, 0))],
        out_shape=[jax.ShapeDtypeStruct((nb, n2, n1, FNET_W), F32),
                   jax.ShapeDtypeStruct((nb, n2, n1, FNET_W), F32)],
        compiler_params=_cparams(("arbitrary", "arbitrary")),
        name="fft_a",
    )(x2, g256, ccm, scm, c1, s1, twc, tws)
    ncol = n1 * FNET_W
    tn = min(2048, ncol)
    out = pl.pallas_call(
        functools.partial(_fftb_kernel, scale=1.0 / math.sqrt(s * FNET_GC)),
        grid=(nb, ncol // tn),
        in_specs=[pl.BlockSpec((None, n2, tn), lambda b, j: (b, 0, j)),
                  pl.BlockSpec((None, n2, tn), lambda b, j: (b, 0, j)),
                  pl.BlockSpec((n2, n2), const2), pl.BlockSpec((n2, n2), const2)],
        out_specs=pl.BlockSpec((None, n2, tn), lambda b, j: (b, 0, j)),
        out_shape=jax.ShapeDtypeStruct((nb, n2, ncol), F32),
        compiler_params=_cparams(("arbitrary", "arbitrary")),
        name="fft_b",
    )(yr.reshape(nb, n2, ncol), yi.reshape(nb, n2, ncol), c2, s2)
    return out.reshape(nb, s, FNET_W)


def _fourier_ctx(f_ctx, g256):
    nb, tc, _ = f_ctx.shape
    ccm, scm = _chan_mats()
    ct, st = _dft_mats(tc)
    const = lambda b: (0, 0)
    return pl.pallas_call(
        functools.partial(_fftc_kernel, scale=1.0 / math.sqrt(tc * FNET_GC)),
        grid=(nb,),
        in_specs=[pl.BlockSpec((None, tc, FNET_W), lambda b: (b, 0, 0)),
                  pl.BlockSpec((1, FNET_W), const),
                  pl.BlockSpec((FNET_W, FNET_W), const), pl.BlockSpec((FNET_W, FNET_W), const),
                  pl.BlockSpec((tc, tc), const), pl.BlockSpec((tc, tc), const)],
        out_specs=pl.BlockSpec((None, tc, FNET_W), lambda b: (b, 0, 0)),
        out_shape=jax.ShapeDtypeStruct((nb, tc, FNET_W), F32),
        compiler_params=_cparams(("arbitrary",)),
        name="fft_ctx",
    )(f_ctx, g256, ccm, scm, ct, st)


def _odout_kernel(yf_ref, yb_ref, xs_ref, z_ref, f_ref, dsk_ref, g_ref, w1_ref, w2_ref, x_ref, mod_ref,
                  lg_ref, lb_ref, o_ref):
    GW = SSD_HPG * SSD_HD
    y = yf_ref[...].astype(F32) + yb_ref[...].astype(F32) + xs_ref[...].astype(F32) * dsk_ref[...]
    y = y * _silu(z_ref[...].astype(F32))
    parts = []
    for g in range(SSD_GROUPS):
        yg = y[:, g * GW:(g + 1) * GW]
        ms = jnp.mean(yg * yg, axis=-1, keepdims=True)
        parts.append(yg * lax.rsqrt(ms + EPS))
    yn = (jnp.concatenate(parts, axis=1) * g_ref[...]).astype(BF16)
    o = jnp.dot(yn, w1_ref[...], preferred_element_type=F32)
    o = o + jnp.dot(f_ref[...].astype(BF16), w2_ref[...], preferred_element_type=F32)
    gate = mod_ref[0, 2:3, :]
    o_ref[...] = _ln(ALPHA * x_ref[...] + gate * o, lg_ref[...], lb_ref[...])


def _odout(yf2, yb2, xbc2, p2, f2, dsk, g768, w1, w2, x2, mod, lg, lb, tps, nb):
    ntok, d = x2.shape
    row = lambda i: (i, 0)
    const = lambda i: (0, 0)
    return pl.pallas_call(
        _odout_kernel,
        grid=(ntok // TM,),
        in_specs=[pl.BlockSpec((TM, SSD_INNER), row), pl.BlockSpec((TM, SSD_INNER), row),
                  pl.BlockSpec((TM, SSD_INNER), row), pl.BlockSpec((TM, SSD_INNER), row),
                  pl.BlockSpec((TM, FNET_W), row),
                  pl.BlockSpec((1, SSD_INNER), const), pl.BlockSpec((1, SSD_INNER), const),
                  pl.BlockSpec((SSD_INNER, d), const), pl.BlockSpec((FNET_W, d), const),
                  pl.BlockSpec((TM, d), row),
                  pl.BlockSpec((1, 6, d), lambda i: (_mod_row(i, tps, nb), 0, 0)),
                  pl.BlockSpec((1, d), const), pl.BlockSpec((1, d), const)],
        out_specs=pl.BlockSpec((TM, d), row),
        out_shape=jax.ShapeDtypeStruct((ntok, d), F32),
        compiler_params=_cparams(("arbitrary",)),
        name="odd_out",
    )(yf2, yb2, xbc2, p2, f2, dsk, g768, w1, w2, x2, mod, lg, lb)


def _router_kernel(x_ref, mod_ref, rw_ref, rb_ref, v_ref, idx_ref, gw_ref, cnt_ref):
    @pl.when(pl.program_id(0) == 0)
    def _():
        cnt_ref[...] = jnp.zeros_like(cnt_ref)

    shift = mod_ref[0, 3:4, :]
    scale = mod_ref[0, 4:5, :]
    v = x_ref[...] * (1.0 + scale) + shift
    v_hi = v.astype(BF16)
    v_ref[...] = v_hi
    v_lo = (v - v_hi.astype(F32)).astype(BF16)
    w_hi, w_lo = _split2(rw_ref[...])
    nt_dot = functools.partial(lax.dot_general, dimension_numbers=(((1,), (1,)), ((), ())),
                               preferred_element_type=F32)
    logits = nt_dot(w_hi, v_hi) + (nt_dot(w_hi, v_lo) + nt_dot(w_lo, v_hi))
    s = 1.0 / (1.0 + jnp.exp(-logits))
    sel = s + rb_ref[:, 0:1]
    izero = jnp.zeros((1, TM), jnp.int32)
    best = None
    for g in range(N_EXPERT_GROUPS):
        a = [sel[g * EXPERTS_PER_GROUP + j:g * EXPERTS_PER_GROUP + j + 1, :] for j in range(EXPERTS_PER_GROUP)]
        sv = [s[g * EXPERTS_PER_GROUP + j:g * EXPERTS_PER_GROUP + j + 1, :] for j in range(EXPERTS_PER_GROUP)]
        m1, i1, s1 = a[0], izero, sv[0]
        for j in range(1, EXPERTS_PER_GROUP):
            gt = a[j] > m1
            m1 = jnp.where(gt, a[j], m1)
            i1 = jnp.where(gt, j, i1)
            s1 = jnp.where(gt, sv[j], s1)
        m2 = jnp.full((1, TM), -jnp.inf, F32)
        i2, s2 = izero, jnp.zeros((1, TM), F32)
        for j in range(EXPERTS_PER_GROUP):
            gt = jnp.logical_and(i1 != j, a[j] > m2)
            m2 = jnp.where(gt, a[j], m2)
            i2 = jnp.where(gt, j, i2)
            s2 = jnp.where(gt, sv[j], s2)
        cand = (m1 + m2, i1 + g * EXPERTS_PER_GROUP, i2 + g * EXPERTS_PER_GROUP, s1, s2)
        if best is None:
            best = cand
        else:
            gt = cand[0] > best[0]
            best = tuple(jnp.where(gt, cn, bs) for cn, bs in zip(cand, best))
    _, e1, e2, w1, w2 = best
    tot = w1 + w2
    idx_ref[0] = jnp.concatenate([e1, e2], axis=0)
    gw_ref[0] = jnp.concatenate([w1 / tot, w2 / tot], axis=0)
    eio = lax.broadcasted_iota(jnp.int32, (N_EXPERTS, TM), 0)
    oh = jnp.logical_or(eio == e1, eio == e2).astype(F32)
    cnt_ref[...] += _ceil_rows(jnp.sum(oh, axis=1, keepdims=True))


def _router(x2, mod, rw_t, rb, tps, nb):
    ntok, d = x2.shape
    nt = ntok // TM
    return pl.pallas_call(
        _router_kernel,
        grid=(nt,),
        in_specs=[pl.BlockSpec((TM, d), lambda i: (i, 0)),
                  pl.BlockSpec((1, 6, d), lambda i: (_mod_row(i, tps, nb), 0, 0)),
                  pl.BlockSpec((N_EXPERTS, d), lambda i: (0, 0)),
                  pl.BlockSpec((N_EXPERTS, LANES), lambda i: (0, 0))],
        out_specs=[pl.BlockSpec((TM, d), lambda i: (i, 0)),
                   pl.BlockSpec((1, TOP_K, TM), lambda i: (i, 0, 0)),
                   pl.BlockSpec((1, TOP_K, TM), lambda i: (i, 0, 0)),
                   pl.BlockSpec((N_EXPERTS, LANES), lambda i: (0, 0))],
        out_shape=[jax.ShapeDtypeStruct((ntok, d), BF16),
                   jax.ShapeDtypeStruct((nt, TOP_K, TM), jnp.int32),
                   jax.ShapeDtypeStruct((nt, TOP_K, TM), F32),
                   jax.ShapeDtypeStruct((N_EXPERTS, LANES), F32)],
        compiler_params=_cparams(("arbitrary",)),
        name="router",
    )(x2, mod, rw_t, rb)


def _ceil_rows(c):
    return jnp.ceil(c * (1.0 / ROWG)) * ROWG


def _meta_kernel(idx_ref, start_ref, qpos_ref, meta_ref, run_ref):
    @pl.when(pl.program_id(0) == 0)
    def _():
        run_ref[...] = jnp.zeros_like(run_ref)

    e1 = idx_ref[0, 0:1, :]
    e2 = idx_ref[0, 1:2, :]
    eio = lax.broadcasted_iota(jnp.int32, (N_EXPERTS, TM), 0)
    oh1 = eio == e1
    oh2 = eio == e2
    oh = jnp.logical_or(oh1, oh2)
    r = lax.broadcasted_iota(jnp.int32, (TM, TM), 0)
    c = lax.broadcasted_iota(jnp.int32, (TM, TM), 1)
    before = (r < c).astype(BF16)
    rank = jnp.dot(oh.astype(BF16), before, preferred_element_type=F32)
    run_len = jnp.broadcast_to(_ceil_rows(jnp.sum(oh.astype(F32), axis=1, keepdims=True)), (N_EXPERTS, LANES))
    er = lax.broadcasted_iota(jnp.int32, (N_EXPERTS, N_EXPERTS), 0)
    ec = lax.broadcasted_iota(jnp.int32, (N_EXPERTS, N_EXPERTS), 1)
    off = jnp.dot((ec < er).astype(F32), run_len, precision=HI, preferred_element_type=F32)
    pos = rank + off[:, 0:1]
    q1 = jnp.sum(jnp.where(oh1, pos, 0.0), axis=0, keepdims=True)
    q2 = jnp.sum(jnp.where(oh2, pos, 0.0), axis=0, keepdims=True)
    qpos_ref[0] = jnp.concatenate([q1, q2], axis=0).astype(jnp.int32)
    lane = lax.broadcasted_iota(jnp.int32, (N_EXPERTS, LANES), 1)
    meta = jnp.where(lane == 0, start_ref[...] + run_ref[...], jnp.where(lane == 1, off, run_len))
    meta_ref[0] = (meta * (1.0 / ROWG)).astype(jnp.int32)
    run_ref[...] += run_len


def _meta(idx_t, start):
    nt = idx_t.shape[0]
    return pl.pallas_call(
        _meta_kernel,
        grid=(nt,),
        in_specs=[pl.BlockSpec((1, TOP_K, TM), lambda i: (i, 0, 0)),
                  pl.BlockSpec((N_EXPERTS, LANES), lambda i: (0, 0))],
        out_specs=[pl.BlockSpec((1, TOP_K, TM), lambda i: (i, 0, 0)),
                   pl.BlockSpec((1, N_EXPERTS, LANES), lambda i: (i, 0, 0))],
        out_shape=[jax.ShapeDtypeStruct((nt, TOP_K, TM), jnp.int32),
                   jax.ShapeDtypeStruct((nt, N_EXPERTS, LANES), jnp.int32)],
        scratch_shapes=[pltpu.VMEM((N_EXPERTS, LANES), F32)],
        compiler_params=_cparams(("arbitrary",)),
        name="moe_meta",
    )(idx_t, start)


def _sort_select(qpos_ref):
    r = lax.broadcasted_iota(jnp.int32, (SORT_ROWS, TM), 0)
    return r == qpos_ref[0, 0:1, :], r == qpos_ref[0, 1:2, :]


def _start_run_copies(i, gb_ref, ob_ref, nb_ref, make_copy):
    big = COPY_GROUPS * ROWG
    for e in range(N_EXPERTS):
        n = nb_ref[i * N_EXPERTS + e]
        buf_row = gb_ref[i * N_EXPERTS + e] * ROWG
        tile_row = ob_ref[i * N_EXPERTS + e] * ROWG
        n_big = lax.shift_right_logical(n, COPY_SHIFT)

        def body_big(k, carry, buf_row=buf_row, tile_row=tile_row):
            make_copy(pl.multiple_of(tile_row + k * big, ROWG), pl.multiple_of(buf_row + k * big, ROWG), big).start()
            return carry

        def body_one(k, carry, buf_row=buf_row, tile_row=tile_row):
            make_copy(pl.multiple_of(tile_row + k * ROWG, ROWG), pl.multiple_of(buf_row + k * ROWG, ROWG),
                      ROWG).start()
            return carry

        lax.fori_loop(0, n_big, body_big, 0)
        lax.fori_loop(lax.shift_left(n_big, COPY_SHIFT), n, body_one, 0)


def _wait_run_copies(i, nb_ref, make_copy):
    n_big = 0
    n_one = 0
    for e in range(N_EXPERTS):
        n = nb_ref[i * N_EXPERTS + e]
        n_big = n_big + lax.shift_right_logical(n, COPY_SHIFT)
        n_one = n_one + lax.bitwise_and(n, COPY_GROUPS - 1)

    def wait_big(k, carry):
        make_copy(0, 0, COPY_GROUPS * ROWG).wait()
        return carry

    def wait_one(k, carry):
        make_copy(0, 0, ROWG).wait()
        return carry

    lax.fori_loop(0, n_big, wait_big, 0)
    lax.fori_loop(0, n_one, wait_one, 0)


def _scatter_kernel(gb_ref, ob_ref, nb_ref, ts_ref, tn_ref, nu_ref, v_ref, qpos_ref, gw_ref, buf_ref, xs_ref, z_ref,
                    sem, zsem):
    i = pl.program_id(0)
    last = pl.num_programs(0) - 1
    slot = i % 2
    d = v_ref.shape[1]
    m0, m1 = _sort_select(qpos_ref)
    sel = jnp.logical_or(m0, m1).astype(BF16)
    xs_ref[slot, :, 0:d] = jnp.dot(sel, v_ref[...], preferred_element_type=F32)
    gate = jnp.sum(jnp.where(m0, gw_ref[0, 0:1, :], 0.0) + jnp.where(m1, gw_ref[0, 1:2, :], 0.0),
                   axis=1, keepdims=True)
    xs_ref[slot, :, d:d + LANES] = jnp.broadcast_to(gate, (SORT_ROWS, LANES))

    def copy_from(s):
        def copy(tile_row, buf_row, rows):
            return pltpu.make_async_copy(xs_ref.at[s, pl.ds(tile_row, rows)], buf_ref.at[pl.ds(buf_row, rows)],
                                         sem.at[s])
        return copy

    _start_run_copies(i, gb_ref, ob_ref, nb_ref, copy_from(slot))

    @pl.when(i > 0)
    def _():
        _wait_run_copies(i - 1, nb_ref, copy_from(1 - slot))

    @pl.when(i == last)
    def _():
        _wait_run_copies(i, nb_ref, copy_from(slot))

    @pl.when(i == last)
    def _():
        z_ref[...] = jnp.zeros_like(z_ref)

        def zcopy(buf_row):
            return pltpu.make_async_copy(z_ref.at[pl.ds(0, ROWG)], buf_ref.at[pl.ds(buf_row, ROWG)], zsem)

        total = 0
        for e in range(N_EXPERTS):
            n = tn_ref[e]
            row0 = ts_ref[e] * ROWG

            def body(k, carry, row0=row0):
                zcopy(pl.multiple_of(row0 + k * ROWG, ROWG)).start()
                return carry

            lax.fori_loop(0, n, body, 0)
            total = total + n

        def wbody(k, carry):
            zcopy(0).wait()
            return carry

        lax.fori_loop(0, total, wbody, 0)

        def zblock(blk):
            return pltpu.make_async_copy(z_ref, buf_ref.at[pl.ds(pl.multiple_of(blk * FFN_ROWS, FFN_ROWS), FFN_ROWS)],
                                         zsem)

        def bbody(blk, carry):
            zblock(blk).start()
            return carry

        def bwait(blk, carry):
            zblock(blk).wait()
            return carry

        nblk = buf_ref.shape[0] // FFN_ROWS
        lax.fori_loop(nu_ref[0], nblk, bbody, 0)
        lax.fori_loop(nu_ref[0], nblk, bwait, 0)


def _scatter(gb, ob, nb8, tail_s, tail_n, n_used, v, qpos, gw_t, nrow):
    ntok, d = v.shape
    grid_spec = pltpu.PrefetchScalarGridSpec(
        num_scalar_prefetch=6,
        grid=(ntok // TM,),
        in_specs=[pl.BlockSpec((TM, d), lambda i, *_: (i, 0)),
                  pl.BlockSpec((1, TOP_K, TM), lambda i, *_: (i, 0, 0)),
                  pl.BlockSpec((1, TOP_K, TM), lambda i, *_: (i, 0, 0))],
        out_specs=pl.BlockSpec(memory_space=pl.ANY),
        scratch_shapes=[pltpu.VMEM((2, SORT_ROWS, d + LANES), F32), pltpu.VMEM((FFN_ROWS, d + LANES), F32),
                        pltpu.SemaphoreType.DMA((2,)), pltpu.SemaphoreType.DMA(())],
    )
    return pl.pallas_call(
        _scatter_kernel,
        grid_spec=grid_spec,
        out_shape=jax.ShapeDtypeStruct((nrow, d + LANES), F32),
        compiler_params=_cparams(("arbitrary",)),
        name="moe_scatter",
    )(gb, ob, nb8, tail_s, tail_n, n_used, v, qpos, gw_t)


def _moe_kernel(be_ref, nu_ref, x_ref, wg_ref, wu_ref, wd_ref, o_ref, wg_bf, wu_bf, wd_bf):
    i = pl.program_id(0)
    d = o_ref.shape[1]

    @pl.when(jnp.logical_or(i == 0, be_ref[i] != be_ref[jnp.maximum(i - 1, 0)]))
    def _():
        wg_bf[...] = wg_ref[...].astype(BF16)
        wu_bf[...] = wu_ref[...].astype(BF16)
        wd_bf[...] = wd_ref[...].astype(BF16)

    @pl.when(i < nu_ref[0])
    def _():
        xb = x_ref[...]
        x = xb[:, 0:d].astype(BF16)
        g = jnp.dot(x, wg_bf[...], preferred_element_type=F32)
        u = jnp.dot(x, wu_bf[...], preferred_element_type=F32)
        h = (_silu(g) * u).astype(BF16)
        y = jnp.dot(h, wd_bf[...], preferred_element_type=F32)
        o_ref[...] = y * xb[:, d:d + 1]

    @pl.when(pl.program_id(0) >= nu_ref[0])
    def _():
        o_ref[...] = jnp.zeros_like(o_ref)


def _moe_ffn(block_e, n_used, buf, layer, wg, wu, wd):
    nrow, dw = buf.shape
    d = dw - LANES
    nblk = nrow // FFN_ROWS
    de = wg.shape[3]
    used = lambda i, nu: jnp.minimum(i, nu[0] - 1)
    grid_spec = pltpu.PrefetchScalarGridSpec(
        num_scalar_prefetch=2,
        grid=(nblk,),
        in_specs=[pl.BlockSpec((FFN_ROWS, dw), lambda i, be, nu: (used(i, nu), 0)),
                  pl.BlockSpec((None, None, d, de), lambda i, be, nu: (layer, be[i], 0, 0)),
                  pl.BlockSpec((None, None, d, de), lambda i, be, nu: (layer, be[i], 0, 0)),
                  pl.BlockSpec((None, None, de, d), lambda i, be, nu: (layer, be[i], 0, 0))],
        out_specs=pl.BlockSpec((FFN_ROWS, d), lambda i, be, nu: (i, 0)),
        scratch_shapes=[pltpu.VMEM((d, de), BF16), pltpu.VMEM((d, de), BF16), pltpu.VMEM((de, d), BF16)],
    )
    return pl.pallas_call(
        _moe_kernel,
        grid_spec=grid_spec,
        out_shape=jax.ShapeDtypeStruct((nrow, d), F32),
        compiler_params=pltpu.CompilerParams(dimension_semantics=("arbitrary",), vmem_limit_bytes=FFN_VMEM_LIMIT),
        name="moe_ffn",
    )(block_e, n_used, buf, wg, wu, wd)


def _ln2_kernel(gb_ref, ob_ref, nb_ref, x_ref, qpos_ref, mod_ref, lg_ref, lb_ref, y_ref, o_ref, ys_ref, sem):
    i = pl.program_id(0)
    slot = i % 2

    def copy_to(s):
        def copy(tile_row, buf_row, rows):
            return pltpu.make_async_copy(y_ref.at[pl.ds(buf_row, rows)], ys_ref.at[s, pl.ds(tile_row, rows)],
                                         sem.at[s])
        return copy

    @pl.when(i == 0)
    def _():
        ys_ref[...] = jnp.zeros_like(ys_ref)
        _start_run_copies(0, gb_ref, ob_ref, nb_ref, copy_to(0))

    @pl.when(i + 1 < pl.num_programs(0))
    def _():
        _start_run_copies(i + 1, gb_ref, ob_ref, nb_ref, copy_to(1 - slot))

    _wait_run_copies(i, nb_ref, copy_to(slot))
    m0, m1 = _sort_select(qpos_ref)
    sel = jnp.logical_or(m0, m1).astype(BF16)
    y = lax.dot_general(sel, ys_ref[slot].astype(BF16), (((0,), (0,)), ((), ())), preferred_element_type=F32)
    gate = mod_ref[0, 5:6, :]
    o_ref[...] = _ln(ALPHA * x_ref[...] + gate * y, lg_ref[...], lb_ref[...])


def _ln2(gb, ob, nb8, x2, qpos, mod, lg, lb, ybuf, tps, nb, latent_only):
    ntok, d = x2.shape
    if latent_only:
        out_map = lambda i, *_: ((i // tps) * (tps - 1) + jnp.maximum(i % tps - 1, 0), 0)
        out_rows = ntok - nb * TM
    else:
        out_map = lambda i, *_: (i, 0)
        out_rows = ntok
    grid_spec = pltpu.PrefetchScalarGridSpec(
        num_scalar_prefetch=3,
        grid=(ntok // TM,),
        in_specs=[pl.BlockSpec((TM, d), lambda i, *_: (i, 0)),
                  pl.BlockSpec((1, TOP_K, TM), lambda i, *_: (i, 0, 0)),
                  pl.BlockSpec((1, 6, d), lambda i, *_: (_mod_row(i, tps, nb), 0, 0)),
                  pl.BlockSpec((1, d), lambda i, *_: (0, 0)),
                  pl.BlockSpec((1, d), lambda i, *_: (0, 0)),
                  pl.BlockSpec(memory_space=pl.ANY)],
        out_specs=pl.BlockSpec((TM, d), out_map),
        scratch_shapes=[pltpu.VMEM((2, SORT_ROWS, d), F32), pltpu.SemaphoreType.DMA((2,))],
    )
    return pl.pallas_call(
        _ln2_kernel,
        grid_spec=grid_spec,
        out_shape=jax.ShapeDtypeStruct((out_rows, d), F32),
        compiler_params=_cparams(("arbitrary",)),
        name="post_moe_ln",
    )(gb, ob, nb8, x2, qpos, mod, lg, lb, ybuf)


def _even_w_in(w):
    o = np.cumsum((0, GLA_QK, GLA_QK, GLA_V, GLA_V, GLA_RANK, GLA_RANK, GQA_Q, GQA_KV, GQA_KV))
    q, k, v, r = w[:, o[0]:o[1]], w[:, o[1]:o[2]], w[:, o[2]:o[3]], w[:, o[3]:o[4]]
    lr = w[:, o[4]:o[6]]
    gq, gk, gv = w[:, o[6]:o[7]], w[:, o[7]:o[8]], w[:, o[8]:o[9]]
    d = w.shape[0]
    zeros = lambda n: jnp.zeros((d, n), w.dtype)
    out = jnp.concatenate([gq, gk, gv, q, k, lr, zeros(LANES - 2 * GLA_RANK), zeros(LANES), v, r], axis=1)
    assert out.shape[1] == EVEN_W
    return out.astype(BF16)


def _odd_w_in(w):
    o = np.cumsum((0, SSD_INNER, SSD_INNER, SSD_BC, SSD_BC, SSD_HEADS, SSD_HEADS, FNET_W))
    d = w.shape[0]
    out = jnp.concatenate([w[:, :o[4]], w[:, o[4]:o[6]], jnp.zeros((d, LANES - 2 * SSD_HEADS), w.dtype),
                           w[:, o[6]:o[7]]], axis=1)
    assert out.shape[1] == ODD_W
    return out.astype(BF16)


def _rope_tables(ctx_len, seq):
    rows = seq // GRID_W
    r = jnp.repeat(jnp.arange(rows, dtype=F32), GRID_W)
    col = jnp.tile(jnp.arange(GRID_W, dtype=F32), rows)
    half = GQA_HD // 2
    inv = ROPE_THETA ** (-jnp.arange(0, half, 2, dtype=F32) / half)
    ar = r[:, None] * inv
    ac = col[:, None] * inv
    ang = jnp.concatenate([ar, ar, ac, ac], -1)
    cos = jnp.concatenate([jnp.ones((ctx_len, GQA_HD), F32), jnp.cos(ang)], 0)
    sin = jnp.concatenate([jnp.zeros((ctx_len, GQA_HD), F32), jnp.sin(ang)], 0)
    return jnp.tile(cos, (1, 2)), jnp.tile(sin, (1, 2))


def _even_layer(x2, mod, nb, t, tps, ctx_len, w_in, w_o, w_dec, b_dec, gla_g, qn_g, kn_g, cos2, sin2, lg, lb):
    p2 = _inproj(x2, mod, _even_w_in(w_in), tps, nb)
    p3 = p2.reshape(nb, t, EVEN_W)
    wd_pads = [jnp.zeros((LANES, GLA_QK), F32).at[di * GLA_RANK:(di + 1) * GLA_RANK].set(w_dec[di])
               for di in range(2)]
    outs = _gla_scan(p3, wd_pads, [b_dec[di][None, :] for di in range(2)])
    qh2, kg2, vt4 = _qkprep(p2, cos2, sin2, jnp.tile(qn_g, 2)[None, :], jnp.tile(kn_g, 2)[None, :], tps)
    att = _attention(qh2.reshape(nb, t, GQA_HEADS * LANES), kg2.reshape(nb, t, GQA_KV_HEADS * LANES), vt4, ctx_len)
    w1, w2 = w_o[:GLA_V].astype(BF16), w_o[GLA_V:].astype(BF16)
    ntok = nb * t
    return _evout(outs[0].reshape(ntok, GLA_V), outs[1].reshape(ntok, GLA_V), p2, att.reshape(ntok, GQA_Q),
                  jnp.tile(gla_g, GLA_HEADS)[None, :], w1, w2, x2, mod, lg, lb, tps, nb)


def _odd_layer(x2, mod, nb, t, tps, ctx_len, w_in, w_o, conv_w, conv_b, dt_bias, a_log, d_skip, ssd_g, fnet_g,
               lg, lb):
    p2 = _inproj(x2, mod, _odd_w_in(w_in), tps, nb)
    p3 = p2.reshape(nb, t, ODD_W)
    w8 = jnp.zeros((8, SSD_CONV_CH), F32).at[:SSD_CONV].set(conv_w)
    xbc2 = _conv(p2, w8, conv_b[None, :], tps)
    xbc3 = xbc2.reshape(nb, t, SSD_CONV_CH)
    params = []
    heads = jnp.arange(SSD_HEADS)
    for di in range(2):
        lanes = di * SSD_HEADS + heads
        bias_pad = jnp.zeros((1, LANES), F32).at[0, lanes].set(dt_bias[di])
        a_pad = jnp.zeros((1, LANES), F32).at[0, lanes].set(-jnp.exp(a_log[di]))
        eexp = jnp.zeros((LANES, SSD_INNER), F32).at[jnp.repeat(lanes, SSD_HD), jnp.arange(SSD_INNER)].set(1.0)
        esel = jnp.zeros((16, LANES), F32).at[heads, lanes].set(1.0)
        params.append((bias_pad, a_pad, eexp, esel))
    ys = _ssd_scan(xbc3, p3, params)
    f3 = p3[:, :, ODD_W - FNET_W:]
    g256 = fnet_g[None, :]
    fmix = jnp.concatenate([_fourier_ctx(f3[:, :ctx_len], g256), _fourier_latent(f3[:, ctx_len:], g256)], axis=1)
    ntok = nb * t
    return _odout(ys[0].reshape(ntok, SSD_INNER), ys[1].reshape(ntok, SSD_INNER), xbc2, p2,
                  fmix.reshape(ntok, FNET_W), jnp.repeat(d_skip, SSD_HD)[None, :], ssd_g[None, :],
                  w_o[:SSD_INNER].astype(BF16), w_o[SSD_INNER:].astype(BF16), x2, mod, lg, lb, tps, nb)


def _moe_layer(x2, mod, nb, tps, rw_t, rb, layer, wg, wu, wd, lg, lb):
    ntok, d = x2.shape
    v, idx_t, gw_t, cnt = _router(x2, mod, rw_t, rb, tps, nb)
    nt = ntok // TM
    rows = cnt[:, 0].astype(jnp.int32)
    padded = (rows + FFN_ROWS - 1) // FFN_ROWS * FFN_ROWS
    pad_end = jnp.cumsum(padded)
    pad_start = pad_end - padded
    nblk = -(-(ntok * TOP_K + nt * N_EXPERTS * (ROWG - 1)) // FFN_ROWS) + N_EXPERTS
    blk_row = jnp.arange(nblk, dtype=jnp.int32)[:, None] * FFN_ROWS
    block_e = jnp.minimum(jnp.sum((blk_row >= pad_end[None, :]).astype(jnp.int32), axis=1), N_EXPERTS - 1)
    n_used = (pad_end[-1:] // FFN_ROWS).astype(jnp.int32)
    start = jnp.broadcast_to(pad_start.astype(F32)[:, None], (N_EXPERTS, LANES))
    qpos, meta = _meta(idx_t, start)
    gb, ob, nb8 = (meta[:, :, j].reshape(nt * N_EXPERTS) for j in range(3))
    tail_s = ((pad_start + rows) // ROWG).astype(jnp.int32)
    tail_n = ((padded - rows) // ROWG).astype(jnp.int32)
    buf = _scatter(gb, ob, nb8, tail_s, tail_n, n_used, v, qpos, gw_t, nblk * FFN_ROWS)
    y = _moe_ffn(block_e, n_used, buf, layer, wg, wu, wd)
    return _ln2(gb, ob, nb8, x2, qpos, mod, lg, lb, y, tps, nb, latent_only=(layer == DEPTH - 1))


def kernel(x, c, ctx, c_ctx, ada_w, ada_b, ln_g, ln_b, ev_w_in, ev_w_o, gla_w_decay, gla_b_decay, gla_norm_g,
           gqa_q_norm_g, gqa_k_norm_g, od_w_in, od_w_o, ssd_conv_w, ssd_conv_b, ssd_dt_bias, ssd_a_log, ssd_d,
           ssd_norm_g, fnet_norm_g, router_w, router_b, exp_w_gate, exp_w_up, exp_w_down):
    nb, seq, d = x.shape
    ctx_len = ctx.shape[1]
    assert ctx_len == TM and seq % TM == 0 and d == D_MODEL and nb <= 7
    t = ctx_len + seq
    tps = t // TM
    ntok = nb * t
    cc = jnp.zeros((8, d), F32).at[:nb].set(c).at[nb].set(c_ctx)
    mod_all = _ada_all(cc, ada_w, ada_b).reshape(DEPTH, 8, 6, d)
    cos2, sin2 = _rope_tables(ctx_len, seq)
    rw_t = router_w.T
    rb = jnp.broadcast_to(router_b[:, None], (N_EXPERTS, LANES))
    x2 = jnp.concatenate([ctx, x], axis=1).reshape(ntok, d)
    for layer in range(DEPTH):
        mod = mod_all[layer]
        i = layer // 2
        lg0, lb0 = ln_g[layer, 0][None, :], ln_b[layer, 0][None, :]
        lg1, lb1 = ln_g[layer, 1][None, :], ln_b[layer, 1][None, :]
        if layer % 2 == 0:
            x2 = _even_layer(x2, mod, nb, t, tps, ctx_len, ev_w_in[i], ev_w_o[i], gla_w_decay[i], gla_b_decay[i],
                             gla_norm_g[i], gqa_q_norm_g[i], gqa_k_norm_g[i], cos2, sin2, lg0, lb0)
        else:
            x2 = _odd_layer(x2, mod, nb, t, tps, ctx_len, od_w_in[i], od_w_o[i], ssd_conv_w[i], ssd_conv_b[i],
                            ssd_dt_bias[i], ssd_a_log[i], ssd_d[i], ssd_norm_g[i], fnet_norm_g[i], lg0, lb0)
        x2 = _moe_layer(x2, mod, nb, tps, rw_t, rb, layer, exp_w_gate, exp_w_up, exp_w_down, lg1, lb1)
    return x2.reshape(nb, seq, d)
```

```python
import functools
import itertools
import math

import jax
import jax.numpy as jnp
import numpy as np
from jax import lax
from jax.experimental import pallas as pl
from jax.experimental.pallas import tpu as pltpu

F32 = jnp.float32
BF16 = jnp.bfloat16
HI = lax.Precision.HIGHEST

D_MODEL = 1024
DEPTH = 4
GRID_W = 64
GLA_HEADS, GLA_DK, GLA_DV, GLA_RANK, GLA_TAU, GLA_CHUNK = 4, 64, 128, 16, 16.0, 64
GQA_HEADS, GQA_KV_HEADS, GQA_HD = 8, 2, 64
ROPE_THETA = 10000.0
SSD_HEADS, SSD_HD, SSD_GROUPS, SSD_STATE, SSD_CONV, SSD_CHUNK = 12, 64, 2, 128, 5, 64
SSD_HPG = SSD_HEADS // SSD_GROUPS
SSD_INNER = SSD_HEADS * SSD_HD
SSD_BC = SSD_GROUPS * SSD_STATE
SSD_CONV_CH = SSD_INNER + 2 * SSD_BC
FNET_GROUPS, FNET_GC = 4, 64
FNET_W = FNET_GROUPS * FNET_GC
N_EXPERTS, N_EXPERT_GROUPS, TOP_K, D_EXPERT = 16, 4, 2, 768
EXPERTS_PER_GROUP = N_EXPERTS // N_EXPERT_GROUPS
GLA_QK = GLA_HEADS * GLA_DK
GLA_V = GLA_HEADS * GLA_DV
GQA_Q = GQA_HEADS * GQA_HD
GQA_KV = GQA_KV_HEADS * GQA_HD
EPS = 1e-6
ALPHA = (2.0 * DEPTH) ** 0.25

LANES = 128
TM = 256
EVEN_W = 2560
ODD_W = 2432
VMEM_LIMIT = 48 * 1024 * 1024
LOG2E = 1.4426950408889634
ATT_TQ = 256
ATT_HEADROOM = 64.0
ATT_MIN_DENOM = 2.0 ** -60
ATT_VROWS = 80
ROWG = 8
SORT_ROWS = TOP_K * TM + N_EXPERTS * (ROWG - 1)
COPY_SHIFT = 2
COPY_GROUPS = 1 << COPY_SHIFT
FFN_ROWS = 512
FFN_VMEM_LIMIT = 56 * 1024 * 1024


def _cparams(sem):
    return pltpu.CompilerParams(dimension_semantics=sem, vmem_limit_bytes=VMEM_LIMIT)


def _silu(x):
    return x * (1.0 / (1.0 + jnp.exp(-x)))


def _softplus(x):
    return jnp.maximum(x, 0.0) + jnp.log1p(jnp.exp(-jnp.abs(x)))


def _mod_row(i, tiles_per_seq, n_batch):
    return jnp.where(i % tiles_per_seq == 0, n_batch, i // tiles_per_seq)


def _ada_kernel(c_ref, w_ref, b_ref, o_ref):
    s = _silu(c_ref[...])
    o_ref[0] = jnp.dot(s, w_ref[0], precision=HI, preferred_element_type=F32) + b_ref[0]


def _ada_all(cc, ada_w, ada_b):
    depth, d, n = ada_w.shape
    tn = 512
    return pl.pallas_call(
        _ada_kernel,
        grid=(depth, n // tn),
        in_specs=[pl.BlockSpec((8, d), lambda l, j: (0, 0)),
                  pl.BlockSpec((1, d, tn), lambda l, j: (l, 0, j)),
                  pl.BlockSpec((1, 1, tn), lambda l, j: (l, 0, j))],
        out_specs=pl.BlockSpec((1, 8, tn), lambda l, j: (l, 0, j)),
        out_shape=jax.ShapeDtypeStruct((depth, 8, n), F32),
        compiler_params=_cparams(("arbitrary", "arbitrary")),
        name="adaln",
    )(cc, ada_w, ada_b.reshape(depth, 1, n))


def _inproj_kernel(x_ref, mod_ref, w_ref, o_ref):
    shift = mod_ref[0, 0:1, :]
    scale = mod_ref[0, 1:2, :]
    u = (x_ref[...] * (1.0 + scale) + shift).astype(BF16)
    o_ref[...] = jnp.dot(u, w_ref[...], preferred_element_type=F32).astype(o_ref.dtype)


def _inproj(x2, mod, w, tps, nb):
    ntok, d = x2.shape
    nw = w.shape[1]
    return pl.pallas_call(
        _inproj_kernel,
        grid=(ntok // TM,),
        in_specs=[pl.BlockSpec((TM, d), lambda i: (i, 0)),
                  pl.BlockSpec((1, 6, d), lambda i: (_mod_row(i, tps, nb), 0, 0)),
                  pl.BlockSpec((d, nw), lambda i: (0, 0))],
        out_specs=pl.BlockSpec((TM, nw), lambda i: (i, 0)),
        out_shape=jax.ShapeDtypeStruct((ntok, nw), BF16),
        compiler_params=_cparams(("arbitrary",)),
        name="inproj",
    )(x2, mod, w)


def _scan_block(i, nblk, reverse):
    if not reverse:
        return i
    return jnp.where(i == 0, 0, nblk - i)


def _tri(n, reverse):
    r = lax.broadcasted_iota(jnp.int32, (n, n), 0)
    c = lax.broadcasted_iota(jnp.int32, (n, n), 1)
    return (c >= r) if reverse else (c <= r)


def _chunk_sum_mat(chunk, reverse):
    l = np.arange(TM)[:, None]
    m = np.arange(TM)[None, :]
    same = (l // chunk) == (m // chunk)
    return (same & ((m >= l) if reverse else (m <= l))).astype(np.float32)


def _split3(x):
    hi = x.astype(BF16)
    r1 = x - hi.astype(F32)
    mid = r1.astype(BF16)
    return hi, mid, (r1 - mid.astype(F32)).astype(BF16)


def _chunk_rows(x, chunk, idx):
    return jnp.concatenate([jnp.broadcast_to(x[c * chunk + idx:c * chunk + idx + 1, :], (chunk, x.shape[1]))
                            for c in range(x.shape[0] // chunk)], axis=0)


def _gla_dir(q_ref, k_ref, v_ref, lr_ref, wd_ref, bd_ref, cm_ref, o_ref, s_ref, *, reverse):
    L = GLA_CHUNK
    nchunk = TM // L
    nt = (((1,), (1,)), ((), ()))
    tn = (((0,), (0,)), ((), ()))
    end = 0 if reverse else L - 1
    mid = L // 2 if reverse else L // 2 - 1

    @pl.when(pl.program_id(2) == 0)
    def _():
        s_ref[...] = jnp.zeros_like(s_ref)

    q = q_ref[...].astype(F32) * (GLA_DK ** -0.5)
    k = k_ref[...].astype(F32)
    v = v_ref[...]
    lr = lr_ref[...]
    z = bd_ref[...] + sum(jnp.dot(lr, wd_ref[j], preferred_element_type=F32) for j in range(3))
    yield
    la = -_softplus(-z) * (1.0 / GLA_TAU)
    cm = cm_ref[...]
    b = sum(jnp.dot(cm, part, preferred_element_type=F32) for part in _split3(la))
    yield
    bmid = _chunk_rows(b, L, mid)
    bend = _chunk_rows(b, L, end)
    qs = q * jnp.exp(b - bmid)
    ks = (k * jnp.exp(bmid - b)).astype(BF16)
    qi = q * jnp.exp(b)
    kend = (k * jnp.exp(bend - b)).astype(BF16)
    r = lax.broadcasted_iota(jnp.int32, (TM, TM), 0)
    c = lax.broadcasted_iota(jnp.int32, (TM, TM), 1)
    mask = jnp.logical_and(r // L == c // L, (c >= r) if reverse else (c <= r))
    lane = lax.broadcasted_iota(jnp.int32, (TM, LANES), 1)
    heads = (lane < GLA_DK, lane >= GLA_DK)
    yield
    att = [lax.dot_general(jnp.where(heads[h], qs, 0.0).astype(BF16), ks, nt, preferred_element_type=F32)
           for h in range(2)]
    yield
    intra = []
    for h in range(2):
        att_h = jnp.where(mask, att[h], 0.0).astype(BF16)
        intra.append(jnp.dot(att_h, v[:, h * GLA_DV:(h + 1) * GLA_DV], preferred_element_type=F32))
    yield
    qi_h = [jnp.where(heads[h], qi, 0.0).astype(BF16) for h in range(2)]
    tot = jnp.concatenate([b[ch * L + end:ch * L + end + 1, :] for ch in range(nchunk)]
                          + [jnp.zeros((ROWG - nchunk, LANES), F32)], axis=0).T
    ds, dec = [], []
    for ch in range(nchunk):
        rows = slice(ch * L, (ch + 1) * L)
        ds.append(lax.dot_general(kend[rows], v[rows], tn, preferred_element_type=F32))
        dec.append(jnp.exp(jnp.broadcast_to(tot[:, ch:ch + 1], (2 * GLA_DK, 2 * GLA_DV))))
    yield
    s = s_ref[...]
    inter = [None] * nchunk
    for ch in (range(nchunk - 1, -1, -1) if reverse else range(nchunk)):
        rows = slice(ch * L, (ch + 1) * L)
        s_bf = s.astype(BF16)
        inter[ch] = jnp.concatenate(
            [jnp.dot(qi_h[h][rows], s_bf[:, h * GLA_DV:(h + 1) * GLA_DV], preferred_element_type=F32)
             for h in range(2)], axis=1)
        s = dec[ch] * s + ds[ch]
        yield
    s_ref[...] = s
    o_ref[...] = (jnp.concatenate(intra, axis=1) + jnp.concatenate(inter, axis=0)).astype(o_ref.dtype)


def _alternate(*stage_generators):
    for _ in itertools.zip_longest(*stage_generators):
        pass


def _gla_kernel(*refs):
    n_in = (len(refs) - 4) // 2
    o_f, o_b, s_f, s_b = refs[2 * n_in:]
    _alternate(_gla_dir(*refs[:n_in], o_f, s_f, reverse=False),
               _gla_dir(*refs[n_in:2 * n_in], o_b, s_b, reverse=True))


def _gla_scan(p3, wd_pads, bds):
    nb, t, _ = p3.shape
    nblk = t // TM
    in_specs, args, out_specs = [], [], []
    for reverse in (False, True):
        blk = functools.partial(_scan_block, nblk=nblk, reverse=reverse)
        in_specs += [pl.BlockSpec((None, TM, LANES), lambda b, p, i, blk=blk: (b, blk(i), 6 + p)),
                     pl.BlockSpec((None, TM, LANES), lambda b, p, i, blk=blk: (b, blk(i), 8 + p)),
                     pl.BlockSpec((None, TM, 2 * GLA_DV), lambda b, p, i, blk=blk: (b, blk(i), 6 + p)),
                     pl.BlockSpec((None, TM, LANES), lambda b, p, i, blk=blk: (b, blk(i), 10)),
                     pl.BlockSpec((3, LANES, LANES), lambda b, p, i: (0, 0, p)),
                     pl.BlockSpec((1, LANES), lambda b, p, i: (0, p)),
                     pl.BlockSpec((TM, TM), lambda b, p, i: (0, 0))]
        args += [p3, p3, p3, p3, jnp.stack(_split3(wd_pads[int(reverse)])), bds[int(reverse)],
                 jnp.asarray(_chunk_sum_mat(GLA_CHUNK, reverse), BF16)]
        out_specs.append(pl.BlockSpec((None, TM, 2 * GLA_DV), lambda b, p, i, blk=blk: (b, blk(i), p)))
    state = pltpu.VMEM((2 * GLA_DK, 2 * GLA_DV), F32)
    return pl.pallas_call(
        _gla_kernel,
        grid=(nb, 2, nblk),
        in_specs=in_specs,
        out_specs=out_specs,
        out_shape=[jax.ShapeDtypeStruct((nb, t, GLA_V), BF16)] * 2,
        scratch_shapes=[state, state],
        compiler_params=_cparams(("arbitrary", "arbitrary", "arbitrary")),
        name="gla_scan",
    )(*args)


def _seg_ones(width, seg):
    r = lax.broadcasted_iota(jnp.int32, (width, width), 0) // seg
    c = lax.broadcasted_iota(jnp.int32, (width, width), 1) // seg
    return jnp.where(r == c, 1.0 / seg, 0.0).astype(BF16)


def _seg_mean(x, bd):
    hi = x.astype(BF16)
    lo = (x - hi.astype(F32)).astype(BF16)
    return jnp.dot(hi, bd, preferred_element_type=F32) + jnp.dot(lo, bd, preferred_element_type=F32)


def _norm_rope(x, g, cos, sin, bd):
    ms = _seg_mean(x * x, bd)
    xn = x * lax.rsqrt(ms + EPS) * g
    lane = lax.broadcasted_iota(jnp.int32, xn.shape, 1)
    quarter = GQA_HD // 4
    up = pltpu.roll(xn, LANES - quarter, 1)
    dn = pltpu.roll(xn, quarter, 1)
    rot = jnp.where(lane % (2 * quarter) < quarter, -up, dn)
    return xn * cos + rot * sin


def _qkprep_kernel(q_ref, k_ref, v_ref, cos_ref, sin_ref, gq_ref, gk_ref, qo_ref, ko_ref, vo_ref):
    bd = _seg_ones(LANES, GQA_HD)
    cos = cos_ref[...]
    sin = sin_ref[...]
    lane = lax.broadcasted_iota(jnp.int32, (TM, LANES), 1)
    low = lane < GQA_HD
    extra = lane == GQA_HD
    one_hot = jnp.where(extra, 1.0, 0.0)
    k_bound = math.sqrt(GQA_HD) * jnp.max(jnp.abs(gk_ref[...]), axis=-1, keepdims=True) * 1.02
    for j in range(GQA_Q // LANES):
        x = q_ref[:, j * LANES:(j + 1) * LANES].astype(F32)
        y = _norm_rope(x, gq_ref[...], cos, sin, bd) * (GQA_HD ** -0.5 * LOG2E)
        y = y.astype(BF16).astype(F32)
        norm = jnp.sqrt(_seg_mean(y * y, bd) * GQA_HD)
        shift = ATT_HEADROOM - norm * k_bound
        lo = jnp.where(low, y, jnp.where(extra, pltpu.roll(shift, GQA_HD, 1), 0.0))
        hi = jnp.where(low, pltpu.roll(y, GQA_HD, 1), jnp.where(extra, shift, 0.0))
        qo_ref[:, (2 * j) * LANES:(2 * j + 1) * LANES] = lo.astype(BF16)
        qo_ref[:, (2 * j + 1) * LANES:(2 * j + 2) * LANES] = hi.astype(BF16)
    k = _norm_rope(k_ref[...].astype(F32), gk_ref[...], cos, sin, bd)
    ko_ref[:, 0:LANES] = jnp.where(low, k, one_hot).astype(BF16)
    ko_ref[:, LANES:2 * LANES] = jnp.where(low, pltpu.roll(k, GQA_HD, 1), one_hot).astype(BF16)
    v = v_ref[...].astype(F32)
    vo_ref[0] = jnp.where(low, v, one_hot).T[0:ATT_VROWS].astype(BF16)
    vo_ref[1] = jnp.where(low, pltpu.roll(v, GQA_HD, 1), one_hot).T[0:ATT_VROWS].astype(BF16)


def _qkprep(p2, cos2, sin2, gq2, gk2, tps):
    ntok = p2.shape[0]
    nb = ntok // (tps * TM)
    return pl.pallas_call(
        _qkprep_kernel,
        grid=(ntok // TM,),
        in_specs=[pl.BlockSpec((TM, GQA_Q), lambda i: (i, 0)),
                  pl.BlockSpec((TM, LANES), lambda i: (i, 4)),
                  pl.BlockSpec((TM, LANES), lambda i: (i, 5)),
                  pl.BlockSpec((TM, LANES), lambda i: (i % tps, 0)),
                  pl.BlockSpec((TM, LANES), lambda i: (i % tps, 0)),
                  pl.BlockSpec((1, LANES), lambda i: (0, 0)),
                  pl.BlockSpec((1, LANES), lambda i: (0, 0))],
        out_specs=[pl.BlockSpec((TM, GQA_HEADS * LANES), lambda i: (i, 0)),
                   pl.BlockSpec((TM, GQA_KV_HEADS * LANES), lambda i: (i, 0)),
                   pl.BlockSpec((None, GQA_KV_HEADS, ATT_VROWS, TM), lambda i: (i // tps, 0, 0, i % tps))],
        out_shape=[jax.ShapeDtypeStruct((ntok, GQA_HEADS * LANES), BF16),
                   jax.ShapeDtypeStruct((ntok, GQA_KV_HEADS * LANES), BF16),
                   jax.ShapeDtypeStruct((nb, GQA_KV_HEADS, ATT_VROWS, tps * TM), BF16)],
        compiler_params=_cparams(("arbitrary",)),
        name="qkprep",
    )(p2, p2, p2, cos2, sin2, gq2, gk2)


def _attn_kernel(q_ref, k_ref, vt_ref, o_ref, *, ctx_len, n_lat_chunks, ck, n_blocks):
    i = pl.program_id(2)
    rep = GQA_HEADS // GQA_KV_HEADS
    is_ctx = i < ctx_len // ATT_TQ
    q4 = jnp.concatenate([q_ref[:, r * LANES:(r + 1) * LANES] for r in range(rep)], axis=0)
    nt = (((1,), (1,)), ((), ()))

    def chunk(acc, kc, vtc):
        st = lax.dot_general(kc, q4, nt, preferred_element_type=F32)
        return acc + jnp.dot(vtc, jnp.exp2(st).astype(BF16), preferred_element_type=F32)

    def store(acc):
        out = acc * (1.0 / acc[GQA_HD:GQA_HD + 1, :])
        out = jnp.concatenate([out, jnp.zeros((LANES - ATT_VROWS, rep * ATT_TQ), F32)], axis=0).T
        low = lax.broadcasted_iota(jnp.int32, (ATT_TQ, LANES), 1) < GQA_HD
        hs = [out[r * ATT_TQ:(r + 1) * ATT_TQ] for r in range(rep)]
        for u in range(rep // 2):
            o_ref[:, u * LANES:(u + 1) * LANES] = jnp.where(
                low, hs[2 * u], pltpu.roll(hs[2 * u + 1], GQA_HD, 1)).astype(o_ref.dtype)

    acc = chunk(jnp.zeros((ATT_VROWS, rep * ATT_TQ), F32), k_ref[0:ctx_len, :], vt_ref[:, 0:ctx_len])

    def body(c, acc):
        off = pl.multiple_of(ctx_len + c * ck, TM)
        return chunk(acc, k_ref[pl.ds(off, ck), :], vt_ref[:, pl.ds(off, ck)])

    acc = lax.fori_loop(0, jnp.where(is_ctx, 0, n_lat_chunks), body, acc)
    healthy = jnp.min(acc[GQA_HD:GQA_HD + 1, :]) >= ATT_MIN_DENOM

    @pl.when(healthy)
    def _():
        store(acc)

    @pl.when(jnp.logical_not(healthy))
    def _():
        def online(c, carry):
            m, acc = carry
            off = pl.multiple_of(c * TM, TM)
            st = lax.dot_general(k_ref[pl.ds(off, TM), :], q4, nt, preferred_element_type=F32)
            mn = jnp.maximum(m, jnp.max(st, axis=0, keepdims=True))
            pt = jnp.exp2(st - mn).astype(BF16)
            acc = jnp.exp2(m - mn) * acc + jnp.dot(vt_ref[:, pl.ds(off, TM)], pt, preferred_element_type=F32)
            return mn, acc

        init = (jnp.full((1, rep * ATT_TQ), -1e30, F32), jnp.zeros((ATT_VROWS, rep * ATT_TQ), F32))
        _, acc2 = lax.fori_loop(0, jnp.where(is_ctx, ctx_len // TM, n_blocks), online, init)
        store(acc2)


def _attention(qh3, kg3, vt4, ctx_len):
    nb, t, _ = qh3.shape
    seq = t - ctx_len
    ck = next(c for c in (2048, 1024, 512, 256) if seq % c == 0)
    rep = GQA_HEADS // GQA_KV_HEADS
    return pl.pallas_call(
        functools.partial(_attn_kernel, ctx_len=ctx_len, n_lat_chunks=seq // ck, ck=ck, n_blocks=t // TM),
        grid=(nb, GQA_KV_HEADS, t // ATT_TQ),
        in_specs=[pl.BlockSpec((None, ATT_TQ, rep * LANES), lambda b, g, i: (b, i, g)),
                  pl.BlockSpec((None, t, LANES), lambda b, g, i: (b, 0, g)),
                  pl.BlockSpec((None, None, ATT_VROWS, t), lambda b, g, i: (b, g, 0, 0))],
        out_specs=pl.BlockSpec((None, ATT_TQ, rep * GQA_HD), lambda b, g, i: (b, i, g)),
        out_shape=jax.ShapeDtypeStruct((nb, t, GQA_Q), BF16),
        compiler_params=_cparams(("arbitrary", "arbitrary", "arbitrary")),
        name="gqa_attn",
    )(qh3, kg3, vt4)


def _ln(x, g, b):
    mu = jnp.mean(x, axis=-1, keepdims=True)
    xc = x - mu
    var = jnp.mean(xc * xc, axis=-1, keepdims=True)
    return xc * lax.rsqrt(var + EPS) * g + b


def _evout_kernel(of_ref, ob_ref, r_ref, att_ref, g_ref, w1_ref, w2_ref, x_ref, mod_ref, lg_ref, lb_ref, rw_ref,
                  rb_ref, o_ref, *route_refs):
    o = of_ref[...].astype(F32) + ob_ref[...].astype(F32)
    r = r_ref[...].astype(F32)
    parts = []
    for h in range(GLA_HEADS):
        oh = o[:, h * GLA_DV:(h + 1) * GLA_DV]
        ms = jnp.mean(oh * oh, axis=-1, keepdims=True)
        parts.append(oh * lax.rsqrt(ms + EPS))
    gl = (jnp.concatenate(parts, axis=1) * g_ref[...] * _silu(r)).astype(BF16)
    y = jnp.dot(gl, w1_ref[...], preferred_element_type=F32)
    y = y + jnp.dot(att_ref[...], w2_ref[...], preferred_element_type=F32)
    gate = mod_ref[0, 2:3, :]
    x_new = _ln(ALPHA * x_ref[...] + gate * y, lg_ref[...], lb_ref[...])
    o_ref[...] = x_new
    _route_tile(x_new, mod_ref, rw_ref, rb_ref, *route_refs)


def _evout(of2, ob2, p2, att2, g512, w1, w2, x2, mod, lg, lb, rw_t, rb, tps, nb):
    ntok, d = x2.shape
    row = lambda i: (i, 0)
    const = lambda i: (0, 0)
    r_in, r_out, r_shapes = _router_specs(ntok, d)
    return pl.pallas_call(
        _evout_kernel,
        grid=(ntok // TM,),
        in_specs=[pl.BlockSpec((TM, GLA_V), row), pl.BlockSpec((TM, GLA_V), row),
                  pl.BlockSpec((TM, GLA_V), lambda i: (i, 4)),
                  pl.BlockSpec((TM, GQA_Q), row),
                  pl.BlockSpec((1, GLA_V), const),
                  pl.BlockSpec((GLA_V, d), const), pl.BlockSpec((GQA_Q, d), const),
                  pl.BlockSpec((TM, d), row),
                  pl.BlockSpec((1, 6, d), lambda i: (_mod_row(i, tps, nb), 0, 0)),
                  pl.BlockSpec((1, d), const), pl.BlockSpec((1, d), const)] + r_in,
        out_specs=[pl.BlockSpec((TM, d), row)] + r_out,
        out_shape=[jax.ShapeDtypeStruct((ntok, d), F32)] + r_shapes,
        compiler_params=_cparams(("arbitrary",)),
        name="even_out",
    )(of2, ob2, p2, att2, g512, w1, w2, x2, mod, lg, lb, rw_t, rb)


def _conv_kernel(prev_ref, cur_ref, next_ref, w_ref, b_ref, o_ref, pad_ref, *, tps):
    j = pl.program_id(0) % tps
    has_prev = (j >= 2).astype(F32)
    has_next = jnp.logical_and(j >= 1, j <= tps - 2).astype(F32)
    pad_ref[0:8, :] = prev_ref[...].astype(F32) * has_prev
    pad_ref[8:8 + TM, :] = cur_ref[...].astype(F32)
    pad_ref[8 + TM:16 + TM, :] = next_ref[...].astype(F32) * has_next
    half = (SSD_CONV - 1) // 2
    acc = jnp.zeros((TM, cur_ref.shape[1]), F32) + b_ref[...]
    for tap in range(SSD_CONV):
        acc = acc + pad_ref[pl.ds(8 + tap - half, TM), :] * w_ref[tap:tap + 1, :]
    o_ref[...] = _silu(acc).astype(o_ref.dtype)


def _conv(p2, w8, bias, tps):
    ntok = p2.shape[0]
    cw = 256
    ncb = SSD_CONV_CH // cw
    cb0 = SSD_INNER // cw
    r8 = TM // 8
    nrow8 = ntok // 8
    return pl.pallas_call(
        functools.partial(_conv_kernel, tps=tps),
        grid=(ntok // TM, ncb),
        in_specs=[pl.BlockSpec((8, cw), lambda i, c: (jnp.maximum(i * r8 - 1, 0), cb0 + c)),
                  pl.BlockSpec((TM, cw), lambda i, c: (i, cb0 + c)),
                  pl.BlockSpec((8, cw), lambda i, c: (jnp.minimum((i + 1) * r8, nrow8 - 1), cb0 + c)),
                  pl.BlockSpec((8, cw), lambda i, c: (0, c)),
                  pl.BlockSpec((1, cw), lambda i, c: (0, c))],
        out_specs=pl.BlockSpec((TM, cw), lambda i, c: (i, c)),
        out_shape=jax.ShapeDtypeStruct((ntok, SSD_CONV_CH), BF16),
        scratch_shapes=[pltpu.VMEM((TM + 16, cw), F32)],
        compiler_params=_cparams(("arbitrary", "arbitrary")),
        name="ssd_conv",
    )(p2, p2, p2, w8, bias)


def _ssd_dir(xbc_ref, dt_ref, bias_ref, a_ref, eexp_ref, esel_ref, cm_ref, o_ref, h_ref, *, reverse):
    L = SSD_CHUNK
    GW = SSD_HPG * SSD_HD
    nchunk = TM // L
    nt = (((1,), (1,)), ((), ()))
    tn = (((0,), (0,)), ((), ()))
    end = 0 if reverse else L - 1
    lane0 = SSD_HEADS if reverse else 0

    @pl.when(pl.program_id(1) == 0)
    def _():
        h_ref[...] = jnp.zeros_like(h_ref)

    xs = xbc_ref[:, 0:SSD_INNER].astype(F32)
    bs = xbc_ref[:, SSD_INNER:SSD_INNER + SSD_BC]
    cs = xbc_ref[:, SSD_INNER + SSD_BC:SSD_CONV_CH]
    dt = _softplus(dt_ref[...].astype(F32) + bias_ref[...])
    a = dt * a_ref[...]
    cm = cm_ref[...]
    cum = sum(jnp.dot(cm, part, preferred_element_type=F32) for part in _split3(a))
    yield
    cum_t = sum(lax.dot_general(esel_ref[...], part, nt, preferred_element_type=F32)
                for part in _split3(cum))
    cend = _chunk_rows(cum, L, end)
    eexp = eexp_ref[...]

    def expand(m):
        return jnp.dot(m.astype(BF16), eexp, preferred_element_type=F32)

    xdt = xs * expand(dt)
    xdt_b = xdt.astype(BF16)
    yield
    xend = (xdt * expand(jnp.exp(cend - cum))).astype(BF16)
    yield
    ecum = expand(jnp.exp(cum))
    tot = jnp.concatenate([cum[ch * L + end:ch * L + end + 1, :] for ch in range(nchunk)]
                          + [jnp.zeros((ROWG - nchunk, LANES), F32)], axis=0)
    edec = jnp.exp(sum(jnp.dot(part, eexp, preferred_element_type=F32) for part in _split3(tot)))

    tri = _tri(L, reverse)
    low = lax.broadcasted_iota(jnp.int32, (L, LANES), 1) < SSD_HD
    intra = [[None] * SSD_GROUPS for _ in range(nchunk)]
    dh = [[None] * SSD_GROUPS for _ in range(nchunk)]
    for ch in range(nchunk):
        rows = slice(ch * L, (ch + 1) * L)
        for g in range(SSD_GROUPS):
            bg = bs[rows, g * SSD_STATE:(g + 1) * SSD_STATE]
            cg = cs[rows, g * SSD_STATE:(g + 1) * SSD_STATE]
            cb = lax.dot_general(cg, bg, nt, preferred_element_type=F32)
            pairs = []
            for pp in range(SSD_HPG // 2):
                yh = []
                for u in range(2):
                    h = g * SSD_HPG + 2 * pp + u
                    seg = cum[rows, lane0 + h:lane0 + h + 1] - cum_t[h:h + 1, rows]
                    dec = jnp.exp(jnp.where(tri, seg, -1e30))
                    mat = (cb * dec).astype(BF16)
                    col = (h - u) * SSD_HD
                    yh.append(jnp.dot(mat, xdt_b[rows, col:col + LANES], preferred_element_type=F32))
                pairs.append(jnp.where(low, yh[0], yh[1]))
                yield
            intra[ch][g] = jnp.concatenate(pairs, axis=1)
            dh[ch][g] = lax.dot_general(bg, xend[rows, g * GW:(g + 1) * GW], tn, preferred_element_type=F32)

    hs = [h_ref[g] for g in range(SSD_GROUPS)]
    out = [None] * nchunk
    for ch in (range(nchunk - 1, -1, -1) if reverse else range(nchunk)):
        rows = slice(ch * L, (ch + 1) * L)
        ys = []
        for g in range(SSD_GROUPS):
            cg = cs[rows, g * SSD_STATE:(g + 1) * SSD_STATE]
            y_inter = jnp.dot(cg, hs[g].astype(BF16), preferred_element_type=F32) * ecum[rows, g * GW:(g + 1) * GW]
            ys.append(intra[ch][g] + y_inter)
            hs[g] = hs[g] * edec[ch:ch + 1, g * GW:(g + 1) * GW] + dh[ch][g]
        out[ch] = jnp.concatenate(ys, axis=1)
        yield
    for g in range(SSD_GROUPS):
        h_ref[g] = hs[g]
    o_ref[...] = jnp.concatenate(out, axis=0).astype(o_ref.dtype)


def _ssd_kernel(*refs):
    n_in = (len(refs) - 4) // 2
    o_f, o_b, h_f, h_b = refs[2 * n_in:]
    _alternate(_ssd_dir(*refs[:n_in], o_f, h_f, reverse=False),
               _ssd_dir(*refs[n_in:2 * n_in], o_b, h_b, reverse=True))


def _ssd_scan(xbc3, p3, params):
    nb, t, _ = xbc3.shape
    nblk = t // TM
    const = lambda b, i: (0, 0)
    in_specs, args, out_specs = [], [], []
    for reverse in (False, True):
        blk = functools.partial(_scan_block, nblk=nblk, reverse=reverse)
        bias_pad, a_pad, eexp, esel = params[int(reverse)]
        in_specs += [pl.BlockSpec((None, TM, SSD_CONV_CH), lambda b, i, blk=blk: (b, blk(i), 0)),
                     pl.BlockSpec((None, TM, LANES), lambda b, i, blk=blk: (b, blk(i), 16)),
                     pl.BlockSpec((1, LANES), const), pl.BlockSpec((1, LANES), const),
                     pl.BlockSpec((LANES, SSD_INNER), const), pl.BlockSpec((16, LANES), const),
                     pl.BlockSpec((TM, TM), const)]
        args += [xbc3, p3, bias_pad, a_pad, eexp.astype(BF16), esel.astype(BF16),
                 jnp.asarray(_chunk_sum_mat(SSD_CHUNK, reverse), BF16)]
        out_specs.append(pl.BlockSpec((None, TM, SSD_INNER), lambda b, i, blk=blk: (b, blk(i), 0)))
    state = pltpu.VMEM((SSD_GROUPS, SSD_STATE, SSD_HPG * SSD_HD), F32)
    return pl.pallas_call(
        _ssd_kernel,
        grid=(nb, nblk),
        in_specs=in_specs,
        out_specs=out_specs,
        out_shape=[jax.ShapeDtypeStruct((nb, t, SSD_INNER), BF16)] * 2,
        scratch_shapes=[state, state],
        compiler_params=_cparams(("arbitrary", "arbitrary")),
        name="ssd_scan",
    )(*args)


def _dft_mats(n):
    k = jnp.arange(n, dtype=jnp.int32)
    ang = ((k[:, None] * k[None, :]) % n).astype(F32) * (2.0 * math.pi / n)
    return jnp.cos(ang), jnp.sin(ang)


def _chan_mats():
    cc, sc = _dft_mats(FNET_GC)
    eye = jnp.eye(FNET_GROUPS, dtype=F32)
    return jnp.kron(eye, cc), jnp.kron(eye, sc)


def _group_rms(x, g, bd):
    return x * lax.rsqrt(_seg_mean(x * x, bd) + EPS) * g


def _split2(x):
    hi = x.astype(BF16)
    return hi, (x - hi.astype(F32)).astype(BF16)


def _mm(a, b):
    a_hi, a_lo = _split2(a)
    b_hi, b_lo = _split2(b)
    dot = functools.partial(jnp.dot, preferred_element_type=F32)
    return dot(a_hi, b_hi) + (dot(a_hi, b_lo) + dot(a_lo, b_hi))


def _ffta_kernel(x_ref, g_ref, cc_ref, sc_ref, c1_ref, s1_ref, twc_ref, tws_ref, yr_ref, yi_ref, *, nb2):
    bd = _seg_ones(FNET_W, FNET_GC)
    for j in range(nb2):
        x = x_ref[:, j * FNET_W:(j + 1) * FNET_W].astype(F32)
        xn = _group_rms(x, g_ref[...], bd)
        vr = _mm(xn, cc_ref[...])
        vi = -_mm(xn, sc_ref[...])
        c1 = c1_ref[...]
        s1 = s1_ref[...]
        yr = _mm(c1, vr) + _mm(s1, vi)
        yi = _mm(c1, vi) - _mm(s1, vr)
        tc = twc_ref[j]
        ts = tws_ref[j]
        yr_ref[j] = yr * tc + yi * ts
        yi_ref[j] = yi * tc - yr * ts


def _fftb_kernel(yr_ref, yi_ref, c2_ref, s2_ref, o_ref, *, scale):
    o_ref[...] = (_mm(c2_ref[...], yr_ref[...]) + _mm(s2_ref[...], yi_ref[...])) * scale


def _fftc_kernel(x_ref, g_ref, cc_ref, sc_ref, ct_ref, st_ref, o_ref, *, scale):
    bd = _seg_ones(FNET_W, FNET_GC)
    xn = _group_rms(x_ref[...].astype(F32), g_ref[...], bd)
    a = _mm(xn, cc_ref[...])
    b = _mm(xn, sc_ref[...])
    o_ref[...] = (_mm(ct_ref[...], a) - _mm(st_ref[...], b)) * scale


def _fourier_latent(f_lat, g256):
    nb, s, _ = f_lat.shape
    n2 = 64
    n1 = s // n2
    nb2 = 8
    ccm, scm = _chan_mats()
    c1, s1 = _dft_mats(n1)
    c2, s2 = _dft_mats(n2)
    t2 = jnp.arange(n2, dtype=jnp.int32)[:, None]
    k1 = jnp.arange(n1, dtype=jnp.int32)[None, :]
    tw = ((t2 * k1) % s).astype(F32) * (2.0 * math.pi / s)
    twc = jnp.cos(tw)[:, :, None]
    tws = jnp.sin(tw)[:, :, None]
    x2 = f_lat.reshape(nb, n1, n2 * FNET_W)
    const2 = lambda b, j: (0, 0)
    yr, yi = pl.pallas_call(
        functools.partial(_ffta_kernel, nb2=nb2),
        grid=(nb, n2 // nb2),
        in_specs=[pl.BlockSpec((None, n1, nb2 * FNET_W), lambda b, j: (b, 0, j)),
                  pl.BlockSpec((1, FNET_W), const2),
                  pl.BlockSpec((FNET_W, FNET_W), const2), pl.BlockSpec((FNET_W, FNET_W), const2),
                  pl.BlockSpec((n1, n1), const2), pl.BlockSpec((n1, n1), const2),
                  pl.BlockSpec((nb2, n1, 1), lambda b, j: (j, 0, 0)),
                  pl.BlockSpec((nb2, n1, 1), lambda b, j: (j, 0, 0))],
        out_specs=[pl.BlockSpec((None, nb2, n1, FNET_W), lambda b, j: (b, j, 0, 0)),
                   pl.BlockSpec((None, nb2, n1, FNET_W), lambda b, j: (b, j, 0, 0))],
        out_shape=[jax.ShapeDtypeStruct((nb, n2, n1, FNET_W), F32),
                   jax.ShapeDtypeStruct((nb, n2, n1, FNET_W), F32)],
        compiler_params=_cparams(("arbitrary", "arbitrary")),
        name="fft_a",
    )(x2, g256, ccm, scm, c1, s1, twc, tws)
    ncol = n1 * FNET_W
    tn = min(2048, ncol)
    out = pl.pallas_call(
        functools.partial(_fftb_kernel, scale=1.0 / math.sqrt(s * FNET_GC)),
        grid=(nb, ncol // tn),
        in_specs=[pl.BlockSpec((None, n2, tn), lambda b, j: (b, 0, j)),
                  pl.BlockSpec((None, n2, tn), lambda b, j: (b, 0, j)),
                  pl.BlockSpec((n2, n2), const2), pl.BlockSpec((n2, n2), const2)],
        out_specs=pl.BlockSpec((None, n2, tn), lambda b, j: (b, 0, j)),
        out_shape=jax.ShapeDtypeStruct((nb, n2, ncol), F32),
        compiler_params=_cparams(("arbitrary", "arbitrary")),
        name="fft_b",
    )(yr.reshape(nb, n2, ncol), yi.reshape(nb, n2, ncol), c2, s2)
    return out.reshape(nb, s, FNET_W)


def _fourier_ctx(f_ctx, g256):
    nb, tc, _ = f_ctx.shape
    ccm, scm = _chan_mats()
    ct, st = _dft_mats(tc)
    const = lambda b: (0, 0)
    return pl.pallas_call(
        functools.partial(_fftc_kernel, scale=1.0 / math.sqrt(tc * FNET_GC)),
        grid=(nb,),
        in_specs=[pl.BlockSpec((None, tc, FNET_W), lambda b: (b, 0, 0)),
                  pl.BlockSpec((1, FNET_W), const),
                  pl.BlockSpec((FNET_W, FNET_W), const), pl.BlockSpec((FNET_W, FNET_W), const),
                  pl.BlockSpec((tc, tc), const), pl.BlockSpec((tc, tc), const)],
        out_specs=pl.BlockSpec((None, tc, FNET_W), lambda b: (b, 0, 0)),
        out_shape=jax.ShapeDtypeStruct((nb, tc, FNET_W), F32),
        compiler_params=_cparams(("arbitrary",)),
        name="fft_ctx",
    )(f_ctx, g256, ccm, scm, ct, st)


def _odout_kernel(yf_ref, yb_ref, xs_ref, z_ref, f_ref, dsk_ref, g_ref, w1_ref, w2_ref, x_ref, mod_ref,
                  lg_ref, lb_ref, rw_ref, rb_ref, o_ref, *route_refs):
    GW = SSD_HPG * SSD_HD
    y = yf_ref[...].astype(F32) + yb_ref[...].astype(F32) + xs_ref[...].astype(F32) * dsk_ref[...]
    y = y * _silu(z_ref[...].astype(F32))
    parts = []
    for g in range(SSD_GROUPS):
        yg = y[:, g * GW:(g + 1) * GW]
        ms = jnp.mean(yg * yg, axis=-1, keepdims=True)
        parts.append(yg * lax.rsqrt(ms + EPS))
    yn = (jnp.concatenate(parts, axis=1) * g_ref[...]).astype(BF16)
    o = jnp.dot(yn, w1_ref[...], preferred_element_type=F32)
    o = o + jnp.dot(f_ref[...].astype(BF16), w2_ref[...], preferred_element_type=F32)
    gate = mod_ref[0, 2:3, :]
    x_new = _ln(ALPHA * x_ref[...] + gate * o, lg_ref[...], lb_ref[...])
    o_ref[...] = x_new
    _route_tile(x_new, mod_ref, rw_ref, rb_ref, *route_refs)


def _odout(yf2, yb2, xbc2, p2, f2, dsk, g768, w1, w2, x2, mod, lg, lb, rw_t, rb, tps, nb):
    ntok, d = x2.shape
    row = lambda i: (i, 0)
    const = lambda i: (0, 0)
    r_in, r_out, r_shapes = _router_specs(ntok, d)
    return pl.pallas_call(
        _odout_kernel,
        grid=(ntok // TM,),
        in_specs=[pl.BlockSpec((TM, SSD_INNER), row), pl.BlockSpec((TM, SSD_INNER), row),
                  pl.BlockSpec((TM, SSD_INNER), row), pl.BlockSpec((TM, SSD_INNER), row),
                  pl.BlockSpec((TM, FNET_W), row),
                  pl.BlockSpec((1, SSD_INNER), const), pl.BlockSpec((1, SSD_INNER), const),
                  pl.BlockSpec((SSD_INNER, d), const), pl.BlockSpec((FNET_W, d), const),
                  pl.BlockSpec((TM, d), row),
                  pl.BlockSpec((1, 6, d), lambda i: (_mod_row(i, tps, nb), 0, 0)),
                  pl.BlockSpec((1, d), const), pl.BlockSpec((1, d), const)] + r_in,
        out_specs=[pl.BlockSpec((TM, d), row)] + r_out,
        out_shape=[jax.ShapeDtypeStruct((ntok, d), F32)] + r_shapes,
        compiler_params=_cparams(("arbitrary",)),
        name="odd_out",
    )(yf2, yb2, xbc2, p2, f2, dsk, g768, w1, w2, x2, mod, lg, lb, rw_t, rb)


def _route_tile(x, mod_ref, rw_ref, rb_ref, v_ref, idx_ref, gw_ref, cnt_ref):
    @pl.when(pl.program_id(0) == 0)
    def _():
        cnt_ref[...] = jnp.zeros_like(cnt_ref)

    shift = mod_ref[0, 3:4, :]
    scale = mod_ref[0, 4:5, :]
    v = x * (1.0 + scale) + shift
    v_hi = v.astype(BF16)
    v_ref[...] = v_hi
    v_lo = (v - v_hi.astype(F32)).astype(BF16)
    w_hi, w_lo = _split2(rw_ref[...])
    nt_dot = functools.partial(lax.dot_general, dimension_numbers=(((1,), (1,)), ((), ())),
                               preferred_element_type=F32)
    logits = nt_dot(w_hi, v_hi) + (nt_dot(w_hi, v_lo) + nt_dot(w_lo, v_hi))
    s = 1.0 / (1.0 + jnp.exp(-logits))
    sel = s + rb_ref[:, 0:1]
    izero = jnp.zeros((1, TM), jnp.int32)
    best = None
    for g in range(N_EXPERT_GROUPS):
        a = [sel[g * EXPERTS_PER_GROUP + j:g * EXPERTS_PER_GROUP + j + 1, :] for j in range(EXPERTS_PER_GROUP)]
        sv = [s[g * EXPERTS_PER_GROUP + j:g * EXPERTS_PER_GROUP + j + 1, :] for j in range(EXPERTS_PER_GROUP)]
        m1, i1, s1 = a[0], izero, sv[0]
        for j in range(1, EXPERTS_PER_GROUP):
            gt = a[j] > m1
            m1 = jnp.where(gt, a[j], m1)
            i1 = jnp.where(gt, j, i1)
            s1 = jnp.where(gt, sv[j], s1)
        m2 = jnp.full((1, TM), -jnp.inf, F32)
        i2, s2 = izero, jnp.zeros((1, TM), F32)
        for j in range(EXPERTS_PER_GROUP):
            gt = jnp.logical_and(i1 != j, a[j] > m2)
            m2 = jnp.where(gt, a[j], m2)
            i2 = jnp.where(gt, j, i2)
            s2 = jnp.where(gt, sv[j], s2)
        cand = (m1 + m2, i1 + g * EXPERTS_PER_GROUP, i2 + g * EXPERTS_PER_GROUP, s1, s2)
        if best is None:
            best = cand
        else:
            gt = cand[0] > best[0]
            best = tuple(jnp.where(gt, cn, bs) for cn, bs in zip(cand, best))
    _, e1, e2, w1, w2 = best
    tot = w1 + w2
    idx_ref[0] = jnp.concatenate([e1, e2], axis=0)
    gw_ref[0] = jnp.concatenate([w1 / tot, w2 / tot], axis=0)
    eio = lax.broadcasted_iota(jnp.int32, (N_EXPERTS, TM), 0)
    oh = jnp.logical_or(eio == e1, eio == e2).astype(F32)
    cnt_ref[...] += _ceil_rows(jnp.sum(oh, axis=1, keepdims=True))


def _router_specs(ntok, d):
    nt = ntok // TM
    in_specs = [pl.BlockSpec((N_EXPERTS, d), lambda i: (0, 0)),
                pl.BlockSpec((N_EXPERTS, LANES), lambda i: (0, 0))]
    out_specs = [pl.BlockSpec((TM, d), lambda i: (i, 0)),
                 pl.BlockSpec((1, TOP_K, TM), lambda i: (i, 0, 0)),
                 pl.BlockSpec((1, TOP_K, TM), lambda i: (i, 0, 0)),
                 pl.BlockSpec((N_EXPERTS, LANES), lambda i: (0, 0))]
    out_shapes = [jax.ShapeDtypeStruct((ntok, d), BF16),
                  jax.ShapeDtypeStruct((nt, TOP_K, TM), jnp.int32),
                  jax.ShapeDtypeStruct((nt, TOP_K, TM), F32),
                  jax.ShapeDtypeStruct((N_EXPERTS, LANES), F32)]
    return in_specs, out_specs, out_shapes


def _ceil_rows(c):
    return jnp.ceil(c * (1.0 / ROWG)) * ROWG


def _meta_kernel(idx_ref, start_ref, qpos_ref, meta_ref, run_ref):
    @pl.when(pl.program_id(0) == 0)
    def _():
        run_ref[...] = jnp.zeros_like(run_ref)

    e1 = idx_ref[0, 0:1, :]
    e2 = idx_ref[0, 1:2, :]
    eio = lax.broadcasted_iota(jnp.int32, (N_EXPERTS, TM), 0)
    oh1 = eio == e1
    oh2 = eio == e2
    oh = jnp.logical_or(oh1, oh2)
    r = lax.broadcasted_iota(jnp.int32, (TM, TM), 0)
    c = lax.broadcasted_iota(jnp.int32, (TM, TM), 1)
    before = (r < c).astype(BF16)
    rank = jnp.dot(oh.astype(BF16), before, preferred_element_type=F32)
    run_len = jnp.broadcast_to(_ceil_rows(jnp.sum(oh.astype(F32), axis=1, keepdims=True)), (N_EXPERTS, LANES))
    er = lax.broadcasted_iota(jnp.int32, (N_EXPERTS, N_EXPERTS), 0)
    ec = lax.broadcasted_iota(jnp.int32, (N_EXPERTS, N_EXPERTS), 1)
    off = jnp.dot((ec < er).astype(F32), run_len, precision=HI, preferred_element_type=F32)
    pos = rank + off[:, 0:1]
    q1 = jnp.sum(jnp.where(oh1, pos, 0.0), axis=0, keepdims=True)
    q2 = jnp.sum(jnp.where(oh2, pos, 0.0), axis=0, keepdims=True)
    qpos_ref[0] = jnp.concatenate([q1, q2], axis=0).astype(jnp.int32)
    lane = lax.broadcasted_iota(jnp.int32, (N_EXPERTS, LANES), 1)
    meta = jnp.where(lane == 0, start_ref[...] + run_ref[...], jnp.where(lane == 1, off, run_len))
    meta_ref[0] = (meta * (1.0 / ROWG)).astype(jnp.int32)
    run_ref[...] += run_len


def _meta(idx_t, start):
    nt = idx_t.shape[0]
    return pl.pallas_call(
        _meta_kernel,
        grid=(nt,),
        in_specs=[pl.BlockSpec((1, TOP_K, TM), lambda i: (i, 0, 0)),
                  pl.BlockSpec((N_EXPERTS, LANES), lambda i: (0, 0))],
        out_specs=[pl.BlockSpec((1, TOP_K, TM), lambda i: (i, 0, 0)),
                   pl.BlockSpec((1, N_EXPERTS, LANES), lambda i: (i, 0, 0))],
        out_shape=[jax.ShapeDtypeStruct((nt, TOP_K, TM), jnp.int32),
                   jax.ShapeDtypeStruct((nt, N_EXPERTS, LANES), jnp.int32)],
        scratch_shapes=[pltpu.VMEM((N_EXPERTS, LANES), F32)],
        compiler_params=_cparams(("arbitrary",)),
        name="moe_meta",
    )(idx_t, start)


def _sort_select(qpos_ref):
    r = lax.broadcasted_iota(jnp.int32, (SORT_ROWS, TM), 0)
    return r == qpos_ref[0, 0:1, :], r == qpos_ref[0, 1:2, :]


def _start_run_copies(i, gb_ref, ob_ref, nb_ref, make_copy):
    big = COPY_GROUPS * ROWG
    for e in range(N_EXPERTS):
        n = nb_ref[i * N_EXPERTS + e]
        buf_row = gb_ref[i * N_EXPERTS + e] * ROWG
        tile_row = ob_ref[i * N_EXPERTS + e] * ROWG
        n_big = lax.shift_right_logical(n, COPY_SHIFT)

        def body_big(k, carry, buf_row=buf_row, tile_row=tile_row):
            make_copy(pl.multiple_of(tile_row + k * big, ROWG), pl.multiple_of(buf_row + k * big, ROWG), big).start()
            return carry

        def body_one(k, carry, buf_row=buf_row, tile_row=tile_row):
            make_copy(pl.multiple_of(tile_row + k * ROWG, ROWG), pl.multiple_of(buf_row + k * ROWG, ROWG),
                      ROWG).start()
            return carry

        lax.fori_loop(0, n_big, body_big, 0)
        lax.fori_loop(lax.shift_left(n_big, COPY_SHIFT), n, body_one, 0)


def _wait_run_copies(i, nb_ref, make_copy):
    n_big = 0
    n_one = 0
    for e in range(N_EXPERTS):
        n = nb_ref[i * N_EXPERTS + e]
        n_big = n_big + lax.shift_right_logical(n, COPY_SHIFT)
        n_one = n_one + lax.bitwise_and(n, COPY_GROUPS - 1)

    def wait_big(k, carry):
        make_copy(0, 0, COPY_GROUPS * ROWG).wait()
        return carry

    def wait_one(k, carry):
        make_copy(0, 0, ROWG).wait()
        return carry

    lax.fori_loop(0, n_big, wait_big, 0)
    lax.fori_loop(0, n_one, wait_one, 0)


def _scatter_kernel(gb_ref, ob_ref, nb_ref, ts_ref, tn_ref, nu_ref, v_ref, qpos_ref, gw_ref, buf_ref, xs_ref, z_ref,
                    sem, zsem):
    i = pl.program_id(0)
    last = pl.num_programs(0) - 1
    slot = i % 2
    d = v_ref.shape[1]
    m0, m1 = _sort_select(qpos_ref)
    sel = jnp.logical_or(m0, m1).astype(BF16)
    xs_ref[slot, :, 0:d] = jnp.dot(sel, v_ref[...], preferred_element_type=F32)
    gate = jnp.sum(jnp.where(m0, gw_ref[0, 0:1, :], 0.0) + jnp.where(m1, gw_ref[0, 1:2, :], 0.0),
                   axis=1, keepdims=True)
    xs_ref[slot, :, d:d + LANES] = jnp.broadcast_to(gate, (SORT_ROWS, LANES))

    def copy_from(s):
        def copy(tile_row, buf_row, rows):
            return pltpu.make_async_copy(xs_ref.at[s, pl.ds(tile_row, rows)], buf_ref.at[pl.ds(buf_row, rows)],
                                         sem.at[s])
        return copy

    _start_run_copies(i, gb_ref, ob_ref, nb_ref, copy_from(slot))

    @pl.when(i > 0)
    def _():
        _wait_run_copies(i - 1, nb_ref, copy_from(1 - slot))

    @pl.when(i == last)
    def _():
        _wait_run_copies(i, nb_ref, copy_from(slot))

    @pl.when(i == last)
    def _():
        z_ref[...] = jnp.zeros_like(z_ref)

        def zcopy(buf_row):
            return pltpu.make_async_copy(z_ref.at[pl.ds(0, ROWG)], buf_ref.at[pl.ds(buf_row, ROWG)], zsem)

        total = 0
        for e in range(N_EXPERTS):
            n = tn_ref[e]
            row0 = ts_ref[e] * ROWG

            def body(k, carry, row0=row0):
                zcopy(pl.multiple_of(row0 + k * ROWG, ROWG)).start()
                return carry

            lax.fori_loop(0, n, body, 0)
            total = total + n

        def wbody(k, carry):
            zcopy(0).wait()
            return carry

        lax.fori_loop(0, total, wbody, 0)

        def zblock(blk):
            return pltpu.make_async_copy(z_ref, buf_ref.at[pl.ds(pl.multiple_of(blk * FFN_ROWS, FFN_ROWS), FFN_ROWS)],
                                         zsem)

        def bbody(blk, carry):
            zblock(blk).start()
            return carry

        def bwait(blk, carry):
            zblock(blk).wait()
            return carry

        nblk = buf_ref.shape[0] // FFN_ROWS
        lax.fori_loop(nu_ref[0], nblk, bbody, 0)
        lax.fori_loop(nu_ref[0], nblk, bwait, 0)


def _scatter(gb, ob, nb8, tail_s, tail_n, n_used, v, qpos, gw_t, nrow):
    ntok, d = v.shape
    grid_spec = pltpu.PrefetchScalarGridSpec(
        num_scalar_prefetch=6,
        grid=(ntok // TM,),
        in_specs=[pl.BlockSpec((TM, d), lambda i, *_: (i, 0)),
                  pl.BlockSpec((1, TOP_K, TM), lambda i, *_: (i, 0, 0)),
                  pl.BlockSpec((1, TOP_K, TM), lambda i, *_: (i, 0, 0))],
        out_specs=pl.BlockSpec(memory_space=pl.ANY),
        scratch_shapes=[pltpu.VMEM((2, SORT_ROWS, d + LANES), F32), pltpu.VMEM((FFN_ROWS, d + LANES), F32),
                        pltpu.SemaphoreType.DMA((2,)), pltpu.SemaphoreType.DMA(())],
    )
    return pl.pallas_call(
        _scatter_kernel,
        grid_spec=grid_spec,
        out_shape=jax.ShapeDtypeStruct((nrow, d + LANES), F32),
        compiler_params=_cparams(("arbitrary",)),
        name="moe_scatter",
    )(gb, ob, nb8, tail_s, tail_n, n_used, v, qpos, gw_t)


def _moe_kernel(be_ref, nu_ref, x_ref, wg_ref, wu_ref, wd_ref, o_ref, wg_bf, wu_bf, wd_bf):
    i = pl.program_id(0)
    d = o_ref.shape[1]

    @pl.when(jnp.logical_or(i == 0, be_ref[i] != be_ref[jnp.maximum(i - 1, 0)]))
    def _():
        wg_bf[...] = wg_ref[...].astype(BF16)
        wu_bf[...] = wu_ref[...].astype(BF16)
        wd_bf[...] = wd_ref[...].astype(BF16)

    @pl.when(i < nu_ref[0])
    def _():
        xb = x_ref[...]
        x = xb[:, 0:d].astype(BF16)
        g = jnp.dot(x, wg_bf[...], preferred_element_type=F32)
        u = jnp.dot(x, wu_bf[...], preferred_element_type=F32)
        h = (_silu(g) * u).astype(BF16)
        y = jnp.dot(h, wd_bf[...], preferred_element_type=F32)
        o_ref[...] = y * xb[:, d:d + 1]

    @pl.when(pl.program_id(0) >= nu_ref[0])
    def _():
        o_ref[...] = jnp.zeros_like(o_ref)


def _moe_ffn(block_e, n_used, buf, layer, wg, wu, wd):
    nrow, dw = buf.shape
    d = dw - LANES
    nblk = nrow // FFN_ROWS
    de = wg.shape[3]
    used = lambda i, nu: jnp.minimum(i, nu[0] - 1)
    grid_spec = pltpu.PrefetchScalarGridSpec(
        num_scalar_prefetch=2,
        grid=(nblk,),
        in_specs=[pl.BlockSpec((FFN_ROWS, dw), lambda i, be, nu: (used(i, nu), 0)),
                  pl.BlockSpec((None, None, d, de), lambda i, be, nu: (layer, be[i], 0, 0)),
                  pl.BlockSpec((None, None, d, de), lambda i, be, nu: (layer, be[i], 0, 0)),
                  pl.BlockSpec((None, None, de, d), lambda i, be, nu: (layer, be[i], 0, 0))],
        out_specs=pl.BlockSpec((FFN_ROWS, d), lambda i, be, nu: (i, 0)),
        scratch_shapes=[pltpu.VMEM((d, de), BF16), pltpu.VMEM((d, de), BF16), pltpu.VMEM((de, d), BF16)],
    )
    return pl.pallas_call(
        _moe_kernel,
        grid_spec=grid_spec,
        out_shape=jax.ShapeDtypeStruct((nrow, d), F32),
        compiler_params=pltpu.CompilerParams(dimension_semantics=("arbitrary",), vmem_limit_bytes=FFN_VMEM_LIMIT),
        name="moe_ffn",
    )(block_e, n_used, buf, wg, wu, wd)


def _ln2_kernel(gb_ref, ob_ref, nb_ref, x_ref, qpos_ref, mod_ref, lg_ref, lb_ref, y_ref, *rest, with_inproj):
    if with_inproj:
        modn_ref, win_ref, o_ref, p_ref, ys_ref, sem = rest
    else:
        o_ref, ys_ref, sem = rest
    i = pl.program_id(0)
    slot = i % 2

    def copy_to(s):
        def copy(tile_row, buf_row, rows):
            return pltpu.make_async_copy(y_ref.at[pl.ds(buf_row, rows)], ys_ref.at[s, pl.ds(tile_row, rows)],
                                         sem.at[s])
        return copy

    @pl.when(i == 0)
    def _():
        ys_ref[...] = jnp.zeros_like(ys_ref)
        _start_run_copies(0, gb_ref, ob_ref, nb_ref, copy_to(0))

    @pl.when(i + 1 < pl.num_programs(0))
    def _():
        _start_run_copies(i + 1, gb_ref, ob_ref, nb_ref, copy_to(1 - slot))

    _wait_run_copies(i, nb_ref, copy_to(slot))
    m0, m1 = _sort_select(qpos_ref)
    sel = jnp.logical_or(m0, m1).astype(BF16)
    y = lax.dot_general(sel, ys_ref[slot].astype(BF16), (((0,), (0,)), ((), ())), preferred_element_type=F32)
    gate = mod_ref[0, 5:6, :]
    x_new = _ln(ALPHA * x_ref[...] + gate * y, lg_ref[...], lb_ref[...])
    o_ref[...] = x_new
    if with_inproj:
        u = (x_new * (1.0 + modn_ref[0, 1:2, :]) + modn_ref[0, 0:1, :]).astype(BF16)
        p_ref[...] = jnp.dot(u, win_ref[...], preferred_element_type=F32).astype(p_ref.dtype)


def _ln2(gb, ob, nb8, x2, qpos, mod, lg, lb, ybuf, tps, nb, next_in):
    ntok, d = x2.shape
    latent_only = next_in is None
    if latent_only:
        out_map = lambda i, *_: ((i // tps) * (tps - 1) + jnp.maximum(i % tps - 1, 0), 0)
        out_rows = ntok - nb * TM
    else:
        out_map = lambda i, *_: (i, 0)
        out_rows = ntok
    in_specs = [pl.BlockSpec((TM, d), lambda i, *_: (i, 0)),
                pl.BlockSpec((1, TOP_K, TM), lambda i, *_: (i, 0, 0)),
                pl.BlockSpec((1, 6, d), lambda i, *_: (_mod_row(i, tps, nb), 0, 0)),
                pl.BlockSpec((1, d), lambda i, *_: (0, 0)),
                pl.BlockSpec((1, d), lambda i, *_: (0, 0)),
                pl.BlockSpec(memory_space=pl.ANY)]
    out_specs = [pl.BlockSpec((TM, d), out_map)]
    out_shape = [jax.ShapeDtypeStruct((out_rows, d), F32)]
    args = [gb, ob, nb8, x2, qpos, mod, lg, lb, ybuf]
    if next_in is not None:
        mod_next, w_next = next_in
        nw = w_next.shape[1]
        in_specs += [pl.BlockSpec((1, 6, d), lambda i, *_: (_mod_row(i, tps, nb), 0, 0)),
                     pl.BlockSpec((d, nw), lambda i, *_: (0, 0))]
        out_specs.append(pl.BlockSpec((TM, nw), lambda i, *_: (i, 0)))
        out_shape.append(jax.ShapeDtypeStruct((ntok, nw), BF16))
        args += [mod_next, w_next]
    grid_spec = pltpu.PrefetchScalarGridSpec(
        num_scalar_prefetch=3,
        grid=(ntok // TM,),
        in_specs=in_specs,
        out_specs=out_specs,
        scratch_shapes=[pltpu.VMEM((2, SORT_ROWS, d), F32), pltpu.SemaphoreType.DMA((2,))],
    )
    outs = pl.pallas_call(
        functools.partial(_ln2_kernel, with_inproj=next_in is not None),
        grid_spec=grid_spec,
        out_shape=out_shape,
        compiler_params=_cparams(("arbitrary",)),
        name="post_moe_ln",
    )(*args)
    return (outs[0], outs[1]) if next_in is not None else (outs[0], None)


def _even_w_in(w):
    o = np.cumsum((0, GLA_QK, GLA_QK, GLA_V, GLA_V, GLA_RANK, GLA_RANK, GQA_Q, GQA_KV, GQA_KV))
    q, k, v, r = w[:, o[0]:o[1]], w[:, o[1]:o[2]], w[:, o[2]:o[3]], w[:, o[3]:o[4]]
    lr = w[:, o[4]:o[6]]
    gq, gk, gv = w[:, o[6]:o[7]], w[:, o[7]:o[8]], w[:, o[8]:o[9]]
    d = w.shape[0]
    zeros = lambda n: jnp.zeros((d, n), w.dtype)
    out = jnp.concatenate([gq, gk, gv, q, k, lr, zeros(LANES - 2 * GLA_RANK), zeros(LANES), v, r], axis=1)
    assert out.shape[1] == EVEN_W
    return out.astype(BF16)


def _odd_w_in(w):
    o = np.cumsum((0, SSD_INNER, SSD_INNER, SSD_BC, SSD_BC, SSD_HEADS, SSD_HEADS, FNET_W))
    d = w.shape[0]
    out = jnp.concatenate([w[:, :o[4]], w[:, o[4]:o[6]], jnp.zeros((d, LANES - 2 * SSD_HEADS), w.dtype),
                           w[:, o[6]:o[7]]], axis=1)
    assert out.shape[1] == ODD_W
    return out.astype(BF16)


def _rope_tables(ctx_len, seq):
    rows = seq // GRID_W
    r = jnp.repeat(jnp.arange(rows, dtype=F32), GRID_W)
    col = jnp.tile(jnp.arange(GRID_W, dtype=F32), rows)
    half = GQA_HD // 2
    inv = ROPE_THETA ** (-jnp.arange(0, half, 2, dtype=F32) / half)
    ar = r[:, None] * inv
    ac = col[:, None] * inv
    ang = jnp.concatenate([ar, ar, ac, ac], -1)
    cos = jnp.concatenate([jnp.ones((ctx_len, GQA_HD), F32), jnp.cos(ang)], 0)
    sin = jnp.concatenate([jnp.zeros((ctx_len, GQA_HD), F32), jnp.sin(ang)], 0)
    return jnp.tile(cos, (1, 2)), jnp.tile(sin, (1, 2))


def _even_layer(x2, p2, mod, nb, t, tps, ctx_len, w_o, w_dec, b_dec, gla_g, qn_g, kn_g, cos2, sin2, lg, lb, rw_t, rb):
    p3 = p2.reshape(nb, t, EVEN_W)
    wd_pads = [jnp.zeros((LANES, GLA_QK), F32).at[di * GLA_RANK:(di + 1) * GLA_RANK].set(w_dec[di])
               for di in range(2)]
    outs = _gla_scan(p3, wd_pads, [b_dec[di][None, :] for di in range(2)])
    qh2, kg2, vt4 = _qkprep(p2, cos2, sin2, jnp.tile(qn_g, 2)[None, :], jnp.tile(kn_g, 2)[None, :], tps)
    att = _attention(qh2.reshape(nb, t, GQA_HEADS * LANES), kg2.reshape(nb, t, GQA_KV_HEADS * LANES), vt4, ctx_len)
    w1, w2 = w_o[:GLA_V].astype(BF16), w_o[GLA_V:].astype(BF16)
    ntok = nb * t
    return _evout(outs[0].reshape(ntok, GLA_V), outs[1].reshape(ntok, GLA_V), p2, att.reshape(ntok, GQA_Q),
                  jnp.tile(gla_g, GLA_HEADS)[None, :], w1, w2, x2, mod, lg, lb, rw_t, rb, tps, nb)


def _odd_layer(x2, p2, mod, nb, t, tps, ctx_len, w_o, conv_w, conv_b, dt_bias, a_log, d_skip, ssd_g, fnet_g,
               lg, lb, rw_t, rb):
    p3 = p2.reshape(nb, t, ODD_W)
    w8 = jnp.zeros((8, SSD_CONV_CH), F32).at[:SSD_CONV].set(conv_w)
    xbc2 = _conv(p2, w8, conv_b[None, :], tps)
    xbc3 = xbc2.reshape(nb, t, SSD_CONV_CH)
    params = []
    heads = jnp.arange(SSD_HEADS)
    for di in range(2):
        lanes = di * SSD_HEADS + heads
        bias_pad = jnp.zeros((1, LANES), F32).at[0, lanes].set(dt_bias[di])
        a_pad = jnp.zeros((1, LANES), F32).at[0, lanes].set(-jnp.exp(a_log[di]))
        eexp = jnp.zeros((LANES, SSD_INNER), F32).at[jnp.repeat(lanes, SSD_HD), jnp.arange(SSD_INNER)].set(1.0)
        esel = jnp.zeros((16, LANES), F32).at[heads, lanes].set(1.0)
        params.append((bias_pad, a_pad, eexp, esel))
    ys = _ssd_scan(xbc3, p3, params)
    f3 = p3[:, :, ODD_W - FNET_W:]
    g256 = fnet_g[None, :]
    fmix = jnp.concatenate([_fourier_ctx(f3[:, :ctx_len], g256), _fourier_latent(f3[:, ctx_len:], g256)], axis=1)
    ntok = nb * t
    return _odout(ys[0].reshape(ntok, SSD_INNER), ys[1].reshape(ntok, SSD_INNER), xbc2, p2,
                  fmix.reshape(ntok, FNET_W), jnp.repeat(d_skip, SSD_HD)[None, :], ssd_g[None, :],
                  w_o[:SSD_INNER].astype(BF16), w_o[SSD_INNER:].astype(BF16), x2, mod, lg, lb, rw_t, rb, tps, nb)


def _moe_layer(routed, mod, nb, tps, layer, wg, wu, wd, lg, lb, next_in):
    x2, v, idx_t, gw_t, cnt = routed
    ntok, d = x2.shape
    nt = ntok // TM
    rows = cnt[:, 0].astype(jnp.int32)
    padded = (rows + FFN_ROWS - 1) // FFN_ROWS * FFN_ROWS
    pad_end = jnp.cumsum(padded)
    pad_start = pad_end - padded
    nblk = -(-(ntok * TOP_K + nt * N_EXPERTS * (ROWG - 1)) // FFN_ROWS) + N_EXPERTS
    blk_row = jnp.arange(nblk, dtype=jnp.int32)[:, None] * FFN_ROWS
    block_e = jnp.minimum(jnp.sum((blk_row >= pad_end[None, :]).astype(jnp.int32), axis=1), N_EXPERTS - 1)
    n_used = (pad_end[-1:] // FFN_ROWS).astype(jnp.int32)
    start = jnp.broadcast_to(pad_start.astype(F32)[:, None], (N_EXPERTS, LANES))
    qpos, meta = _meta(idx_t, start)
    gb, ob, nb8 = (meta[:, :, j].reshape(nt * N_EXPERTS) for j in range(3))
    tail_s = ((pad_start + rows) // ROWG).astype(jnp.int32)
    tail_n = ((padded - rows) // ROWG).astype(jnp.int32)
    buf = _scatter(gb, ob, nb8, tail_s, tail_n, n_used, v, qpos, gw_t, nblk * FFN_ROWS)
    y = _moe_ffn(block_e, n_used, buf, layer, wg, wu, wd)
    return _ln2(gb, ob, nb8, x2, qpos, mod, lg, lb, y, tps, nb, next_in)


def kernel(x, c, ctx, c_ctx, ada_w, ada_b, ln_g, ln_b, ev_w_in, ev_w_o, gla_w_decay, gla_b_decay, gla_norm_g,
           gqa_q_norm_g, gqa_k_norm_g, od_w_in, od_w_o, ssd_conv_w, ssd_conv_b, ssd_dt_bias, ssd_a_log, ssd_d,
           ssd_norm_g, fnet_norm_g, router_w, router_b, exp_w_gate, exp_w_up, exp_w_down):
    nb, seq, d = x.shape
    ctx_len = ctx.shape[1]
    assert ctx_len == TM and seq % TM == 0 and d == D_MODEL and nb <= 7
    t = ctx_len + seq
    tps = t // TM
    ntok = nb * t
    cc = jnp.zeros((8, d), F32).at[:nb].set(c).at[nb].set(c_ctx)
    mod_all = _ada_all(cc, ada_w, ada_b).reshape(DEPTH, 8, 6, d)
    cos2, sin2 = _rope_tables(ctx_len, seq)
    rw_t = router_w.T
    rb = jnp.broadcast_to(router_b[:, None], (N_EXPERTS, LANES))
    x2 = jnp.concatenate([ctx, x], axis=1).reshape(ntok, d)

    def w_in_of(layer):
        return _even_w_in(ev_w_in[layer // 2]) if layer % 2 == 0 else _odd_w_in(od_w_in[layer // 2])

    p2 = _inproj(x2, mod_all[0], w_in_of(0), tps, nb)
    for layer in range(DEPTH):
        mod = mod_all[layer]
        i = layer // 2
        lg0, lb0 = ln_g[layer, 0][None, :], ln_b[layer, 0][None, :]
        lg1, lb1 = ln_g[layer, 1][None, :], ln_b[layer, 1][None, :]
        if layer % 2 == 0:
            routed = _even_layer(x2, p2, mod, nb, t, tps, ctx_len, ev_w_o[i], gla_w_decay[i], gla_b_decay[i],
                                 gla_norm_g[i], gqa_q_norm_g[i], gqa_k_norm_g[i], cos2, sin2, lg0, lb0, rw_t, rb)
        else:
            routed = _odd_layer(x2, p2, mod, nb, t, tps, ctx_len, od_w_o[i], ssd_conv_w[i], ssd_conv_b[i],
                                ssd_dt_bias[i], ssd_a_log[i], ssd_d[i], ssd_norm_g[i], fnet_norm_g[i], lg0, lb0,
                                rw_t, rb)
        next_in = (mod_all[layer + 1], w_in_of(layer + 1)) if layer + 1 < DEPTH else None
        x2, p2 = _moe_layer(routed, mod, nb, tps, layer, exp_w_gate, exp_w_up, exp_w_down, lg1, lb1, next_in)
    return x2.reshape(nb, seq, d)
```

```python
import functools
import itertools
import math

import jax
import jax.numpy as jnp
import numpy as np
from jax import lax
from jax.experimental import pallas as pl
from jax.experimental.pallas import tpu as pltpu

F32 = jnp.float32
BF16 = jnp.bfloat16
HI = lax.Precision.HIGHEST

D_MODEL = 1024
DEPTH = 4
GRID_W = 64
GLA_HEADS, GLA_DK, GLA_DV, GLA_RANK, GLA_TAU, GLA_CHUNK = 4, 64, 128, 16, 16.0, 64
GQA_HEADS, GQA_KV_HEADS, GQA_HD = 8, 2, 64
ROPE_THETA = 10000.0
SSD_HEADS, SSD_HD, SSD_GROUPS, SSD_STATE, SSD_CONV, SSD_CHUNK = 12, 64, 2, 128, 5, 64
SSD_HPG = SSD_HEADS // SSD_GROUPS
SSD_INNER = SSD_HEADS * SSD_HD
SSD_BC = SSD_GROUPS * SSD_STATE
SSD_CONV_CH = SSD_INNER + 2 * SSD_BC
FNET_GROUPS, FNET_GC = 4, 64
FNET_W = FNET_GROUPS * FNET_GC
N_EXPERTS, N_EXPERT_GROUPS, TOP_K, D_EXPERT = 16, 4, 2, 768
EXPERTS_PER_GROUP = N_EXPERTS // N_EXPERT_GROUPS
GLA_QK = GLA_HEADS * GLA_DK
GLA_V = GLA_HEADS * GLA_DV
GQA_Q = GQA_HEADS * GQA_HD
GQA_KV = GQA_KV_HEADS * GQA_HD
EPS = 1e-6
ALPHA = (2.0 * DEPTH) ** 0.25

LANES = 128
TM = 256
EVEN_W = 2560
ODD_W = 2432
VMEM_LIMIT = 48 * 1024 * 1024
LOG2E = 1.4426950408889634
ATT_TQ = 256
ATT_HEADROOM = 64.0
ATT_MIN_DENOM = 2.0 ** -60
ATT_VROWS = LANES
ROWG = 8
SORT_ROWS = TOP_K * TM + N_EXPERTS * (ROWG - 1)
COPY_SHIFT = 2
COPY_GROUPS = 1 << COPY_SHIFT
FFN_ROWS = 512
FFN_VMEM_LIMIT = 56 * 1024 * 1024


def _cparams(sem):
    return pltpu.CompilerParams(dimension_semantics=sem, vmem_limit_bytes=VMEM_LIMIT)


def _silu(x):
    return x * (1.0 / (1.0 + jnp.exp(-x)))


def _softplus(x):
    return jnp.maximum(x, 0.0) + jnp.log1p(jnp.exp(-jnp.abs(x)))


def _mod_row(i, tiles_per_seq, n_batch):
    return jnp.where(i % tiles_per_seq == 0, n_batch, i // tiles_per_seq)


def _ada_kernel(c_ref, w_ref, b_ref, o_ref):
    s = _silu(c_ref[...])
    o_ref[0] = jnp.dot(s, w_ref[0], precision=HI, preferred_element_type=F32) + b_ref[0]


def _ada_all(cc, ada_w, ada_b):
    depth, d, n = ada_w.shape
    tn = 512
    return pl.pallas_call(
        _ada_kernel,
        grid=(depth, n // tn),
        in_specs=[pl.BlockSpec((8, d), lambda l, j: (0, 0)),
                  pl.BlockSpec((1, d, tn), lambda l, j: (l, 0, j)),
                  pl.BlockSpec((1, 1, tn), lambda l, j: (l, 0, j))],
        out_specs=pl.BlockSpec((1, 8, tn), lambda l, j: (l, 0, j)),
        out_shape=jax.ShapeDtypeStruct((depth, 8, n), F32),
        compiler_params=_cparams(("arbitrary", "arbitrary")),
        name="adaln",
    )(cc, ada_w, ada_b.reshape(depth, 1, n))


def _inproj_kernel(x_ref, mod_ref, w_ref, o_ref):
    shift = mod_ref[0, 0:1, :]
    scale = mod_ref[0, 1:2, :]
    u = (x_ref[...] * (1.0 + scale) + shift).astype(BF16)
    o_ref[...] = jnp.dot(u, w_ref[...], preferred_element_type=F32).astype(o_ref.dtype)


def _inproj(x2, mod, w, tps, nb):
    ntok, d = x2.shape
    nw = w.shape[1]
    return pl.pallas_call(
        _inproj_kernel,
        grid=(ntok // TM,),
        in_specs=[pl.BlockSpec((TM, d), lambda i: (i, 0)),
                  pl.BlockSpec((1, 6, d), lambda i: (_mod_row(i, tps, nb), 0, 0)),
                  pl.BlockSpec((d, nw), lambda i: (0, 0))],
        out_specs=pl.BlockSpec((TM, nw), lambda i: (i, 0)),
        out_shape=jax.ShapeDtypeStruct((ntok, nw), BF16),
        compiler_params=_cparams(("arbitrary",)),
        name="inproj",
    )(x2, mod, w)


def _scan_block(i, nblk, reverse):
    if not reverse:
        return i
    return jnp.where(i == 0, 0, nblk - i)


def _tri(n, reverse):
    r = lax.broadcasted_iota(jnp.int32, (n, n), 0)
    c = lax.broadcasted_iota(jnp.int32, (n, n), 1)
    return (c >= r) if reverse else (c <= r)


def _chunk_sum_mat(chunk, reverse):
    l = np.arange(TM)[:, None]
    m = np.arange(TM)[None, :]
    same = (l // chunk) == (m // chunk)
    return (same & ((m >= l) if reverse else (m <= l))).astype(np.float32)


def _split3(x):
    hi = x.astype(BF16)
    r1 = x - hi.astype(F32)
    mid = r1.astype(BF16)
    return hi, mid, (r1 - mid.astype(F32)).astype(BF16)


def _chunk_rows(x, chunk, idx):
    return jnp.concatenate([jnp.broadcast_to(x[c * chunk + idx:c * chunk + idx + 1, :], (chunk, x.shape[1]))
                            for c in range(x.shape[0] // chunk)], axis=0)


def _gla_dir(q_ref, k_ref, v_ref, lr_ref, wd_ref, bd_ref, cm_ref, o_ref, s_ref, *, reverse):
    L = GLA_CHUNK
    nchunk = TM // L
    nt = (((1,), (1,)), ((), ()))
    tn = (((0,), (0,)), ((), ()))
    end = 0 if reverse else L - 1
    mid = L // 2 if reverse else L // 2 - 1

    @pl.when(pl.program_id(1) == 0)
    def _():
        s_ref[...] = jnp.zeros_like(s_ref)

    q = q_ref[...].astype(F32) * (GLA_DK ** -0.5)
    k = k_ref[...].astype(F32)
    v = v_ref[...]
    lr = lr_ref[...]
    z = bd_ref[...] + sum(jnp.dot(lr, wd_ref[j], preferred_element_type=F32) for j in range(3))
    yield
    la = -_softplus(-z) * (1.0 / GLA_TAU)
    cm = cm_ref[...]
    b = sum(jnp.dot(cm, part, preferred_element_type=F32) for part in _split3(la))
    yield
    bmid = _chunk_rows(b, L, mid)
    bend = _chunk_rows(b, L, end)
    qs = q * jnp.exp(b - bmid)
    ks = (k * jnp.exp(bmid - b)).astype(BF16)
    qi = q * jnp.exp(b)
    kend = (k * jnp.exp(bend - b)).astype(BF16)
    r = lax.broadcasted_iota(jnp.int32, (TM, TM), 0)
    c = lax.broadcasted_iota(jnp.int32, (TM, TM), 1)
    mask = jnp.logical_and(r // L == c // L, (c >= r) if reverse else (c <= r))
    lane = lax.broadcasted_iota(jnp.int32, (TM, LANES), 1)
    heads = (lane < GLA_DK, lane >= GLA_DK)
    yield
    att = [lax.dot_general(jnp.where(heads[h], qs, 0.0).astype(BF16), ks, nt, preferred_element_type=F32)
           for h in range(2)]
    yield
    intra = []
    for h in range(2):
        att_h = jnp.where(mask, att[h], 0.0).astype(BF16)
        intra.append(jnp.dot(att_h, v[:, h * GLA_DV:(h + 1) * GLA_DV], preferred_element_type=F32))
    yield
    qi_h = [jnp.where(heads[h], qi, 0.0).astype(BF16) for h in range(2)]
    tot = jnp.concatenate([b[ch * L + end:ch * L + end + 1, :] for ch in range(nchunk)]
                          + [jnp.zeros((ROWG - nchunk, LANES), F32)], axis=0).T
    ds, dec = [], []
    for ch in range(nchunk):
        rows = slice(ch * L, (ch + 1) * L)
        ds.append(lax.dot_general(kend[rows], v[rows], tn, preferred_element_type=F32))
        dec.append(jnp.exp(jnp.broadcast_to(tot[:, ch:ch + 1], (2 * GLA_DK, 2 * GLA_DV))))
    yield
    s = s_ref[...]
    inter = [None] * nchunk
    for ch in (range(nchunk - 1, -1, -1) if reverse else range(nchunk)):
        rows = slice(ch * L, (ch + 1) * L)
        s_bf = s.astype(BF16)
        inter[ch] = jnp.concatenate(
            [jnp.dot(qi_h[h][rows], s_bf[:, h * GLA_DV:(h + 1) * GLA_DV], preferred_element_type=F32)
             for h in range(2)], axis=1)
        s = dec[ch] * s + ds[ch]
        yield
    s_ref[...] = s
    o_ref[...] = (jnp.concatenate(intra, axis=1) + jnp.concatenate(inter, axis=0)).astype(o_ref.dtype)


def _alternate(*stage_generators):
    for _ in itertools.zip_longest(*stage_generators):
        pass


GLA_PAIRS = GLA_HEADS // 2


def _gla_kernel(*refs):
    n_streams = 2 * GLA_PAIRS
    n_in = (len(refs) - 2 - n_streams) // n_streams
    outs = refs[n_streams * n_in:n_streams * n_in + 2]
    states = refs[n_streams * n_in + 2:]
    scans = []
    for s in range(n_streams):
        reverse, pair = divmod(s, GLA_PAIRS)
        o_view = outs[reverse].at[:, pl.ds(pair * 2 * GLA_DV, 2 * GLA_DV)]
        scans.append(_gla_dir(*refs[s * n_in:(s + 1) * n_in], o_view, states[s], reverse=bool(reverse)))
    _alternate(*scans)


def _gla_scan(p3, wd_pads, bds):
    nb, t, _ = p3.shape
    nblk = t // TM
    in_specs, args, out_specs = [], [], []
    for reverse in (False, True):
        blk = functools.partial(_scan_block, nblk=nblk, reverse=reverse)
        wd3 = jnp.stack(_split3(wd_pads[int(reverse)]))
        cm = jnp.asarray(_chunk_sum_mat(GLA_CHUNK, reverse), BF16)
        for p in range(GLA_PAIRS):
            in_specs += [pl.BlockSpec((None, TM, LANES), lambda b, i, blk=blk, p=p: (b, blk(i), 6 + p)),
                         pl.BlockSpec((None, TM, LANES), lambda b, i, blk=blk, p=p: (b, blk(i), 8 + p)),
                         pl.BlockSpec((None, TM, 2 * GLA_DV), lambda b, i, blk=blk, p=p: (b, blk(i), 6 + p)),
                         pl.BlockSpec((None, TM, LANES), lambda b, i, blk=blk: (b, blk(i), 10)),
                         pl.BlockSpec((3, LANES, LANES), lambda b, i, p=p: (0, 0, p)),
                         pl.BlockSpec((1, LANES), lambda b, i, p=p: (0, p)),
                         pl.BlockSpec((TM, TM), lambda b, i: (0, 0))]
            args += [p3, p3, p3, p3, wd3, bds[int(reverse)], cm]
        out_specs.append(pl.BlockSpec((None, TM, GLA_V), lambda b, i, blk=blk: (b, blk(i), 0)))
    state = pltpu.VMEM((2 * GLA_DK, 2 * GLA_DV), F32)
    return pl.pallas_call(
        _gla_kernel,
        grid=(nb, nblk),
        in_specs=in_specs,
        out_specs=out_specs,
        out_shape=[jax.ShapeDtypeStruct((nb, t, GLA_V), BF16)] * 2,
        scratch_shapes=[state] * (2 * GLA_PAIRS),
        compiler_params=_cparams(("arbitrary", "arbitrary")),
        name="gla_scan",
    )(*args)


def _seg_ones(width, seg):
    r = lax.broadcasted_iota(jnp.int32, (width, width), 0) // seg
    c = lax.broadcasted_iota(jnp.int32, (width, width), 1) // seg
    return jnp.where(r == c, 1.0 / seg, 0.0).astype(BF16)


def _seg_mean(x, bd):
    hi = x.astype(BF16)
    lo = (x - hi.astype(F32)).astype(BF16)
    return jnp.dot(hi, bd, preferred_element_type=F32) + jnp.dot(lo, bd, preferred_element_type=F32)


def _norm_rope(x, g, cos, sin, bd):
    ms = _seg_mean(x * x, bd)
    xn = x * lax.rsqrt(ms + EPS) * g
    lane = lax.broadcasted_iota(jnp.int32, xn.shape, 1)
    quarter = GQA_HD // 4
    up = pltpu.roll(xn, LANES - quarter, 1)
    dn = pltpu.roll(xn, quarter, 1)
    rot = jnp.where(lane % (2 * quarter) < quarter, -up, dn)
    return xn * cos + rot * sin


def _qkprep_kernel(q_ref, k_ref, v_ref, cos_ref, sin_ref, gq_ref, gk_ref, qo_ref, ko_ref, vo_ref):
    bd = _seg_ones(LANES, GQA_HD)
    cos = cos_ref[...]
    sin = sin_ref[...]
    lane = lax.broadcasted_iota(jnp.int32, (TM, LANES), 1)
    low = lane < GQA_HD
    extra = lane == GQA_HD
    one_hot = jnp.where(extra, 1.0, 0.0)
    k_bound = math.sqrt(GQA_HD) * jnp.max(jnp.abs(gk_ref[...]), axis=-1, keepdims=True) * 1.02
    for j in range(GQA_Q // LANES):
        x = q_ref[:, j * LANES:(j + 1) * LANES].astype(F32)
        y = _norm_rope(x, gq_ref[...], cos, sin, bd) * (GQA_HD ** -0.5 * LOG2E)
        y = y.astype(BF16).astype(F32)
        norm = jnp.sqrt(_seg_mean(y * y, bd) * GQA_HD)
        shift = ATT_HEADROOM - norm * k_bound
        lo = jnp.where(low, y, jnp.where(extra, pltpu.roll(shift, GQA_HD, 1), 0.0))
        hi = jnp.where(low, pltpu.roll(y, GQA_HD, 1), jnp.where(extra, shift, 0.0))
        qo_ref[:, (2 * j) * LANES:(2 * j + 1) * LANES] = lo.astype(BF16)
        qo_ref[:, (2 * j + 1) * LANES:(2 * j + 2) * LANES] = hi.astype(BF16)
    k = _norm_rope(k_ref[...].astype(F32), gk_ref[...], cos, sin, bd)
    ko_ref[:, 0:LANES] = jnp.where(low, k, one_hot).astype(BF16)
    ko_ref[:, LANES:2 * LANES] = jnp.where(low, pltpu.roll(k, GQA_HD, 1), one_hot).astype(BF16)
    v = v_ref[...].astype(F32)
    vo_ref[0] = jnp.where(low, v, one_hot).T[0:ATT_VROWS].astype(BF16)
    vo_ref[1] = jnp.where(low, pltpu.roll(v, GQA_HD, 1), one_hot).T[0:ATT_VROWS].astype(BF16)


def _qkprep(p2, cos2, sin2, gq2, gk2, tps):
    ntok = p2.shape[0]
    nb = ntok // (tps * TM)
    return pl.pallas_call(
        _qkprep_kernel,
        grid=(ntok // TM,),
        in_specs=[pl.BlockSpec((TM, GQA_Q), lambda i: (i, 0)),
                  pl.BlockSpec((TM, LANES), lambda i: (i, 4)),
                  pl.BlockSpec((TM, LANES), lambda i: (i, 5)),
                  pl.BlockSpec((TM, LANES), lambda i: (i % tps, 0)),
                  pl.BlockSpec((TM, LANES), lambda i: (i % tps, 0)),
                  pl.BlockSpec((1, LANES), lambda i: (0, 0)),
                  pl.BlockSpec((1, LANES), lambda i: (0, 0))],
        out_specs=[pl.BlockSpec((TM, GQA_HEADS * LANES), lambda i: (i, 0)),
                   pl.BlockSpec((TM, GQA_KV_HEADS * LANES), lambda i: (i, 0)),
                   pl.BlockSpec((None, GQA_KV_HEADS, ATT_VROWS, TM), lambda i: (i // tps, 0, 0, i % tps))],
        out_shape=[jax.ShapeDtypeStruct((ntok, GQA_HEADS * LANES), BF16),
                   jax.ShapeDtypeStruct((ntok, GQA_KV_HEADS * LANES), BF16),
                   jax.ShapeDtypeStruct((nb, GQA_KV_HEADS, ATT_VROWS, tps * TM), BF16)],
        compiler_params=_cparams(("arbitrary",)),
        name="qkprep",
    )(p2, p2, p2, cos2, sin2, gq2, gk2)


def _attn_kernel(q_ref, k_ref, vt_ref, o_ref, *, ctx_len, n_lat_chunks, ck, n_blocks):
    i = pl.program_id(2)
    rep = GQA_HEADS // GQA_KV_HEADS
    is_ctx = i < ctx_len // ATT_TQ
    q4 = jnp.concatenate([q_ref[:, r * LANES:(r + 1) * LANES] for r in range(rep)], axis=0)
    nt = (((1,), (1,)), ((), ()))

    def chunk(acc, kc, vtc):
        st = lax.dot_general(kc, q4, nt, preferred_element_type=F32)
        return acc + jnp.dot(vtc, jnp.exp2(st).astype(BF16), preferred_element_type=F32)

    def store(acc):
        out = (acc * (1.0 / acc[GQA_HD:GQA_HD + 1, :])).T
        low = lax.broadcasted_iota(jnp.int32, (ATT_TQ, LANES), 1) < GQA_HD
        hs = [out[r * ATT_TQ:(r + 1) * ATT_TQ] for r in range(rep)]
        for u in range(rep // 2):
            o_ref[:, u * LANES:(u + 1) * LANES] = jnp.where(
                low, hs[2 * u], pltpu.roll(hs[2 * u + 1], GQA_HD, 1)).astype(o_ref.dtype)

    acc = chunk(jnp.zeros((ATT_VROWS, rep * ATT_TQ), F32), k_ref[0:ctx_len, :], vt_ref[:, 0:ctx_len])

    def body(c, acc):
        off = pl.multiple_of(ctx_len + c * ck, TM)
        return chunk(acc, k_ref[pl.ds(off, ck), :], vt_ref[:, pl.ds(off, ck)])

    acc = lax.fori_loop(0, jnp.where(is_ctx, 0, n_lat_chunks), body, acc)
    healthy = jnp.min(acc[GQA_HD:GQA_HD + 1, :]) >= ATT_MIN_DENOM

    @pl.when(healthy)
    def _():
        store(acc)

    @pl.when(jnp.logical_not(healthy))
    def _():
        def online(c, carry):
            m, acc = carry
            off = pl.multiple_of(c * TM, TM)
            st = lax.dot_general(k_ref[pl.ds(off, TM), :], q4, nt, preferred_element_type=F32)
            mn = jnp.maximum(m, jnp.max(st, axis=0, keepdims=True))
            pt = jnp.exp2(st - mn).astype(BF16)
            acc = jnp.exp2(m - mn) * acc + jnp.dot(vt_ref[:, pl.ds(off, TM)], pt, preferred_element_type=F32)
            return mn, acc

        init = (jnp.full((1, rep * ATT_TQ), -1e30, F32), jnp.zeros((ATT_VROWS, rep * ATT_TQ), F32))
        _, acc2 = lax.fori_loop(0, jnp.where(is_ctx, ctx_len // TM, n_blocks), online, init)
        store(acc2)


def _attention(qh3, kg3, vt4, ctx_len):
    nb, t, _ = qh3.shape
    seq = t - ctx_len
    ck = next(c for c in (2048, 1024, 512, 256) if seq % c == 0)
    rep = GQA_HEADS // GQA_KV_HEADS
    return pl.pallas_call(
        functools.partial(_attn_kernel, ctx_len=ctx_len, n_lat_chunks=seq // ck, ck=ck, n_blocks=t // TM),
        grid=(nb, GQA_KV_HEADS, t // ATT_TQ),
        in_specs=[pl.BlockSpec((None, ATT_TQ, rep * LANES), lambda b, g, i: (b, i, g)),
                  pl.BlockSpec((None, t, LANES), lambda b, g, i: (b, 0, g)),
                  pl.BlockSpec((None, None, ATT_VROWS, t), lambda b, g, i: (b, g, 0, 0))],
        out_specs=pl.BlockSpec((None, ATT_TQ, rep * GQA_HD), lambda b, g, i: (b, i, g)),
        out_shape=jax.ShapeDtypeStruct((nb, t, GQA_Q), BF16),
        compiler_params=_cparams(("arbitrary", "arbitrary", "arbitrary")),
        name="gqa_attn",
    )(qh3, kg3, vt4)


def _ln(x, g, b):
    mu = jnp.mean(x, axis=-1, keepdims=True)
    xc = x - mu
    var = jnp.mean(xc * xc, axis=-1, keepdims=True)
    return xc * lax.rsqrt(var + EPS) * g + b


def _evout_kernel(of_ref, ob_ref, r_ref, att_ref, g_ref, w1_ref, w2_ref, x_ref, mod_ref, lg_ref, lb_ref, rw_ref,
                  rb_ref, o_ref, *route_refs):
    o = of_ref[...].astype(F32) + ob_ref[...].astype(F32)
    r = r_ref[...].astype(F32)
    parts = []
    for h in range(GLA_HEADS):
        oh = o[:, h * GLA_DV:(h + 1) * GLA_DV]
        ms = jnp.mean(oh * oh, axis=-1, keepdims=True)
        parts.append(oh * lax.rsqrt(ms + EPS))
    gl = (jnp.concatenate(parts, axis=1) * g_ref[...] * _silu(r)).astype(BF16)
    y = jnp.dot(gl, w1_ref[...], preferred_element_type=F32)
    y = y + jnp.dot(att_ref[...], w2_ref[...], preferred_element_type=F32)
    gate = mod_ref[0, 2:3, :]
    x_new = _ln(ALPHA * x_ref[...] + gate * y, lg_ref[...], lb_ref[...])
    o_ref[...] = x_new
    _route_tile(x_new, mod_ref, rw_ref, rb_ref, *route_refs)


def _evout(of2, ob2, p2, att2, g512, w1, w2, x2, mod, lg, lb, rw_t, rb, tps, nb):
    ntok, d = x2.shape
    row = lambda i: (i, 0)
    const = lambda i: (0, 0)
    r_in, r_out, r_shapes = _router_specs(ntok, d)
    return pl.pallas_call(
        _evout_kernel,
        grid=(ntok // TM,),
        in_specs=[pl.BlockSpec((TM, GLA_V), row), pl.BlockSpec((TM, GLA_V), row),
                  pl.BlockSpec((TM, GLA_V), lambda i: (i, 4)),
                  pl.BlockSpec((TM, GQA_Q), row),
                  pl.BlockSpec((1, GLA_V), const),
                  pl.BlockSpec((GLA_V, d), const), pl.BlockSpec((GQA_Q, d), const),
                  pl.BlockSpec((TM, d), row),
                  pl.BlockSpec((1, 6, d), lambda i: (_mod_row(i, tps, nb), 0, 0)),
                  pl.BlockSpec((1, d), const), pl.BlockSpec((1, d), const)] + r_in,
        out_specs=[pl.BlockSpec((TM, d), row)] + r_out,
        out_shape=[jax.ShapeDtypeStruct((ntok, d), F32)] + r_shapes,
        compiler_params=_cparams(("arbitrary",)),
        name="even_out",
    )(of2, ob2, p2, att2, g512, w1, w2, x2, mod, lg, lb, rw_t, rb)


def _conv_kernel(prev_ref, cur_ref, next_ref, w_ref, b_ref, o_ref, pad_ref, *, tps):
    j = pl.program_id(0) % tps
    has_prev = (j >= 2).astype(F32)
    has_next = jnp.logical_and(j >= 1, j <= tps - 2).astype(F32)
    pad_ref[0:8, :] = prev_ref[...].astype(F32) * has_prev
    pad_ref[8:8 + TM, :] = cur_ref[...].astype(F32)
    pad_ref[8 + TM:16 + TM, :] = next_ref[...].astype(F32) * has_next
    half = (SSD_CONV - 1) // 2
    cw = 256
    for c0 in range(0, SSD_CONV_CH, cw):
        cols = slice(c0, c0 + cw)
        acc = jnp.zeros((TM, cw), F32) + b_ref[:, cols]
        for tap in range(SSD_CONV):
            acc = acc + pad_ref[pl.ds(8 + tap - half, TM), cols] * w_ref[tap:tap + 1, cols]
        o_ref[:, cols] = _silu(acc).astype(o_ref.dtype)


def _conv(p2, w8, bias, tps):
    ntok = p2.shape[0]
    cw = SSD_CONV_CH
    r8 = TM // 8
    nrow8 = ntok // 8
    return pl.pallas_call(
        functools.partial(_conv_kernel, tps=tps),
        grid=(ntok // TM,),
        in_specs=[pl.BlockSpec((8, cw), lambda i: (jnp.maximum(i * r8 - 1, 0), 0)),
                  pl.BlockSpec((TM, cw), lambda i: (i, 0)),
                  pl.BlockSpec((8, cw), lambda i: (jnp.minimum((i + 1) * r8, nrow8 - 1), 0)),
                  pl.BlockSpec((8, cw), lambda i: (0, 0)),
                  pl.BlockSpec((1, cw), lambda i: (0, 0))],
        out_specs=pl.BlockSpec((TM, cw), lambda i: (i, 0)),
        out_shape=jax.ShapeDtypeStruct((ntok, SSD_CONV_CH), BF16),
        scratch_shapes=[pltpu.VMEM((TM + 16, cw), F32)],
        compiler_params=_cparams(("arbitrary",)),
        name="ssd_conv",
    )(p2, p2, p2, w8, bias)


def _ssd_dir(xbc_ref, dt_ref, bias_ref, a_ref, eexp_ref, esel_ref, cm_ref, o_ref, h_ref, *, reverse):
    L = SSD_CHUNK
    GW = SSD_HPG * SSD_HD
    nchunk = TM // L
    nt = (((1,), (1,)), ((), ()))
    tn = (((0,), (0,)), ((), ()))
    end = 0 if reverse else L - 1
    lane0 = SSD_HEADS if reverse else 0

    @pl.when(pl.program_id(1) == 0)
    def _():
        h_ref[...] = jnp.zeros_like(h_ref)

    xs = xbc_ref[:, 0:SSD_INNER].astype(F32)
    bs = xbc_ref[:, SSD_INNER:SSD_INNER + SSD_BC]
    cs = xbc_ref[:, SSD_INNER + SSD_BC:SSD_CONV_CH]
    dt = _softplus(dt_ref[...].astype(F32) + bias_ref[...])
    a = dt * a_ref[...]
    cm = cm_ref[...]
    cum = sum(jnp.dot(cm, part, preferred_element_type=F32) for part in _split3(a))
    yield
    cum_t = sum(lax.dot_general(esel_ref[...], part, nt, preferred_element_type=F32)
                for part in _split3(cum))
    cend = _chunk_rows(cum, L, end)
    eexp = eexp_ref[...]

    def expand(m):
        return jnp.dot(m.astype(BF16), eexp, preferred_element_type=F32)

    xdt = xs * expand(dt)
    xdt_b = xdt.astype(BF16)
    yield
    xend = (xdt * expand(jnp.exp(cend - cum))).astype(BF16)
    yield
    ecum = expand(jnp.exp(cum))
    tot = jnp.concatenate([cum[ch * L + end:ch * L + end + 1, :] for ch in range(nchunk)]
                          + [jnp.zeros((ROWG - nchunk, LANES), F32)], axis=0)
    edec = jnp.exp(sum(jnp.dot(part, eexp, preferred_element_type=F32) for part in _split3(tot)))

    tri = _tri(L, reverse)
    low = lax.broadcasted_iota(jnp.int32, (L, LANES), 1) < SSD_HD
    intra = [[None] * SSD_GROUPS for _ in range(nchunk)]
    dh = [[None] * SSD_GROUPS for _ in range(nchunk)]
    for ch in range(nchunk):
        rows = slice(ch * L, (ch + 1) * L)
        for g in range(SSD_GROUPS):
            bg = bs[rows, g * SSD_STATE:(g + 1) * SSD_STATE]
            cg = cs[rows, g * SSD_STATE:(g + 1) * SSD_STATE]
            cb = lax.dot_general(cg, bg, nt, preferred_element_type=F32)
            pairs = []
            for pp in range(SSD_HPG // 2):
                yh = []
                for u in range(2):
                    h = g * SSD_HPG + 2 * pp + u
                    seg = cum[rows, lane0 + h:lane0 + h + 1] - cum_t[h:h + 1, rows]
                    dec = jnp.exp(jnp.where(tri, seg, -1e30))
                    mat = (cb * dec).astype(BF16)
                    col = (h - u) * SSD_HD
                    yh.append(jnp.dot(mat, xdt_b[rows, col:col + LANES], preferred_element_type=F32))
                pairs.append(jnp.where(low, yh[0], yh[1]))
                yield
            intra[ch][g] = jnp.concatenate(pairs, axis=1)
            dh[ch][g] = lax.dot_general(bg, xend[rows, g * GW:(g + 1) * GW], tn, preferred_element_type=F32)

    hs = [h_ref[g] for g in range(SSD_GROUPS)]
    out = [None] * nchunk
    for ch in (range(nchunk - 1, -1, -1) if reverse else range(nchunk)):
        rows = slice(ch * L, (ch + 1) * L)
        ys = []
        for g in range(SSD_GROUPS):
            cg = cs[rows, g * SSD_STATE:(g + 1) * SSD_STATE]
            y_inter = jnp.dot(cg, hs[g].astype(BF16), preferred_element_type=F32) * ecum[rows, g * GW:(g + 1) * GW]
            ys.append(intra[ch][g] + y_inter)
            hs[g] = hs[g] * edec[ch:ch + 1, g * GW:(g + 1) * GW] + dh[ch][g]
        out[ch] = jnp.concatenate(ys, axis=1)
        yield
    for g in range(SSD_GROUPS):
        h_ref[g] = hs[g]
    o_ref[...] = jnp.concatenate(out, axis=0).astype(o_ref.dtype)


def _ssd_kernel(*refs):
    n_in = (len(refs) - 4) // 2
    o_f, o_b, h_f, h_b = refs[2 * n_in:]
    _alternate(_ssd_dir(*refs[:n_in], o_f, h_f, reverse=False),
               _ssd_dir(*refs[n_in:2 * n_in], o_b, h_b, reverse=True))


def _ssd_scan(xbc3, p3, params):
    nb, t, _ = xbc3.shape
    nblk = t // TM
    const = lambda b, i: (0, 0)
    in_specs, args, out_specs = [], [], []
    for reverse in (False, True):
        blk = functools.partial(_scan_block, nblk=nblk, reverse=reverse)
        bias_pad, a_pad, eexp, esel = params[int(reverse)]
        in_specs += [pl.BlockSpec((None, TM, SSD_CONV_CH), lambda b, i, blk=blk: (b, blk(i), 0)),
                     pl.BlockSpec((None, TM, LANES), lambda b, i, blk=blk: (b, blk(i), 16)),
                     pl.BlockSpec((1, LANES), const), pl.BlockSpec((1, LANES), const),
                     pl.BlockSpec((LANES, SSD_INNER), const), pl.BlockSpec((16, LANES), const),
                     pl.BlockSpec((TM, TM), const)]
        args += [xbc3, p3, bias_pad, a_pad, eexp.astype(BF16), esel.astype(BF16),
                 jnp.asarray(_chunk_sum_mat(SSD_CHUNK, reverse), BF16)]
        out_specs.append(pl.BlockSpec((None, TM, SSD_INNER), lambda b, i, blk=blk: (b, blk(i), 0)))
    state = pltpu.VMEM((SSD_GROUPS, SSD_STATE, SSD_HPG * SSD_HD), F32)
    return pl.pallas_call(
        _ssd_kernel,
        grid=(nb, nblk),
        in_specs=in_specs,
        out_specs=out_specs,
        out_shape=[jax.ShapeDtypeStruct((nb, t, SSD_INNER), BF16)] * 2,
        scratch_shapes=[state, state],
        compiler_params=_cparams(("arbitrary", "arbitrary")),
        name="ssd_scan",
    )(*args)


def _dft_mats(n):
    k = jnp.arange(n, dtype=jnp.int32)
    ang = ((k[:, None] * k[None, :]) % n).astype(F32) * (2.0 * math.pi / n)
    return jnp.cos(ang), jnp.sin(ang)


def _chan_mats():
    cc, sc = _dft_mats(FNET_GC)
    eye = jnp.eye(FNET_GROUPS, dtype=F32)
    return jnp.kron(eye, cc), jnp.kron(eye, sc)


def _group_rms(x, g, bd):
    return x * lax.rsqrt(_seg_mean(x * x, bd) + EPS) * g


def _split2(x):
    hi = x.astype(BF16)
    return hi, (x - hi.astype(F32)).astype(BF16)


def _mm(a, b):
    a_hi, a_lo = _split2(a)
    b_hi, b_lo = _split2(b)
    dot = functools.partial(jnp.dot, preferred_element_type=F32)
    return dot(a_hi, b_hi) + (dot(a_hi, b_lo) + dot(a_lo, b_hi))


def _ffta_kernel(x_ref, g_ref, cc_ref, sc_ref, c1_ref, s1_ref, twc_ref, tws_ref, yr_ref, yi_ref, *, nb2):
    bd = _seg_ones(FNET_W, FNET_GC)
    for j in range(nb2):
        x = x_ref[:, j * FNET_W:(j + 1) * FNET_W].astype(F32)
        xn = _group_rms(x, g_ref[...], bd)
        vr = _mm(xn, cc_ref[...])
        vi = -_mm(xn, sc_ref[...])
        c1 = c1_ref[...]
        s1 = s1_ref[...]
        yr = _mm(c1, vr) + _mm(s1, vi)
        yi = _mm(c1, vi) - _mm(s1, vr)
        tc = twc_ref[j]
        ts = tws_ref[j]
        yr_ref[j] = yr * tc + yi * ts
        yi_ref[j] = yi * tc - yr * ts


def _fftb_kernel(yr_ref, yi_ref, c2_ref, s2_ref, o_ref, *, scale):
    o_ref[...] = (_mm(c2_ref[...], yr_ref[...]) + _mm(s2_ref[...], yi_ref[...])) * scale


def _fftc_kernel(x_ref, g_ref, cc_ref, sc_ref, ct_ref, st_ref, o_ref, *, scale):
    bd = _seg_ones(FNET_W, FNET_GC)
    xn = _group_rms(x_ref[...].astype(F32), g_ref[...], bd)
    a = _mm(xn, cc_ref[...])
    b = _mm(xn, sc_ref[...])
    o_ref[...] = (_mm(ct_ref[...], a) - _mm(st_ref[...], b)) * scale


def _fourier_latent(f_lat, g256):
    nb, s, _ = f_lat.shape
    n2 = 64
    n1 = s // n2
    nb2 = 8
    ccm, scm = _chan_mats()
    c1, s1 = _dft_mats(n1)
    c2, s2 = _dft_mats(n2)
    t2 = jnp.arange(n2, dtype=jnp.int32)[:, None]
    k1 = jnp.arange(n1, dtype=jnp.int32)[None, :]
    tw = ((t2 * k1) % s).astype(F32) * (2.0 * math.pi / s)
    twc = jnp.cos(tw)[:, :, None]
    tws = jnp.sin(tw)[:, :, None]
    x2 = f_lat.reshape(nb, n1, n2 * FNET_W)
    const2 = lambda b, j: (0, 0)
    yr, yi = pl.pallas_call(
        functools.partial(_ffta_kernel, nb2=nb2),
        grid=(nb, n2 // nb2),
        in_specs=[pl.BlockSpec((None, n1, nb2 * FNET_W), lambda b, j: (b, 0, j)),
                  pl.BlockSpec((1, FNET_W), const2),
                  pl.BlockSpec((FNET_W, FNET_W), const2), pl.BlockSpec((FNET_W, FNET_W), const2),
                  pl.BlockSpec((n1, n1), const2), pl.BlockSpec((n1, n1), const2),
                  pl.BlockSpec((nb2, n1, 1), lambda b, j: (j, 0, 0)),
                  pl.BlockSpec((nb2, n1, 1), lambda b, j: (j, 0, 0))],
        out_specs=[pl.BlockSpec((None, nb2, n1, FNET_W), lambda b, j: (b, j, 0, 0)),
                   pl.BlockSpec((None, nb2, n1, FNET_W), lambda b, j: (b, j, 0, 0))],
        out_shape=[jax.ShapeDtypeStruct((nb, n2, n1, FNET_W), F32),
                   jax.ShapeDtypeStruct((nb, n2, n1, FNET_W), F32)],
        compiler_params=_cparams(("arbitrary", "arbitrary")),
        name="fft_a",
    )(x2, g256, ccm, scm, c1, s1, twc, tws)
    ncol = n1 * FNET_W
    tn = min(2048, ncol)
    out = pl.pallas_call(
        functools.partial(_fftb_kernel, scale=1.0 / math.sqrt(s * FNET_GC)),
        grid=(nb, ncol // tn),
        in_specs=[pl.BlockSpec((None, n2, tn), lambda b, j: (b, 0, j)),
                  pl.BlockSpec((None, n2, tn), lambda b, j: (b, 0, j)),
                  pl.BlockSpec((n2, n2), const2), pl.BlockSpec((n2, n2), const2)],
        out_specs=pl.BlockSpec((None, n2, tn), lambda b, j: (b, 0, j)),
        out_shape=jax.ShapeDtypeStruct((nb, n2, ncol), F32),
        compiler_params=_cparams(("arbitrary", "arbitrary")),
        name="fft_b",
    )(yr.reshape(nb, n2, ncol), yi.reshape(nb, n2, ncol), c2, s2)
    return out.reshape(nb, s, FNET_W)


def _fourier_ctx(f_ctx, g256):
    nb, tc, _ = f_ctx.shape
    ccm, scm = _chan_mats()
    ct, st = _dft_mats(tc)
    const = lambda b: (0, 0)
    return pl.pallas_call(
        functools.partial(_fftc_kernel, scale=1.0 / math.sqrt(tc * FNET_GC)),
        grid=(nb,),
        in_specs=[pl.BlockSpec((None, tc, FNET_W), lambda b: (b, 0, 0)),
                  pl.BlockSpec((1, FNET_W), const),
                  pl.BlockSpec((FNET_W, FNET_W), const), pl.BlockSpec((FNET_W, FNET_W), const),
                  pl.BlockSpec((tc, tc), const), pl.BlockSpec((tc, tc), const)],
        out_specs=pl.BlockSpec((None, tc, FNET_W), lambda b: (b, 0, 0)),
        out_shape=jax.ShapeDtypeStruct((nb, tc, FNET_W), F32),
        compiler_params=_cparams(("arbitrary",)),
        name="fft_ctx",
    )(f_ctx, g256, ccm, scm, ct, st)


def _odout_kernel(yf_ref, yb_ref, xs_ref, z0_ref, z1_ref, z2_ref, f_ref, dsk_ref, g_ref, w1_ref, w2_ref, x_ref,
                  mod_ref, lg_ref, lb_ref, rw_ref, rb_ref, o_ref, *route_refs):
    GW = SSD_HPG * SSD_HD
    y = yf_ref[...].astype(F32) + yb_ref[...].astype(F32) + xs_ref[...].astype(F32) * dsk_ref[...]
    z = jnp.concatenate([z0_ref[...], z1_ref[...], z2_ref[...]], axis=1).astype(F32)
    y = y * _silu(z)
    parts = []
    for g in range(SSD_GROUPS):
        yg = y[:, g * GW:(g + 1) * GW]
        ms = jnp.mean(yg * yg, axis=-1, keepdims=True)
        parts.append(yg * lax.rsqrt(ms + EPS))
    yn = (jnp.concatenate(parts, axis=1) * g_ref[...]).astype(BF16)
    o = jnp.dot(yn, w1_ref[...], preferred_element_type=F32)
    o = o + jnp.dot(f_ref[...].astype(BF16), w2_ref[...], preferred_element_type=F32)
    gate = mod_ref[0, 2:3, :]
    x_new = _ln(ALPHA * x_ref[...] + gate * o, lg_ref[...], lb_ref[...])
    o_ref[...] = x_new
    _route_tile(x_new, mod_ref, rw_ref, rb_ref, *route_refs)


def _odout(yf2, yb2, xbc2, p2, f2, dsk, g768, w1, w2, x2, mod, lg, lb, rw_t, rb, tps, nb):
    ntok, d = x2.shape
    row = lambda i: (i, 0)
    const = lambda i: (0, 0)
    r_in, r_out, r_shapes = _router_specs(ntok, d)
    zw = 256
    return pl.pallas_call(
        _odout_kernel,
        grid=(ntok // TM,),
        in_specs=[pl.BlockSpec((TM, SSD_INNER), row), pl.BlockSpec((TM, SSD_INNER), row),
                  pl.BlockSpec((TM, SSD_INNER), row)]
        + [pl.BlockSpec((TM, zw), lambda i, j=j: (i, SSD_CONV_CH // zw + j)) for j in range(SSD_INNER // zw)]
        + [pl.BlockSpec((TM, FNET_W), row),
                  pl.BlockSpec((1, SSD_INNER), const), pl.BlockSpec((1, SSD_INNER), const),
                  pl.BlockSpec((SSD_INNER, d), const), pl.BlockSpec((FNET_W, d), const),
                  pl.BlockSpec((TM, d), row),
                  pl.BlockSpec((1, 6, d), lambda i: (_mod_row(i, tps, nb), 0, 0)),
                  pl.BlockSpec((1, d), const), pl.BlockSpec((1, d), const)] + r_in,
        out_specs=[pl.BlockSpec((TM, d), row)] + r_out,
        out_shape=[jax.ShapeDtypeStruct((ntok, d), F32)] + r_shapes,
        compiler_params=_cparams(("arbitrary",)),
        name="odd_out",
    )(yf2, yb2, xbc2, *([p2] * (SSD_INNER // zw)), f2, dsk, g768, w1, w2, x2, mod, lg, lb, rw_t, rb)


def _route_tile(x, mod_ref, rw_ref, rb_ref, v_ref, idx_ref, gw_ref, cnt_ref):
    @pl.when(pl.program_id(0) == 0)
    def _():
        cnt_ref[...] = jnp.zeros_like(cnt_ref)

    shift = mod_ref[0, 3:4, :]
    scale = mod_ref[0, 4:5, :]
    v = x * (1.0 + scale) + shift
    v_hi = v.astype(BF16)
    v_ref[...] = v_hi
    v_lo = (v - v_hi.astype(F32)).astype(BF16)
    w_hi, w_lo = _split2(rw_ref[...])
    nt_dot = functools.partial(lax.dot_general, dimension_numbers=(((1,), (1,)), ((), ())),
                               preferred_element_type=F32)
    logits = nt_dot(w_hi, v_hi) + (nt_dot(w_hi, v_lo) + nt_dot(w_lo, v_hi))
    s = 1.0 / (1.0 + jnp.exp(-logits))
    sel = s + rb_ref[:, 0:1]
    izero = jnp.zeros((1, TM), jnp.int32)
    best = None
    for g in range(N_EXPERT_GROUPS):
        a = [sel[g * EXPERTS_PER_GROUP + j:g * EXPERTS_PER_GROUP + j + 1, :] for j in range(EXPERTS_PER_GROUP)]
        sv = [s[g * EXPERTS_PER_GROUP + j:g * EXPERTS_PER_GROUP + j + 1, :] for j in range(EXPERTS_PER_GROUP)]
        m1, i1, s1 = a[0], izero, sv[0]
        for j in range(1, EXPERTS_PER_GROUP):
            gt = a[j] > m1
            m1 = jnp.where(gt, a[j], m1)
            i1 = jnp.where(gt, j, i1)
            s1 = jnp.where(gt, sv[j], s1)
        m2 = jnp.full((1, TM), -jnp.inf, F32)
        i2, s2 = izero, jnp.zeros((1, TM), F32)
        for j in range(EXPERTS_PER_GROUP):
            gt = jnp.logical_and(i1 != j, a[j] > m2)
            m2 = jnp.where(gt, a[j], m2)
            i2 = jnp.where(gt, j, i2)
            s2 = jnp.where(gt, sv[j], s2)
        cand = (m1 + m2, i1 + g * EXPERTS_PER_GROUP, i2 + g * EXPERTS_PER_GROUP, s1, s2)
        if best is None:
            best = cand
        else:
            gt = cand[0] > best[0]
            best = tuple(jnp.where(gt, cn, bs) for cn, bs in zip(cand, best))
    _, e1, e2, w1, w2 = best
    tot = w1 + w2
    idx_ref[0] = jnp.concatenate([e1, e2], axis=0)
    gw_ref[0] = jnp.concatenate([w1 / tot, w2 / tot], axis=0)
    eio = lax.broadcasted_iota(jnp.int32, (N_EXPERTS, TM), 0)
    oh = jnp.logical_or(eio == e1, eio == e2).astype(F32)
    cnt_ref[...] += _ceil_rows(jnp.sum(oh, axis=1, keepdims=True))


def _router_specs(ntok, d):
    nt = ntok // TM
    in_specs = [pl.BlockSpec((N_EXPERTS, d), lambda i: (0, 0)),
                pl.BlockSpec((N_EXPERTS, LANES), lambda i: (0, 0))]
    out_specs = [pl.BlockSpec((TM, d), lambda i: (i, 0)),
                 pl.BlockSpec((1, TOP_K, TM), lambda i: (i, 0, 0)),
                 pl.BlockSpec((1, TOP_K, TM), lambda i: (i, 0, 0)),
                 pl.BlockSpec((N_EXPERTS, LANES), lambda i: (0, 0))]
    out_shapes = [jax.ShapeDtypeStruct((ntok, d), BF16),
                  jax.ShapeDtypeStruct((nt, TOP_K, TM), jnp.int32),
                  jax.ShapeDtypeStruct((nt, TOP_K, TM), F32),
                  jax.ShapeDtypeStruct((N_EXPERTS, LANES), F32)]
    return in_specs, out_specs, out_shapes


def _ceil_rows(c):
    return jnp.ceil(c * (1.0 / ROWG)) * ROWG


def _meta_kernel(idx_ref, start_ref, qpos_ref, meta_ref, run_ref):
    @pl.when(pl.program_id(0) == 0)
    def _():
        run_ref[...] = jnp.zeros_like(run_ref)

    e1 = idx_ref[0, 0:1, :]
    e2 = idx_ref[0, 1:2, :]
    eio = lax.broadcasted_iota(jnp.int32, (N_EXPERTS, TM), 0)
    oh1 = eio == e1
    oh2 = eio == e2
    oh = jnp.logical_or(oh1, oh2)
    r = lax.broadcasted_iota(jnp.int32, (TM, TM), 0)
    c = lax.broadcasted_iota(jnp.int32, (TM, TM), 1)
    before = (r < c).astype(BF16)
    rank = jnp.dot(oh.astype(BF16), before, preferred_element_type=F32)
    run_len = jnp.broadcast_to(_ceil_rows(jnp.sum(oh.astype(F32), axis=1, keepdims=True)), (N_EXPERTS, LANES))
    er = lax.broadcasted_iota(jnp.int32, (N_EXPERTS, N_EXPERTS), 0)
    ec = lax.broadcasted_iota(jnp.int32, (N_EXPERTS, N_EXPERTS), 1)
    off = jnp.dot((ec < er).astype(F32), run_len, precision=HI, preferred_element_type=F32)
    pos = rank + off[:, 0:1]
    q1 = jnp.sum(jnp.where(oh1, pos, 0.0), axis=0, keepdims=True)
    q2 = jnp.sum(jnp.where(oh2, pos, 0.0), axis=0, keepdims=True)
    qpos_ref[0] = jnp.concatenate([q1, q2], axis=0).astype(jnp.int32)
    lane = lax.broadcasted_iota(jnp.int32, (N_EXPERTS, LANES), 1)
    meta = jnp.where(lane == 0, start_ref[...] + run_ref[...], jnp.where(lane == 1, off, run_len))
    meta_ref[0] = (meta * (1.0 / ROWG)).astype(jnp.int32)
    run_ref[...] += run_len


def _meta(idx_t, start):
    nt = idx_t.shape[0]
    return pl.pallas_call(
        _meta_kernel,
        grid=(nt,),
        in_specs=[pl.BlockSpec((1, TOP_K, TM), lambda i: (i, 0, 0)),
                  pl.BlockSpec((N_EXPERTS, LANES), lambda i: (0, 0))],
        out_specs=[pl.BlockSpec((1, TOP_K, TM), lambda i: (i, 0, 0)),
                   pl.BlockSpec((1, N_EXPERTS, LANES), lambda i: (i, 0, 0))],
        out_shape=[jax.ShapeDtypeStruct((nt, TOP_K, TM), jnp.int32),
                   jax.ShapeDtypeStruct((nt, N_EXPERTS, LANES), jnp.int32)],
        scratch_shapes=[pltpu.VMEM((N_EXPERTS, LANES), F32)],
        compiler_params=_cparams(("arbitrary",)),
        name="moe_meta",
    )(idx_t, start)


def _sort_select(qpos_ref):
    r = lax.broadcasted_iota(jnp.int32, (SORT_ROWS, TM), 0)
    return r == qpos_ref[0, 0:1, :], r == qpos_ref[0, 1:2, :]


def _start_run_copies(i, gb_ref, ob_ref, nb_ref, make_copy):
    big = COPY_GROUPS * ROWG
    for e in range(N_EXPERTS):
        n = nb_ref[i * N_EXPERTS + e]
        buf_row = gb_ref[i * N_EXPERTS + e] * ROWG
        tile_row = ob_ref[i * N_EXPERTS + e] * ROWG
        n_big = lax.shift_right_logical(n, COPY_SHIFT)

        def body_big(k, carry, buf_row=buf_row, tile_row=tile_row):
            make_copy(pl.multiple_of(tile_row + k * big, ROWG), pl.multiple_of(buf_row + k * big, ROWG), big).start()
            return carry

        def body_one(k, carry, buf_row=buf_row, tile_row=tile_row):
            make_copy(pl.multiple_of(tile_row + k * ROWG, ROWG), pl.multiple_of(buf_row + k * ROWG, ROWG),
                      ROWG).start()
            return carry

        lax.fori_loop(0, n_big, body_big, 0)
        lax.fori_loop(lax.shift_left(n_big, COPY_SHIFT), n, body_one, 0)


def _wait_run_copies(i, nb_ref, make_copy):
    n_big = 0
    n_one = 0
    for e in range(N_EXPERTS):
        n = nb_ref[i * N_EXPERTS + e]
        n_big = n_big + lax.shift_right_logical(n, COPY_SHIFT)
        n_one = n_one + lax.bitwise_and(n, COPY_GROUPS - 1)

    def wait_big(k, carry):
        make_copy(0, 0, COPY_GROUPS * ROWG).wait()
        return carry

    def wait_one(k, carry):
        make_copy(0, 0, ROWG).wait()
        return carry

    lax.fori_loop(0, n_big, wait_big, 0)
    lax.fori_loop(0, n_one, wait_one, 0)


def _scatter_kernel(gb_ref, ob_ref, nb_ref, ts_ref, tn_ref, nu_ref, v_ref, qpos_ref, gw_ref, buf_ref, xs_ref, z_ref,
                    sem, zsem):
    i = pl.program_id(0)
    last = pl.num_programs(0) - 1
    slot = i % 2
    d = v_ref.shape[1]
    m0, m1 = _sort_select(qpos_ref)
    sel = jnp.logical_or(m0, m1).astype(BF16)
    xs_ref[slot, :, 0:d] = jnp.dot(sel, v_ref[...], preferred_element_type=F32)
    gate = jnp.sum(jnp.where(m0, gw_ref[0, 0:1, :], 0.0) + jnp.where(m1, gw_ref[0, 1:2, :], 0.0),
                   axis=1, keepdims=True)
    xs_ref[slot, :, d:d + LANES] = jnp.broadcast_to(gate, (SORT_ROWS, LANES))

    def copy_from(s):
        def copy(tile_row, buf_row, rows):
            return pltpu.make_async_copy(xs_ref.at[s, pl.ds(tile_row, rows)], buf_ref.at[pl.ds(buf_row, rows)],
                                         sem.at[s])
        return copy

    _start_run_copies(i, gb_ref, ob_ref, nb_ref, copy_from(slot))

    @pl.when(i > 0)
    def _():
        _wait_run_copies(i - 1, nb_ref, copy_from(1 - slot))

    @pl.when(i == last)
    def _():
        _wait_run_copies(i, nb_ref, copy_from(slot))

    @pl.when(i == last)
    def _():
        z_ref[...] = jnp.zeros_like(z_ref)

        def zcopy(buf_row):
            return pltpu.make_async_copy(z_ref.at[pl.ds(0, ROWG)], buf_ref.at[pl.ds(buf_row, ROWG)], zsem)

        total = 0
        for e in range(N_EXPERTS):
            n = tn_ref[e]
            row0 = ts_ref[e] * ROWG

            def body(k, carry, row0=row0):
                zcopy(pl.multiple_of(row0 + k * ROWG, ROWG)).start()
                return carry

            lax.fori_loop(0, n, body, 0)
            total = total + n

        def wbody(k, carry):
            zcopy(0).wait()
            return carry

        lax.fori_loop(0, total, wbody, 0)

        def zblock(blk):
            return pltpu.make_async_copy(z_ref, buf_ref.at[pl.ds(pl.multiple_of(blk * FFN_ROWS, FFN_ROWS), FFN_ROWS)],
                                         zsem)

        def bbody(blk, carry):
            zblock(blk).start()
            return carry

        def bwait(blk, carry):
            zblock(blk).wait()
            return carry

        nblk = buf_ref.shape[0] // FFN_ROWS
        lax.fori_loop(nu_ref[0], nblk, bbody, 0)
        lax.fori_loop(nu_ref[0], nblk, bwait, 0)


def _scatter(gb, ob, nb8, tail_s, tail_n, n_used, v, qpos, gw_t, nrow):
    ntok, d = v.shape
    grid_spec = pltpu.PrefetchScalarGridSpec(
        num_scalar_prefetch=6,
        grid=(ntok // TM,),
        in_specs=[pl.BlockSpec((TM, d), lambda i, *_: (i, 0)),
                  pl.BlockSpec((1, TOP_K, TM), lambda i, *_: (i, 0, 0)),
                  pl.BlockSpec((1, TOP_K, TM), lambda i, *_: (i, 0, 0))],
        out_specs=pl.BlockSpec(memory_space=pl.ANY),
        scratch_shapes=[pltpu.VMEM((2, SORT_ROWS, d + LANES), F32), pltpu.VMEM((FFN_ROWS, d + LANES), F32),
                        pltpu.SemaphoreType.DMA((2,)), pltpu.SemaphoreType.DMA(())],
    )
    return pl.pallas_call(
        _scatter_kernel,
        grid_spec=grid_spec,
        out_shape=jax.ShapeDtypeStruct((nrow, d + LANES), F32),
        compiler_params=_cparams(("arbitrary",)),
        name="moe_scatter",
    )(gb, ob, nb8, tail_s, tail_n, n_used, v, qpos, gw_t)


def _moe_kernel(be_ref, nu_ref, x_ref, wg_ref, wu_ref, wd_ref, o_ref, wg_bf, wu_bf, wd_bf):
    i = pl.program_id(0)
    d = o_ref.shape[1]

    @pl.when(jnp.logical_or(i == 0, be_ref[i] != be_ref[jnp.maximum(i - 1, 0)]))
    def _():
        wg_bf[...] = wg_ref[...].astype(BF16)
        wu_bf[...] = wu_ref[...].astype(BF16)
        wd_bf[...] = wd_ref[...].astype(BF16)

    @pl.when(i < nu_ref[0])
    def _():
        xb = x_ref[...]
        x = xb[:, 0:d].astype(BF16)
        g = jnp.dot(x, wg_bf[...], preferred_element_type=F32)
        u = jnp.dot(x, wu_bf[...], preferred_element_type=F32)
        h = (_silu(g) * u).astype(BF16)
        y = jnp.dot(h, wd_bf[...], preferred_element_type=F32)
        o_ref[...] = y * xb[:, d:d + 1]

    @pl.when(pl.program_id(0) >= nu_ref[0])
    def _():
        o_ref[...] = jnp.zeros_like(o_ref)


def _moe_ffn(block_e, n_used, buf, layer, wg, wu, wd):
    nrow, dw = buf.shape
    d = dw - LANES
    nblk = nrow // FFN_ROWS
    de = wg.shape[3]
    used = lambda i, nu: jnp.minimum(i, nu[0] - 1)
    grid_spec = pltpu.PrefetchScalarGridSpec(
        num_scalar_prefetch=2,
        grid=(nblk,),
        in_specs=[pl.BlockSpec((FFN_ROWS, dw), lambda i, be, nu: (used(i, nu), 0)),
                  pl.BlockSpec((None, None, d, de), lambda i, be, nu: (layer, be[i], 0, 0)),
                  pl.BlockSpec((None, None, d, de), lambda i, be, nu: (layer, be[i], 0, 0)),
                  pl.BlockSpec((None, None, de, d), lambda i, be, nu: (layer, be[i], 0, 0))],
        out_specs=pl.BlockSpec((FFN_ROWS, d), lambda i, be, nu: (i, 0)),
        scratch_shapes=[pltpu.VMEM((d, de), BF16), pltpu.VMEM((d, de), BF16), pltpu.VMEM((de, d), BF16)],
    )
    return pl.pallas_call(
        _moe_kernel,
        grid_spec=grid_spec,
        out_shape=jax.ShapeDtypeStruct((nrow, d), F32),
        compiler_params=pltpu.CompilerParams(dimension_semantics=("arbitrary",), vmem_limit_bytes=FFN_VMEM_LIMIT),
        name="moe_ffn",
    )(block_e, n_used, buf, wg, wu, wd)


def _ln2_kernel(gb_ref, ob_ref, nb_ref, x_ref, qpos_ref, mod_ref, lg_ref, lb_ref, y_ref, *rest, with_inproj):
    if with_inproj:
        modn_ref, win_ref, o_ref, p_ref, ys_ref, sem = rest
    else:
        o_ref, ys_ref, sem = rest
    i = pl.program_id(0)
    slot = i % 2

    def copy_to(s):
        def copy(tile_row, buf_row, rows):
            return pltpu.make_async_copy(y_ref.at[pl.ds(buf_row, rows)], ys_ref.at[s, pl.ds(tile_row, rows)],
                                         sem.at[s])
        return copy

    @pl.when(i == 0)
    def _():
        ys_ref[...] = jnp.zeros_like(ys_ref)
        _start_run_copies(0, gb_ref, ob_ref, nb_ref, copy_to(0))

    @pl.when(i + 1 < pl.num_programs(0))
    def _():
        _start_run_copies(i + 1, gb_ref, ob_ref, nb_ref, copy_to(1 - slot))

    _wait_run_copies(i, nb_ref, copy_to(slot))
    m0, m1 = _sort_select(qpos_ref)
    sel = jnp.logical_or(m0, m1).astype(BF16)
    y = lax.dot_general(sel, ys_ref[slot].astype(BF16), (((0,), (0,)), ((), ())), preferred_element_type=F32)
    gate = mod_ref[0, 5:6, :]
    x_new = _ln(ALPHA * x_ref[...] + gate * y, lg_ref[...], lb_ref[...])
    o_ref[...] = x_new
    if with_inproj:
        u = (x_new * (1.0 + modn_ref[0, 1:2, :]) + modn_ref[0, 0:1, :]).astype(BF16)
        p_ref[...] = jnp.dot(u, win_ref[...], preferred_element_type=F32).astype(p_ref.dtype)


def _ln2(gb, ob, nb8, x2, qpos, mod, lg, lb, ybuf, tps, nb, next_in):
    ntok, d = x2.shape
    latent_only = next_in is None
    if latent_only:
        out_map = lambda i, *_: ((i // tps) * (tps - 1) + jnp.maximum(i % tps - 1, 0), 0)
        out_rows = ntok - nb * TM
    else:
        out_map = lambda i, *_: (i, 0)
        out_rows = ntok
    in_specs = [pl.BlockSpec((TM, d), lambda i, *_: (i, 0)),
                pl.BlockSpec((1, TOP_K, TM), lambda i, *_: (i, 0, 0)),
                pl.BlockSpec((1, 6, d), lambda i, *_: (_mod_row(i, tps, nb), 0, 0)),
                pl.BlockSpec((1, d), lambda i, *_: (0, 0)),
                pl.BlockSpec((1, d), lambda i, *_: (0, 0)),
                pl.BlockSpec(memory_space=pl.ANY)]
    out_specs = [pl.BlockSpec((TM, d), out_map)]
    out_shape = [jax.ShapeDtypeStruct((out_rows, d), F32)]
    args = [gb, ob, nb8, x2, qpos, mod, lg, lb, ybuf]
    if next_in is not None:
        mod_next, w_next = next_in
        nw = w_next.shape[1]
        in_specs += [pl.BlockSpec((1, 6, d), lambda i, *_: (_mod_row(i, tps, nb), 0, 0)),
                     pl.BlockSpec((d, nw), lambda i, *_: (0, 0))]
        out_specs.append(pl.BlockSpec((TM, nw), lambda i, *_: (i, 0)))
        out_shape.append(jax.ShapeDtypeStruct((ntok, nw), BF16))
        args += [mod_next, w_next]
    grid_spec = pltpu.PrefetchScalarGridSpec(
        num_scalar_prefetch=3,
        grid=(ntok // TM,),
        in_specs=in_specs,
        out_specs=out_specs,
        scratch_shapes=[pltpu.VMEM((2, SORT_ROWS, d), F32), pltpu.SemaphoreType.DMA((2,))],
    )
    outs = pl.pallas_call(
        functools.partial(_ln2_kernel, with_inproj=next_in is not None),
        grid_spec=grid_spec,
        out_shape=out_shape,
        compiler_params=_cparams(("arbitrary",)),
        name="post_moe_ln",
    )(*args)
    return (outs[0], outs[1]) if next_in is not None else (outs[0], None)


def _even_w_in(w):
    o = np.cumsum((0, GLA_QK, GLA_QK, GLA_V, GLA_V, GLA_RANK, GLA_RANK, GQA_Q, GQA_KV, GQA_KV))
    q, k, v, r = w[:, o[0]:o[1]], w[:, o[1]:o[2]], w[:, o[2]:o[3]], w[:, o[3]:o[4]]
    lr = w[:, o[4]:o[6]]
    gq, gk, gv = w[:, o[6]:o[7]], w[:, o[7]:o[8]], w[:, o[8]:o[9]]
    d = w.shape[0]
    zeros = lambda n: jnp.zeros((d, n), w.dtype)
    out = jnp.concatenate([gq, gk, gv, q, k, lr, zeros(LANES - 2 * GLA_RANK), zeros(LANES), v, r], axis=1)
    assert out.shape[1] == EVEN_W
    return out.astype(BF16)


def _odd_w_in(w):
    o = np.cumsum((0, SSD_INNER, SSD_INNER, SSD_BC, SSD_BC, SSD_HEADS, SSD_HEADS, FNET_W))
    d = w.shape[0]
    out = jnp.concatenate([w[:, o[1]:o[4]], w[:, :o[1]], w[:, o[4]:o[6]],
                           jnp.zeros((d, LANES - 2 * SSD_HEADS), w.dtype), w[:, o[6]:o[7]]], axis=1)
    assert out.shape[1] == ODD_W
    return out.astype(BF16)


def _rope_tables(ctx_len, seq):
    rows = seq // GRID_W
    r = jnp.repeat(jnp.arange(rows, dtype=F32), GRID_W)
    col = jnp.tile(jnp.arange(GRID_W, dtype=F32), rows)
    half = GQA_HD // 2
    inv = ROPE_THETA ** (-jnp.arange(0, half, 2, dtype=F32) / half)
    ar = r[:, None] * inv
    ac = col[:, None] * inv
    ang = jnp.concatenate([ar, ar, ac, ac], -1)
    cos = jnp.concatenate([jnp.ones((ctx_len, GQA_HD), F32), jnp.cos(ang)], 0)
    sin = jnp.concatenate([jnp.zeros((ctx_len, GQA_HD), F32), jnp.sin(ang)], 0)
    return jnp.tile(cos, (1, 2)), jnp.tile(sin, (1, 2))


def _even_layer(x2, p2, mod, nb, t, tps, ctx_len, w_o, w_dec, b_dec, gla_g, qn_g, kn_g, cos2, sin2, lg, lb, rw_t, rb):
    p3 = p2.reshape(nb, t, EVEN_W)
    wd_pads = [jnp.zeros((LANES, GLA_QK), F32).at[di * GLA_RANK:(di + 1) * GLA_RANK].set(w_dec[di])
               for di in range(2)]
    outs = _gla_scan(p3, wd_pads, [b_dec[di][None, :] for di in range(2)])
    qh2, kg2, vt4 = _qkprep(p2, cos2, sin2, jnp.tile(qn_g, 2)[None, :], jnp.tile(kn_g, 2)[None, :], tps)
    att = _attention(qh2.reshape(nb, t, GQA_HEADS * LANES), kg2.reshape(nb, t, GQA_KV_HEADS * LANES), vt4, ctx_len)
    w1, w2 = w_o[:GLA_V].astype(BF16), w_o[GLA_V:].astype(BF16)
    ntok = nb * t
    return _evout(outs[0].reshape(ntok, GLA_V), outs[1].reshape(ntok, GLA_V), p2, att.reshape(ntok, GQA_Q),
                  jnp.tile(gla_g, GLA_HEADS)[None, :], w1, w2, x2, mod, lg, lb, rw_t, rb, tps, nb)


def _odd_layer(x2, p2, mod, nb, t, tps, ctx_len, w_o, conv_w, conv_b, dt_bias, a_log, d_skip, ssd_g, fnet_g,
               lg, lb, rw_t, rb):
    p3 = p2.reshape(nb, t, ODD_W)
    w8 = jnp.zeros((8, SSD_CONV_CH), F32).at[:SSD_CONV].set(conv_w)
    xbc2 = _conv(p2, w8, conv_b[None, :], tps)
    xbc3 = xbc2.reshape(nb, t, SSD_CONV_CH)
    params = []
    heads = jnp.arange(SSD_HEADS)
    for di in range(2):
        lanes = di * SSD_HEADS + heads
        bias_pad = jnp.zeros((1, LANES), F32).at[0, lanes].set(dt_bias[di])
        a_pad = jnp.zeros((1, LANES), F32).at[0, lanes].set(-jnp.exp(a_log[di]))
        eexp = jnp.zeros((LANES, SSD_INNER), F32).at[jnp.repeat(lanes, SSD_HD), jnp.arange(SSD_INNER)].set(1.0)
        esel = jnp.zeros((16, LANES), F32).at[heads, lanes].set(1.0)
        params.append((bias_pad, a_pad, eexp, esel))
    ys = _ssd_scan(xbc3, p3, params)
    f3 = p3[:, :, ODD_W - FNET_W:]
    g256 = fnet_g[None, :]
    fmix = jnp.concatenate([_fourier_ctx(f3[:, :ctx_len], g256), _fourier_latent(f3[:, ctx_len:], g256)], axis=1)
    ntok = nb * t
    return _odout(ys[0].reshape(ntok, SSD_INNER), ys[1].reshape(ntok, SSD_INNER), xbc2, p2,
                  fmix.reshape(ntok, FNET_W), jnp.repeat(d_skip, SSD_HD)[None, :], ssd_g[None, :],
                  w_o[:SSD_INNER].astype(BF16), w_o[SSD_INNER:].astype(BF16), x2, mod, lg, lb, rw_t, rb, tps, nb)


def _moe_layer(routed, mod, nb, tps, layer, wg, wu, wd, lg, lb, next_in):
    x2, v, idx_t, gw_t, cnt = routed
    ntok, d = x2.shape
    nt = ntok // TM
    rows = cnt[:, 0].astype(jnp.int32)
    padded = (rows + FFN_ROWS - 1) // FFN_ROWS * FFN_ROWS
    pad_end = jnp.cumsum(padded)
    pad_start = pad_end - padded
    nblk = -(-(ntok * TOP_K + nt * N_EXPERTS * (ROWG - 1)) // FFN_ROWS) + N_EXPERTS
    blk_row = jnp.arange(nblk, dtype=jnp.int32)[:, None] * FFN_ROWS
    block_e = jnp.minimum(jnp.sum((blk_row >= pad_end[None, :]).astype(jnp.int32), axis=1), N_EXPERTS - 1)
    n_used = (pad_end[-1:] // FFN_ROWS).astype(jnp.int32)
    start = jnp.broadcast_to(pad_start.astype(F32)[:, None], (N_EXPERTS, LANES))
    qpos, meta = _meta(idx_t, start)
    gb, ob, nb8 = (meta[:, :, j].reshape(nt * N_EXPERTS) for j in range(3))
    tail_s = ((pad_start + rows) // ROWG).astype(jnp.int32)
    tail_n = ((padded - rows) // ROWG).astype(jnp.int32)
    buf = _scatter(gb, ob, nb8, tail_s, tail_n, n_used, v, qpos, gw_t, nblk * FFN_ROWS)
    y = _moe_ffn(block_e, n_used, buf, layer, wg, wu, wd)
    return _ln2(gb, ob, nb8, x2, qpos, mod, lg, lb, y, tps, nb, next_in)


def kernel(x, c, ctx, c_ctx, ada_w, ada_b, ln_g, ln_b, ev_w_in, ev_w_o, gla_w_decay, gla_b_decay, gla_norm_g,
           gqa_q_norm_g, gqa_k_norm_g, od_w_in, od_w_o, ssd_conv_w, ssd_conv_b, ssd_dt_bias, ssd_a_log, ssd_d,
           ssd_norm_g, fnet_norm_g, router_w, router_b, exp_w_gate, exp_w_up, exp_w_down):
    nb, seq, d = x.shape
    ctx_len = ctx.shape[1]
    assert ctx_len == TM and seq % TM == 0 and d == D_MODEL and nb <= 7
    t = ctx_len + seq
    tps = t // TM
    ntok = nb * t
    cc = jnp.zeros((8, d), F32).at[:nb].set(c).at[nb].set(c_ctx)
    mod_all = _ada_all(cc, ada_w, ada_b).reshape(DEPTH, 8, 6, d)
    cos2, sin2 = _rope_tables(ctx_len, seq)
    rw_t = router_w.T
    rb = jnp.broadcast_to(router_b[:, None], (N_EXPERTS, LANES))
    x2 = jnp.concatenate([ctx, x], axis=1).reshape(ntok, d)

    def w_in_of(layer):
        return _even_w_in(ev_w_in[layer // 2]) if layer % 2 == 0 else _odd_w_in(od_w_in[layer // 2])

    p2 = _inproj(x2, mod_all[0], w_in_of(0), tps, nb)
    for layer in range(DEPTH):
        mod = mod_all[layer]
        i = layer // 2
        lg0, lb0 = ln_g[layer, 0][None, :], ln_b[layer, 0][None, :]
        lg1, lb1 = ln_g[layer, 1][None, :], ln_b[layer, 1][None, :]
        if layer % 2 == 0:
            routed = _even_layer(x2, p2, mod, nb, t, tps, ctx_len, ev_w_o[i], gla_w_decay[i], gla_b_decay[i],
                                 gla_norm_g[i], gqa_q_norm_g[i], gqa_k_norm_g[i], cos2, sin2, lg0, lb0, rw_t, rb)
        else:
            routed = _odd_layer(x2, p2, mod, nb, t, tps, ctx_len, od_w_o[i], ssd_conv_w[i], ssd_conv_b[i],
                                ssd_dt_bias[i], ssd_a_log[i], ssd_d[i], ssd_norm_g[i], fnet_norm_g[i], lg0, lb0,
                                rw_t, rb)
        next_in = (mod_all[layer + 1], w_in_of(layer + 1)) if layer + 1 < DEPTH else None
        x2, p2 = _moe_layer(routed, mod, nb, tps, layer, exp_w_gate, exp_w_up, exp_w_down, lg1, lb1, next_in)
    return x2.reshape(nb, seq, d)
```

```python
import functools
import itertools
import math

import jax
import jax.numpy as jnp
import numpy as np
from jax import lax
from jax.experimental import pallas as pl
from jax.experimental.pallas import tpu as pltpu

F32 = jnp.float32
BF16 = jnp.bfloat16
HI = lax.Precision.HIGHEST

D_MODEL = 1024
DEPTH = 4
GRID_W = 64
GLA_HEADS, GLA_DK, GLA_DV, GLA_RANK, GLA_TAU, GLA_CHUNK = 4, 64, 128, 16, 16.0, 64
GQA_HEADS, GQA_KV_HEADS, GQA_HD = 8, 2, 64
ROPE_THETA = 10000.0
SSD_HEADS, SSD_HD, SSD_GROUPS, SSD_STATE, SSD_CONV, SSD_CHUNK = 12, 64, 2, 128, 5, 64
SSD_HPG = SSD_HEADS // SSD_GROUPS
SSD_INNER = SSD_HEADS * SSD_HD
SSD_BC = SSD_GROUPS * SSD_STATE
SSD_CONV_CH = SSD_INNER + 2 * SSD_BC
FNET_GROUPS, FNET_GC = 4, 64
FNET_W = FNET_GROUPS * FNET_GC
N_EXPERTS, N_EXPERT_GROUPS, TOP_K, D_EXPERT = 16, 4, 2, 768
EXPERTS_PER_GROUP = N_EXPERTS // N_EXPERT_GROUPS
GLA_QK = GLA_HEADS * GLA_DK
GLA_V = GLA_HEADS * GLA_DV
GQA_Q = GQA_HEADS * GQA_HD
GQA_KV = GQA_KV_HEADS * GQA_HD
EPS = 1e-6
ALPHA = (2.0 * DEPTH) ** 0.25

LANES = 128
TM = 256
EVEN_W = 2560
ODD_W = 2432
VMEM_LIMIT = 48 * 1024 * 1024
LOG2E = 1.4426950408889634
ATT_TQ = 256
ATT_HEADROOM = 64.0
ATT_MIN_DENOM = 2.0 ** -60
ATT_VROWS = LANES
ROWG = 8
SORT_ROWS = TOP_K * TM + N_EXPERTS * (ROWG - 1)
COPY_SHIFT = 2
COPY_GROUPS = 1 << COPY_SHIFT
FFN_ROWS = 512
FFN_VMEM_LIMIT = 56 * 1024 * 1024


def _cparams(sem):
    return pltpu.CompilerParams(dimension_semantics=sem, vmem_limit_bytes=VMEM_LIMIT)


def _silu(x):
    return x * (1.0 / (1.0 + jnp.exp(-x)))


def _softplus(x):
    return jnp.maximum(x, 0.0) + jnp.log1p(jnp.exp(-jnp.abs(x)))


def _mod_row(i, tiles_per_seq, n_batch):
    return jnp.where(i % tiles_per_seq == 0, n_batch, i // tiles_per_seq)


def _ada_kernel(c_ref, w_ref, b_ref, o_ref):
    s = _silu(c_ref[...])
    o_ref[0] = jnp.dot(s, w_ref[0], precision=HI, preferred_element_type=F32) + b_ref[0]


def _ada_all(cc, ada_w, ada_b):
    depth, d, n = ada_w.shape
    tn = 1536
    return pl.pallas_call(
        _ada_kernel,
        grid=(depth, n // tn),
        in_specs=[pl.BlockSpec((8, d), lambda l, j: (0, 0)),
                  pl.BlockSpec((1, d, tn), lambda l, j: (l, 0, j)),
                  pl.BlockSpec((1, 1, tn), lambda l, j: (l, 0, j))],
        out_specs=pl.BlockSpec((1, 8, tn), lambda l, j: (l, 0, j)),
        out_shape=jax.ShapeDtypeStruct((depth, 8, n), F32),
        compiler_params=_cparams(("arbitrary", "arbitrary")),
        name="adaln",
    )(cc, ada_w, ada_b.reshape(depth, 1, n))


def _inproj_kernel(x_ref, mod_ref, w_ref, o_ref):
    shift = mod_ref[0, 0:1, :]
    scale = mod_ref[0, 1:2, :]
    u = (x_ref[...] * (1.0 + scale) + shift).astype(BF16)
    o_ref[...] = jnp.dot(u, w_ref[...], preferred_element_type=F32).astype(o_ref.dtype)


def _inproj(x2, mod, w, tps, nb):
    ntok, d = x2.shape
    nw = w.shape[1]
    return pl.pallas_call(
        _inproj_kernel,
        grid=(ntok // TM,),
        in_specs=[pl.BlockSpec((TM, d), lambda i: (i, 0)),
                  pl.BlockSpec((1, 6, d), lambda i: (_mod_row(i, tps, nb), 0, 0)),
                  pl.BlockSpec((d, nw), lambda i: (0, 0))],
        out_specs=pl.BlockSpec((TM, nw), lambda i: (i, 0)),
        out_shape=jax.ShapeDtypeStruct((ntok, nw), BF16),
        compiler_params=_cparams(("arbitrary",)),
        name="inproj",
    )(x2, mod, w)


def _scan_block(i, nblk, reverse):
    if not reverse:
        return i
    return jnp.where(i == 0, 0, nblk - i)


def _tri(n, reverse):
    r = lax.broadcasted_iota(jnp.int32, (n, n), 0)
    c = lax.broadcasted_iota(jnp.int32, (n, n), 1)
    return (c >= r) if reverse else (c <= r)


def _chunk_sum_mat(chunk, reverse):
    l = np.arange(TM)[:, None]
    m = np.arange(TM)[None, :]
    same = (l // chunk) == (m // chunk)
    return (same & ((m >= l) if reverse else (m <= l))).astype(np.float32)


def _split3(x):
    hi = x.astype(BF16)
    r1 = x - hi.astype(F32)
    mid = r1.astype(BF16)
    return hi, mid, (r1 - mid.astype(F32)).astype(BF16)


def _chunk_rows(x, chunk, idx):
    return jnp.concatenate([jnp.broadcast_to(x[c * chunk + idx:c * chunk + idx + 1, :], (chunk, x.shape[1]))
                            for c in range(x.shape[0] // chunk)], axis=0)


def _gla_dir(q_ref, k_ref, v_ref, lr_ref, wd_ref, bd_ref, cm_ref, o_ref, s_ref, *, reverse):
    L = GLA_CHUNK
    nchunk = TM // L
    nt = (((1,), (1,)), ((), ()))
    tn = (((0,), (0,)), ((), ()))
    end = 0 if reverse else L - 1
    mid = L // 2 if reverse else L // 2 - 1

    @pl.when(pl.program_id(1) == 0)
    def _():
        s_ref[...] = jnp.zeros_like(s_ref)

    q = q_ref[...].astype(F32) * (GLA_DK ** -0.5)
    k = k_ref[...].astype(F32)
    v = v_ref[...]
    lr = lr_ref[...]
    z = bd_ref[...] + sum(jnp.dot(lr, wd_ref[j], preferred_element_type=F32) for j in range(3))
    yield
    la = -_softplus(-z) * (1.0 / GLA_TAU)
    cm = cm_ref[...]
    b = sum(jnp.dot(cm, part, preferred_element_type=F32) for part in _split3(la))
    yield
    bmid = _chunk_rows(b, L, mid)
    bend = _chunk_rows(b, L, end)
    qs = q * jnp.exp(b - bmid)
    ks = (k * jnp.exp(bmid - b)).astype(BF16)
    qi = q * jnp.exp(b)
    kend = (k * jnp.exp(bend - b)).astype(BF16)
    r = lax.broadcasted_iota(jnp.int32, (TM, TM), 0)
    c = lax.broadcasted_iota(jnp.int32, (TM, TM), 1)
    mask = jnp.logical_and(r // L == c // L, (c >= r) if reverse else (c <= r))
    lane = lax.broadcasted_iota(jnp.int32, (TM, LANES), 1)
    heads = (lane < GLA_DK, lane >= GLA_DK)
    yield
    att = [lax.dot_general(jnp.where(heads[h], qs, 0.0).astype(BF16), ks, nt, preferred_element_type=F32)
           for h in range(2)]
    yield
    intra = []
    for h in range(2):
        att_h = jnp.where(mask, att[h], 0.0).astype(BF16)
        intra.append(jnp.dot(att_h, v[:, h * GLA_DV:(h + 1) * GLA_DV], preferred_element_type=F32))
    yield
    qi_h = [jnp.where(heads[h], qi, 0.0).astype(BF16) for h in range(2)]
    tot = jnp.concatenate([b[ch * L + end:ch * L + end + 1, :] for ch in range(nchunk)]
                          + [jnp.zeros((ROWG - nchunk, LANES), F32)], axis=0).T
    ds, dec = [], []
    for ch in range(nchunk):
        rows = slice(ch * L, (ch + 1) * L)
        ds.append(lax.dot_general(kend[rows], v[rows], tn, preferred_element_type=F32))
        dec.append(jnp.exp(jnp.broadcast_to(tot[:, ch:ch + 1], (2 * GLA_DK, 2 * GLA_DV))))
    yield
    s = s_ref[...]
    inter = [None] * nchunk
    for ch in (range(nchunk - 1, -1, -1) if reverse else range(nchunk)):
        rows = slice(ch * L, (ch + 1) * L)
        s_bf = s.astype(BF16)
        inter[ch] = jnp.concatenate(
            [jnp.dot(qi_h[h][rows], s_bf[:, h * GLA_DV:(h + 1) * GLA_DV], preferred_element_type=F32)
             for h in range(2)], axis=1)
        s = dec[ch] * s + ds[ch]
        yield
    s_ref[...] = s
    o_ref[...] = (jnp.concatenate(intra, axis=1) + jnp.concatenate(inter, axis=0)).astype(o_ref.dtype)


def _alternate(*stage_generators):
    for _ in itertools.zip_longest(*stage_generators):
        pass


GLA_PAIRS = GLA_HEADS // 2


def _gla_kernel(*refs):
    n_streams = 2 * GLA_PAIRS
    n_in = (len(refs) - 2 - n_streams) // n_streams
    outs = refs[n_streams * n_in:n_streams * n_in + 2]
    states = refs[n_streams * n_in + 2:]
    scans = []
    for s in range(n_streams):
        reverse, pair = divmod(s, GLA_PAIRS)
        o_view = outs[reverse].at[:, pl.ds(pair * 2 * GLA_DV, 2 * GLA_DV)]
        scans.append(_gla_dir(*refs[s * n_in:(s + 1) * n_in], o_view, states[s], reverse=bool(reverse)))
    _alternate(*scans)


def _gla_scan(p3, wd_pads, bds):
    nb, t, _ = p3.shape
    nblk = t // TM
    in_specs, args, out_specs = [], [], []
    for reverse in (False, True):
        blk = functools.partial(_scan_block, nblk=nblk, reverse=reverse)
        wd3 = jnp.stack(_split3(wd_pads[int(reverse)]))
        cm = jnp.asarray(_chunk_sum_mat(GLA_CHUNK, reverse), BF16)
        for p in range(GLA_PAIRS):
            in_specs += [pl.BlockSpec((None, TM, LANES), lambda b, i, blk=blk, p=p: (b, blk(i), 6 + p)),
                         pl.BlockSpec((None, TM, LANES), lambda b, i, blk=blk, p=p: (b, blk(i), 8 + p)),
                         pl.BlockSpec((None, TM, 2 * GLA_DV), lambda b, i, blk=blk, p=p: (b, blk(i), 6 + p)),
                         pl.BlockSpec((None, TM, LANES), lambda b, i, blk=blk: (b, blk(i), 10)),
                         pl.BlockSpec((3, LANES, LANES), lambda b, i, p=p: (0, 0, p)),
                         pl.BlockSpec((1, LANES), lambda b, i, p=p: (0, p)),
                         pl.BlockSpec((TM, TM), lambda b, i: (0, 0))]
            args += [p3, p3, p3, p3, wd3, bds[int(reverse)], cm]
        out_specs.append(pl.BlockSpec((None, TM, GLA_V), lambda b, i, blk=blk: (b, blk(i), 0)))
    state = pltpu.VMEM((2 * GLA_DK, 2 * GLA_DV), F32)
    return pl.pallas_call(
        _gla_kernel,
        grid=(nb, nblk),
        in_specs=in_specs,
        out_specs=out_specs,
        out_shape=[jax.ShapeDtypeStruct((nb, t, GLA_V), BF16)] * 2,
        scratch_shapes=[state] * (2 * GLA_PAIRS),
        compiler_params=_cparams(("arbitrary", "arbitrary")),
        name="gla_scan",
    )(*args)


def _seg_ones(width, seg):
    r = lax.broadcasted_iota(jnp.int32, (width, width), 0) // seg
    c = lax.broadcasted_iota(jnp.int32, (width, width), 1) // seg
    return jnp.where(r == c, 1.0 / seg, 0.0).astype(BF16)


def _seg_mean(x, bd):
    hi = x.astype(BF16)
    lo = (x - hi.astype(F32)).astype(BF16)
    return jnp.dot(hi, bd, preferred_element_type=F32) + jnp.dot(lo, bd, preferred_element_type=F32)


def _norm_rope(x, g, cos, sin, bd):
    ms = _seg_mean(x * x, bd)
    xn = x * lax.rsqrt(ms + EPS) * g
    lane = lax.broadcasted_iota(jnp.int32, xn.shape, 1)
    quarter = GQA_HD // 4
    up = pltpu.roll(xn, LANES - quarter, 1)
    dn = pltpu.roll(xn, quarter, 1)
    rot = jnp.where(lane % (2 * quarter) < quarter, -up, dn)
    return xn * cos + rot * sin


def _qkprep_kernel(q_ref, k_ref, v_ref, cos_ref, sin_ref, gq_ref, gk_ref, qo_ref, ko_ref, vo_ref):
    bd = _seg_ones(LANES, GQA_HD)
    cos = cos_ref[...]
    sin = sin_ref[...]
    lane = lax.broadcasted_iota(jnp.int32, (TM, LANES), 1)
    low = lane < GQA_HD
    extra = lane == GQA_HD
    one_hot = jnp.where(extra, 1.0, 0.0)
    k_bound = math.sqrt(GQA_HD) * jnp.max(jnp.abs(gk_ref[...]), axis=-1, keepdims=True) * 1.02
    for j in range(GQA_Q // LANES):
        x = q_ref[:, j * LANES:(j + 1) * LANES].astype(F32)
        y = _norm_rope(x, gq_ref[...], cos, sin, bd) * (GQA_HD ** -0.5 * LOG2E)
        y = y.astype(BF16).astype(F32)
        norm = jnp.sqrt(_seg_mean(y * y, bd) * GQA_HD)
        shift = ATT_HEADROOM - norm * k_bound
        lo = jnp.where(low, y, jnp.where(extra, pltpu.roll(shift, GQA_HD, 1), 0.0))
        hi = jnp.where(low, pltpu.roll(y, GQA_HD, 1), jnp.where(extra, shift, 0.0))
        qo_ref[:, (2 * j) * LANES:(2 * j + 1) * LANES] = lo.astype(BF16)
        qo_ref[:, (2 * j + 1) * LANES:(2 * j + 2) * LANES] = hi.astype(BF16)
    k = _norm_rope(k_ref[...].astype(F32), gk_ref[...], cos, sin, bd)
    ko_ref[:, 0:LANES] = jnp.where(low, k, one_hot).astype(BF16)
    ko_ref[:, LANES:2 * LANES] = jnp.where(low, pltpu.roll(k, GQA_HD, 1), one_hot).astype(BF16)
    v = v_ref[...].astype(F32)
    vo_ref[0] = jnp.where(low, v, one_hot).T[0:ATT_VROWS].astype(BF16)
    vo_ref[1] = jnp.where(low, pltpu.roll(v, GQA_HD, 1), one_hot).T[0:ATT_VROWS].astype(BF16)


def _qkprep(p2, cos2, sin2, gq2, gk2, tps):
    ntok = p2.shape[0]
    nb = ntok // (tps * TM)
    return pl.pallas_call(
        _qkprep_kernel,
        grid=(ntok // TM,),
        in_specs=[pl.BlockSpec((TM, GQA_Q), lambda i: (i, 0)),
                  pl.BlockSpec((TM, LANES), lambda i: (i, 4)),
                  pl.BlockSpec((TM, LANES), lambda i: (i, 5)),
                  pl.BlockSpec((TM, LANES), lambda i: (i % tps, 0)),
                  pl.BlockSpec((TM, LANES), lambda i: (i % tps, 0)),
                  pl.BlockSpec((1, LANES), lambda i: (0, 0)),
                  pl.BlockSpec((1, LANES), lambda i: (0, 0))],
        out_specs=[pl.BlockSpec((TM, GQA_HEADS * LANES), lambda i: (i, 0)),
                   pl.BlockSpec((TM, GQA_KV_HEADS * LANES), lambda i: (i, 0)),
                   pl.BlockSpec((None, GQA_KV_HEADS, ATT_VROWS, TM), lambda i: (i // tps, 0, 0, i % tps))],
        out_shape=[jax.ShapeDtypeStruct((ntok, GQA_HEADS * LANES), BF16),
                   jax.ShapeDtypeStruct((ntok, GQA_KV_HEADS * LANES), BF16),
                   jax.ShapeDtypeStruct((nb, GQA_KV_HEADS, ATT_VROWS, tps * TM), BF16)],
        compiler_params=_cparams(("arbitrary",)),
        name="qkprep",
    )(p2, p2, p2, cos2, sin2, gq2, gk2)


def _attn_kernel(q_ref, k_ref, vt_ref, o_ref, *, ctx_len, n_lat_chunks, ck, n_blocks):
    i = pl.program_id(2)
    rep = GQA_HEADS // GQA_KV_HEADS
    is_ctx = i < ctx_len // ATT_TQ
    q4 = jnp.concatenate([q_ref[:, r * LANES:(r + 1) * LANES] for r in range(rep)], axis=0)
    nt = (((1,), (1,)), ((), ()))

    def chunk(acc, kc, vtc):
        st = lax.dot_general(kc, q4, nt, preferred_element_type=F32)
        return acc + jnp.dot(vtc, jnp.exp2(st).astype(BF16), preferred_element_type=F32)

    def store(acc):
        out = (acc * (1.0 / acc[GQA_HD:GQA_HD + 1, :])).T
        low = lax.broadcasted_iota(jnp.int32, (ATT_TQ, LANES), 1) < GQA_HD
        hs = [out[r * ATT_TQ:(r + 1) * ATT_TQ] for r in range(rep)]
        for u in range(rep // 2):
            o_ref[:, u * LANES:(u + 1) * LANES] = jnp.where(
                low, hs[2 * u], pltpu.roll(hs[2 * u + 1], GQA_HD, 1)).astype(o_ref.dtype)

    acc = chunk(jnp.zeros((ATT_VROWS, rep * ATT_TQ), F32), k_ref[0:ctx_len, :], vt_ref[:, 0:ctx_len])

    def body(c, acc):
        off = pl.multiple_of(ctx_len + c * ck, TM)
        return chunk(acc, k_ref[pl.ds(off, ck), :], vt_ref[:, pl.ds(off, ck)])

    acc = lax.fori_loop(0, jnp.where(is_ctx, 0, n_lat_chunks), body, acc)
    healthy = jnp.min(acc[GQA_HD:GQA_HD + 1, :]) >= ATT_MIN_DENOM

    @pl.when(healthy)
    def _():
        store(acc)

    @pl.when(jnp.logical_not(healthy))
    def _():
        def online(c, carry):
            m, acc = carry
            off = pl.multiple_of(c * TM, TM)
            st = lax.dot_general(k_ref[pl.ds(off, TM), :], q4, nt, preferred_element_type=F32)
            mn = jnp.maximum(m, jnp.max(st, axis=0, keepdims=True))
            pt = jnp.exp2(st - mn).astype(BF16)
            acc = jnp.exp2(m - mn) * acc + jnp.dot(vt_ref[:, pl.ds(off, TM)], pt, preferred_element_type=F32)
            return mn, acc

        init = (jnp.full((1, rep * ATT_TQ), -1e30, F32), jnp.zeros((ATT_VROWS, rep * ATT_TQ), F32))
        _, acc2 = lax.fori_loop(0, jnp.where(is_ctx, ctx_len // TM, n_blocks), online, init)
        store(acc2)


def _attention(qh3, kg3, vt4, ctx_len):
    nb, t, _ = qh3.shape
    seq = t - ctx_len
    ck = next(c for c in (2048, 1024, 512, 256) if seq % c == 0)
    rep = GQA_HEADS // GQA_KV_HEADS
    return pl.pallas_call(
        functools.partial(_attn_kernel, ctx_len=ctx_len, n_lat_chunks=seq // ck, ck=ck, n_blocks=t // TM),
        grid=(nb, GQA_KV_HEADS, t // ATT_TQ),
        in_specs=[pl.BlockSpec((None, ATT_TQ, rep * LANES), lambda b, g, i: (b, i, g)),
                  pl.BlockSpec((None, t, LANES), lambda b, g, i: (b, 0, g)),
                  pl.BlockSpec((None, None, ATT_VROWS, t), lambda b, g, i: (b, g, 0, 0))],
        out_specs=pl.BlockSpec((None, ATT_TQ, rep * GQA_HD), lambda b, g, i: (b, i, g)),
        out_shape=jax.ShapeDtypeStruct((nb, t, GQA_Q), BF16),
        compiler_params=_cparams(("arbitrary", "arbitrary", "arbitrary")),
        name="gqa_attn",
    )(qh3, kg3, vt4)


def _ln(x, g, b):
    mu = jnp.mean(x, axis=-1, keepdims=True)
    xc = x - mu
    var = jnp.mean(xc * xc, axis=-1, keepdims=True)
    return xc * lax.rsqrt(var + EPS) * g + b


def _evout_kernel(of_ref, ob_ref, r_ref, att_ref, g_ref, w1_ref, w2_ref, x_ref, mod_ref, lg_ref, lb_ref, rw_ref,
                  rb_ref, o_ref, *route_refs):
    o = of_ref[...].astype(F32) + ob_ref[...].astype(F32)
    r = r_ref[...].astype(F32)
    parts = []
    for h in range(GLA_HEADS):
        oh = o[:, h * GLA_DV:(h + 1) * GLA_DV]
        ms = jnp.mean(oh * oh, axis=-1, keepdims=True)
        parts.append(oh * lax.rsqrt(ms + EPS))
    gl = (jnp.concatenate(parts, axis=1) * g_ref[...] * _silu(r)).astype(BF16)
    y = jnp.dot(gl, w1_ref[...], preferred_element_type=F32)
    y = y + jnp.dot(att_ref[...], w2_ref[...], preferred_element_type=F32)
    gate = mod_ref[0, 2:3, :]
    x_new = _ln(ALPHA * x_ref[...] + gate * y, lg_ref[...], lb_ref[...])
    o_ref[...] = x_new
    _route_tile(x_new, mod_ref, rw_ref, rb_ref, *route_refs)


def _evout(of2, ob2, p2, att2, g512, w1, w2, x2, mod, lg, lb, rw_t, rb, tps, nb):
    ntok, d = x2.shape
    row = lambda i: (i, 0)
    const = lambda i: (0, 0)
    r_in, r_out, r_shapes = _router_specs(ntok, d)
    return pl.pallas_call(
        _evout_kernel,
        grid=(ntok // TM,),
        in_specs=[pl.BlockSpec((TM, GLA_V), row), pl.BlockSpec((TM, GLA_V), row),
                  pl.BlockSpec((TM, GLA_V), lambda i: (i, 4)),
                  pl.BlockSpec((TM, GQA_Q), row),
                  pl.BlockSpec((1, GLA_V), const),
                  pl.BlockSpec((GLA_V, d), const), pl.BlockSpec((GQA_Q, d), const),
                  pl.BlockSpec((TM, d), row),
                  pl.BlockSpec((1, 6, d), lambda i: (_mod_row(i, tps, nb), 0, 0)),
                  pl.BlockSpec((1, d), const), pl.BlockSpec((1, d), const)] + r_in,
        out_specs=[pl.BlockSpec((TM, d), row)] + r_out,
        out_shape=[jax.ShapeDtypeStruct((ntok, d), F32)] + r_shapes,
        compiler_params=_cparams(("arbitrary",)),
        name="even_out",
    )(of2, ob2, p2, att2, g512, w1, w2, x2, mod, lg, lb, rw_t, rb)


def _conv_kernel(prev_ref, cur_ref, next_ref, w_ref, b_ref, o_ref, pad_ref, *, tps):
    j = pl.program_id(0) % tps
    has_prev = (j >= 2).astype(F32)
    has_next = jnp.logical_and(j >= 1, j <= tps - 2).astype(F32)
    pad_ref[0:8, :] = prev_ref[...].astype(F32) * has_prev
    pad_ref[8:8 + TM, :] = cur_ref[...].astype(F32)
    pad_ref[8 + TM:16 + TM, :] = next_ref[...].astype(F32) * has_next
    half = (SSD_CONV - 1) // 2
    cw = 256
    for c0 in range(0, SSD_CONV_CH, cw):
        cols = slice(c0, c0 + cw)
        acc = jnp.zeros((TM, cw), F32) + b_ref[:, cols]
        for tap in range(SSD_CONV):
            acc = acc + pad_ref[pl.ds(8 + tap - half, TM), cols] * w_ref[tap:tap + 1, cols]
        o_ref[:, cols] = _silu(acc).astype(o_ref.dtype)


def _conv(p2, w8, bias, tps):
    ntok = p2.shape[0]
    cw = SSD_CONV_CH
    r8 = TM // 8
    nrow8 = ntok // 8
    return pl.pallas_call(
        functools.partial(_conv_kernel, tps=tps),
        grid=(ntok // TM,),
        in_specs=[pl.BlockSpec((8, cw), lambda i: (jnp.maximum(i * r8 - 1, 0), 0)),
                  pl.BlockSpec((TM, cw), lambda i: (i, 0)),
                  pl.BlockSpec((8, cw), lambda i: (jnp.minimum((i + 1) * r8, nrow8 - 1), 0)),
                  pl.BlockSpec((8, cw), lambda i: (0, 0)),
                  pl.BlockSpec((1, cw), lambda i: (0, 0))],
        out_specs=pl.BlockSpec((TM, cw), lambda i: (i, 0)),
        out_shape=jax.ShapeDtypeStruct((ntok, SSD_CONV_CH), BF16),
        scratch_shapes=[pltpu.VMEM((TM + 16, cw), F32)],
        compiler_params=_cparams(("arbitrary",)),
        name="ssd_conv",
    )(p2, p2, p2, w8, bias)


def _ssd_dir(xbc_ref, dt_ref, bias_ref, a_ref, eexp_ref, esel_ref, cm_ref, o_ref, h_ref, *, reverse):
    L = SSD_CHUNK
    GW = SSD_HPG * SSD_HD
    nchunk = TM // L
    nt = (((1,), (1,)), ((), ()))
    tn = (((0,), (0,)), ((), ()))
    end = 0 if reverse else L - 1
    lane0 = SSD_HEADS if reverse else 0

    @pl.when(pl.program_id(1) == 0)
    def _():
        h_ref[...] = jnp.zeros_like(h_ref)

    xs = xbc_ref[:, 0:SSD_INNER].astype(F32)
    bs = xbc_ref[:, SSD_INNER:SSD_INNER + SSD_BC]
    cs = xbc_ref[:, SSD_INNER + SSD_BC:SSD_CONV_CH]
    dt = _softplus(dt_ref[...].astype(F32) + bias_ref[...])
    a = dt * a_ref[...]
    cm = cm_ref[...]
    cum = sum(jnp.dot(cm, part, preferred_element_type=F32) for part in _split3(a))
    yield
    cum_t = sum(lax.dot_general(esel_ref[...], part, nt, preferred_element_type=F32)
                for part in _split3(cum))
    cend = _chunk_rows(cum, L, end)
    eexp = eexp_ref[...]

    def expand(m):
        return jnp.dot(m.astype(BF16), eexp, preferred_element_type=F32)

    xdt = xs * expand(dt)
    xdt_b = xdt.astype(BF16)
    yield
    xend = (xdt * expand(jnp.exp(cend - cum))).astype(BF16)
    yield
    ecum = expand(jnp.exp(cum))
    tot = jnp.concatenate([cum[ch * L + end:ch * L + end + 1, :] for ch in range(nchunk)]
                          + [jnp.zeros((ROWG - nchunk, LANES), F32)], axis=0)
    edec = jnp.exp(sum(jnp.dot(part, eexp, preferred_element_type=F32) for part in _split3(tot)))

    tri = _tri(L, reverse)
    low = lax.broadcasted_iota(jnp.int32, (L, LANES), 1) < SSD_HD
    intra = [[None] * SSD_GROUPS for _ in range(nchunk)]
    dh = [[None] * SSD_GROUPS for _ in range(nchunk)]
    for ch in range(nchunk):
        rows = slice(ch * L, (ch + 1) * L)
        for g in range(SSD_GROUPS):
            bg = bs[rows, g * SSD_STATE:(g + 1) * SSD_STATE]
            cg = cs[rows, g * SSD_STATE:(g + 1) * SSD_STATE]
            cb = lax.dot_general(cg, bg, nt, preferred_element_type=F32)
            pairs = []
            for pp in range(SSD_HPG // 2):
                yh = []
                for u in range(2):
                    h = g * SSD_HPG + 2 * pp + u
                    seg = cum[rows, lane0 + h:lane0 + h + 1] - cum_t[h:h + 1, rows]
                    dec = jnp.exp(jnp.where(tri, seg, -1e30))
                    mat = (cb * dec).astype(BF16)
                    col = (h - u) * SSD_HD
                    yh.append(jnp.dot(mat, xdt_b[rows, col:col + LANES], preferred_element_type=F32))
                pairs.append(jnp.where(low, yh[0], yh[1]))
                yield
            intra[ch][g] = jnp.concatenate(pairs, axis=1)
            dh[ch][g] = lax.dot_general(bg, xend[rows, g * GW:(g + 1) * GW], tn, preferred_element_type=F32)

    hs = [h_ref[g] for g in range(SSD_GROUPS)]
    out = [None] * nchunk
    for ch in (range(nchunk - 1, -1, -1) if reverse else range(nchunk)):
        rows = slice(ch * L, (ch + 1) * L)
        ys = []
        for g in range(SSD_GROUPS):
            cg = cs[rows, g * SSD_STATE:(g + 1) * SSD_STATE]
            y_inter = jnp.dot(cg, hs[g].astype(BF16), preferred_element_type=F32) * ecum[rows, g * GW:(g + 1) * GW]
            ys.append(intra[ch][g] + y_inter)
            hs[g] = hs[g] * edec[ch:ch + 1, g * GW:(g + 1) * GW] + dh[ch][g]
        out[ch] = jnp.concatenate(ys, axis=1)
        yield
    for g in range(SSD_GROUPS):
        h_ref[g] = hs[g]
    o_ref[...] = jnp.concatenate(out, axis=0).astype(o_ref.dtype)


def _ssd_kernel(*refs):
    n_in = (len(refs) - 4) // 2
    o_f, o_b, h_f, h_b = refs[2 * n_in:]
    _alternate(_ssd_dir(*refs[:n_in], o_f, h_f, reverse=False),
               _ssd_dir(*refs[n_in:2 * n_in], o_b, h_b, reverse=True))


def _ssd_scan(xbc3, p3, params):
    nb, t, _ = xbc3.shape
    nblk = t // TM
    const = lambda b, i: (0, 0)
    in_specs, args, out_specs = [], [], []
    for reverse in (False, True):
        blk = functools.partial(_scan_block, nblk=nblk, reverse=reverse)
        bias_pad, a_pad, eexp, esel = params[int(reverse)]
        in_specs += [pl.BlockSpec((None, TM, SSD_CONV_CH), lambda b, i, blk=blk: (b, blk(i), 0)),
                     pl.BlockSpec((None, TM, LANES), lambda b, i, blk=blk: (b, blk(i), 16)),
                     pl.BlockSpec((1, LANES), const), pl.BlockSpec((1, LANES), const),
                     pl.BlockSpec((LANES, SSD_INNER), const), pl.BlockSpec((16, LANES), const),
                     pl.BlockSpec((TM, TM), const)]
        args += [xbc3, p3, bias_pad, a_pad, eexp.astype(BF16), esel.astype(BF16),
                 jnp.asarray(_chunk_sum_mat(SSD_CHUNK, reverse), BF16)]
        out_specs.append(pl.BlockSpec((None, TM, SSD_INNER), lambda b, i, blk=blk: (b, blk(i), 0)))
    state = pltpu.VMEM((SSD_GROUPS, SSD_STATE, SSD_HPG * SSD_HD), F32)
    return pl.pallas_call(
        _ssd_kernel,
        grid=(nb, nblk),
        in_specs=in_specs,
        out_specs=out_specs,
        out_shape=[jax.ShapeDtypeStruct((nb, t, SSD_INNER), BF16)] * 2,
        scratch_shapes=[state, state],
        compiler_params=_cparams(("arbitrary", "arbitrary")),
        name="ssd_scan",
    )(*args)


def _dft_mats(n):
    k = jnp.arange(n, dtype=jnp.int32)
    ang = ((k[:, None] * k[None, :]) % n).astype(F32) * (2.0 * math.pi / n)
    return jnp.cos(ang), jnp.sin(ang)


def _chan_mats():
    cc, sc = _dft_mats(FNET_GC)
    eye = jnp.eye(FNET_GROUPS, dtype=F32)
    return jnp.kron(eye, cc), jnp.kron(eye, sc)


def _group_rms(x, g, bd):
    return x * lax.rsqrt(_seg_mean(x * x, bd) + EPS) * g


def _split2(x):
    hi = x.astype(BF16)
    return hi, (x - hi.astype(F32)).astype(BF16)


def _mm(a, b):
    a_hi, a_lo = _split2(a)
    b_hi, b_lo = _split2(b)
    dot = functools.partial(jnp.dot, preferred_element_type=F32)
    return dot(a_hi, b_hi) + (dot(a_hi, b_lo) + dot(a_lo, b_hi))


def _ffta_kernel(x_ref, g_ref, cc_ref, sc_ref, c1_ref, s1_ref, twc_ref, tws_ref, yr_ref, yi_ref, *, nb2):
    bd = _seg_ones(FNET_W, FNET_GC)
    for j in range(nb2):
        x = x_ref[:, j * FNET_W:(j + 1) * FNET_W].astype(F32)
        xn = _group_rms(x, g_ref[...], bd)
        vr = _mm(xn, cc_ref[...])
        vi = -_mm(xn, sc_ref[...])
        c1 = c1_ref[...]
        s1 = s1_ref[...]
        yr = _mm(c1, vr) + _mm(s1, vi)
        yi = _mm(c1, vi) - _mm(s1, vr)
        tc = twc_ref[j]
        ts = tws_ref[j]
        yr_ref[j] = yr * tc + yi * ts
        yi_ref[j] = yi * tc - yr * ts


def _fftb_kernel(yr_ref, yi_ref, c2_ref, s2_ref, o_ref, *, scale):
    o_ref[...] = (_mm(c2_ref[...], yr_ref[...]) + _mm(s2_ref[...], yi_ref[...])) * scale


def _fftc_kernel(x_ref, g_ref, cc_ref, sc_ref, ct_ref, st_ref, o_ref, *, scale):
    bd = _seg_ones(FNET_W, FNET_GC)
    xn = _group_rms(x_ref[...].astype(F32), g_ref[...], bd)
    a = _mm(xn, cc_ref[...])
    b = _mm(xn, sc_ref[...])
    o_ref[...] = (_mm(ct_ref[...], a) - _mm(st_ref[...], b)) * scale


def _fourier_latent(f_lat, g256):
    nb, s, _ = f_lat.shape
    n2 = 64
    n1 = s // n2
    nb2 = 8
    ccm, scm = _chan_mats()
    c1, s1 = _dft_mats(n1)
    c2, s2 = _dft_mats(n2)
    t2 = jnp.arange(n2, dtype=jnp.int32)[:, None]
    k1 = jnp.arange(n1, dtype=jnp.int32)[None, :]
    tw = ((t2 * k1) % s).astype(F32) * (2.0 * math.pi / s)
    twc = jnp.cos(tw)[:, :, None]
    tws = jnp.sin(tw)[:, :, None]
    x2 = f_lat.reshape(nb, n1, n2 * FNET_W)
    const2 = lambda b, j: (0, 0)
    yr, yi = pl.pallas_call(
        functools.partial(_ffta_kernel, nb2=nb2),
        grid=(nb, n2 // nb2),
        in_specs=[pl.BlockSpec((None, n1, nb2 * FNET_W), lambda b, j: (b, 0, j)),
                  pl.BlockSpec((1, FNET_W), const2),
                  pl.BlockSpec((FNET_W, FNET_W), const2), pl.BlockSpec((FNET_W, FNET_W), const2),
                  pl.BlockSpec((n1, n1), const2), pl.BlockSpec((n1, n1), const2),
                  pl.BlockSpec((nb2, n1, 1), lambda b, j: (j, 0, 0)),
                  pl.BlockSpec((nb2, n1, 1), lambda b, j: (j, 0, 0))],
        out_specs=[pl.BlockSpec((None, nb2, n1, FNET_W), lambda b, j: (b, j, 0, 0)),
                   pl.BlockSpec((None, nb2, n1, FNET_W), lambda b, j: (b, j, 0, 0))],
        out_shape=[jax.ShapeDtypeStruct((nb, n2, n1, FNET_W), F32),
                   jax.ShapeDtypeStruct((nb, n2, n1, FNET_W), F32)],
        compiler_params=_cparams(("arbitrary", "arbitrary")),
        name="fft_a",
    )(x2, g256, ccm, scm, c1, s1, twc, tws)
    ncol = n1 * FNET_W
    tn = min(8192, ncol)
    out = pl.pallas_call(
        functools.partial(_fftb_kernel, scale=1.0 / math.sqrt(s * FNET_GC)),
        grid=(nb, ncol // tn),
        in_specs=[pl.BlockSpec((None, n2, tn), lambda b, j: (b, 0, j)),
                  pl.BlockSpec((None, n2, tn), lambda b, j: (b, 0, j)),
                  pl.BlockSpec((n2, n2), const2), pl.BlockSpec((n2, n2), const2)],
        out_specs=pl.BlockSpec((None, n2, tn), lambda b, j: (b, 0, j)),
        out_shape=jax.ShapeDtypeStruct((nb, n2, ncol), F32),
        compiler_params=_cparams(("arbitrary", "arbitrary")),
        name="fft_b",
    )(yr.reshape(nb, n2, ncol), yi.reshape(nb, n2, ncol), c2, s2)
    return out.reshape(nb, s, FNET_W)


def _fourier_ctx(f_ctx, g256):
    nb, tc, _ = f_ctx.shape
    ccm, scm = _chan_mats()
    ct, st = _dft_mats(tc)
    const = lambda b: (0, 0)
    return pl.pallas_call(
        functools.partial(_fftc_kernel, scale=1.0 / math.sqrt(tc * FNET_GC)),
        grid=(nb,),
        in_specs=[pl.BlockSpec((None, tc, FNET_W), lambda b: (b, 0, 0)),
                  pl.BlockSpec((1, FNET_W), const),
                  pl.BlockSpec((FNET_W, FNET_W), const), pl.BlockSpec((FNET_W, FNET_W), const),
                  pl.BlockSpec((tc, tc), const), pl.BlockSpec((tc, tc), const)],
        out_specs=pl.BlockSpec((None, tc, FNET_W), lambda b: (b, 0, 0)),
        out_shape=jax.ShapeDtypeStruct((nb, tc, FNET_W), F32),
        compiler_params=_cparams(("arbitrary",)),
        name="fft_ctx",
    )(f_ctx, g256, ccm, scm, ct, st)


def _odout_kernel(yf_ref, yb_ref, xs_ref, z0_ref, z1_ref, z2_ref, f_ref, dsk_ref, g_ref, w1_ref, w2_ref, x_ref,
                  mod_ref, lg_ref, lb_ref, rw_ref, rb_ref, o_ref, *route_refs):
    GW = SSD_HPG * SSD_HD
    y = yf_ref[...].astype(F32) + yb_ref[...].astype(F32) + xs_ref[...].astype(F32) * dsk_ref[...]
    z = jnp.concatenate([z0_ref[...], z1_ref[...], z2_ref[...]], axis=1).astype(F32)
    y = y * _silu(z)
    parts = []
    for g in range(SSD_GROUPS):
        yg = y[:, g * GW:(g + 1) * GW]
        ms = jnp.mean(yg * yg, axis=-1, keepdims=True)
        parts.append(yg * lax.rsqrt(ms + EPS))
    yn = (jnp.concatenate(parts, axis=1) * g_ref[...]).astype(BF16)
    o = jnp.dot(yn, w1_ref[...], preferred_element_type=F32)
    o = o + jnp.dot(f_ref[...].astype(BF16), w2_ref[...], preferred_element_type=F32)
    gate = mod_ref[0, 2:3, :]
    x_new = _ln(ALPHA * x_ref[...] + gate * o, lg_ref[...], lb_ref[...])
    o_ref[...] = x_new
    _route_tile(x_new, mod_ref, rw_ref, rb_ref, *route_refs)


def _odout(yf2, yb2, xbc2, p2, f2, dsk, g768, w1, w2, x2, mod, lg, lb, rw_t, rb, tps, nb):
    ntok, d = x2.shape
    row = lambda i: (i, 0)
    const = lambda i: (0, 0)
    r_in, r_out, r_shapes = _router_specs(ntok, d)
    zw = 256
    return pl.pallas_call(
        _odout_kernel,
        grid=(ntok // TM,),
        in_specs=[pl.BlockSpec((TM, SSD_INNER), row), pl.BlockSpec((TM, SSD_INNER), row),
                  pl.BlockSpec((TM, SSD_INNER), row)]
        + [pl.BlockSpec((TM, zw), lambda i, j=j: (i, SSD_CONV_CH // zw + j)) for j in range(SSD_INNER // zw)]
        + [pl.BlockSpec((TM, FNET_W), row),
                  pl.BlockSpec((1, SSD_INNER), const), pl.BlockSpec((1, SSD_INNER), const),
                  pl.BlockSpec((SSD_INNER, d), const), pl.BlockSpec((FNET_W, d), const),
                  pl.BlockSpec((TM, d), row),
                  pl.BlockSpec((1, 6, d), lambda i: (_mod_row(i, tps, nb), 0, 0)),
                  pl.BlockSpec((1, d), const), pl.BlockSpec((1, d), const)] + r_in,
        out_specs=[pl.BlockSpec((TM, d), row)] + r_out,
        out_shape=[jax.ShapeDtypeStruct((ntok, d), F32)] + r_shapes,
        compiler_params=_cparams(("arbitrary",)),
        name="odd_out",
    )(yf2, yb2, xbc2, *([p2] * (SSD_INNER // zw)), f2, dsk, g768, w1, w2, x2, mod, lg, lb, rw_t, rb)


def _route_tile(x, mod_ref, rw_ref, rb_ref, v_ref, idx_ref, gw_ref, cnt_ref):
    @pl.when(pl.program_id(0) == 0)
    def _():
        cnt_ref[...] = jnp.zeros_like(cnt_ref)

    shift = mod_ref[0, 3:4, :]
    scale = mod_ref[0, 4:5, :]
    v = x * (1.0 + scale) + shift
    v_hi = v.astype(BF16)
    v_ref[...] = v_hi
    v_lo = (v - v_hi.astype(F32)).astype(BF16)
    w_hi, w_lo = _split2(rw_ref[...])
    nt_dot = functools.partial(lax.dot_general, dimension_numbers=(((1,), (1,)), ((), ())),
                               preferred_element_type=F32)
    logits = nt_dot(w_hi, v_hi) + (nt_dot(w_hi, v_lo) + nt_dot(w_lo, v_hi))
    s = 1.0 / (1.0 + jnp.exp(-logits))
    sel = s + rb_ref[:, 0:1]
    izero = jnp.zeros((1, TM), jnp.int32)
    best = None
    for g in range(N_EXPERT_GROUPS):
        a = [sel[g * EXPERTS_PER_GROUP + j:g * EXPERTS_PER_GROUP + j + 1, :] for j in range(EXPERTS_PER_GROUP)]
        sv = [s[g * EXPERTS_PER_GROUP + j:g * EXPERTS_PER_GROUP + j + 1, :] for j in range(EXPERTS_PER_GROUP)]
        m1, i1, s1 = a[0], izero, sv[0]
        for j in range(1, EXPERTS_PER_GROUP):
            gt = a[j] > m1
            m1 = jnp.where(gt, a[j], m1)
            i1 = jnp.where(gt, j, i1)
            s1 = jnp.where(gt, sv[j], s1)
        m2 = jnp.full((1, TM), -jnp.inf, F32)
        i2, s2 = izero, jnp.zeros((1, TM), F32)
        for j in range(EXPERTS_PER_GROUP):
            gt = jnp.logical_and(i1 != j, a[j] > m2)
            m2 = jnp.where(gt, a[j], m2)
            i2 = jnp.where(gt, j, i2)
            s2 = jnp.where(gt, sv[j], s2)
        cand = (m1 + m2, i1 + g * EXPERTS_PER_GROUP, i2 + g * EXPERTS_PER_GROUP, s1, s2)
        if best is None:
            best = cand
        else:
            gt = cand[0] > best[0]
            best = tuple(jnp.where(gt, cn, bs) for cn, bs in zip(cand, best))
    _, e1, e2, w1, w2 = best
    tot = w1 + w2
    idx_ref[0] = jnp.concatenate([e1, e2], axis=0)
    gw_ref[0] = jnp.concatenate([w1 / tot, w2 / tot], axis=0)
    eio = lax.broadcasted_iota(jnp.int32, (N_EXPERTS, TM), 0)
    oh = jnp.logical_or(eio == e1, eio == e2).astype(F32)
    cnt_ref[...] += _ceil_rows(jnp.sum(oh, axis=1, keepdims=True))


def _router_specs(ntok, d):
    nt = ntok // TM
    in_specs = [pl.BlockSpec((N_EXPERTS, d), lambda i: (0, 0)),
                pl.BlockSpec((N_EXPERTS, LANES), lambda i: (0, 0))]
    out_specs = [pl.BlockSpec((TM, d), lambda i: (i, 0)),
                 pl.BlockSpec((1, TOP_K, TM), lambda i: (i, 0, 0)),
                 pl.BlockSpec((1, TOP_K, TM), lambda i: (i, 0, 0)),
                 pl.BlockSpec((N_EXPERTS, LANES), lambda i: (0, 0))]
    out_shapes = [jax.ShapeDtypeStruct((ntok, d), BF16),
                  jax.ShapeDtypeStruct((nt, TOP_K, TM), jnp.int32),
                  jax.ShapeDtypeStruct((nt, TOP_K, TM), F32),
                  jax.ShapeDtypeStruct((N_EXPERTS, LANES), F32)]
    return in_specs, out_specs, out_shapes


def _ceil_rows(c):
    return jnp.ceil(c * (1.0 / ROWG)) * ROWG


def _meta_kernel(idx_ref, start_ref, qpos_ref, meta_ref, run_ref):
    @pl.when(pl.program_id(0) == 0)
    def _():
        run_ref[...] = jnp.zeros_like(run_ref)

    eio = lax.broadcasted_iota(jnp.int32, (N_EXPERTS, TM), 0)
    r = lax.broadcasted_iota(jnp.int32, (TM, TM), 0)
    c = lax.broadcasted_iota(jnp.int32, (TM, TM), 1)
    before = (r < c).astype(BF16)
    er = lax.broadcasted_iota(jnp.int32, (N_EXPERTS, N_EXPERTS), 0)
    ec = lax.broadcasted_iota(jnp.int32, (N_EXPERTS, N_EXPERTS), 1)
    lower = (ec < er).astype(F32)
    lane = lax.broadcasted_iota(jnp.int32, (N_EXPERTS, LANES), 1)
    run = run_ref[...]
    for g in range(idx_ref.shape[0]):
        oh1 = eio == idx_ref[g, 0:1, :]
        oh2 = eio == idx_ref[g, 1:2, :]
        oh = jnp.logical_or(oh1, oh2)
        rank = jnp.dot(oh.astype(BF16), before, preferred_element_type=F32)
        run_len = jnp.broadcast_to(_ceil_rows(jnp.sum(oh.astype(F32), axis=1, keepdims=True)), (N_EXPERTS, LANES))
        off = jnp.dot(lower, run_len, precision=HI, preferred_element_type=F32)
        pos = rank + off[:, 0:1]
        q1 = jnp.sum(jnp.where(oh1, pos, 0.0), axis=0, keepdims=True)
        q2 = jnp.sum(jnp.where(oh2, pos, 0.0), axis=0, keepdims=True)
        qpos_ref[g] = jnp.concatenate([q1, q2], axis=0).astype(jnp.int32)
        meta = jnp.where(lane == 0, start_ref[...] + run, jnp.where(lane == 1, off, run_len))
        meta_ref[g] = (meta * (1.0 / ROWG)).astype(jnp.int32)
        run = run + run_len
    run_ref[...] = run


def _meta(idx_t, start):
    nt = idx_t.shape[0]
    grp = next(g for g in (4, 3, 2, 1) if nt % g == 0)
    return pl.pallas_call(
        _meta_kernel,
        grid=(nt // grp,),
        in_specs=[pl.BlockSpec((grp, TOP_K, TM), lambda i: (i, 0, 0)),
                  pl.BlockSpec((N_EXPERTS, LANES), lambda i: (0, 0))],
        out_specs=[pl.BlockSpec((grp, TOP_K, TM), lambda i: (i, 0, 0)),
                   pl.BlockSpec((grp, N_EXPERTS, LANES), lambda i: (i, 0, 0))],
        out_shape=[jax.ShapeDtypeStruct((nt, TOP_K, TM), jnp.int32),
                   jax.ShapeDtypeStruct((nt, N_EXPERTS, LANES), jnp.int32)],
        scratch_shapes=[pltpu.VMEM((N_EXPERTS, LANES), F32)],
        compiler_params=_cparams(("arbitrary",)),
        name="moe_meta",
    )(idx_t, start)


def _sort_select(qpos_ref):
    r = lax.broadcasted_iota(jnp.int32, (SORT_ROWS, TM), 0)
    return r == qpos_ref[0, 0:1, :], r == qpos_ref[0, 1:2, :]


def _start_run_copies(i, gb_ref, ob_ref, nb_ref, make_copy):
    big = COPY_GROUPS * ROWG
    for e in range(N_EXPERTS):
        n = nb_ref[i * N_EXPERTS + e]
        buf_row = gb_ref[i * N_EXPERTS + e] * ROWG
        tile_row = ob_ref[i * N_EXPERTS + e] * ROWG
        n_big = lax.shift_right_logical(n, COPY_SHIFT)

        def body_big(k, carry, buf_row=buf_row, tile_row=tile_row):
            make_copy(pl.multiple_of(tile_row + k * big, ROWG), pl.multiple_of(buf_row + k * big, ROWG), big).start()
            return carry

        def body_one(k, carry, buf_row=buf_row, tile_row=tile_row):
            make_copy(pl.multiple_of(tile_row + k * ROWG, ROWG), pl.multiple_of(buf_row + k * ROWG, ROWG),
                      ROWG).start()
            return carry

        lax.fori_loop(0, n_big, body_big, 0)
        lax.fori_loop(lax.shift_left(n_big, COPY_SHIFT), n, body_one, 0)


def _wait_run_copies(i, nb_ref, make_copy):
    n_big = 0
    n_one = 0
    for e in range(N_EXPERTS):
        n = nb_ref[i * N_EXPERTS + e]
        n_big = n_big + lax.shift_right_logical(n, COPY_SHIFT)
        n_one = n_one + lax.bitwise_and(n, COPY_GROUPS - 1)

    def wait_big(k, carry):
        make_copy(0, 0, COPY_GROUPS * ROWG).wait()
        return carry

    def wait_one(k, carry):
        make_copy(0, 0, ROWG).wait()
        return carry

    lax.fori_loop(0, n_big, wait_big, 0)
    lax.fori_loop(0, n_one, wait_one, 0)


def _scatter_kernel(gb_ref, ob_ref, nb_ref, ts_ref, tn_ref, nu_ref, v_ref, qpos_ref, gw_ref, buf_ref, xs_ref, z_ref,
                    sem, zsem):
    i = pl.program_id(0)
    last = pl.num_programs(0) - 1
    slot = i % 2
    d = v_ref.shape[1]
    m0, m1 = _sort_select(qpos_ref)
    sel = jnp.logical_or(m0, m1).astype(BF16)
    xs_ref[slot, :, 0:d] = jnp.dot(sel, v_ref[...], preferred_element_type=F32)
    gate = jnp.sum(jnp.where(m0, gw_ref[0, 0:1, :], 0.0) + jnp.where(m1, gw_ref[0, 1:2, :], 0.0),
                   axis=1, keepdims=True)
    xs_ref[slot, :, d:d + LANES] = jnp.broadcast_to(gate, (SORT_ROWS, LANES))

    def copy_from(s):
        def copy(tile_row, buf_row, rows):
            return pltpu.make_async_copy(xs_ref.at[s, pl.ds(tile_row, rows)], buf_ref.at[pl.ds(buf_row, rows)],
                                         sem.at[s])
        return copy

    _start_run_copies(i, gb_ref, ob_ref, nb_ref, copy_from(slot))

    @pl.when(i > 0)
    def _():
        _wait_run_copies(i - 1, nb_ref, copy_from(1 - slot))

    @pl.when(i == last)
    def _():
        _wait_run_copies(i, nb_ref, copy_from(slot))

    @pl.when(i == last)
    def _():
        z_ref[...] = jnp.zeros_like(z_ref)

        def zcopy(buf_row):
            return pltpu.make_async_copy(z_ref.at[pl.ds(0, ROWG)], buf_ref.at[pl.ds(buf_row, ROWG)], zsem)

        total = 0
        for e in range(N_EXPERTS):
            n = tn_ref[e]
            row0 = ts_ref[e] * ROWG

            def body(k, carry, row0=row0):
                zcopy(pl.multiple_of(row0 + k * ROWG, ROWG)).start()
                return carry

            lax.fori_loop(0, n, body, 0)
            total = total + n

        def wbody(k, carry):
            zcopy(0).wait()
            return carry

        lax.fori_loop(0, total, wbody, 0)

        def zblock(blk):
            return pltpu.make_async_copy(z_ref, buf_ref.at[pl.ds(pl.multiple_of(blk * FFN_ROWS, FFN_ROWS), FFN_ROWS)],
                                         zsem)

        def bbody(blk, carry):
            zblock(blk).start()
            return carry

        def bwait(blk, carry):
            zblock(blk).wait()
            return carry

        nblk = buf_ref.shape[0] // FFN_ROWS
        lax.fori_loop(nu_ref[0], nblk, bbody, 0)
        lax.fori_loop(nu_ref[0], nblk, bwait, 0)


def _scatter(gb, ob, nb8, tail_s, tail_n, n_used, v, qpos, gw_t, nrow):
    ntok, d = v.shape
    grid_spec = pltpu.PrefetchScalarGridSpec(
        num_scalar_prefetch=6,
        grid=(ntok // TM,),
        in_specs=[pl.BlockSpec((TM, d), lambda i, *_: (i, 0)),
                  pl.BlockSpec((1, TOP_K, TM), lambda i, *_: (i, 0, 0)),
                  pl.BlockSpec((1, TOP_K, TM), lambda i, *_: (i, 0, 0))],
        out_specs=pl.BlockSpec(memory_space=pl.ANY),
        scratch_shapes=[pltpu.VMEM((2, SORT_ROWS, d + LANES), F32), pltpu.VMEM((FFN_ROWS, d + LANES), F32),
                        pltpu.SemaphoreType.DMA((2,)), pltpu.SemaphoreType.DMA(())],
    )
    return pl.pallas_call(
        _scatter_kernel,
        grid_spec=grid_spec,
        out_shape=jax.ShapeDtypeStruct((nrow, d + LANES), F32),
        compiler_params=_cparams(("arbitrary",)),
        name="moe_scatter",
    )(gb, ob, nb8, tail_s, tail_n, n_used, v, qpos, gw_t)


def _moe_kernel(be_ref, nu_ref, x_ref, wg_ref, wu_ref, wd_ref, o_ref, wg_bf, wu_bf, wd_bf):
    i = pl.program_id(0)
    d = o_ref.shape[1]

    @pl.when(jnp.logical_or(i == 0, be_ref[i] != be_ref[jnp.maximum(i - 1, 0)]))
    def _():
        wg_bf[...] = wg_ref[...].astype(BF16)
        wu_bf[...] = wu_ref[...].astype(BF16)
        wd_bf[...] = wd_ref[...].astype(BF16)

    @pl.when(i < nu_ref[0])
    def _():
        xb = x_ref[...]
        x = xb[:, 0:d].astype(BF16)
        g = jnp.dot(x, wg_bf[...], preferred_element_type=F32)
        u = jnp.dot(x, wu_bf[...], preferred_element_type=F32)
        h = (_silu(g) * u).astype(BF16)
        y = jnp.dot(h, wd_bf[...], preferred_element_type=F32)
        o_ref[...] = y * xb[:, d:d + 1]

    @pl.when(pl.program_id(0) >= nu_ref[0])
    def _():
        o_ref[...] = jnp.zeros_like(o_ref)


def _moe_ffn(block_e, n_used, buf, layer, wg, wu, wd):
    nrow, dw = buf.shape
    d = dw - LANES
    nblk = nrow // FFN_ROWS
    de = wg.shape[3]
    used = lambda i, nu: jnp.minimum(i, nu[0] - 1)
    grid_spec = pltpu.PrefetchScalarGridSpec(
        num_scalar_prefetch=2,
        grid=(nblk,),
        in_specs=[pl.BlockSpec((FFN_ROWS, dw), lambda i, be, nu: (used(i, nu), 0)),
                  pl.BlockSpec((None, None, d, de), lambda i, be, nu: (layer, be[i], 0, 0)),
                  pl.BlockSpec((None, None, d, de), lambda i, be, nu: (layer, be[i], 0, 0)),
                  pl.BlockSpec((None, None, de, d), lambda i, be, nu: (layer, be[i], 0, 0))],
        out_specs=pl.BlockSpec((FFN_ROWS, d), lambda i, be, nu: (i, 0)),
        scratch_shapes=[pltpu.VMEM((d, de), BF16), pltpu.VMEM((d, de), BF16), pltpu.VMEM((de, d), BF16)],
    )
    return pl.pallas_call(
        _moe_kernel,
        grid_spec=grid_spec,
        out_shape=jax.ShapeDtypeStruct((nrow, d), F32),
        compiler_params=pltpu.CompilerParams(dimension_semantics=("arbitrary",), vmem_limit_bytes=FFN_VMEM_LIMIT),
        name="moe_ffn",
    )(block_e, n_used, buf, wg, wu, wd)


def _ln2_kernel(gb_ref, ob_ref, nb_ref, x_ref, qpos_ref, mod_ref, lg_ref, lb_ref, y_ref, *rest, with_inproj):
    if with_inproj:
        modn_ref, win_ref, o_ref, p_ref, ys_ref, sem = rest
    else:
        o_ref, ys_ref, sem = rest
    i = pl.program_id(0)
    slot = i % 2

    def copy_to(s):
        def copy(tile_row, buf_row, rows):
            return pltpu.make_async_copy(y_ref.at[pl.ds(buf_row, rows)], ys_ref.at[s, pl.ds(tile_row, rows)],
                                         sem.at[s])
        return copy

    @pl.when(i == 0)
    def _():
        ys_ref[...] = jnp.zeros_like(ys_ref)
        _start_run_copies(0, gb_ref, ob_ref, nb_ref, copy_to(0))

    @pl.when(i + 1 < pl.num_programs(0))
    def _():
        _start_run_copies(i + 1, gb_ref, ob_ref, nb_ref, copy_to(1 - slot))

    _wait_run_copies(i, nb_ref, copy_to(slot))
    m0, m1 = _sort_select(qpos_ref)
    sel = jnp.logical_or(m0, m1).astype(BF16)
    y = lax.dot_general(sel, ys_ref[slot].astype(BF16), (((0,), (0,)), ((), ())), preferred_element_type=F32)
    gate = mod_ref[0, 5:6, :]
    x_new = _ln(ALPHA * x_ref[...] + gate * y, lg_ref[...], lb_ref[...])
    o_ref[...] = x_new
    if with_inproj:
        u = (x_new * (1.0 + modn_ref[0, 1:2, :]) + modn_ref[0, 0:1, :]).astype(BF16)
        p_ref[...] = jnp.dot(u, win_ref[...], preferred_element_type=F32).astype(p_ref.dtype)


def _ln2(gb, ob, nb8, x2, qpos, mod, lg, lb, ybuf, tps, nb, next_in):
    ntok, d = x2.shape
    latent_only = next_in is None
    if latent_only:
        out_map = lambda i, *_: ((i // tps) * (tps - 1) + jnp.maximum(i % tps - 1, 0), 0)
        out_rows = ntok - nb * TM
    else:
        out_map = lambda i, *_: (i, 0)
        out_rows = ntok
    in_specs = [pl.BlockSpec((TM, d), lambda i, *_: (i, 0)),
                pl.BlockSpec((1, TOP_K, TM), lambda i, *_: (i, 0, 0)),
                pl.BlockSpec((1, 6, d), lambda i, *_: (_mod_row(i, tps, nb), 0, 0)),
                pl.BlockSpec((1, d), lambda i, *_: (0, 0)),
                pl.BlockSpec((1, d), lambda i, *_: (0, 0)),
                pl.BlockSpec(memory_space=pl.ANY)]
    out_specs = [pl.BlockSpec((TM, d), out_map)]
    out_shape = [jax.ShapeDtypeStruct((out_rows, d), F32)]
    args = [gb, ob, nb8, x2, qpos, mod, lg, lb, ybuf]
    if next_in is not None:
        mod_next, w_next = next_in
        nw = w_next.shape[1]
        in_specs += [pl.BlockSpec((1, 6, d), lambda i, *_: (_mod_row(i, tps, nb), 0, 0)),
                     pl.BlockSpec((d, nw), lambda i, *_: (0, 0))]
        out_specs.append(pl.BlockSpec((TM, nw), lambda i, *_: (i, 0)))
        out_shape.append(jax.ShapeDtypeStruct((ntok, nw), BF16))
        args += [mod_next, w_next]
    grid_spec = pltpu.PrefetchScalarGridSpec(
        num_scalar_prefetch=3,
        grid=(ntok // TM,),
        in_specs=in_specs,
        out_specs=out_specs,
        scratch_shapes=[pltpu.VMEM((2, SORT_ROWS, d), F32), pltpu.SemaphoreType.DMA((2,))],
    )
    outs = pl.pallas_call(
        functools.partial(_ln2_kernel, with_inproj=next_in is not None),
        grid_spec=grid_spec,
        out_shape=out_shape,
        compiler_params=_cparams(("arbitrary",)),
        name="post_moe_ln",
    )(*args)
    return (outs[0], outs[1]) if next_in is not None else (outs[0], None)


def _even_w_in(w):
    o = np.cumsum((0, GLA_QK, GLA_QK, GLA_V, GLA_V, GLA_RANK, GLA_RANK, GQA_Q, GQA_KV, GQA_KV))
    q, k, v, r = w[:, o[0]:o[1]], w[:, o[1]:o[2]], w[:, o[2]:o[3]], w[:, o[3]:o[4]]
    lr = w[:, o[4]:o[6]]
    gq, gk, gv = w[:, o[6]:o[7]], w[:, o[7]:o[8]], w[:, o[8]:o[9]]
    d = w.shape[0]
    zeros = lambda n: jnp.zeros((d, n), w.dtype)
    out = jnp.concatenate([gq, gk, gv, q, k, lr, zeros(LANES - 2 * GLA_RANK), zeros(LANES), v, r], axis=1)
    assert out.shape[1] == EVEN_W
    return out.astype(BF16)


def _odd_w_in(w):
    o = np.cumsum((0, SSD_INNER, SSD_INNER, SSD_BC, SSD_BC, SSD_HEADS, SSD_HEADS, FNET_W))
    d = w.shape[0]
    out = jnp.concatenate([w[:, o[1]:o[4]], w[:, :o[1]], w[:, o[4]:o[6]],
                           jnp.zeros((d, LANES - 2 * SSD_HEADS), w.dtype), w[:, o[6]:o[7]]], axis=1)
    assert out.shape[1] == ODD_W
    return out.astype(BF16)


def _rope_tables(ctx_len, seq):
    rows = seq // GRID_W
    r = jnp.repeat(jnp.arange(rows, dtype=F32), GRID_W)
    col = jnp.tile(jnp.arange(GRID_W, dtype=F32), rows)
    half = GQA_HD // 2
    inv = ROPE_THETA ** (-jnp.arange(0, half, 2, dtype=F32) / half)
    ar = r[:, None] * inv
    ac = col[:, None] * inv
    ang = jnp.concatenate([ar, ar, ac, ac], -1)
    cos = jnp.concatenate([jnp.ones((ctx_len, GQA_HD), F32), jnp.cos(ang)], 0)
    sin = jnp.concatenate([jnp.zeros((ctx_len, GQA_HD), F32), jnp.sin(ang)], 0)
    return jnp.tile(cos, (1, 2)), jnp.tile(sin, (1, 2))


def _even_layer(x2, p2, mod, nb, t, tps, ctx_len, w_o, w_dec, b_dec, gla_g, qn_g, kn_g, cos2, sin2, lg, lb, rw_t, rb):
    p3 = p2.reshape(nb, t, EVEN_W)
    wd_pads = [jnp.zeros((LANES, GLA_QK), F32).at[di * GLA_RANK:(di + 1) * GLA_RANK].set(w_dec[di])
               for di in range(2)]
    outs = _gla_scan(p3, wd_pads, [b_dec[di][None, :] for di in range(2)])
    qh2, kg2, vt4 = _qkprep(p2, cos2, sin2, jnp.tile(qn_g, 2)[None, :], jnp.tile(kn_g, 2)[None, :], tps)
    att = _attention(qh2.reshape(nb, t, GQA_HEADS * LANES), kg2.reshape(nb, t, GQA_KV_HEADS * LANES), vt4, ctx_len)
    w1, w2 = w_o[:GLA_V].astype(BF16), w_o[GLA_V:].astype(BF16)
    ntok = nb * t
    return _evout(outs[0].reshape(ntok, GLA_V), outs[1].reshape(ntok, GLA_V), p2, att.reshape(ntok, GQA_Q),
                  jnp.tile(gla_g, GLA_HEADS)[None, :], w1, w2, x2, mod, lg, lb, rw_t, rb, tps, nb)


def _odd_layer(x2, p2, mod, nb, t, tps, ctx_len, w_o, conv_w, conv_b, dt_bias, a_log, d_skip, ssd_g, fnet_g,
               lg, lb, rw_t, rb):
    p3 = p2.reshape(nb, t, ODD_W)
    w8 = jnp.zeros((8, SSD_CONV_CH), F32).at[:SSD_CONV].set(conv_w)
    xbc2 = _conv(p2, w8, conv_b[None, :], tps)
    xbc3 = xbc2.reshape(nb, t, SSD_CONV_CH)
    params = []
    for di in range(2):
        lo = di * SSD_HEADS
        pad = (lo, LANES - lo - SSD_HEADS)
        bias_pad = jnp.pad(dt_bias[di], pad)[None, :]
        a_pad = jnp.pad(-jnp.exp(a_log[di]), pad)[None, :]
        eexp = np.zeros((LANES, SSD_INNER), np.float32)
        eexp[lo + np.arange(SSD_INNER) // SSD_HD, np.arange(SSD_INNER)] = 1.0
        esel = np.zeros((16, LANES), np.float32)
        esel[np.arange(SSD_HEADS), lo + np.arange(SSD_HEADS)] = 1.0
        params.append((bias_pad, a_pad, jnp.asarray(eexp), jnp.asarray(esel)))
    ys = _ssd_scan(xbc3, p3, params)
    f3 = p3[:, :, ODD_W - FNET_W:]
    g256 = fnet_g[None, :]
    fmix = jnp.concatenate([_fourier_ctx(f3[:, :ctx_len], g256), _fourier_latent(f3[:, ctx_len:], g256)], axis=1)
    ntok = nb * t
    return _odout(ys[0].reshape(ntok, SSD_INNER), ys[1].reshape(ntok, SSD_INNER), xbc2, p2,
                  fmix.reshape(ntok, FNET_W), jnp.repeat(d_skip, SSD_HD)[None, :], ssd_g[None, :],
                  w_o[:SSD_INNER].astype(BF16), w_o[SSD_INNER:].astype(BF16), x2, mod, lg, lb, rw_t, rb, tps, nb)


def _moe_layer(routed, mod, nb, tps, layer, wg, wu, wd, lg, lb, next_in):
    x2, v, idx_t, gw_t, cnt = routed
    ntok, d = x2.shape
    nt = ntok // TM
    rows = cnt[:, 0].astype(jnp.int32)
    padded = (rows + FFN_ROWS - 1) // FFN_ROWS * FFN_ROWS
    pad_end = jnp.cumsum(padded)
    pad_start = pad_end - padded
    nblk = -(-(ntok * TOP_K + nt * N_EXPERTS * (ROWG - 1)) // FFN_ROWS) + N_EXPERTS
    blk_row = jnp.arange(nblk, dtype=jnp.int32)[:, None] * FFN_ROWS
    block_e = jnp.minimum(jnp.sum((blk_row >= pad_end[None, :]).astype(jnp.int32), axis=1), N_EXPERTS - 1)
    n_used = (pad_end[-1:] // FFN_ROWS).astype(jnp.int32)
    start = jnp.broadcast_to(pad_start.astype(F32)[:, None], (N_EXPERTS, LANES))
    qpos, meta = _meta(idx_t, start)
    gb, ob, nb8 = (meta[:, :, j].reshape(nt * N_EXPERTS) for j in range(3))
    tail_s = ((pad_start + rows) // ROWG).astype(jnp.int32)
    tail_n = ((padded - rows) // ROWG).astype(jnp.int32)
    buf = _scatter(gb, ob, nb8, tail_s, tail_n, n_used, v, qpos, gw_t, nblk * FFN_ROWS)
    y = _moe_ffn(block_e, n_used, buf, layer, wg, wu, wd)
    return _ln2(gb, ob, nb8, x2, qpos, mod, lg, lb, y, tps, nb, next_in)


def kernel(x, c, ctx, c_ctx, ada_w, ada_b, ln_g, ln_b, ev_w_in, ev_w_o, gla_w_decay, gla_b_decay, gla_norm_g,
           gqa_q_norm_g, gqa_k_norm_g, od_w_in, od_w_o, ssd_conv_w, ssd_conv_b, ssd_dt_bias, ssd_a_log, ssd_d,
           ssd_norm_g, fnet_norm_g, router_w, router_b, exp_w_gate, exp_w_up, exp_w_down):
    nb, seq, d = x.shape
    ctx_len = ctx.shape[1]
    assert ctx_len == TM and seq % TM == 0 and d == D_MODEL and nb <= 7
    t = ctx_len + seq
    tps = t // TM
    ntok = nb * t
    cc = jnp.zeros((8, d), F32).at[:nb].set(c).at[nb].set(c_ctx)
    mod_all = _ada_all(cc, ada_w, ada_b).reshape(DEPTH, 8, 6, d)
    cos2, sin2 = _rope_tables(ctx_len, seq)
    rw_t = router_w.T
    rb = jnp.broadcast_to(router_b[:, None], (N_EXPERTS, LANES))
    x2 = jnp.concatenate([ctx, x], axis=1).reshape(ntok, d)

    def w_in_of(layer):
        return _even_w_in(ev_w_in[layer // 2]) if layer % 2 == 0 else _odd_w_in(od_w_in[layer // 2])

    p2 = _inproj(x2, mod_all[0], w_in_of(0), tps, nb)
    for layer in range(DEPTH):
        mod = mod_all[layer]
        i = layer // 2
        lg0, lb0 = ln_g[layer, 0][None, :], ln_b[layer, 0][None, :]
        lg1, lb1 = ln_g[layer, 1][None, :], ln_b[layer, 1][None, :]
        if layer % 2 == 0:
            routed = _even_layer(x2, p2, mod, nb, t, tps, ctx_len, ev_w_o[i], gla_w_decay[i], gla_b_decay[i],
                                 gla_norm_g[i], gqa_q_norm_g[i], gqa_k_norm_g[i], cos2, sin2, lg0, lb0, rw_t, rb)
        else:
            routed = _odd_layer(x2, p2, mod, nb, t, tps, ctx_len, od_w_o[i], ssd_conv_w[i], ssd_conv_b[i],
                                ssd_dt_bias[i], ssd_a_log[i], ssd_d[i], ssd_norm_g[i], fnet_norm_g[i], lg0, lb0,
                                rw_t, rb)
        next_in = (mod_all[layer + 1], w_in_of(layer + 1)) if layer + 1 < DEPTH else None
        x2, p2 = _moe_layer(routed, mod, nb, tps, layer, exp_w_gate, exp_w_up, exp_w_down, lg1, lb1, next_in)
    return x2.reshape(nb, seq, d)
```

```python
import functools
import itertools
import math

import jax
import jax.numpy as jnp
import numpy as np
from jax import lax
from jax.experimental import pallas as pl
from jax.experimental.pallas import tpu as pltpu

F32 = jnp.float32
BF16 = jnp.bfloat16
HI = lax.Precision.HIGHEST

D_MODEL = 1024
DEPTH = 4
GRID_W = 64
GLA_HEADS, GLA_DK, GLA_DV, GLA_RANK, GLA_TAU, GLA_CHUNK = 4, 64, 128, 16, 16.0, 64
GQA_HEADS, GQA_KV_HEADS, GQA_HD = 8, 2, 64
ROPE_THETA = 10000.0
SSD_HEADS, SSD_HD, SSD_GROUPS, SSD_STATE, SSD_CONV, SSD_CHUNK = 12, 64, 2, 128, 5, 64
SSD_HPG = SSD_HEADS // SSD_GROUPS
SSD_INNER = SSD_HEADS * SSD_HD
SSD_BC = SSD_GROUPS * SSD_STATE
SSD_CONV_CH = SSD_INNER + 2 * SSD_BC
FNET_GROUPS, FNET_GC = 4, 64
FNET_W = FNET_GROUPS * FNET_GC
N_EXPERTS, N_EXPERT_GROUPS, TOP_K, D_EXPERT = 16, 4, 2, 768
EXPERTS_PER_GROUP = N_EXPERTS // N_EXPERT_GROUPS
GLA_QK = GLA_HEADS * GLA_DK
GLA_V = GLA_HEADS * GLA_DV
GQA_Q = GQA_HEADS * GQA_HD
GQA_KV = GQA_KV_HEADS * GQA_HD
EPS = 1e-6
ALPHA = (2.0 * DEPTH) ** 0.25

LANES = 128
TM = 256
EVEN_W = 2560
ODD_W = 2432
VMEM_LIMIT = 48 * 1024 * 1024
LOG2E = 1.4426950408889634
ATT_TQ = 256
ATT_HEADROOM = 64.0
ATT_MIN_DENOM = 2.0 ** -60
ATT_VROWS = LANES
ROWG = 8
SORT_ROWS = TOP_K * TM + N_EXPERTS * (ROWG - 1)
ROW_STREAM = 128
COPY_SHIFT = 2
COPY_GROUPS = 1 << COPY_SHIFT
FFN_ROWS = 512
FFN_VMEM_LIMIT = 56 * 1024 * 1024


def _cparams(sem):
    return pltpu.CompilerParams(dimension_semantics=sem, vmem_limit_bytes=VMEM_LIMIT)


def _silu(x):
    return x * (1.0 / (1.0 + jnp.exp(-x)))


def _softplus(x):
    return jnp.maximum(x, 0.0) + jnp.log1p(jnp.exp(-jnp.abs(x)))


def _mod_row(i, tiles_per_seq, n_batch):
    return jnp.where(i % tiles_per_seq == 0, n_batch, i // tiles_per_seq)


def _ada_kernel(c_ref, w_ref, b_ref, o_ref):
    s = _silu(c_ref[...])
    o_ref[0] = jnp.dot(s, w_ref[0], precision=HI, preferred_element_type=F32) + b_ref[0]


def _ada_all(cc, ada_w, ada_b):
    depth, d, n = ada_w.shape
    tn = 1536
    return pl.pallas_call(
        _ada_kernel,
        grid=(depth, n // tn),
        in_specs=[pl.BlockSpec((8, d), lambda l, j: (0, 0)),
                  pl.BlockSpec((1, d, tn), lambda l, j: (l, 0, j)),
                  pl.BlockSpec((1, 1, tn), lambda l, j: (l, 0, j))],
        out_specs=pl.BlockSpec((1, 8, tn), lambda l, j: (l, 0, j)),
        out_shape=jax.ShapeDtypeStruct((depth, 8, n), F32),
        compiler_params=_cparams(("arbitrary", "arbitrary")),
        name="adaln",
    )(cc, ada_w, ada_b.reshape(depth, 1, n))


def _inproj_kernel(x_ref, mod_ref, w_ref, o_ref):
    shift = mod_ref[0, 0:1, :]
    scale = mod_ref[0, 1:2, :]
    u = (x_ref[...] * (1.0 + scale) + shift).astype(BF16)
    o_ref[...] = jnp.dot(u, w_ref[...], preferred_element_type=F32).astype(o_ref.dtype)


def _inproj(x2, mod, w, tps, nb):
    ntok, d = x2.shape
    nw = w.shape[1]
    return pl.pallas_call(
        _inproj_kernel,
        grid=(ntok // TM,),
        in_specs=[pl.BlockSpec((TM, d), lambda i: (i, 0)),
                  pl.BlockSpec((1, 6, d), lambda i: (_mod_row(i, tps, nb), 0, 0)),
                  pl.BlockSpec((d, nw), lambda i: (0, 0))],
        out_specs=pl.BlockSpec((TM, nw), lambda i: (i, 0)),
        out_shape=jax.ShapeDtypeStruct((ntok, nw), BF16),
        compiler_params=_cparams(("arbitrary",)),
        name="inproj",
    )(x2, mod, w)


def _scan_block(i, nblk, reverse):
    if not reverse:
        return i
    return jnp.where(i == 0, 0, nblk - i)


def _tri(n, reverse):
    r = lax.broadcasted_iota(jnp.int32, (n, n), 0)
    c = lax.broadcasted_iota(jnp.int32, (n, n), 1)
    return (c >= r) if reverse else (c <= r)


def _chunk_sum_mat(chunk, reverse):
    l = np.arange(TM)[:, None]
    m = np.arange(TM)[None, :]
    same = (l // chunk) == (m // chunk)
    return (same & ((m >= l) if reverse else (m <= l))).astype(np.float32)


def _split3(x):
    hi = x.astype(BF16)
    r1 = x - hi.astype(F32)
    mid = r1.astype(BF16)
    return hi, mid, (r1 - mid.astype(F32)).astype(BF16)


def _chunk_rows(x, chunk, idx):
    return jnp.concatenate([jnp.broadcast_to(x[c * chunk + idx:c * chunk + idx + 1, :], (chunk, x.shape[1]))
                            for c in range(x.shape[0] // chunk)], axis=0)


def _gla_dir(q_ref, k_ref, v_ref, lr_ref, wd_ref, bd_ref, cm_ref, o_ref, s_ref, *, reverse):
    L = GLA_CHUNK
    nchunk = TM // L
    nt = (((1,), (1,)), ((), ()))
    tn = (((0,), (0,)), ((), ()))
    end = 0 if reverse else L - 1
    mid = L // 2 if reverse else L // 2 - 1

    @pl.when(pl.program_id(1) == 0)
    def _():
        s_ref[...] = jnp.zeros_like(s_ref)

    q = q_ref[...].astype(F32) * (GLA_DK ** -0.5)
    k = k_ref[...].astype(F32)
    v = v_ref[...]
    lr = lr_ref[...]
    z = bd_ref[...] + sum(jnp.dot(lr, wd_ref[j], preferred_element_type=F32) for j in range(3))
    yield
    la = -_softplus(-z) * (1.0 / GLA_TAU)
    cm = cm_ref[...]
    b = sum(jnp.dot(cm, part, preferred_element_type=F32) for part in _split3(la))
    yield
    bmid = _chunk_rows(b, L, mid)
    bend = _chunk_rows(b, L, end)
    qs = q * jnp.exp(b - bmid)
    ks = (k * jnp.exp(bmid - b)).astype(BF16)
    qi = q * jnp.exp(b)
    kend = (k * jnp.exp(bend - b)).astype(BF16)
    r = lax.broadcasted_iota(jnp.int32, (TM, TM), 0)
    c = lax.broadcasted_iota(jnp.int32, (TM, TM), 1)
    mask = jnp.logical_and(r // L == c // L, (c >= r) if reverse else (c <= r))
    lane = lax.broadcasted_iota(jnp.int32, (TM, LANES), 1)
    heads = (lane < GLA_DK, lane >= GLA_DK)
    yield
    att = [lax.dot_general(jnp.where(heads[h], qs, 0.0).astype(BF16), ks, nt, preferred_element_type=F32)
           for h in range(2)]
    yield
    intra = []
    for h in range(2):
        att_h = jnp.where(mask, att[h], 0.0).astype(BF16)
        intra.append(jnp.dot(att_h, v[:, h * GLA_DV:(h + 1) * GLA_DV], preferred_element_type=F32))
    yield
    qi_h = [jnp.where(heads[h], qi, 0.0).astype(BF16) for h in range(2)]
    tot = jnp.concatenate([b[ch * L + end:ch * L + end + 1, :] for ch in range(nchunk)]
                          + [jnp.zeros((ROWG - nchunk, LANES), F32)], axis=0).T
    ds, dec = [], []
    for ch in range(nchunk):
        rows = slice(ch * L, (ch + 1) * L)
        ds.append(lax.dot_general(kend[rows], v[rows], tn, preferred_element_type=F32))
        dec.append(jnp.exp(jnp.broadcast_to(tot[:, ch:ch + 1], (2 * GLA_DK, 2 * GLA_DV))))
    yield
    s = s_ref[...]
    inter = [None] * nchunk
    for ch in (range(nchunk - 1, -1, -1) if reverse else range(nchunk)):
        rows = slice(ch * L, (ch + 1) * L)
        s_bf = s.astype(BF16)
        inter[ch] = jnp.concatenate(
            [jnp.dot(qi_h[h][rows], s_bf[:, h * GLA_DV:(h + 1) * GLA_DV], preferred_element_type=F32)
             for h in range(2)], axis=1)
        s = dec[ch] * s + ds[ch]
        yield
    s_ref[...] = s
    o_ref[...] = (jnp.concatenate(intra, axis=1) + jnp.concatenate(inter, axis=0)).astype(o_ref.dtype)


def _alternate(*stage_generators):
    for _ in itertools.zip_longest(*stage_generators):
        pass


GLA_PAIRS = GLA_HEADS // 2


def _gla_kernel(*refs):
    n_streams = 2 * GLA_PAIRS
    n_in = (len(refs) - 2 - n_streams) // n_streams
    outs = refs[n_streams * n_in:n_streams * n_in + 2]
    states = refs[n_streams * n_in + 2:]
    scans = []
    for s in range(n_streams):
        reverse, pair = divmod(s, GLA_PAIRS)
        o_view = outs[reverse].at[:, pl.ds(pair * 2 * GLA_DV, 2 * GLA_DV)]
        scans.append(_gla_dir(*refs[s * n_in:(s + 1) * n_in], o_view, states[s], reverse=bool(reverse)))
    _alternate(*scans)


def _gla_scan(p3, wd_pads, bds):
    nb, t, _ = p3.shape
    nblk = t // TM
    in_specs, args, out_specs = [], [], []
    for reverse in (False, True):
        blk = functools.partial(_scan_block, nblk=nblk, reverse=reverse)
        wd3 = jnp.stack(_split3(wd_pads[int(reverse)]))
        cm = jnp.asarray(_chunk_sum_mat(GLA_CHUNK, reverse), BF16)
        for p in range(GLA_PAIRS):
            in_specs += [pl.BlockSpec((None, TM, LANES), lambda b, i, blk=blk, p=p: (b, blk(i), 6 + p)),
                         pl.BlockSpec((None, TM, LANES), lambda b, i, blk=blk, p=p: (b, blk(i), 8 + p)),
                         pl.BlockSpec((None, TM, 2 * GLA_DV), lambda b, i, blk=blk, p=p: (b, blk(i), 6 + p)),
                         pl.BlockSpec((None, TM, LANES), lambda b, i, blk=blk: (b, blk(i), 10)),
                         pl.BlockSpec((3, LANES, LANES), lambda b, i, p=p: (0, 0, p)),
                         pl.BlockSpec((1, LANES), lambda b, i, p=p: (0, p)),
                         pl.BlockSpec((TM, TM), lambda b, i: (0, 0))]
            args += [p3, p3, p3, p3, wd3, bds[int(reverse)], cm]
        out_specs.append(pl.BlockSpec((None, TM, GLA_V), lambda b, i, blk=blk: (b, blk(i), 0)))
    state = pltpu.VMEM((2 * GLA_DK, 2 * GLA_DV), F32)
    return pl.pallas_call(
        _gla_kernel,
        grid=(nb, nblk),
        in_specs=in_specs,
        out_specs=out_specs,
        out_shape=[jax.ShapeDtypeStruct((nb, t, GLA_V), BF16)] * 2,
        scratch_shapes=[state] * (2 * GLA_PAIRS),
        compiler_params=_cparams(("arbitrary", "arbitrary")),
        name="gla_scan",
    )(*args)


def _seg_ones(width, seg):
    r = lax.broadcasted_iota(jnp.int32, (width, width), 0) // seg
    c = lax.broadcasted_iota(jnp.int32, (width, width), 1) // seg
    return jnp.where(r == c, 1.0 / seg, 0.0).astype(BF16)


def _seg_mean(x, bd):
    hi = x.astype(BF16)
    lo = (x - hi.astype(F32)).astype(BF16)
    return jnp.dot(hi, bd, preferred_element_type=F32) + jnp.dot(lo, bd, preferred_element_type=F32)


def _norm_rope(x, g, cos, sin, bd):
    ms = _seg_mean(x * x, bd)
    xn = x * lax.rsqrt(ms + EPS) * g
    lane = lax.broadcasted_iota(jnp.int32, xn.shape, 1)
    quarter = GQA_HD // 4
    up = pltpu.roll(xn, LANES - quarter, 1)
    dn = pltpu.roll(xn, quarter, 1)
    rot = jnp.where(lane % (2 * quarter) < quarter, -up, dn)
    return xn * cos + rot * sin


def _qkprep_kernel(q_ref, k_ref, v_ref, cos_ref, sin_ref, gq_ref, gk_ref, qo_ref, ko_ref, vo_ref):
    bd = _seg_ones(LANES, GQA_HD)
    cos = cos_ref[...]
    sin = sin_ref[...]
    lane = lax.broadcasted_iota(jnp.int32, (TM, LANES), 1)
    low = lane < GQA_HD
    extra = lane == GQA_HD
    one_hot = jnp.where(extra, 1.0, 0.0)
    k_bound = math.sqrt(GQA_HD) * jnp.max(jnp.abs(gk_ref[...]), axis=-1, keepdims=True) * 1.02
    for j in range(GQA_Q // LANES):
        x = q_ref[:, j * LANES:(j + 1) * LANES].astype(F32)
        y = _norm_rope(x, gq_ref[...], cos, sin, bd) * (GQA_HD ** -0.5 * LOG2E)
        y = y.astype(BF16).astype(F32)
        norm = jnp.sqrt(_seg_mean(y * y, bd) * GQA_HD)
        shift = ATT_HEADROOM - norm * k_bound
        lo = jnp.where(low, y, jnp.where(extra, pltpu.roll(shift, GQA_HD, 1), 0.0))
        hi = jnp.where(low, pltpu.roll(y, GQA_HD, 1), jnp.where(extra, shift, 0.0))
        qo_ref[:, (2 * j) * LANES:(2 * j + 1) * LANES] = lo.astype(BF16)
        qo_ref[:, (2 * j + 1) * LANES:(2 * j + 2) * LANES] = hi.astype(BF16)
    k = _norm_rope(k_ref[...].astype(F32), gk_ref[...], cos, sin, bd)
    ko_ref[:, 0:LANES] = jnp.where(low, k, one_hot).astype(BF16)
    ko_ref[:, LANES:2 * LANES] = jnp.where(low, pltpu.roll(k, GQA_HD, 1), one_hot).astype(BF16)
    v = v_ref[...].astype(F32)
    vo_ref[0] = jnp.where(low, v, one_hot).T[0:ATT_VROWS].astype(BF16)
    vo_ref[1] = jnp.where(low, pltpu.roll(v, GQA_HD, 1), one_hot).T[0:ATT_VROWS].astype(BF16)


def _qkprep(p2, cos2, sin2, gq2, gk2, tps):
    ntok = p2.shape[0]
    nb = ntok // (tps * TM)
    return pl.pallas_call(
        _qkprep_kernel,
        grid=(ntok // TM,),
        in_specs=[pl.BlockSpec((TM, GQA_Q), lambda i: (i, 0)),
                  pl.BlockSpec((TM, LANES), lambda i: (i, 4)),
                  pl.BlockSpec((TM, LANES), lambda i: (i, 5)),
                  pl.BlockSpec((TM, LANES), lambda i: (i % tps, 0)),
                  pl.BlockSpec((TM, LANES), lambda i: (i % tps, 0)),
                  pl.BlockSpec((1, LANES), lambda i: (0, 0)),
                  pl.BlockSpec((1, LANES), lambda i: (0, 0))],
        out_specs=[pl.BlockSpec((TM, GQA_HEADS * LANES), lambda i: (i, 0)),
                   pl.BlockSpec((TM, GQA_KV_HEADS * LANES), lambda i: (i, 0)),
                   pl.BlockSpec((None, GQA_KV_HEADS, ATT_VROWS, TM), lambda i: (i // tps, 0, 0, i % tps))],
        out_shape=[jax.ShapeDtypeStruct((ntok, GQA_HEADS * LANES), BF16),
                   jax.ShapeDtypeStruct((ntok, GQA_KV_HEADS * LANES), BF16),
                   jax.ShapeDtypeStruct((nb, GQA_KV_HEADS, ATT_VROWS, tps * TM), BF16)],
        compiler_params=_cparams(("arbitrary",)),
        name="qkprep",
    )(p2, p2, p2, cos2, sin2, gq2, gk2)


def _attn_kernel(q_ref, k_ref, vt_ref, o_ref, *, ctx_len, n_lat_chunks, ck, n_blocks):
    i = pl.program_id(2)
    rep = GQA_HEADS // GQA_KV_HEADS
    is_ctx = i < ctx_len // ATT_TQ
    q4 = jnp.concatenate([q_ref[:, r * LANES:(r + 1) * LANES] for r in range(rep)], axis=0)
    nt = (((1,), (1,)), ((), ()))

    def chunk(acc, kc, vtc):
        st = lax.dot_general(kc, q4, nt, preferred_element_type=F32)
        return acc + jnp.dot(vtc, jnp.exp2(st).astype(BF16), preferred_element_type=F32)

    def store(acc):
        out = (acc * (1.0 / acc[GQA_HD:GQA_HD + 1, :])).T
        low = lax.broadcasted_iota(jnp.int32, (ATT_TQ, LANES), 1) < GQA_HD
        hs = [out[r * ATT_TQ:(r + 1) * ATT_TQ] for r in range(rep)]
        for u in range(rep // 2):
            o_ref[:, u * LANES:(u + 1) * LANES] = jnp.where(
                low, hs[2 * u], pltpu.roll(hs[2 * u + 1], GQA_HD, 1)).astype(o_ref.dtype)

    acc = chunk(jnp.zeros((ATT_VROWS, rep * ATT_TQ), F32), k_ref[0:ctx_len, :], vt_ref[:, 0:ctx_len])

    def body(c, acc):
        off = pl.multiple_of(ctx_len + c * ck, TM)
        return chunk(acc, k_ref[pl.ds(off, ck), :], vt_ref[:, pl.ds(off, ck)])

    acc = lax.fori_loop(0, jnp.where(is_ctx, 0, n_lat_chunks), body, acc)
    healthy = jnp.min(acc[GQA_HD:GQA_HD + 1, :]) >= ATT_MIN_DENOM

    @pl.when(healthy)
    def _():
        store(acc)

    @pl.when(jnp.logical_not(healthy))
    def _():
        def online(c, carry):
            m, acc = carry
            off = pl.multiple_of(c * TM, TM)
            st = lax.dot_general(k_ref[pl.ds(off, TM), :], q4, nt, preferred_element_type=F32)
            mn = jnp.maximum(m, jnp.max(st, axis=0, keepdims=True))
            pt = jnp.exp2(st - mn).astype(BF16)
            acc = jnp.exp2(m - mn) * acc + jnp.dot(vt_ref[:, pl.ds(off, TM)], pt, preferred_element_type=F32)
            return mn, acc

        init = (jnp.full((1, rep * ATT_TQ), -1e30, F32), jnp.zeros((ATT_VROWS, rep * ATT_TQ), F32))
        _, acc2 = lax.fori_loop(0, jnp.where(is_ctx, ctx_len // TM, n_blocks), online, init)
        store(acc2)


def _attention(qh3, kg3, vt4, ctx_len):
    nb, t, _ = qh3.shape
    seq = t - ctx_len
    ck = next(c for c in (2048, 1024, 512, 256) if seq % c == 0)
    rep = GQA_HEADS // GQA_KV_HEADS
    return pl.pallas_call(
        functools.partial(_attn_kernel, ctx_len=ctx_len, n_lat_chunks=seq // ck, ck=ck, n_blocks=t // TM),
        grid=(nb, GQA_KV_HEADS, t // ATT_TQ),
        in_specs=[pl.BlockSpec((None, ATT_TQ, rep * LANES), lambda b, g, i: (b, i, g)),
                  pl.BlockSpec((None, t, LANES), lambda b, g, i: (b, 0, g)),
                  pl.BlockSpec((None, None, ATT_VROWS, t), lambda b, g, i: (b, g, 0, 0))],
        out_specs=pl.BlockSpec((None, ATT_TQ, rep * GQA_HD), lambda b, g, i: (b, i, g)),
        out_shape=jax.ShapeDtypeStruct((nb, t, GQA_Q), BF16),
        compiler_params=_cparams(("arbitrary", "arbitrary", "arbitrary")),
        name="gqa_attn",
    )(qh3, kg3, vt4)


def _ln(x, g, b):
    mu = jnp.mean(x, axis=-1, keepdims=True)
    xc = x - mu
    var = jnp.mean(xc * xc, axis=-1, keepdims=True)
    return xc * lax.rsqrt(var + EPS) * g + b


def _evout_kernel(of_ref, ob_ref, r_ref, att_ref, g_ref, w1_ref, w2_ref, x_ref, mod_ref, lg_ref, lb_ref, rw_ref,
                  rb_ref, o_ref, v_ref, idx_ref, gw_ref, cnt_ref):
    def stream(rows, counts):
        o = of_ref[rows, :].astype(F32) + ob_ref[rows, :].astype(F32)
        r = r_ref[rows, :].astype(F32)
        parts = []
        for h in range(GLA_HEADS):
            oh = o[:, h * GLA_DV:(h + 1) * GLA_DV]
            ms = jnp.mean(oh * oh, axis=-1, keepdims=True)
            parts.append(oh * lax.rsqrt(ms + EPS))
        gl = (jnp.concatenate(parts, axis=1) * g_ref[...] * _silu(r)).astype(BF16)
        yield
        y = jnp.dot(gl, w1_ref[...], preferred_element_type=F32)
        y = y + jnp.dot(att_ref[rows, :], w2_ref[...], preferred_element_type=F32)
        yield
        gate = mod_ref[0, 2:3, :]
        x_new = _ln(ALPHA * x_ref[rows, :] + gate * y, lg_ref[...], lb_ref[...])
        o_ref[rows, :] = x_new
        yield
        yield from _route_rows(x_new, rows, mod_ref, rw_ref, rb_ref, v_ref, idx_ref, gw_ref, counts)

    _alternate_row_streams(stream, cnt_ref)


def _evout(of2, ob2, p2, att2, g512, w1, w2, x2, mod, lg, lb, rw_t, rb, tps, nb):
    ntok, d = x2.shape
    row = lambda i: (i, 0)
    const = lambda i: (0, 0)
    r_in, r_out, r_shapes = _router_specs(ntok, d)
    return pl.pallas_call(
        _evout_kernel,
        grid=(ntok // TM,),
        in_specs=[pl.BlockSpec((TM, GLA_V), row), pl.BlockSpec((TM, GLA_V), row),
                  pl.BlockSpec((TM, GLA_V), lambda i: (i, 4)),
                  pl.BlockSpec((TM, GQA_Q), row),
                  pl.BlockSpec((1, GLA_V), const),
                  pl.BlockSpec((GLA_V, d), const), pl.BlockSpec((GQA_Q, d), const),
                  pl.BlockSpec((TM, d), row),
                  pl.BlockSpec((1, 6, d), lambda i: (_mod_row(i, tps, nb), 0, 0)),
                  pl.BlockSpec((1, d), const), pl.BlockSpec((1, d), const)] + r_in,
        out_specs=[pl.BlockSpec((TM, d), row)] + r_out,
        out_shape=[jax.ShapeDtypeStruct((ntok, d), F32)] + r_shapes,
        compiler_params=_cparams(("arbitrary",)),
        name="even_out",
    )(of2, ob2, p2, att2, g512, w1, w2, x2, mod, lg, lb, rw_t, rb)


def _conv_kernel(prev_ref, cur_ref, next_ref, w_ref, b_ref, o_ref, pad_ref, *, tps):
    j = pl.program_id(0) % tps
    has_prev = (j >= 2).astype(F32)
    has_next = jnp.logical_and(j >= 1, j <= tps - 2).astype(F32)
    pad_ref[0:8, :] = prev_ref[...].astype(F32) * has_prev
    pad_ref[8:8 + TM, :] = cur_ref[...].astype(F32)
    pad_ref[8 + TM:16 + TM, :] = next_ref[...].astype(F32) * has_next
    half = (SSD_CONV - 1) // 2
    cw = 256
    for c0 in range(0, SSD_CONV_CH, cw):
        cols = slice(c0, c0 + cw)
        acc = jnp.zeros((TM, cw), F32) + b_ref[:, cols]
        for tap in range(SSD_CONV):
            acc = acc + pad_ref[pl.ds(8 + tap - half, TM), cols] * w_ref[tap:tap + 1, cols]
        o_ref[:, cols] = _silu(acc).astype(o_ref.dtype)


def _conv(p2, w8, bias, tps):
    ntok = p2.shape[0]
    cw = SSD_CONV_CH
    r8 = TM // 8
    nrow8 = ntok // 8
    return pl.pallas_call(
        functools.partial(_conv_kernel, tps=tps),
        grid=(ntok // TM,),
        in_specs=[pl.BlockSpec((8, cw), lambda i: (jnp.maximum(i * r8 - 1, 0), 0)),
                  pl.BlockSpec((TM, cw), lambda i: (i, 0)),
                  pl.BlockSpec((8, cw), lambda i: (jnp.minimum((i + 1) * r8, nrow8 - 1), 0)),
                  pl.BlockSpec((8, cw), lambda i: (0, 0)),
                  pl.BlockSpec((1, cw), lambda i: (0, 0))],
        out_specs=pl.BlockSpec((TM, cw), lambda i: (i, 0)),
        out_shape=jax.ShapeDtypeStruct((ntok, SSD_CONV_CH), BF16),
        scratch_shapes=[pltpu.VMEM((TM + 16, cw), F32)],
        compiler_params=_cparams(("arbitrary",)),
        name="ssd_conv",
    )(p2, p2, p2, w8, bias)


def _ssd_dir(xbc_ref, dt_ref, bias_ref, a_ref, eexp_ref, esel_ref, cm_ref, o_ref, h_ref, *, reverse):
    L = SSD_CHUNK
    GW = SSD_HPG * SSD_HD
    nchunk = TM // L
    nt = (((1,), (1,)), ((), ()))
    tn = (((0,), (0,)), ((), ()))
    end = 0 if reverse else L - 1
    lane0 = SSD_HEADS if reverse else 0

    @pl.when(pl.program_id(1) == 0)
    def _():
        h_ref[...] = jnp.zeros_like(h_ref)

    xs = xbc_ref[:, 0:SSD_INNER].astype(F32)
    bs = xbc_ref[:, SSD_INNER:SSD_INNER + SSD_BC]
    cs = xbc_ref[:, SSD_INNER + SSD_BC:SSD_CONV_CH]
    dt = _softplus(dt_ref[...].astype(F32) + bias_ref[...])
    a = dt * a_ref[...]
    cm = cm_ref[...]
    cum = sum(jnp.dot(cm, part, preferred_element_type=F32) for part in _split3(a))
    yield
    cum_t = sum(lax.dot_general(esel_ref[...], part, nt, preferred_element_type=F32)
                for part in _split3(cum))
    cend = _chunk_rows(cum, L, end)
    eexp = eexp_ref[...]

    def expand(m):
        return jnp.dot(m.astype(BF16), eexp, preferred_element_type=F32)

    xdt = xs * expand(dt)
    xdt_b = xdt.astype(BF16)
    yield
    xend = (xdt * expand(jnp.exp(cend - cum))).astype(BF16)
    yield
    ecum = expand(jnp.exp(cum))
    tot = jnp.concatenate([cum[ch * L + end:ch * L + end + 1, :] for ch in range(nchunk)]
                          + [jnp.zeros((ROWG - nchunk, LANES), F32)], axis=0)
    edec = jnp.exp(sum(jnp.dot(part, eexp, preferred_element_type=F32) for part in _split3(tot)))

    tri = _tri(L, reverse)
    low = lax.broadcasted_iota(jnp.int32, (L, LANES), 1) < SSD_HD
    intra = [[None] * SSD_GROUPS for _ in range(nchunk)]
    dh = [[None] * SSD_GROUPS for _ in range(nchunk)]
    for ch in range(nchunk):
        rows = slice(ch * L, (ch + 1) * L)
        for g in range(SSD_GROUPS):
            bg = bs[rows, g * SSD_STATE:(g + 1) * SSD_STATE]
            cg = cs[rows, g * SSD_STATE:(g + 1) * SSD_STATE]
            cb = lax.dot_general(cg, bg, nt, preferred_element_type=F32)
            pairs = []
            for pp in range(SSD_HPG // 2):
                yh = []
                for u in range(2):
                    h = g * SSD_HPG + 2 * pp + u
                    seg = cum[rows, lane0 + h:lane0 + h + 1] - cum_t[h:h + 1, rows]
                    dec = jnp.exp(jnp.where(tri, seg, -1e30))
                    mat = (cb * dec).astype(BF16)
                    col = (h - u) * SSD_HD
                    yh.append(jnp.dot(mat, xdt_b[rows, col:col + LANES], preferred_element_type=F32))
                pairs.append(jnp.where(low, yh[0], yh[1]))
                yield
            intra[ch][g] = jnp.concatenate(pairs, axis=1)
            dh[ch][g] = lax.dot_general(bg, xend[rows, g * GW:(g + 1) * GW], tn, preferred_element_type=F32)

    hs = [h_ref[g] for g in range(SSD_GROUPS)]
    out = [None] * nchunk
    for ch in (range(nchunk - 1, -1, -1) if reverse else range(nchunk)):
        rows = slice(ch * L, (ch + 1) * L)
        ys = []
        for g in range(SSD_GROUPS):
            cg = cs[rows, g * SSD_STATE:(g + 1) * SSD_STATE]
            y_inter = jnp.dot(cg, hs[g].astype(BF16), preferred_element_type=F32) * ecum[rows, g * GW:(g + 1) * GW]
            ys.append(intra[ch][g] + y_inter)
            hs[g] = hs[g] * edec[ch:ch + 1, g * GW:(g + 1) * GW] + dh[ch][g]
        out[ch] = jnp.concatenate(ys, axis=1)
        yield
    for g in range(SSD_GROUPS):
        h_ref[g] = hs[g]
    o_ref[...] = jnp.concatenate(out, axis=0).astype(o_ref.dtype)


def _ssd_kernel(*refs):
    n_in = (len(refs) - 4) // 2
    o_f, o_b, h_f, h_b = refs[2 * n_in:]
    _alternate(_ssd_dir(*refs[:n_in], o_f, h_f, reverse=False),
               _ssd_dir(*refs[n_in:2 * n_in], o_b, h_b, reverse=True))


def _ssd_scan(xbc3, p3, params):
    nb, t, _ = xbc3.shape
    nblk = t // TM
    const = lambda b, i: (0, 0)
    in_specs, args, out_specs = [], [], []
    for reverse in (False, True):
        blk = functools.partial(_scan_block, nblk=nblk, reverse=reverse)
        bias_pad, a_pad, eexp, esel = params[int(reverse)]
        in_specs += [pl.BlockSpec((None, TM, SSD_CONV_CH), lambda b, i, blk=blk: (b, blk(i), 0)),
                     pl.BlockSpec((None, TM, LANES), lambda b, i, blk=blk: (b, blk(i), 16)),
                     pl.BlockSpec((1, LANES), const), pl.BlockSpec((1, LANES), const),
                     pl.BlockSpec((LANES, SSD_INNER), const), pl.BlockSpec((16, LANES), const),
                     pl.BlockSpec((TM, TM), const)]
        args += [xbc3, p3, bias_pad, a_pad, eexp.astype(BF16), esel.astype(BF16),
                 jnp.asarray(_chunk_sum_mat(SSD_CHUNK, reverse), BF16)]
        out_specs.append(pl.BlockSpec((None, TM, SSD_INNER), lambda b, i, blk=blk: (b, blk(i), 0)))
    state = pltpu.VMEM((SSD_GROUPS, SSD_STATE, SSD_HPG * SSD_HD), F32)
    return pl.pallas_call(
        _ssd_kernel,
        grid=(nb, nblk),
        in_specs=in_specs,
        out_specs=out_specs,
        out_shape=[jax.ShapeDtypeStruct((nb, t, SSD_INNER), BF16)] * 2,
        scratch_shapes=[state, state],
        compiler_params=_cparams(("arbitrary", "arbitrary")),
        name="ssd_scan",
    )(*args)


def _dft_mats(n):
    k = jnp.arange(n, dtype=jnp.int32)
    ang = ((k[:, None] * k[None, :]) % n).astype(F32) * (2.0 * math.pi / n)
    return jnp.cos(ang), jnp.sin(ang)


def _chan_mats():
    cc, sc = _dft_mats(FNET_GC)
    eye = jnp.eye(FNET_GROUPS, dtype=F32)
    return jnp.kron(eye, cc), jnp.kron(eye, sc)


def _group_rms(x, g, bd):
    return x * lax.rsqrt(_seg_mean(x * x, bd) + EPS) * g


def _split2(x):
    hi = x.astype(BF16)
    return hi, (x - hi.astype(F32)).astype(BF16)


def _mm(a, b):
    a_hi, a_lo = _split2(a)
    b_hi, b_lo = _split2(b)
    dot = functools.partial(jnp.dot, preferred_element_type=F32)
    return dot(a_hi, b_hi) + (dot(a_hi, b_lo) + dot(a_lo, b_hi))


def _ffta_kernel(x_ref, g_ref, cc_ref, sc_ref, c1_ref, s1_ref, twc_ref, tws_ref, yr_ref, yi_ref, *, nb2):
    bd = _seg_ones(FNET_W, FNET_GC)
    for j in range(nb2):
        x = x_ref[:, j * FNET_W:(j + 1) * FNET_W].astype(F32)
        xn = _group_rms(x, g_ref[...], bd)
        vr = _mm(xn, cc_ref[...])
        vi = -_mm(xn, sc_ref[...])
        c1 = c1_ref[...]
        s1 = s1_ref[...]
        yr = _mm(c1, vr) + _mm(s1, vi)
        yi = _mm(c1, vi) - _mm(s1, vr)
        tc = twc_ref[j]
        ts = tws_ref[j]
        yr_ref[j] = yr * tc + yi * ts
        yi_ref[j] = yi * tc - yr * ts


def _fftb_kernel(yr_ref, yi_ref, c2_ref, s2_ref, o_ref, *, scale):
    o_ref[...] = (_mm(c2_ref[...], yr_ref[...]) + _mm(s2_ref[...], yi_ref[...])) * scale


def _fftc_kernel(x_ref, g_ref, cc_ref, sc_ref, ct_ref, st_ref, o_ref, *, scale):
    bd = _seg_ones(FNET_W, FNET_GC)
    xn = _group_rms(x_ref[...].astype(F32), g_ref[...], bd)
    a = _mm(xn, cc_ref[...])
    b = _mm(xn, sc_ref[...])
    o_ref[...] = (_mm(ct_ref[...], a) - _mm(st_ref[...], b)) * scale


def _fourier_latent(f_lat, g256):
    nb, s, _ = f_lat.shape
    n2 = 64
    n1 = s // n2
    nb2 = 8
    ccm, scm = _chan_mats()
    c1, s1 = _dft_mats(n1)
    c2, s2 = _dft_mats(n2)
    t2 = jnp.arange(n2, dtype=jnp.int32)[:, None]
    k1 = jnp.arange(n1, dtype=jnp.int32)[None, :]
    tw = ((t2 * k1) % s).astype(F32) * (2.0 * math.pi / s)
    twc = jnp.cos(tw)[:, :, None]
    tws = jnp.sin(tw)[:, :, None]
    x2 = f_lat.reshape(nb, n1, n2 * FNET_W)
    const2 = lambda b, j: (0, 0)
    yr, yi = pl.pallas_call(
        functools.partial(_ffta_kernel, nb2=nb2),
        grid=(nb, n2 // nb2),
        in_specs=[pl.BlockSpec((None, n1, nb2 * FNET_W), lambda b, j: (b, 0, j)),
                  pl.BlockSpec((1, FNET_W), const2),
                  pl.BlockSpec((FNET_W, FNET_W), const2), pl.BlockSpec((FNET_W, FNET_W), const2),
                  pl.BlockSpec((n1, n1), const2), pl.BlockSpec((n1, n1), const2),
                  pl.BlockSpec((nb2, n1, 1), lambda b, j: (j, 0, 0)),
                  pl.BlockSpec((nb2, n1, 1), lambda b, j: (j, 0, 0))],
        out_specs=[pl.BlockSpec((None, nb2, n1, FNET_W), lambda b, j: (b, j, 0, 0)),
                   pl.BlockSpec((None, nb2, n1, FNET_W), lambda b, j: (b, j, 0, 0))],
        out_shape=[jax.ShapeDtypeStruct((nb, n2, n1, FNET_W), F32),
                   jax.ShapeDtypeStruct((nb, n2, n1, FNET_W), F32)],
        compiler_params=_cparams(("arbitrary", "arbitrary")),
        name="fft_a",
    )(x2, g256, ccm, scm, c1, s1, twc, tws)
    ncol = n1 * FNET_W
    tn = min(8192, ncol)
    out = pl.pallas_call(
        functools.partial(_fftb_kernel, scale=1.0 / math.sqrt(s * FNET_GC)),
        grid=(nb, ncol // tn),
        in_specs=[pl.BlockSpec((None, n2, tn), lambda b, j: (b, 0, j)),
                  pl.BlockSpec((None, n2, tn), lambda b, j: (b, 0, j)),
                  pl.BlockSpec((n2, n2), const2), pl.BlockSpec((n2, n2), const2)],
        out_specs=pl.BlockSpec((None, n2, tn), lambda b, j: (b, 0, j)),
        out_shape=jax.ShapeDtypeStruct((nb, n2, ncol), F32),
        compiler_params=_cparams(("arbitrary", "arbitrary")),
        name="fft_b",
    )(yr.reshape(nb, n2, ncol), yi.reshape(nb, n2, ncol), c2, s2)
    return out.reshape(nb, s, FNET_W)


def _fourier_ctx(f_ctx, g256):
    nb, tc, _ = f_ctx.shape
    ccm, scm = _chan_mats()
    ct, st = _dft_mats(tc)
    const = lambda b: (0, 0)
    return pl.pallas_call(
        functools.partial(_fftc_kernel, scale=1.0 / math.sqrt(tc * FNET_GC)),
        grid=(nb,),
        in_specs=[pl.BlockSpec((None, tc, FNET_W), lambda b: (b, 0, 0)),
                  pl.BlockSpec((1, FNET_W), const),
                  pl.BlockSpec((FNET_W, FNET_W), const), pl.BlockSpec((FNET_W, FNET_W), const),
                  pl.BlockSpec((tc, tc), const), pl.BlockSpec((tc, tc), const)],
        out_specs=pl.BlockSpec((None, tc, FNET_W), lambda b: (b, 0, 0)),
        out_shape=jax.ShapeDtypeStruct((nb, tc, FNET_W), F32),
        compiler_params=_cparams(("arbitrary",)),
        name="fft_ctx",
    )(f_ctx, g256, ccm, scm, ct, st)


def _odout_kernel(yf_ref, yb_ref, xs_ref, z0_ref, z1_ref, z2_ref, f_ref, dsk_ref, g_ref, w1_ref, w2_ref, x_ref,
                  mod_ref, lg_ref, lb_ref, rw_ref, rb_ref, o_ref, v_ref, idx_ref, gw_ref, cnt_ref):
    GW = SSD_HPG * SSD_HD

    def stream(rows, counts):
        y = yf_ref[rows, :].astype(F32) + yb_ref[rows, :].astype(F32) + xs_ref[rows, :].astype(F32) * dsk_ref[...]
        z = jnp.concatenate([z0_ref[rows, :], z1_ref[rows, :], z2_ref[rows, :]], axis=1).astype(F32)
        y = y * _silu(z)
        parts = []
        for g in range(SSD_GROUPS):
            yg = y[:, g * GW:(g + 1) * GW]
            ms = jnp.mean(yg * yg, axis=-1, keepdims=True)
            parts.append(yg * lax.rsqrt(ms + EPS))
        yn = (jnp.concatenate(parts, axis=1) * g_ref[...]).astype(BF16)
        yield
        o = jnp.dot(yn, w1_ref[...], preferred_element_type=F32)
        o = o + jnp.dot(f_ref[rows, :].astype(BF16), w2_ref[...], preferred_element_type=F32)
        yield
        gate = mod_ref[0, 2:3, :]
        x_new = _ln(ALPHA * x_ref[rows, :] + gate * o, lg_ref[...], lb_ref[...])
        o_ref[rows, :] = x_new
        yield
        yield from _route_rows(x_new, rows, mod_ref, rw_ref, rb_ref, v_ref, idx_ref, gw_ref, counts)

    _alternate_row_streams(stream, cnt_ref)


def _odout(yf2, yb2, xbc2, p2, f2, dsk, g768, w1, w2, x2, mod, lg, lb, rw_t, rb, tps, nb):
    ntok, d = x2.shape
    row = lambda i: (i, 0)
    const = lambda i: (0, 0)
    r_in, r_out, r_shapes = _router_specs(ntok, d)
    zw = 256
    return pl.pallas_call(
        _odout_kernel,
        grid=(ntok // TM,),
        in_specs=[pl.BlockSpec((TM, SSD_INNER), row), pl.BlockSpec((TM, SSD_INNER), row),
                  pl.BlockSpec((TM, SSD_INNER), row)]
        + [pl.BlockSpec((TM, zw), lambda i, j=j: (i, SSD_CONV_CH // zw + j)) for j in range(SSD_INNER // zw)]
        + [pl.BlockSpec((TM, FNET_W), row),
                  pl.BlockSpec((1, SSD_INNER), const), pl.BlockSpec((1, SSD_INNER), const),
                  pl.BlockSpec((SSD_INNER, d), const), pl.BlockSpec((FNET_W, d), const),
                  pl.BlockSpec((TM, d), row),
                  pl.BlockSpec((1, 6, d), lambda i: (_mod_row(i, tps, nb), 0, 0)),
                  pl.BlockSpec((1, d), const), pl.BlockSpec((1, d), const)] + r_in,
        out_specs=[pl.BlockSpec((TM, d), row)] + r_out,
        out_shape=[jax.ShapeDtypeStruct((ntok, d), F32)] + r_shapes,
        compiler_params=_cparams(("arbitrary",)),
        name="odd_out",
    )(yf2, yb2, xbc2, *([p2] * (SSD_INNER // zw)), f2, dsk, g768, w1, w2, x2, mod, lg, lb, rw_t, rb)


def _route_rows(x, rows, mod_ref, rw_ref, rb_ref, v_ref, idx_ref, gw_ref, counts):
    n = x.shape[0]
    shift = mod_ref[0, 3:4, :]
    scale = mod_ref[0, 4:5, :]
    v = x * (1.0 + scale) + shift
    v_hi = v.astype(BF16)
    v_ref[rows, :] = v_hi
    v_lo = (v - v_hi.astype(F32)).astype(BF16)
    w_hi, w_lo = _split2(rw_ref[...])
    nt_dot = functools.partial(lax.dot_general, dimension_numbers=(((1,), (1,)), ((), ())),
                               preferred_element_type=F32)
    logits = nt_dot(w_hi, v_hi) + (nt_dot(w_hi, v_lo) + nt_dot(w_lo, v_hi))
    yield
    s = 1.0 / (1.0 + jnp.exp(-logits))
    sel = s + rb_ref[:, 0:1]
    izero = jnp.zeros((1, n), jnp.int32)
    best = None
    for g in range(N_EXPERT_GROUPS):
        a = [sel[g * EXPERTS_PER_GROUP + j:g * EXPERTS_PER_GROUP + j + 1, :] for j in range(EXPERTS_PER_GROUP)]
        sv = [s[g * EXPERTS_PER_GROUP + j:g * EXPERTS_PER_GROUP + j + 1, :] for j in range(EXPERTS_PER_GROUP)]
        m1, i1, s1 = a[0], izero, sv[0]
        for j in range(1, EXPERTS_PER_GROUP):
            gt = a[j] > m1
            m1 = jnp.where(gt, a[j], m1)
            i1 = jnp.where(gt, j, i1)
            s1 = jnp.where(gt, sv[j], s1)
        m2 = jnp.full((1, n), -jnp.inf, F32)
        i2, s2 = izero, jnp.zeros((1, n), F32)
        for j in range(EXPERTS_PER_GROUP):
            gt = jnp.logical_and(i1 != j, a[j] > m2)
            m2 = jnp.where(gt, a[j], m2)
            i2 = jnp.where(gt, j, i2)
            s2 = jnp.where(gt, sv[j], s2)
        cand = (m1 + m2, i1 + g * EXPERTS_PER_GROUP, i2 + g * EXPERTS_PER_GROUP, s1, s2)
        if best is None:
            best = cand
        else:
            gt = cand[0] > best[0]
            best = tuple(jnp.where(gt, cn, bs) for cn, bs in zip(cand, best))
        yield
    _, e1, e2, w1, w2 = best
    tot = w1 + w2
    idx_ref[0, :, rows] = jnp.concatenate([e1, e2], axis=0)
    gw_ref[0, :, rows] = jnp.concatenate([w1 / tot, w2 / tot], axis=0)
    eio = lax.broadcasted_iota(jnp.int32, (N_EXPERTS, n), 0)
    oh = jnp.logical_or(eio == e1, eio == e2).astype(F32)
    counts.append(jnp.sum(oh, axis=1, keepdims=True))


def _alternate_row_streams(make_stream, cnt_ref):
    @pl.when(pl.program_id(0) == 0)
    def _():
        cnt_ref[...] = jnp.zeros_like(cnt_ref)

    counts = []
    _alternate(*[make_stream(slice(r, r + ROW_STREAM), counts) for r in range(0, TM, ROW_STREAM)])
    cnt_ref[...] += _ceil_rows(sum(counts))


def _router_specs(ntok, d):
    nt = ntok // TM
    in_specs = [pl.BlockSpec((N_EXPERTS, d), lambda i: (0, 0)),
                pl.BlockSpec((N_EXPERTS, LANES), lambda i: (0, 0))]
    out_specs = [pl.BlockSpec((TM, d), lambda i: (i, 0)),
                 pl.BlockSpec((1, TOP_K, TM), lambda i: (i, 0, 0)),
                 pl.BlockSpec((1, TOP_K, TM), lambda i: (i, 0, 0)),
                 pl.BlockSpec((N_EXPERTS, LANES), lambda i: (0, 0))]
    out_shapes = [jax.ShapeDtypeStruct((ntok, d), BF16),
                  jax.ShapeDtypeStruct((nt, TOP_K, TM), jnp.int32),
                  jax.ShapeDtypeStruct((nt, TOP_K, TM), F32),
                  jax.ShapeDtypeStruct((N_EXPERTS, LANES), F32)]
    return in_specs, out_specs, out_shapes


def _ceil_rows(c):
    return jnp.ceil(c * (1.0 / ROWG)) * ROWG


def _meta_kernel(idx_ref, start_ref, qpos_ref, meta_ref, run_ref):
    @pl.when(pl.program_id(0) == 0)
    def _():
        run_ref[...] = jnp.zeros_like(run_ref)

    eio = lax.broadcasted_iota(jnp.int32, (N_EXPERTS, TM), 0)
    r = lax.broadcasted_iota(jnp.int32, (TM, TM), 0)
    c = lax.broadcasted_iota(jnp.int32, (TM, TM), 1)
    before = (r < c).astype(BF16)
    er = lax.broadcasted_iota(jnp.int32, (N_EXPERTS, N_EXPERTS), 0)
    ec = lax.broadcasted_iota(jnp.int32, (N_EXPERTS, N_EXPERTS), 1)
    lower = (ec < er).astype(F32)
    lane = lax.broadcasted_iota(jnp.int32, (N_EXPERTS, LANES), 1)
    run = run_ref[...]
    for g in range(idx_ref.shape[0]):
        oh1 = eio == idx_ref[g, 0:1, :]
        oh2 = eio == idx_ref[g, 1:2, :]
        oh = jnp.logical_or(oh1, oh2)
        rank = jnp.dot(oh.astype(BF16), before, preferred_element_type=F32)
        run_len = jnp.broadcast_to(_ceil_rows(jnp.sum(oh.astype(F32), axis=1, keepdims=True)), (N_EXPERTS, LANES))
        off = jnp.dot(lower, run_len, precision=HI, preferred_element_type=F32)
        pos = rank + off[:, 0:1]
        q1 = jnp.sum(jnp.where(oh1, pos, 0.0), axis=0, keepdims=True)
        q2 = jnp.sum(jnp.where(oh2, pos, 0.0), axis=0, keepdims=True)
        qpos_ref[g] = jnp.concatenate([q1, q2], axis=0).astype(jnp.int32)
        meta = jnp.where(lane == 0, start_ref[...] + run, jnp.where(lane == 1, off, run_len))
        meta_ref[g] = (meta * (1.0 / ROWG)).astype(jnp.int32)
        run = run + run_len
    run_ref[...] = run


def _meta(idx_t, start):
    nt = idx_t.shape[0]
    grp = next(g for g in (4, 3, 2, 1) if nt % g == 0)
    return pl.pallas_call(
        _meta_kernel,
        grid=(nt // grp,),
        in_specs=[pl.BlockSpec((grp, TOP_K, TM), lambda i: (i, 0, 0)),
                  pl.BlockSpec((N_EXPERTS, LANES), lambda i: (0, 0))],
        out_specs=[pl.BlockSpec((grp, TOP_K, TM), lambda i: (i, 0, 0)),
                   pl.BlockSpec((grp, N_EXPERTS, LANES), lambda i: (i, 0, 0))],
        out_shape=[jax.ShapeDtypeStruct((nt, TOP_K, TM), jnp.int32),
                   jax.ShapeDtypeStruct((nt, N_EXPERTS, LANES), jnp.int32)],
        scratch_shapes=[pltpu.VMEM((N_EXPERTS, LANES), F32)],
        compiler_params=_cparams(("arbitrary",)),
        name="moe_meta",
    )(idx_t, start)


def _sort_select(qpos_ref):
    r = lax.broadcasted_iota(jnp.int32, (SORT_ROWS, TM), 0)
    return r == qpos_ref[0, 0:1, :], r == qpos_ref[0, 1:2, :]


def _start_run_copies(i, gb_ref, ob_ref, nb_ref, make_copy):
    big = COPY_GROUPS * ROWG
    for e in range(N_EXPERTS):
        n = nb_ref[i * N_EXPERTS + e]
        buf_row = gb_ref[i * N_EXPERTS + e] * ROWG
        tile_row = ob_ref[i * N_EXPERTS + e] * ROWG
        n_big = lax.shift_right_logical(n, COPY_SHIFT)

        def body_big(k, carry, buf_row=buf_row, tile_row=tile_row):
            make_copy(pl.multiple_of(tile_row + k * big, ROWG), pl.multiple_of(buf_row + k * big, ROWG), big).start()
            return carry

        def body_one(k, carry, buf_row=buf_row, tile_row=tile_row):
            make_copy(pl.multiple_of(tile_row + k * ROWG, ROWG), pl.multiple_of(buf_row + k * ROWG, ROWG),
                      ROWG).start()
            return carry

        lax.fori_loop(0, n_big, body_big, 0)
        lax.fori_loop(lax.shift_left(n_big, COPY_SHIFT), n, body_one, 0)


def _wait_run_copies(i, nb_ref, make_copy):
    n_big = 0
    n_one = 0
    for e in range(N_EXPERTS):
        n = nb_ref[i * N_EXPERTS + e]
        n_big = n_big + lax.shift_right_logical(n, COPY_SHIFT)
        n_one = n_one + lax.bitwise_and(n, COPY_GROUPS - 1)

    def wait_big(k, carry):
        make_copy(0, 0, COPY_GROUPS * ROWG).wait()
        return carry

    def wait_one(k, carry):
        make_copy(0, 0, ROWG).wait()
        return carry

    lax.fori_loop(0, n_big, wait_big, 0)
    lax.fori_loop(0, n_one, wait_one, 0)


def _scatter_kernel(gb_ref, ob_ref, nb_ref, ts_ref, tn_ref, nu_ref, v_ref, qpos_ref, gw_ref, buf_ref, xs_ref, z_ref,
                    sem, zsem):
    i = pl.program_id(0)
    last = pl.num_programs(0) - 1
    slot = i % 2
    d = v_ref.shape[1]
    m0, m1 = _sort_select(qpos_ref)
    sel = jnp.logical_or(m0, m1).astype(BF16)
    xs_ref[slot, :, 0:d] = jnp.dot(sel, v_ref[...], preferred_element_type=F32)
    gate = jnp.sum(jnp.where(m0, gw_ref[0, 0:1, :], 0.0) + jnp.where(m1, gw_ref[0, 1:2, :], 0.0),
                   axis=1, keepdims=True)
    xs_ref[slot, :, d:d + LANES] = jnp.broadcast_to(gate, (SORT_ROWS, LANES))

    def copy_from(s):
        def copy(tile_row, buf_row, rows):
            return pltpu.make_async_copy(xs_ref.at[s, pl.ds(tile_row, rows)], buf_ref.at[pl.ds(buf_row, rows)],
                                         sem.at[s])
        return copy

    _start_run_copies(i, gb_ref, ob_ref, nb_ref, copy_from(slot))

    @pl.when(i > 0)
    def _():
        _wait_run_copies(i - 1, nb_ref, copy_from(1 - slot))

    @pl.when(i == last)
    def _():
        _wait_run_copies(i, nb_ref, copy_from(slot))

    @pl.when(i == last)
    def _():
        z_ref[...] = jnp.zeros_like(z_ref)

        def zcopy(buf_row):
            return pltpu.make_async_copy(z_ref.at[pl.ds(0, ROWG)], buf_ref.at[pl.ds(buf_row, ROWG)], zsem)

        total = 0
        for e in range(N_EXPERTS):
            n = tn_ref[e]
            row0 = ts_ref[e] * ROWG

            def body(k, carry, row0=row0):
                zcopy(pl.multiple_of(row0 + k * ROWG, ROWG)).start()
                return carry

            lax.fori_loop(0, n, body, 0)
            total = total + n

        def wbody(k, carry):
            zcopy(0).wait()
            return carry

        lax.fori_loop(0, total, wbody, 0)

        def zblock(blk):
            return pltpu.make_async_copy(z_ref, buf_ref.at[pl.ds(pl.multiple_of(blk * FFN_ROWS, FFN_ROWS), FFN_ROWS)],
                                         zsem)

        def bbody(blk, carry):
            zblock(blk).start()
            return carry

        def bwait(blk, carry):
            zblock(blk).wait()
            return carry

        nblk = buf_ref.shape[0] // FFN_ROWS
        lax.fori_loop(nu_ref[0], nblk, bbody, 0)
        lax.fori_loop(nu_ref[0], nblk, bwait, 0)


def _scatter(gb, ob, nb8, tail_s, tail_n, n_used, v, qpos, gw_t, nrow):
    ntok, d = v.shape
    grid_spec = pltpu.PrefetchScalarGridSpec(
        num_scalar_prefetch=6,
        grid=(ntok // TM,),
        in_specs=[pl.BlockSpec((TM, d), lambda i, *_: (i, 0)),
                  pl.BlockSpec((1, TOP_K, TM), lambda i, *_: (i, 0, 0)),
                  pl.BlockSpec((1, TOP_K, TM), lambda i, *_: (i, 0, 0))],
        out_specs=pl.BlockSpec(memory_space=pl.ANY),
        scratch_shapes=[pltpu.VMEM((2, SORT_ROWS, d + LANES), F32), pltpu.VMEM((FFN_ROWS, d + LANES), F32),
                        pltpu.SemaphoreType.DMA((2,)), pltpu.SemaphoreType.DMA(())],
    )
    return pl.pallas_call(
        _scatter_kernel,
        grid_spec=grid_spec,
        out_shape=jax.ShapeDtypeStruct((nrow, d + LANES), F32),
        compiler_params=_cparams(("arbitrary",)),
        name="moe_scatter",
    )(gb, ob, nb8, tail_s, tail_n, n_used, v, qpos, gw_t)


def _moe_kernel(be_ref, nu_ref, x_ref, wg_ref, wu_ref, wd_ref, o_ref, wg_bf, wu_bf, wd_bf):
    i = pl.program_id(0)
    d = o_ref.shape[1]

    @pl.when(jnp.logical_or(i == 0, be_ref[i] != be_ref[jnp.maximum(i - 1, 0)]))
    def _():
        wg_bf[...] = wg_ref[...].astype(BF16)
        wu_bf[...] = wu_ref[...].astype(BF16)
        wd_bf[...] = wd_ref[...].astype(BF16)

    @pl.when(i < nu_ref[0])
    def _():
        xb = x_ref[...]
        x = xb[:, 0:d].astype(BF16)
        g = jnp.dot(x, wg_bf[...], preferred_element_type=F32)
        u = jnp.dot(x, wu_bf[...], preferred_element_type=F32)
        h = (_silu(g) * u).astype(BF16)
        y = jnp.dot(h, wd_bf[...], preferred_element_type=F32)
        o_ref[...] = y * xb[:, d:d + 1]

    @pl.when(pl.program_id(0) >= nu_ref[0])
    def _():
        o_ref[...] = jnp.zeros_like(o_ref)


def _moe_ffn(block_e, n_used, buf, layer, wg, wu, wd):
    nrow, dw = buf.shape
    d = dw - LANES
    nblk = nrow // FFN_ROWS
    de = wg.shape[3]
    used = lambda i, nu: jnp.minimum(i, nu[0] - 1)
    grid_spec = pltpu.PrefetchScalarGridSpec(
        num_scalar_prefetch=2,
        grid=(nblk,),
        in_specs=[pl.BlockSpec((FFN_ROWS, dw), lambda i, be, nu: (used(i, nu), 0)),
                  pl.BlockSpec((None, None, d, de), lambda i, be, nu: (layer, be[i], 0, 0)),
                  pl.BlockSpec((None, None, d, de), lambda i, be, nu: (layer, be[i], 0, 0)),
                  pl.BlockSpec((None, None, de, d), lambda i, be, nu: (layer, be[i], 0, 0))],
        out_specs=pl.BlockSpec((FFN_ROWS, d), lambda i, be, nu: (i, 0)),
        scratch_shapes=[pltpu.VMEM((d, de), BF16), pltpu.VMEM((d, de), BF16), pltpu.VMEM((de, d), BF16)],
    )
    return pl.pallas_call(
        _moe_kernel,
        grid_spec=grid_spec,
        out_shape=jax.ShapeDtypeStruct((nrow, d), F32),
        compiler_params=pltpu.CompilerParams(dimension_semantics=("arbitrary",), vmem_limit_bytes=FFN_VMEM_LIMIT),
        name="moe_ffn",
    )(block_e, n_used, buf, wg, wu, wd)


def _ln2_kernel(gb_ref, ob_ref, nb_ref, x_ref, qpos_ref, mod_ref, lg_ref, lb_ref, y_ref, *rest, with_inproj):
    if with_inproj:
        modn_ref, win_ref, o_ref, p_ref, ys_ref, sem = rest
    else:
        o_ref, ys_ref, sem = rest
    i = pl.program_id(0)
    slot = i % 2

    def copy_to(s):
        def copy(tile_row, buf_row, rows):
            return pltpu.make_async_copy(y_ref.at[pl.ds(buf_row, rows)], ys_ref.at[s, pl.ds(tile_row, rows)],
                                         sem.at[s])
        return copy

    @pl.when(i == 0)
    def _():
        ys_ref[...] = jnp.zeros_like(ys_ref)
        _start_run_copies(0, gb_ref, ob_ref, nb_ref, copy_to(0))

    @pl.when(i + 1 < pl.num_programs(0))
    def _():
        _start_run_copies(i + 1, gb_ref, ob_ref, nb_ref, copy_to(1 - slot))

    _wait_run_copies(i, nb_ref, copy_to(slot))
    m0, m1 = _sort_select(qpos_ref)
    sel = jnp.logical_or(m0, m1).astype(BF16)
    ys = ys_ref[slot].astype(BF16)

    def stream(rows):
        y = lax.dot_general(sel[:, rows], ys, (((0,), (0,)), ((), ())), preferred_element_type=F32)
        yield
        gate = mod_ref[0, 5:6, :]
        x_new = _ln(ALPHA * x_ref[rows, :] + gate * y, lg_ref[...], lb_ref[...])
        o_ref[rows, :] = x_new
        yield
        if with_inproj:
            u = (x_new * (1.0 + modn_ref[0, 1:2, :]) + modn_ref[0, 0:1, :]).astype(BF16)
            p_ref[rows, :] = jnp.dot(u, win_ref[...], preferred_element_type=F32).astype(p_ref.dtype)

    _alternate(*[stream(slice(r, r + ROW_STREAM)) for r in range(0, TM, ROW_STREAM)])


def _ln2(gb, ob, nb8, x2, qpos, mod, lg, lb, ybuf, tps, nb, next_in):
    ntok, d = x2.shape
    latent_only = next_in is None
    if latent_only:
        out_map = lambda i, *_: ((i // tps) * (tps - 1) + jnp.maximum(i % tps - 1, 0), 0)
        out_rows = ntok - nb * TM
    else:
        out_map = lambda i, *_: (i, 0)
        out_rows = ntok
    in_specs = [pl.BlockSpec((TM, d), lambda i, *_: (i, 0)),
                pl.BlockSpec((1, TOP_K, TM), lambda i, *_: (i, 0, 0)),
                pl.BlockSpec((1, 6, d), lambda i, *_: (_mod_row(i, tps, nb), 0, 0)),
                pl.BlockSpec((1, d), lambda i, *_: (0, 0)),
                pl.BlockSpec((1, d), lambda i, *_: (0, 0)),
                pl.BlockSpec(memory_space=pl.ANY)]
    out_specs = [pl.BlockSpec((TM, d), out_map)]
    out_shape = [jax.ShapeDtypeStruct((out_rows, d), F32)]
    args = [gb, ob, nb8, x2, qpos, mod, lg, lb, ybuf]
    if next_in is not None:
        mod_next, w_next = next_in
        nw = w_next.shape[1]
        in_specs += [pl.BlockSpec((1, 6, d), lambda i, *_: (_mod_row(i, tps, nb), 0, 0)),
                     pl.BlockSpec((d, nw), lambda i, *_: (0, 0))]
        out_specs.append(pl.BlockSpec((TM, nw), lambda i, *_: (i, 0)))
        out_shape.append(jax.ShapeDtypeStruct((ntok, nw), BF16))
        args += [mod_next, w_next]
    grid_spec = pltpu.PrefetchScalarGridSpec(
        num_scalar_prefetch=3,
        grid=(ntok // TM,),
        in_specs=in_specs,
        out_specs=out_specs,
        scratch_shapes=[pltpu.VMEM((2, SORT_ROWS, d), F32), pltpu.SemaphoreType.DMA((2,))],
    )
    outs = pl.pallas_call(
        functools.partial(_ln2_kernel, with_inproj=next_in is not None),
        grid_spec=grid_spec,
        out_shape=out_shape,
        compiler_params=_cparams(("arbitrary",)),
        name="post_moe_ln",
    )(*args)
    return (outs[0], outs[1]) if next_in is not None else (outs[0], None)


def _even_w_in(w):
    o = np.cumsum((0, GLA_QK, GLA_QK, GLA_V, GLA_V, GLA_RANK, GLA_RANK, GQA_Q, GQA_KV, GQA_KV))
    q, k, v, r = w[:, o[0]:o[1]], w[:, o[1]:o[2]], w[:, o[2]:o[3]], w[:, o[3]:o[4]]
    lr = w[:, o[4]:o[6]]
    gq, gk, gv = w[:, o[6]:o[7]], w[:, o[7]:o[8]], w[:, o[8]:o[9]]
    d = w.shape[0]
    zeros = lambda n: jnp.zeros((d, n), w.dtype)
    out = jnp.concatenate([gq, gk, gv, q, k, lr, zeros(LANES - 2 * GLA_RANK), zeros(LANES), v, r], axis=1)
    assert out.shape[1] == EVEN_W
    return out.astype(BF16)


def _odd_w_in(w):
    o = np.cumsum((0, SSD_INNER, SSD_INNER, SSD_BC, SSD_BC, SSD_HEADS, SSD_HEADS, FNET_W))
    d = w.shape[0]
    out = jnp.concatenate([w[:, o[1]:o[4]], w[:, :o[1]], w[:, o[4]:o[6]],
                           jnp.zeros((d, LANES - 2 * SSD_HEADS), w.dtype), w[:, o[6]:o[7]]], axis=1)
    assert out.shape[1] == ODD_W
    return out.astype(BF16)


def _rope_tables(ctx_len, seq):
    rows = seq // GRID_W
    r = jnp.repeat(jnp.arange(rows, dtype=F32), GRID_W)
    col = jnp.tile(jnp.arange(GRID_W, dtype=F32), rows)
    half = GQA_HD // 2
    inv = ROPE_THETA ** (-jnp.arange(0, half, 2, dtype=F32) / half)
    ar = r[:, None] * inv
    ac = col[:, None] * inv
    ang = jnp.concatenate([ar, ar, ac, ac], -1)
    cos = jnp.concatenate([jnp.ones((ctx_len, GQA_HD), F32), jnp.cos(ang)], 0)
    sin = jnp.concatenate([jnp.zeros((ctx_len, GQA_HD), F32), jnp.sin(ang)], 0)
    return jnp.tile(cos, (1, 2)), jnp.tile(sin, (1, 2))


def _even_layer(x2, p2, mod, nb, t, tps, ctx_len, w_o, w_dec, b_dec, gla_g, qn_g, kn_g, cos2, sin2, lg, lb, rw_t, rb):
    p3 = p2.reshape(nb, t, EVEN_W)
    wd_pads = [jnp.zeros((LANES, GLA_QK), F32).at[di * GLA_RANK:(di + 1) * GLA_RANK].set(w_dec[di])
               for di in range(2)]
    outs = _gla_scan(p3, wd_pads, [b_dec[di][None, :] for di in range(2)])
    qh2, kg2, vt4 = _qkprep(p2, cos2, sin2, jnp.tile(qn_g, 2)[None, :], jnp.tile(kn_g, 2)[None, :], tps)
    att = _attention(qh2.reshape(nb, t, GQA_HEADS * LANES), kg2.reshape(nb, t, GQA_KV_HEADS * LANES), vt4, ctx_len)
    w1, w2 = w_o[:GLA_V].astype(BF16), w_o[GLA_V:].astype(BF16)
    ntok = nb * t
    return _evout(outs[0].reshape(ntok, GLA_V), outs[1].reshape(ntok, GLA_V), p2, att.reshape(ntok, GQA_Q),
                  jnp.tile(gla_g, GLA_HEADS)[None, :], w1, w2, x2, mod, lg, lb, rw_t, rb, tps, nb)


def _odd_layer(x2, p2, mod, nb, t, tps, ctx_len, w_o, conv_w, conv_b, dt_bias, a_log, d_skip, ssd_g, fnet_g,
               lg, lb, rw_t, rb):
    p3 = p2.reshape(nb, t, ODD_W)
    w8 = jnp.zeros((8, SSD_CONV_CH), F32).at[:SSD_CONV].set(conv_w)
    xbc2 = _conv(p2, w8, conv_b[None, :], tps)
    xbc3 = xbc2.reshape(nb, t, SSD_CONV_CH)
    params = []
    for di in range(2):
        lo = di * SSD_HEADS
        pad = (lo, LANES - lo - SSD_HEADS)
        bias_pad = jnp.pad(dt_bias[di], pad)[None, :]
        a_pad = jnp.pad(-jnp.exp(a_log[di]), pad)[None, :]
        eexp = np.zeros((LANES, SSD_INNER), np.float32)
        eexp[lo + np.arange(SSD_INNER) // SSD_HD, np.arange(SSD_INNER)] = 1.0
        esel = np.zeros((16, LANES), np.float32)
        esel[np.arange(SSD_HEADS), lo + np.arange(SSD_HEADS)] = 1.0
        params.append((bias_pad, a_pad, jnp.asarray(eexp), jnp.asarray(esel)))
    ys = _ssd_scan(xbc3, p3, params)
    f3 = p3[:, :, ODD_W - FNET_W:]
    g256 = fnet_g[None, :]
    fmix = jnp.concatenate([_fourier_ctx(f3[:, :ctx_len], g256), _fourier_latent(f3[:, ctx_len:], g256)], axis=1)
    ntok = nb * t
    return _odout(ys[0].reshape(ntok, SSD_INNER), ys[1].reshape(ntok, SSD_INNER), xbc2, p2,
                  fmix.reshape(ntok, FNET_W), jnp.repeat(d_skip, SSD_HD)[None, :], ssd_g[None, :],
                  w_o[:SSD_INNER].astype(BF16), w_o[SSD_INNER:].astype(BF16), x2, mod, lg, lb, rw_t, rb, tps, nb)


def _moe_layer(routed, mod, nb, tps, layer, wg, wu, wd, lg, lb, next_in):
    x2, v, idx_t, gw_t, cnt = routed
    ntok, d = x2.shape
    nt = ntok // TM
    rows = cnt[:, 0].astype(jnp.int32)
    padded = (rows + FFN_ROWS - 1) // FFN_ROWS * FFN_ROWS
    pad_end = jnp.cumsum(padded)
    pad_start = pad_end - padded
    nblk = -(-(ntok * TOP_K + nt * N_EXPERTS * (ROWG - 1)) // FFN_ROWS) + N_EXPERTS
    blk_row = jnp.arange(nblk, dtype=jnp.int32)[:, None] * FFN_ROWS
    block_e = jnp.minimum(jnp.sum((blk_row >= pad_end[None, :]).astype(jnp.int32), axis=1), N_EXPERTS - 1)
    n_used = (pad_end[-1:] // FFN_ROWS).astype(jnp.int32)
    start = jnp.broadcast_to(pad_start.astype(F32)[:, None], (N_EXPERTS, LANES))
    qpos, meta = _meta(idx_t, start)
    gb, ob, nb8 = (meta[:, :, j].reshape(nt * N_EXPERTS) for j in range(3))
    tail_s = ((pad_start + rows) // ROWG).astype(jnp.int32)
    tail_n = ((padded - rows) // ROWG).astype(jnp.int32)
    buf = _scatter(gb, ob, nb8, tail_s, tail_n, n_used, v, qpos, gw_t, nblk * FFN_ROWS)
    y = _moe_ffn(block_e, n_used, buf, layer, wg, wu, wd)
    return _ln2(gb, ob, nb8, x2, qpos, mod, lg, lb, y, tps, nb, next_in)


def kernel(x, c, ctx, c_ctx, ada_w, ada_b, ln_g, ln_b, ev_w_in, ev_w_o, gla_w_decay, gla_b_decay, gla_norm_g,
           gqa_q_norm_g, gqa_k_norm_g, od_w_in, od_w_o, ssd_conv_w, ssd_conv_b, ssd_dt_bias, ssd_a_log, ssd_d,
           ssd_norm_g, fnet_norm_g, router_w, router_b, exp_w_gate, exp_w_up, exp_w_down):
    nb, seq, d = x.shape
    ctx_len = ctx.shape[1]
    assert ctx_len == TM and seq % TM == 0 and d == D_MODEL and nb <= 7
    t = ctx_len + seq
    tps = t // TM
    ntok = nb * t
    cc = jnp.zeros((8, d), F32).at[:nb].set(c).at[nb].set(c_ctx)
    mod_all = _ada_all(cc, ada_w, ada_b).reshape(DEPTH, 8, 6, d)
    cos2, sin2 = _rope_tables(ctx_len, seq)
    rw_t = router_w.T
    rb = jnp.broadcast_to(router_b[:, None], (N_EXPERTS, LANES))
    x2 = jnp.concatenate([ctx, x], axis=1).reshape(ntok, d)

    def w_in_of(layer):
        return _even_w_in(ev_w_in[layer // 2]) if layer % 2 == 0 else _odd_w_in(od_w_in[layer // 2])

    p2 = _inproj(x2, mod_all[0], w_in_of(0), tps, nb)
    for layer in range(DEPTH):
        mod = mod_all[layer]
        i = layer // 2
        lg0, lb0 = ln_g[layer, 0][None, :], ln_b[layer, 0][None, :]
        lg1, lb1 = ln_g[layer, 1][None, :], ln_b[layer, 1][None, :]
        if layer % 2 == 0:
            routed = _even_layer(x2, p2, mod, nb, t, tps, ctx_len, ev_w_o[i], gla_w_decay[i], gla_b_decay[i],
                                 gla_norm_g[i], gqa_q_norm_g[i], gqa_k_norm_g[i], cos2, sin2, lg0, lb0, rw_t, rb)
        else:
            routed = _odd_layer(x2, p2, mod, nb, t, tps, ctx_len, od_w_o[i], ssd_conv_w[i], ssd_conv_b[i],
                                ssd_dt_bias[i], ssd_a_log[i], ssd_d[i], ssd_norm_g[i], fnet_norm_g[i], lg0, lb0,
                                rw_t, rb)
        next_in = (mod_all[layer + 1], w_in_of(layer + 1)) if layer + 1 < DEPTH else None
        x2, p2 = _moe_layer(routed, mod, nb, tps, layer, exp_w_gate, exp_w_up, exp_w_down, lg1, lb1, next_in)
    return x2.reshape(nb, seq, d)
```

```python
import functools
import itertools
import math

import jax
import jax.numpy as jnp
import numpy as np
from jax import lax
from jax.experimental import pallas as pl
from jax.experimental.pallas import tpu as pltpu

F32 = jnp.float32
BF16 = jnp.bfloat16
HI = lax.Precision.HIGHEST

D_MODEL = 1024
DEPTH = 4
GRID_W = 64
GLA_HEADS, GLA_DK, GLA_DV, GLA_RANK, GLA_TAU, GLA_CHUNK = 4, 64, 128, 16, 16.0, 64
GQA_HEADS, GQA_KV_HEADS, GQA_HD = 8, 2, 64
ROPE_THETA = 10000.0
SSD_HEADS, SSD_HD, SSD_GROUPS, SSD_STATE, SSD_CONV, SSD_CHUNK = 12, 64, 2, 128, 5, 64
SSD_HPG = SSD_HEADS // SSD_GROUPS
SSD_INNER = SSD_HEADS * SSD_HD
SSD_BC = SSD_GROUPS * SSD_STATE
SSD_CONV_CH = SSD_INNER + 2 * SSD_BC
FNET_GROUPS, FNET_GC = 4, 64
FNET_W = FNET_GROUPS * FNET_GC
N_EXPERTS, N_EXPERT_GROUPS, TOP_K, D_EXPERT = 16, 4, 2, 768
EXPERTS_PER_GROUP = N_EXPERTS // N_EXPERT_GROUPS
GLA_QK = GLA_HEADS * GLA_DK
GLA_V = GLA_HEADS * GLA_DV
GQA_Q = GQA_HEADS * GQA_HD
GQA_KV = GQA_KV_HEADS * GQA_HD
EPS = 1e-6
ALPHA = (2.0 * DEPTH) ** 0.25

LANES = 128
TM = 256
EVEN_W = 2560
ODD_W = 2432
VMEM_LIMIT = 48 * 1024 * 1024
LOG2E = 1.4426950408889634
ATT_TQ = 256
ATT_HEADROOM = 64.0
ATT_MIN_DENOM = 2.0 ** -60
ATT_VROWS = LANES
ROWG = 8
SORT_ROWS = TOP_K * TM + N_EXPERTS * (ROWG - 1)
ROW_STREAM = 128
COPY_SHIFT = 2
COPY_GROUPS = 1 << COPY_SHIFT
FFN_ROWS = 512
FFN_VMEM_LIMIT = 56 * 1024 * 1024


def _cparams(sem):
    return pltpu.CompilerParams(dimension_semantics=sem, vmem_limit_bytes=VMEM_LIMIT)


def _silu(x):
    return x * (1.0 / (1.0 + jnp.exp(-x)))


def _softplus(x):
    return jnp.maximum(x, 0.0) + jnp.log1p(jnp.exp(-jnp.abs(x)))


def _mod_row(i, tiles_per_seq, n_batch):
    return jnp.where(i % tiles_per_seq == 0, n_batch, i // tiles_per_seq)


def _ada_kernel(c_ref, w_ref, b_ref, o_ref):
    s = _silu(c_ref[...])
    o_ref[0] = jnp.dot(s, w_ref[0], precision=HI, preferred_element_type=F32) + b_ref[0]


def _ada_all(cc, ada_w, ada_b):
    depth, d, n = ada_w.shape
    tn = 1536
    return pl.pallas_call(
        _ada_kernel,
        grid=(depth, n // tn),
        in_specs=[pl.BlockSpec((8, d), lambda l, j: (0, 0)),
                  pl.BlockSpec((1, d, tn), lambda l, j: (l, 0, j)),
                  pl.BlockSpec((1, 1, tn), lambda l, j: (l, 0, j))],
        out_specs=pl.BlockSpec((1, 8, tn), lambda l, j: (l, 0, j)),
        out_shape=jax.ShapeDtypeStruct((depth, 8, n), F32),
        compiler_params=_cparams(("arbitrary", "arbitrary")),
        name="adaln",
    )(cc, ada_w, ada_b.reshape(depth, 1, n))


def _inproj_kernel(x_ref, mod_ref, w_ref, o_ref):
    shift = mod_ref[0, 0:1, :]
    scale = mod_ref[0, 1:2, :]
    u = (x_ref[...] * (1.0 + scale) + shift).astype(BF16)
    o_ref[...] = jnp.dot(u, w_ref[...], preferred_element_type=F32).astype(o_ref.dtype)


def _inproj(x2, mod, w, tps, nb):
    ntok, d = x2.shape
    nw = w.shape[1]
    return pl.pallas_call(
        _inproj_kernel,
        grid=(ntok // TM,),
        in_specs=[pl.BlockSpec((TM, d), lambda i: (i, 0)),
                  pl.BlockSpec((1, 6, d), lambda i: (_mod_row(i, tps, nb), 0, 0)),
                  pl.BlockSpec((d, nw), lambda i: (0, 0))],
        out_specs=pl.BlockSpec((TM, nw), lambda i: (i, 0)),
        out_shape=jax.ShapeDtypeStruct((ntok, nw), BF16),
        compiler_params=_cparams(("arbitrary",)),
        name="inproj",
    )(x2, mod, w)


def _scan_block(i, nblk, reverse):
    if not reverse:
        return i
    return jnp.where(i == 0, 0, nblk - i)


def _tri(n, reverse):
    r = lax.broadcasted_iota(jnp.int32, (n, n), 0)
    c = lax.broadcasted_iota(jnp.int32, (n, n), 1)
    return (c >= r) if reverse else (c <= r)


def _chunk_sum_mat(chunk, reverse):
    l = np.arange(TM)[:, None]
    m = np.arange(TM)[None, :]
    same = (l // chunk) == (m // chunk)
    return (same & ((m >= l) if reverse else (m <= l))).astype(np.float32)


def _split3(x):
    hi = x.astype(BF16)
    r1 = x - hi.astype(F32)
    mid = r1.astype(BF16)
    return hi, mid, (r1 - mid.astype(F32)).astype(BF16)


def _chunk_rows(x, chunk, idx):
    return jnp.concatenate([jnp.broadcast_to(x[c * chunk + idx:c * chunk + idx + 1, :], (chunk, x.shape[1]))
                            for c in range(x.shape[0] // chunk)], axis=0)


def _gla_dir(q_ref, k_ref, v_ref, lr_ref, wd_ref, bd_ref, cm_ref, o_ref, s_ref, *, reverse):
    L = GLA_CHUNK
    nchunk = TM // L
    nt = (((1,), (1,)), ((), ()))
    tn = (((0,), (0,)), ((), ()))
    end = 0 if reverse else L - 1
    mid = L // 2 if reverse else L // 2 - 1

    @pl.when(pl.program_id(1) == 0)
    def _():
        s_ref[...] = jnp.zeros_like(s_ref)

    q = q_ref[...].astype(F32) * (GLA_DK ** -0.5)
    k = k_ref[...].astype(F32)
    v = v_ref[...]
    lr = lr_ref[...]
    z = bd_ref[...] + sum(jnp.dot(lr, wd_ref[j], preferred_element_type=F32) for j in range(3))
    yield
    la = -_softplus(-z) * (1.0 / GLA_TAU)
    cm = cm_ref[...]
    b = sum(jnp.dot(cm, part, preferred_element_type=F32) for part in _split3(la))
    yield
    bmid = _chunk_rows(b, L, mid)
    bend = _chunk_rows(b, L, end)
    qs = q * jnp.exp(b - bmid)
    ks = (k * jnp.exp(bmid - b)).astype(BF16)
    qi = q * jnp.exp(b)
    kend = (k * jnp.exp(bend - b)).astype(BF16)
    r = lax.broadcasted_iota(jnp.int32, (TM, TM), 0)
    c = lax.broadcasted_iota(jnp.int32, (TM, TM), 1)
    mask = jnp.logical_and(r // L == c // L, (c >= r) if reverse else (c <= r))
    lane = lax.broadcasted_iota(jnp.int32, (TM, LANES), 1)
    heads = (lane < GLA_DK, lane >= GLA_DK)
    yield
    att = [lax.dot_general(jnp.where(heads[h], qs, 0.0).astype(BF16), ks, nt, preferred_element_type=F32)
           for h in range(2)]
    yield
    intra = []
    for h in range(2):
        att_h = jnp.where(mask, att[h], 0.0).astype(BF16)
        intra.append(jnp.dot(att_h, v[:, h * GLA_DV:(h + 1) * GLA_DV], preferred_element_type=F32))
    yield
    qi_h = [jnp.where(heads[h], qi, 0.0).astype(BF16) for h in range(2)]
    tot = jnp.concatenate([b[ch * L + end:ch * L + end + 1, :] for ch in range(nchunk)]
                          + [jnp.zeros((ROWG - nchunk, LANES), F32)], axis=0).T
    ds, dec = [], []
    for ch in range(nchunk):
        rows = slice(ch * L, (ch + 1) * L)
        ds.append(lax.dot_general(kend[rows], v[rows], tn, preferred_element_type=F32))
        dec.append(jnp.exp(jnp.broadcast_to(tot[:, ch:ch + 1], (2 * GLA_DK, 2 * GLA_DV))))
    yield
    s = s_ref[...]
    inter = [None] * nchunk
    for ch in (range(nchunk - 1, -1, -1) if reverse else range(nchunk)):
        rows = slice(ch * L, (ch + 1) * L)
        s_bf = s.astype(BF16)
        inter[ch] = jnp.concatenate(
            [jnp.dot(qi_h[h][rows], s_bf[:, h * GLA_DV:(h + 1) * GLA_DV], preferred_element_type=F32)
             for h in range(2)], axis=1)
        s = dec[ch] * s + ds[ch]
        yield
    s_ref[...] = s
    o_ref[...] = (jnp.concatenate(intra, axis=1) + jnp.concatenate(inter, axis=0)).astype(o_ref.dtype)


def _alternate(*stage_generators):
    for _ in itertools.zip_longest(*stage_generators):
        pass


GLA_PAIRS = GLA_HEADS // 2


def _gla_kernel(*refs):
    n_streams = 2 * GLA_PAIRS
    n_in = (len(refs) - 2 - n_streams) // n_streams
    outs = refs[n_streams * n_in:n_streams * n_in + 2]
    states = refs[n_streams * n_in + 2:]
    scans = []
    for s in range(n_streams):
        reverse, pair = divmod(s, GLA_PAIRS)
        o_view = outs[reverse].at[:, pl.ds(pair * 2 * GLA_DV, 2 * GLA_DV)]
        scans.append(_gla_dir(*refs[s * n_in:(s + 1) * n_in], o_view, states[s], reverse=bool(reverse)))
    _alternate(*scans)


def _gla_scan(p3, wd_pads, bds):
    nb, t, _ = p3.shape
    nblk = t // TM
    in_specs, args, out_specs = [], [], []
    for reverse in (False, True):
        blk = functools.partial(_scan_block, nblk=nblk, reverse=reverse)
        wd3 = jnp.stack(_split3(wd_pads[int(reverse)]))
        cm = jnp.asarray(_chunk_sum_mat(GLA_CHUNK, reverse), BF16)
        for p in range(GLA_PAIRS):
            in_specs += [pl.BlockSpec((None, TM, LANES), lambda b, i, blk=blk, p=p: (b, blk(i), 6 + p)),
                         pl.BlockSpec((None, TM, LANES), lambda b, i, blk=blk, p=p: (b, blk(i), 8 + p)),
                         pl.BlockSpec((None, TM, 2 * GLA_DV), lambda b, i, blk=blk, p=p: (b, blk(i), 6 + p)),
                         pl.BlockSpec((None, TM, LANES), lambda b, i, blk=blk: (b, blk(i), 10)),
                         pl.BlockSpec((3, LANES, LANES), lambda b, i, p=p: (0, 0, p)),
                         pl.BlockSpec((1, LANES), lambda b, i, p=p: (0, p)),
                         pl.BlockSpec((TM, TM), lambda b, i: (0, 0))]
            args += [p3, p3, p3, p3, wd3, bds[int(reverse)], cm]
        out_specs.append(pl.BlockSpec((None, TM, GLA_V), lambda b, i, blk=blk: (b, blk(i), 0)))
    state = pltpu.VMEM((2 * GLA_DK, 2 * GLA_DV), F32)
    return pl.pallas_call(
        _gla_kernel,
        grid=(nb, nblk),
        in_specs=in_specs,
        out_specs=out_specs,
        out_shape=[jax.ShapeDtypeStruct((nb, t, GLA_V), BF16)] * 2,
        scratch_shapes=[state] * (2 * GLA_PAIRS),
        compiler_params=_cparams(("arbitrary", "arbitrary")),
        name="gla_scan",
    )(*args)


def _seg_ones(width, seg):
    r = lax.broadcasted_iota(jnp.int32, (width, width), 0) // seg
    c = lax.broadcasted_iota(jnp.int32, (width, width), 1) // seg
    return jnp.where(r == c, 1.0 / seg, 0.0).astype(BF16)


def _seg_mean(x, bd):
    hi = x.astype(BF16)
    lo = (x - hi.astype(F32)).astype(BF16)
    return jnp.dot(hi, bd, preferred_element_type=F32) + jnp.dot(lo, bd, preferred_element_type=F32)


def _norm_rope(x, g, cos, sin, bd):
    ms = _seg_mean(x * x, bd)
    xn = x * lax.rsqrt(ms + EPS) * g
    lane = lax.broadcasted_iota(jnp.int32, xn.shape, 1)
    quarter = GQA_HD // 4
    up = pltpu.roll(xn, LANES - quarter, 1)
    dn = pltpu.roll(xn, quarter, 1)
    rot = jnp.where(lane % (2 * quarter) < quarter, -up, dn)
    return xn * cos + rot * sin


def _qkprep_kernel(q_ref, k_ref, v_ref, cos_ref, sin_ref, gq_ref, gk_ref, qo_ref, ko_ref, vo_ref):
    bd = _seg_ones(LANES, GQA_HD)
    cos = cos_ref[...]
    sin = sin_ref[...]
    lane = lax.broadcasted_iota(jnp.int32, (TM, LANES), 1)
    low = lane < GQA_HD
    extra = lane == GQA_HD
    one_hot = jnp.where(extra, 1.0, 0.0)
    k_bound = math.sqrt(GQA_HD) * jnp.max(jnp.abs(gk_ref[...]), axis=-1, keepdims=True) * 1.02
    for j in range(GQA_Q // LANES):
        x = q_ref[:, j * LANES:(j + 1) * LANES].astype(F32)
        y = _norm_rope(x, gq_ref[...], cos, sin, bd) * (GQA_HD ** -0.5 * LOG2E)
        y = y.astype(BF16).astype(F32)
        norm = jnp.sqrt(_seg_mean(y * y, bd) * GQA_HD)
        shift = ATT_HEADROOM - norm * k_bound
        lo = jnp.where(low, y, jnp.where(extra, pltpu.roll(shift, GQA_HD, 1), 0.0))
        hi = jnp.where(low, pltpu.roll(y, GQA_HD, 1), jnp.where(extra, shift, 0.0))
        qo_ref[:, (2 * j) * LANES:(2 * j + 1) * LANES] = lo.astype(BF16)
        qo_ref[:, (2 * j + 1) * LANES:(2 * j + 2) * LANES] = hi.astype(BF16)
    k = _norm_rope(k_ref[...].astype(F32), gk_ref[...], cos, sin, bd)
    ko_ref[:, 0:LANES] = jnp.where(low, k, one_hot).astype(BF16)
    ko_ref[:, LANES:2 * LANES] = jnp.where(low, pltpu.roll(k, GQA_HD, 1), one_hot).astype(BF16)
    v = v_ref[...].astype(F32)
    vo_ref[0] = jnp.where(low, v, one_hot).T[0:ATT_VROWS].astype(BF16)
    vo_ref[1] = jnp.where(low, pltpu.roll(v, GQA_HD, 1), one_hot).T[0:ATT_VROWS].astype(BF16)


def _qkprep(p2, cos2, sin2, gq2, gk2, tps):
    ntok = p2.shape[0]
    nb = ntok // (tps * TM)
    return pl.pallas_call(
        _qkprep_kernel,
        grid=(ntok // TM,),
        in_specs=[pl.BlockSpec((TM, GQA_Q), lambda i: (i, 0)),
                  pl.BlockSpec((TM, LANES), lambda i: (i, 4)),
                  pl.BlockSpec((TM, LANES), lambda i: (i, 5)),
                  pl.BlockSpec((TM, LANES), lambda i: (i % tps, 0)),
                  pl.BlockSpec((TM, LANES), lambda i: (i % tps, 0)),
                  pl.BlockSpec((1, LANES), lambda i: (0, 0)),
                  pl.BlockSpec((1, LANES), lambda i: (0, 0))],
        out_specs=[pl.BlockSpec((TM, GQA_HEADS * LANES), lambda i: (i, 0)),
                   pl.BlockSpec((TM, GQA_KV_HEADS * LANES), lambda i: (i, 0)),
                   pl.BlockSpec((None, GQA_KV_HEADS, ATT_VROWS, TM), lambda i: (i // tps, 0, 0, i % tps))],
        out_shape=[jax.ShapeDtypeStruct((ntok, GQA_HEADS * LANES), BF16),
                   jax.ShapeDtypeStruct((ntok, GQA_KV_HEADS * LANES), BF16),
                   jax.ShapeDtypeStruct((nb, GQA_KV_HEADS, ATT_VROWS, tps * TM), BF16)],
        compiler_params=_cparams(("arbitrary",)),
        name="qkprep",
    )(p2, p2, p2, cos2, sin2, gq2, gk2)


def _attn_kernel(q_ref, k_ref, vt_ref, o_ref, *, ctx_len, n_lat_chunks, ck, n_blocks):
    i = pl.program_id(2)
    rep = GQA_HEADS // GQA_KV_HEADS
    is_ctx = i < ctx_len // ATT_TQ
    q4 = jnp.concatenate([q_ref[:, r * LANES:(r + 1) * LANES] for r in range(rep)], axis=0)
    nt = (((1,), (1,)), ((), ()))

    def chunk(acc, kc, vtc):
        st = lax.dot_general(kc, q4, nt, preferred_element_type=F32)
        return acc + jnp.dot(vtc, jnp.exp2(st).astype(BF16), preferred_element_type=F32)

    def store(acc):
        out = (acc * (1.0 / acc[GQA_HD:GQA_HD + 1, :])).T
        low = lax.broadcasted_iota(jnp.int32, (ATT_TQ, LANES), 1) < GQA_HD
        hs = [out[r * ATT_TQ:(r + 1) * ATT_TQ] for r in range(rep)]
        for u in range(rep // 2):
            o_ref[:, u * LANES:(u + 1) * LANES] = jnp.where(
                low, hs[2 * u], pltpu.roll(hs[2 * u + 1], GQA_HD, 1)).astype(o_ref.dtype)

    acc = chunk(jnp.zeros((ATT_VROWS, rep * ATT_TQ), F32), k_ref[0:ctx_len, :], vt_ref[:, 0:ctx_len])

    def body(c, acc):
        off = pl.multiple_of(ctx_len + c * ck, TM)
        return chunk(acc, k_ref[pl.ds(off, ck), :], vt_ref[:, pl.ds(off, ck)])

    acc = lax.fori_loop(0, jnp.where(is_ctx, 0, n_lat_chunks), body, acc)
    healthy = jnp.min(acc[GQA_HD:GQA_HD + 1, :]) >= ATT_MIN_DENOM

    @pl.when(healthy)
    def _():
        store(acc)

    @pl.when(jnp.logical_not(healthy))
    def _():
        def online(c, carry):
            m, acc = carry
            off = pl.multiple_of(c * TM, TM)
            st = lax.dot_general(k_ref[pl.ds(off, TM), :], q4, nt, preferred_element_type=F32)
            mn = jnp.maximum(m, jnp.max(st, axis=0, keepdims=True))
            pt = jnp.exp2(st - mn).astype(BF16)
            acc = jnp.exp2(m - mn) * acc + jnp.dot(vt_ref[:, pl.ds(off, TM)], pt, preferred_element_type=F32)
            return mn, acc

        init = (jnp.full((1, rep * ATT_TQ), -1e30, F32), jnp.zeros((ATT_VROWS, rep * ATT_TQ), F32))
        _, acc2 = lax.fori_loop(0, jnp.where(is_ctx, ctx_len // TM, n_blocks), online, init)
        store(acc2)


def _attention(qh3, kg3, vt4, ctx_len):
    nb, t, _ = qh3.shape
    seq = t - ctx_len
    ck = next(c for c in (4096, 2048, 1024, 512, 256) if seq % c == 0)
    rep = GQA_HEADS // GQA_KV_HEADS
    return pl.pallas_call(
        functools.partial(_attn_kernel, ctx_len=ctx_len, n_lat_chunks=seq // ck, ck=ck, n_blocks=t // TM),
        grid=(nb, GQA_KV_HEADS, t // ATT_TQ),
        in_specs=[pl.BlockSpec((None, ATT_TQ, rep * LANES), lambda b, g, i: (b, i, g)),
                  pl.BlockSpec((None, t, LANES), lambda b, g, i: (b, 0, g)),
                  pl.BlockSpec((None, None, ATT_VROWS, t), lambda b, g, i: (b, g, 0, 0))],
        out_specs=pl.BlockSpec((None, ATT_TQ, rep * GQA_HD), lambda b, g, i: (b, i, g)),
        out_shape=jax.ShapeDtypeStruct((nb, t, GQA_Q), BF16),
        compiler_params=_cparams(("arbitrary", "arbitrary", "arbitrary")),
        name="gqa_attn",
    )(qh3, kg3, vt4)


def _ln(x, g, b):
    mu = jnp.mean(x, axis=-1, keepdims=True)
    xc = x - mu
    var = jnp.mean(xc * xc, axis=-1, keepdims=True)
    return xc * lax.rsqrt(var + EPS) * g + b


def _evout_kernel(of_ref, ob_ref, r_ref, att_ref, g_ref, w1_ref, w2_ref, x_ref, mod_ref, lg_ref, lb_ref, rw_ref,
                  rb_ref, o_ref, v_ref, idx_ref, gw_ref, cnt_ref):
    def stream(rows, counts):
        o = of_ref[rows, :].astype(F32) + ob_ref[rows, :].astype(F32)
        r = r_ref[rows, :].astype(F32)
        parts = []
        for h in range(GLA_HEADS):
            oh = o[:, h * GLA_DV:(h + 1) * GLA_DV]
            ms = jnp.mean(oh * oh, axis=-1, keepdims=True)
            parts.append(oh * lax.rsqrt(ms + EPS))
        gl = (jnp.concatenate(parts, axis=1) * g_ref[...] * _silu(r)).astype(BF16)
        yield
        y = jnp.dot(gl, w1_ref[...], preferred_element_type=F32)
        y = y + jnp.dot(att_ref[rows, :], w2_ref[...], preferred_element_type=F32)
        yield
        gate = mod_ref[0, 2:3, :]
        x_new = _ln(ALPHA * x_ref[rows, :] + gate * y, lg_ref[...], lb_ref[...])
        o_ref[rows, :] = x_new
        yield
        yield from _route_rows(x_new, rows, mod_ref, rw_ref, rb_ref, v_ref, idx_ref, gw_ref, counts)

    _alternate_row_streams(stream, cnt_ref)


def _evout(of2, ob2, p2, att2, g512, w1, w2, x2, mod, lg, lb, rw_t, rb, tps, nb):
    ntok, d = x2.shape
    row = lambda i: (i, 0)
    const = lambda i: (0, 0)
    r_in, r_out, r_shapes = _router_specs(ntok, d)
    return pl.pallas_call(
        _evout_kernel,
        grid=(ntok // TM,),
        in_specs=[pl.BlockSpec((TM, GLA_V), row), pl.BlockSpec((TM, GLA_V), row),
                  pl.BlockSpec((TM, GLA_V), lambda i: (i, 4)),
                  pl.BlockSpec((TM, GQA_Q), row),
                  pl.BlockSpec((1, GLA_V), const),
                  pl.BlockSpec((GLA_V, d), const), pl.BlockSpec((GQA_Q, d), const),
                  pl.BlockSpec((TM, d), row),
                  pl.BlockSpec((1, 6, d), lambda i: (_mod_row(i, tps, nb), 0, 0)),
                  pl.BlockSpec((1, d), const), pl.BlockSpec((1, d), const)] + r_in,
        out_specs=[pl.BlockSpec((TM, d), row)] + r_out,
        out_shape=[jax.ShapeDtypeStruct((ntok, d), F32)] + r_shapes,
        compiler_params=_cparams(("arbitrary",)),
        name="even_out",
    )(of2, ob2, p2, att2, g512, w1, w2, x2, mod, lg, lb, rw_t, rb)


def _conv_kernel(prev_ref, cur_ref, next_ref, w_ref, b_ref, o_ref, pad_ref, *, tps):
    j = pl.program_id(0) % tps
    has_prev = (j >= 2).astype(F32)
    has_next = jnp.logical_and(j >= 1, j <= tps - 2).astype(F32)
    pad_ref[0:8, :] = prev_ref[...].astype(F32) * has_prev
    pad_ref[8:8 + TM, :] = cur_ref[...].astype(F32)
    pad_ref[8 + TM:16 + TM, :] = next_ref[...].astype(F32) * has_next
    half = (SSD_CONV - 1) // 2
    cw = 256
    for c0 in range(0, SSD_CONV_CH, cw):
        cols = slice(c0, c0 + cw)
        acc = jnp.zeros((TM, cw), F32) + b_ref[:, cols]
        for tap in range(SSD_CONV):
            acc = acc + pad_ref[pl.ds(8 + tap - half, TM), cols] * w_ref[tap:tap + 1, cols]
        o_ref[:, cols] = _silu(acc).astype(o_ref.dtype)


def _conv(p2, w8, bias, tps):
    ntok = p2.shape[0]
    cw = SSD_CONV_CH
    r8 = TM // 8
    nrow8 = ntok // 8
    return pl.pallas_call(
        functools.partial(_conv_kernel, tps=tps),
        grid=(ntok // TM,),
        in_specs=[pl.BlockSpec((8, cw), lambda i: (jnp.maximum(i * r8 - 1, 0), 0)),
                  pl.BlockSpec((TM, cw), lambda i: (i, 0)),
                  pl.BlockSpec((8, cw), lambda i: (jnp.minimum((i + 1) * r8, nrow8 - 1), 0)),
                  pl.BlockSpec((8, cw), lambda i: (0, 0)),
                  pl.BlockSpec((1, cw), lambda i: (0, 0))],
        out_specs=pl.BlockSpec((TM, cw), lambda i: (i, 0)),
        out_shape=jax.ShapeDtypeStruct((ntok, SSD_CONV_CH), BF16),
        scratch_shapes=[pltpu.VMEM((TM + 16, cw), F32)],
        compiler_params=_cparams(("arbitrary",)),
        name="ssd_conv",
    )(p2, p2, p2, w8, bias)


def _ssd_dir(xbc_ref, dt_ref, bias_ref, a_ref, eexp_ref, esel_ref, cm_ref, o_ref, h_ref, *, reverse):
    L = SSD_CHUNK
    GW = SSD_HPG * SSD_HD
    nchunk = TM // L
    nt = (((1,), (1,)), ((), ()))
    tn = (((0,), (0,)), ((), ()))
    end = 0 if reverse else L - 1
    lane0 = SSD_HEADS if reverse else 0

    @pl.when(pl.program_id(1) == 0)
    def _():
        h_ref[...] = jnp.zeros_like(h_ref)

    xs = xbc_ref[:, 0:SSD_INNER].astype(F32)
    bs = xbc_ref[:, SSD_INNER:SSD_INNER + SSD_BC]
    cs = xbc_ref[:, SSD_INNER + SSD_BC:SSD_CONV_CH]
    dt = _softplus(dt_ref[...].astype(F32) + bias_ref[...])
    a = dt * a_ref[...]
    cm = cm_ref[...]
    cum = sum(jnp.dot(cm, part, preferred_element_type=F32) for part in _split3(a))
    yield
    cum_t = sum(lax.dot_general(esel_ref[...], part, nt, preferred_element_type=F32)
                for part in _split3(cum))
    cend = _chunk_rows(cum, L, end)
    eexp = eexp_ref[...]

    def expand(m):
        return jnp.dot(m.astype(BF16), eexp, preferred_element_type=F32)

    xdt = xs * expand(dt)
    xdt_b = xdt.astype(BF16)
    yield
    xend = (xdt * expand(jnp.exp(cend - cum))).astype(BF16)
    yield
    ecum = expand(jnp.exp(cum))
    tot = jnp.concatenate([cum[ch * L + end:ch * L + end + 1, :] for ch in range(nchunk)]
                          + [jnp.zeros((ROWG - nchunk, LANES), F32)], axis=0)
    edec = jnp.exp(sum(jnp.dot(part, eexp, preferred_element_type=F32) for part in _split3(tot)))

    tri = _tri(L, reverse)
    low = lax.broadcasted_iota(jnp.int32, (L, LANES), 1) < SSD_HD
    intra = [[None] * SSD_GROUPS for _ in range(nchunk)]
    dh = [[None] * SSD_GROUPS for _ in range(nchunk)]
    for ch in range(nchunk):
        rows = slice(ch * L, (ch + 1) * L)
        for g in range(SSD_GROUPS):
            bg = bs[rows, g * SSD_STATE:(g + 1) * SSD_STATE]
            cg = cs[rows, g * SSD_STATE:(g + 1) * SSD_STATE]
            cb = lax.dot_general(cg, bg, nt, preferred_element_type=F32)
            pairs = []
            for pp in range(SSD_HPG // 2):
                yh = []
                for u in range(2):
                    h = g * SSD_HPG + 2 * pp + u
                    seg = cum[rows, lane0 + h:lane0 + h + 1] - cum_t[h:h + 1, rows]
                    dec = jnp.exp(jnp.where(tri, seg, -1e30))
                    mat = (cb * dec).astype(BF16)
                    col = (h - u) * SSD_HD
                    yh.append(jnp.dot(mat, xdt_b[rows, col:col + LANES], preferred_element_type=F32))
                pairs.append(jnp.where(low, yh[0], yh[1]))
                yield
            intra[ch][g] = jnp.concatenate(pairs, axis=1)
            dh[ch][g] = lax.dot_general(bg, xend[rows, g * GW:(g + 1) * GW], tn, preferred_element_type=F32)

    hs = [h_ref[g] for g in range(SSD_GROUPS)]
    out = [None] * nchunk
    for ch in (range(nchunk - 1, -1, -1) if reverse else range(nchunk)):
        rows = slice(ch * L, (ch + 1) * L)
        ys = []
        for g in range(SSD_GROUPS):
            cg = cs[rows, g * SSD_STATE:(g + 1) * SSD_STATE]
            y_inter = jnp.dot(cg, hs[g].astype(BF16), preferred_element_type=F32) * ecum[rows, g * GW:(g + 1) * GW]
            ys.append(intra[ch][g] + y_inter)
            hs[g] = hs[g] * edec[ch:ch + 1, g * GW:(g + 1) * GW] + dh[ch][g]
        out[ch] = jnp.concatenate(ys, axis=1)
        yield
    for g in range(SSD_GROUPS):
        h_ref[g] = hs[g]
    o_ref[...] = jnp.concatenate(out, axis=0).astype(o_ref.dtype)


def _ssd_kernel(*refs):
    n_in = (len(refs) - 4) // 2
    o_f, o_b, h_f, h_b = refs[2 * n_in:]
    _alternate(_ssd_dir(*refs[:n_in], o_f, h_f, reverse=False),
               _ssd_dir(*refs[n_in:2 * n_in], o_b, h_b, reverse=True))


def _ssd_scan(xbc3, p3, params):
    nb, t, _ = xbc3.shape
    nblk = t // TM
    const = lambda b, i: (0, 0)
    in_specs, args, out_specs = [], [], []
    for reverse in (False, True):
        blk = functools.partial(_scan_block, nblk=nblk, reverse=reverse)
        bias_pad, a_pad, eexp, esel = params[int(reverse)]
        in_specs += [pl.BlockSpec((None, TM, SSD_CONV_CH), lambda b, i, blk=blk: (b, blk(i), 0)),
                     pl.BlockSpec((None, TM, LANES), lambda b, i, blk=blk: (b, blk(i), 16)),
                     pl.BlockSpec((1, LANES), const), pl.BlockSpec((1, LANES), const),
                     pl.BlockSpec((LANES, SSD_INNER), const), pl.BlockSpec((16, LANES), const),
                     pl.BlockSpec((TM, TM), const)]
        args += [xbc3, p3, bias_pad, a_pad, eexp.astype(BF16), esel.astype(BF16),
                 jnp.asarray(_chunk_sum_mat(SSD_CHUNK, reverse), BF16)]
        out_specs.append(pl.BlockSpec((None, TM, SSD_INNER), lambda b, i, blk=blk: (b, blk(i), 0)))
    state = pltpu.VMEM((SSD_GROUPS, SSD_STATE, SSD_HPG * SSD_HD), F32)
    return pl.pallas_call(
        _ssd_kernel,
        grid=(nb, nblk),
        in_specs=in_specs,
        out_specs=out_specs,
        out_shape=[jax.ShapeDtypeStruct((nb, t, SSD_INNER), BF16)] * 2,
        scratch_shapes=[state, state],
        compiler_params=_cparams(("arbitrary", "arbitrary")),
        name="ssd_scan",
    )(*args)


def _dft_mats(n):
    k = jnp.arange(n, dtype=jnp.int32)
    ang = ((k[:, None] * k[None, :]) % n).astype(F32) * (2.0 * math.pi / n)
    return jnp.cos(ang), jnp.sin(ang)


def _chan_mats():
    cc, sc = _dft_mats(FNET_GC)
    eye = jnp.eye(FNET_GROUPS, dtype=F32)
    return jnp.kron(eye, cc), jnp.kron(eye, sc)


def _group_rms(x, g, bd):
    return x * lax.rsqrt(_seg_mean(x * x, bd) + EPS) * g


def _split2(x):
    hi = x.astype(BF16)
    return hi, (x - hi.astype(F32)).astype(BF16)


def _mm(a, b):
    a_hi, a_lo = _split2(a)
    b_hi, b_lo = _split2(b)
    dot = functools.partial(jnp.dot, preferred_element_type=F32)
    return dot(a_hi, b_hi) + (dot(a_hi, b_lo) + dot(a_lo, b_hi))


def _ffta_kernel(x_ref, g_ref, cc_ref, sc_ref, c1_ref, s1_ref, twc_ref, tws_ref, yr_ref, yi_ref, *, nb2):
    bd = _seg_ones(FNET_W, FNET_GC)
    for j in range(nb2):
        x = x_ref[:, j * FNET_W:(j + 1) * FNET_W].astype(F32)
        xn = _group_rms(x, g_ref[...], bd)
        vr = _mm(xn, cc_ref[...])
        vi = -_mm(xn, sc_ref[...])
        c1 = c1_ref[...]
        s1 = s1_ref[...]
        yr = _mm(c1, vr) + _mm(s1, vi)
        yi = _mm(c1, vi) - _mm(s1, vr)
        tc = twc_ref[j]
        ts = tws_ref[j]
        yr_ref[j] = yr * tc + yi * ts
        yi_ref[j] = yi * tc - yr * ts


def _fftb_kernel(yr_ref, yi_ref, c2_ref, s2_ref, o_ref, *, scale):
    o_ref[...] = (_mm(c2_ref[...], yr_ref[...]) + _mm(s2_ref[...], yi_ref[...])) * scale


def _fftc_kernel(x_ref, g_ref, cc_ref, sc_ref, ct_ref, st_ref, o_ref, *, scale):
    bd = _seg_ones(FNET_W, FNET_GC)
    xn = _group_rms(x_ref[...].astype(F32), g_ref[...], bd)
    a = _mm(xn, cc_ref[...])
    b = _mm(xn, sc_ref[...])
    o_ref[...] = (_mm(ct_ref[...], a) - _mm(st_ref[...], b)) * scale


def _fourier_latent(f_lat, g256):
    nb, s, _ = f_lat.shape
    n2 = 64
    n1 = s // n2
    nb2 = 8
    ccm, scm = _chan_mats()
    c1, s1 = _dft_mats(n1)
    c2, s2 = _dft_mats(n2)
    t2 = jnp.arange(n2, dtype=jnp.int32)[:, None]
    k1 = jnp.arange(n1, dtype=jnp.int32)[None, :]
    tw = ((t2 * k1) % s).astype(F32) * (2.0 * math.pi / s)
    twc = jnp.cos(tw)[:, :, None]
    tws = jnp.sin(tw)[:, :, None]
    x2 = f_lat.reshape(nb, n1, n2 * FNET_W)
    const2 = lambda b, j: (0, 0)
    yr, yi = pl.pallas_call(
        functools.partial(_ffta_kernel, nb2=nb2),
        grid=(nb, n2 // nb2),
        in_specs=[pl.BlockSpec((None, n1, nb2 * FNET_W), lambda b, j: (b, 0, j)),
                  pl.BlockSpec((1, FNET_W), const2),
                  pl.BlockSpec((FNET_W, FNET_W), const2), pl.BlockSpec((FNET_W, FNET_W), const2),
                  pl.BlockSpec((n1, n1), const2), pl.BlockSpec((n1, n1), const2),
                  pl.BlockSpec((nb2, n1, 1), lambda b, j: (j, 0, 0)),
                  pl.BlockSpec((nb2, n1, 1), lambda b, j: (j, 0, 0))],
        out_specs=[pl.BlockSpec((None, nb2, n1, FNET_W), lambda b, j: (b, j, 0, 0)),
                   pl.BlockSpec((None, nb2, n1, FNET_W), lambda b, j: (b, j, 0, 0))],
        out_shape=[jax.ShapeDtypeStruct((nb, n2, n1, FNET_W), F32),
                   jax.ShapeDtypeStruct((nb, n2, n1, FNET_W), F32)],
        compiler_params=_cparams(("arbitrary", "arbitrary")),
        name="fft_a",
    )(x2, g256, ccm, scm, c1, s1, twc, tws)
    ncol = n1 * FNET_W
    tn = min(8192, ncol)
    out = pl.pallas_call(
        functools.partial(_fftb_kernel, scale=1.0 / math.sqrt(s * FNET_GC)),
        grid=(nb, ncol // tn),
        in_specs=[pl.BlockSpec((None, n2, tn), lambda b, j: (b, 0, j)),
                  pl.BlockSpec((None, n2, tn), lambda b, j: (b, 0, j)),
                  pl.BlockSpec((n2, n2), const2), pl.BlockSpec((n2, n2), const2)],
        out_specs=pl.BlockSpec((None, n2, tn), lambda b, j: (b, 0, j)),
        out_shape=jax.ShapeDtypeStruct((nb, n2, ncol), F32),
        compiler_params=_cparams(("arbitrary", "arbitrary")),
        name="fft_b",
    )(yr.reshape(nb, n2, ncol), yi.reshape(nb, n2, ncol), c2, s2)
    return out.reshape(nb, s, FNET_W)


def _fourier_ctx(f_ctx, g256):
    nb, tc, _ = f_ctx.shape
    ccm, scm = _chan_mats()
    ct, st = _dft_mats(tc)
    const = lambda b: (0, 0)
    return pl.pallas_call(
        functools.partial(_fftc_kernel, scale=1.0 / math.sqrt(tc * FNET_GC)),
        grid=(nb,),
        in_specs=[pl.BlockSpec((None, tc, FNET_W), lambda b: (b, 0, 0)),
                  pl.BlockSpec((1, FNET_W), const),
                  pl.BlockSpec((FNET_W, FNET_W), const), pl.BlockSpec((FNET_W, FNET_W), const),
                  pl.BlockSpec((tc, tc), const), pl.BlockSpec((tc, tc), const)],
        out_specs=pl.BlockSpec((None, tc, FNET_W), lambda b: (b, 0, 0)),
        out_shape=jax.ShapeDtypeStruct((nb, tc, FNET_W), F32),
        compiler_params=_cparams(("arbitrary",)),
        name="fft_ctx",
    )(f_ctx, g256, ccm, scm, ct, st)


def _odout_kernel(yf_ref, yb_ref, xs_ref, z0_ref, z1_ref, z2_ref, f_ref, dsk_ref, g_ref, w1_ref, w2_ref, x_ref,
                  mod_ref, lg_ref, lb_ref, rw_ref, rb_ref, o_ref, v_ref, idx_ref, gw_ref, cnt_ref):
    GW = SSD_HPG * SSD_HD

    def stream(rows, counts):
        y = yf_ref[rows, :].astype(F32) + yb_ref[rows, :].astype(F32) + xs_ref[rows, :].astype(F32) * dsk_ref[...]
        z = jnp.concatenate([z0_ref[rows, :], z1_ref[rows, :], z2_ref[rows, :]], axis=1).astype(F32)
        y = y * _silu(z)
        parts = []
        for g in range(SSD_GROUPS):
            yg = y[:, g * GW:(g + 1) * GW]
            ms = jnp.mean(yg * yg, axis=-1, keepdims=True)
            parts.append(yg * lax.rsqrt(ms + EPS))
        yn = (jnp.concatenate(parts, axis=1) * g_ref[...]).astype(BF16)
        yield
        o = jnp.dot(yn, w1_ref[...], preferred_element_type=F32)
        o = o + jnp.dot(f_ref[rows, :].astype(BF16), w2_ref[...], preferred_element_type=F32)
        yield
        gate = mod_ref[0, 2:3, :]
        x_new = _ln(ALPHA * x_ref[rows, :] + gate * o, lg_ref[...], lb_ref[...])
        o_ref[rows, :] = x_new
        yield
        yield from _route_rows(x_new, rows, mod_ref, rw_ref, rb_ref, v_ref, idx_ref, gw_ref, counts)

    _alternate_row_streams(stream, cnt_ref)


def _odout(yf2, yb2, xbc2, p2, f2, dsk, g768, w1, w2, x2, mod, lg, lb, rw_t, rb, tps, nb):
    ntok, d = x2.shape
    row = lambda i: (i, 0)
    const = lambda i: (0, 0)
    r_in, r_out, r_shapes = _router_specs(ntok, d)
    zw = 256
    return pl.pallas_call(
        _odout_kernel,
        grid=(ntok // TM,),
        in_specs=[pl.BlockSpec((TM, SSD_INNER), row), pl.BlockSpec((TM, SSD_INNER), row),
                  pl.BlockSpec((TM, SSD_INNER), row)]
        + [pl.BlockSpec((TM, zw), lambda i, j=j: (i, SSD_CONV_CH // zw + j)) for j in range(SSD_INNER // zw)]
        + [pl.BlockSpec((TM, FNET_W), row),
                  pl.BlockSpec((1, SSD_INNER), const), pl.BlockSpec((1, SSD_INNER), const),
                  pl.BlockSpec((SSD_INNER, d), const), pl.BlockSpec((FNET_W, d), const),
                  pl.BlockSpec((TM, d), row),
                  pl.BlockSpec((1, 6, d), lambda i: (_mod_row(i, tps, nb), 0, 0)),
                  pl.BlockSpec((1, d), const), pl.BlockSpec((1, d), const)] + r_in,
        out_specs=[pl.BlockSpec((TM, d), row)] + r_out,
        out_shape=[jax.ShapeDtypeStruct((ntok, d), F32)] + r_shapes,
        compiler_params=_cparams(("arbitrary",)),
        name="odd_out",
    )(yf2, yb2, xbc2, *([p2] * (SSD_INNER // zw)), f2, dsk, g768, w1, w2, x2, mod, lg, lb, rw_t, rb)


def _route_rows(x, rows, mod_ref, rw_ref, rb_ref, v_ref, idx_ref, gw_ref, counts):
    n = x.shape[0]
    shift = mod_ref[0, 3:4, :]
    scale = mod_ref[0, 4:5, :]
    v = x * (1.0 + scale) + shift
    v_hi = v.astype(BF16)
    v_ref[rows, :] = v_hi
    v_lo = (v - v_hi.astype(F32)).astype(BF16)
    w_hi, w_lo = _split2(rw_ref[...])
    nt_dot = functools.partial(lax.dot_general, dimension_numbers=(((1,), (1,)), ((), ())),
                               preferred_element_type=F32)
    logits = nt_dot(w_hi, v_hi) + (nt_dot(w_hi, v_lo) + nt_dot(w_lo, v_hi))
    yield
    s = 1.0 / (1.0 + jnp.exp(-logits))
    sel = s + rb_ref[:, 0:1]
    izero = jnp.zeros((1, n), jnp.int32)
    best = None
    for g in range(N_EXPERT_GROUPS):
        a = [sel[g * EXPERTS_PER_GROUP + j:g * EXPERTS_PER_GROUP + j + 1, :] for j in range(EXPERTS_PER_GROUP)]
        sv = [s[g * EXPERTS_PER_GROUP + j:g * EXPERTS_PER_GROUP + j + 1, :] for j in range(EXPERTS_PER_GROUP)]
        m1, i1, s1 = a[0], izero, sv[0]
        for j in range(1, EXPERTS_PER_GROUP):
            gt = a[j] > m1
            m1 = jnp.where(gt, a[j], m1)
            i1 = jnp.where(gt, j, i1)
            s1 = jnp.where(gt, sv[j], s1)
        m2 = jnp.full((1, n), -jnp.inf, F32)
        i2, s2 = izero, jnp.zeros((1, n), F32)
        for j in range(EXPERTS_PER_GROUP):
            gt = jnp.logical_and(i1 != j, a[j] > m2)
            m2 = jnp.where(gt, a[j], m2)
            i2 = jnp.where(gt, j, i2)
            s2 = jnp.where(gt, sv[j], s2)
        cand = (m1 + m2, i1 + g * EXPERTS_PER_GROUP, i2 + g * EXPERTS_PER_GROUP, s1, s2)
        if best is None:
            best = cand
        else:
            gt = cand[0] > best[0]
            best = tuple(jnp.where(gt, cn, bs) for cn, bs in zip(cand, best))
        yield
    _, e1, e2, w1, w2 = best
    tot = w1 + w2
    idx_ref[0, :, rows] = jnp.concatenate([e1, e2], axis=0)
    gw_ref[0, :, rows] = jnp.concatenate([w1 / tot, w2 / tot], axis=0)
    eio = lax.broadcasted_iota(jnp.int32, (N_EXPERTS, n), 0)
    oh = jnp.logical_or(eio == e1, eio == e2).astype(F32)
    counts.append(jnp.sum(oh, axis=1, keepdims=True))


def _alternate_row_streams(make_stream, cnt_ref):
    @pl.when(pl.program_id(0) == 0)
    def _():
        cnt_ref[...] = jnp.zeros_like(cnt_ref)

    counts = []
    _alternate(*[make_stream(slice(r, r + ROW_STREAM), counts) for r in range(0, TM, ROW_STREAM)])
    cnt_ref[...] += _ceil_rows(sum(counts))


def _router_specs(ntok, d):
    nt = ntok // TM
    in_specs = [pl.BlockSpec((N_EXPERTS, d), lambda i: (0, 0)),
                pl.BlockSpec((N_EXPERTS, LANES), lambda i: (0, 0))]
    out_specs = [pl.BlockSpec((TM, d), lambda i: (i, 0)),
                 pl.BlockSpec((1, TOP_K, TM), lambda i: (i, 0, 0)),
                 pl.BlockSpec((1, TOP_K, TM), lambda i: (i, 0, 0)),
                 pl.BlockSpec((N_EXPERTS, LANES), lambda i: (0, 0))]
    out_shapes = [jax.ShapeDtypeStruct((ntok, d), BF16),
                  jax.ShapeDtypeStruct((nt, TOP_K, TM), jnp.int32),
                  jax.ShapeDtypeStruct((nt, TOP_K, TM), F32),
                  jax.ShapeDtypeStruct((N_EXPERTS, LANES), F32)]
    return in_specs, out_specs, out_shapes


def _ceil_rows(c):
    return jnp.ceil(c * (1.0 / ROWG)) * ROWG


def _meta_kernel(idx_ref, start_ref, qpos_ref, meta_ref, run_ref):
    @pl.when(pl.program_id(0) == 0)
    def _():
        run_ref[...] = jnp.zeros_like(run_ref)

    eio = lax.broadcasted_iota(jnp.int32, (N_EXPERTS, TM), 0)
    r = lax.broadcasted_iota(jnp.int32, (TM, TM), 0)
    c = lax.broadcasted_iota(jnp.int32, (TM, TM), 1)
    before = (r < c).astype(BF16)
    er = lax.broadcasted_iota(jnp.int32, (N_EXPERTS, N_EXPERTS), 0)
    ec = lax.broadcasted_iota(jnp.int32, (N_EXPERTS, N_EXPERTS), 1)
    lower = (ec < er).astype(F32)
    lane = lax.broadcasted_iota(jnp.int32, (N_EXPERTS, LANES), 1)
    run = run_ref[...]
    for g in range(idx_ref.shape[0]):
        oh1 = eio == idx_ref[g, 0:1, :]
        oh2 = eio == idx_ref[g, 1:2, :]
        oh = jnp.logical_or(oh1, oh2)
        rank = jnp.dot(oh.astype(BF16), before, preferred_element_type=F32)
        run_len = jnp.broadcast_to(_ceil_rows(jnp.sum(oh.astype(F32), axis=1, keepdims=True)), (N_EXPERTS, LANES))
        off = jnp.dot(lower, run_len, precision=HI, preferred_element_type=F32)
        pos = rank + off[:, 0:1]
        q1 = jnp.sum(jnp.where(oh1, pos, 0.0), axis=0, keepdims=True)
        q2 = jnp.sum(jnp.where(oh2, pos, 0.0), axis=0, keepdims=True)
        qpos_ref[g] = jnp.concatenate([q1, q2], axis=0).astype(jnp.int32)
        meta = jnp.where(lane == 0, start_ref[...] + run, jnp.where(lane == 1, off, run_len))
        meta_ref[g] = (meta * (1.0 / ROWG)).astype(jnp.int32)
        run = run + run_len
    run_ref[...] = run


def _meta(idx_t, start):
    nt = idx_t.shape[0]
    grp = next(g for g in (4, 3, 2, 1) if nt % g == 0)
    return pl.pallas_call(
        _meta_kernel,
        grid=(nt // grp,),
        in_specs=[pl.BlockSpec((grp, TOP_K, TM), lambda i: (i, 0, 0)),
                  pl.BlockSpec((N_EXPERTS, LANES), lambda i: (0, 0))],
        out_specs=[pl.BlockSpec((grp, TOP_K, TM), lambda i: (i, 0, 0)),
                   pl.BlockSpec((grp, N_EXPERTS, LANES), lambda i: (i, 0, 0))],
        out_shape=[jax.ShapeDtypeStruct((nt, TOP_K, TM), jnp.int32),
                   jax.ShapeDtypeStruct((nt, N_EXPERTS, LANES), jnp.int32)],
        scratch_shapes=[pltpu.VMEM((N_EXPERTS, LANES), F32)],
        compiler_params=_cparams(("arbitrary",)),
        name="moe_meta",
    )(idx_t, start)


def _sort_select(qpos_ref):
    r = lax.broadcasted_iota(jnp.int32, (SORT_ROWS, TM), 0)
    return r == qpos_ref[0, 0:1, :], r == qpos_ref[0, 1:2, :]


def _start_run_copies(i, gb_ref, ob_ref, nb_ref, make_copy):
    big = COPY_GROUPS * ROWG
    for e in range(N_EXPERTS):
        n = nb_ref[i * N_EXPERTS + e]
        buf_row = gb_ref[i * N_EXPERTS + e] * ROWG
        tile_row = ob_ref[i * N_EXPERTS + e] * ROWG
        n_big = lax.shift_right_logical(n, COPY_SHIFT)

        def body_big(k, carry, buf_row=buf_row, tile_row=tile_row):
            make_copy(pl.multiple_of(tile_row + k * big, ROWG), pl.multiple_of(buf_row + k * big, ROWG), big).start()
            return carry

        def body_one(k, carry, buf_row=buf_row, tile_row=tile_row):
            make_copy(pl.multiple_of(tile_row + k * ROWG, ROWG), pl.multiple_of(buf_row + k * ROWG, ROWG),
                      ROWG).start()
            return carry

        lax.fori_loop(0, n_big, body_big, 0)
        lax.fori_loop(lax.shift_left(n_big, COPY_SHIFT), n, body_one, 0)


def _wait_run_copies(i, nb_ref, make_copy):
    n_big = 0
    n_one = 0
    for e in range(N_EXPERTS):
        n = nb_ref[i * N_EXPERTS + e]
        n_big = n_big + lax.shift_right_logical(n, COPY_SHIFT)
        n_one = n_one + lax.bitwise_and(n, COPY_GROUPS - 1)

    def wait_big(k, carry):
        make_copy(0, 0, COPY_GROUPS * ROWG).wait()
        return carry

    def wait_one(k, carry):
        make_copy(0, 0, ROWG).wait()
        return carry

    lax.fori_loop(0, n_big, wait_big, 0)
    lax.fori_loop(0, n_one, wait_one, 0)


def _scatter_kernel(gb_ref, ob_ref, nb_ref, ts_ref, tn_ref, nu_ref, v_ref, qpos_ref, gw_ref, buf_ref, xs_ref, z_ref,
                    sem, zsem):
    i = pl.program_id(0)
    last = pl.num_programs(0) - 1
    slot = i % 2
    d = v_ref.shape[1]
    m0, m1 = _sort_select(qpos_ref)
    sel = jnp.logical_or(m0, m1).astype(BF16)
    xs_ref[slot, :, 0:d] = jnp.dot(sel, v_ref[...], preferred_element_type=F32)
    gate = jnp.sum(jnp.where(m0, gw_ref[0, 0:1, :], 0.0) + jnp.where(m1, gw_ref[0, 1:2, :], 0.0),
                   axis=1, keepdims=True)
    xs_ref[slot, :, d:d + LANES] = jnp.broadcast_to(gate, (SORT_ROWS, LANES))

    def copy_from(s):
        def copy(tile_row, buf_row, rows):
            return pltpu.make_async_copy(xs_ref.at[s, pl.ds(tile_row, rows)], buf_ref.at[pl.ds(buf_row, rows)],
                                         sem.at[s])
        return copy

    _start_run_copies(i, gb_ref, ob_ref, nb_ref, copy_from(slot))

    @pl.when(i > 0)
    def _():
        _wait_run_copies(i - 1, nb_ref, copy_from(1 - slot))

    @pl.when(i == last)
    def _():
        _wait_run_copies(i, nb_ref, copy_from(slot))

    @pl.when(i == last)
    def _():
        z_ref[...] = jnp.zeros_like(z_ref)

        def zcopy(buf_row):
            return pltpu.make_async_copy(z_ref.at[pl.ds(0, ROWG)], buf_ref.at[pl.ds(buf_row, ROWG)], zsem)

        total = 0
        for e in range(N_EXPERTS):
            n = tn_ref[e]
            row0 = ts_ref[e] * ROWG

            def body(k, carry, row0=row0):
                zcopy(pl.multiple_of(row0 + k * ROWG, ROWG)).start()
                return carry

            lax.fori_loop(0, n, body, 0)
            total = total + n

        def wbody(k, carry):
            zcopy(0).wait()
            return carry

        lax.fori_loop(0, total, wbody, 0)

        def zblock(blk):
            return pltpu.make_async_copy(z_ref, buf_ref.at[pl.ds(pl.multiple_of(blk * FFN_ROWS, FFN_ROWS), FFN_ROWS)],
                                         zsem)

        def bbody(blk, carry):
            zblock(blk).start()
            return carry

        def bwait(blk, carry):
            zblock(blk).wait()
            return carry

        nblk = buf_ref.shape[0] // FFN_ROWS
        lax.fori_loop(nu_ref[0], nblk, bbody, 0)
        lax.fori_loop(nu_ref[0], nblk, bwait, 0)


def _scatter(gb, ob, nb8, tail_s, tail_n, n_used, v, qpos, gw_t, nrow):
    ntok, d = v.shape
    grid_spec = pltpu.PrefetchScalarGridSpec(
        num_scalar_prefetch=6,
        grid=(ntok // TM,),
        in_specs=[pl.BlockSpec((TM, d), lambda i, *_: (i, 0)),
                  pl.BlockSpec((1, TOP_K, TM), lambda i, *_: (i, 0, 0)),
                  pl.BlockSpec((1, TOP_K, TM), lambda i, *_: (i, 0, 0))],
        out_specs=pl.BlockSpec(memory_space=pl.ANY),
        scratch_shapes=[pltpu.VMEM((2, SORT_ROWS, d + LANES), F32), pltpu.VMEM((FFN_ROWS, d + LANES), F32),
                        pltpu.SemaphoreType.DMA((2,)), pltpu.SemaphoreType.DMA(())],
    )
    return pl.pallas_call(
        _scatter_kernel,
        grid_spec=grid_spec,
        out_shape=jax.ShapeDtypeStruct((nrow, d + LANES), F32),
        compiler_params=_cparams(("arbitrary",)),
        name="moe_scatter",
    )(gb, ob, nb8, tail_s, tail_n, n_used, v, qpos, gw_t)


def _moe_kernel(be_ref, nu_ref, x_ref, wg_ref, wu_ref, wd_ref, o_ref, wg_bf, wu_bf, wd_bf):
    i = pl.program_id(0)
    d = o_ref.shape[1]

    @pl.when(jnp.logical_or(i == 0, be_ref[i] != be_ref[jnp.maximum(i - 1, 0)]))
    def _():
        wg_bf[...] = wg_ref[...].astype(BF16)
        wu_bf[...] = wu_ref[...].astype(BF16)
        wd_bf[...] = wd_ref[...].astype(BF16)

    @pl.when(i < nu_ref[0])
    def _():
        xb = x_ref[...]
        x = xb[:, 0:d].astype(BF16)
        g = jnp.dot(x, wg_bf[...], preferred_element_type=F32)
        u = jnp.dot(x, wu_bf[...], preferred_element_type=F32)
        h = (_silu(g) * u).astype(BF16)
        y = jnp.dot(h, wd_bf[...], preferred_element_type=F32)
        o_ref[...] = y * xb[:, d:d + 1]

    @pl.when(pl.program_id(0) >= nu_ref[0])
    def _():
        o_ref[...] = jnp.zeros_like(o_ref)


def _moe_ffn(block_e, n_used, buf, layer, wg, wu, wd):
    nrow, dw = buf.shape
    d = dw - LANES
    nblk = nrow // FFN_ROWS
    de = wg.shape[3]
    used = lambda i, nu: jnp.minimum(i, nu[0] - 1)
    grid_spec = pltpu.PrefetchScalarGridSpec(
        num_scalar_prefetch=2,
        grid=(nblk,),
        in_specs=[pl.BlockSpec((FFN_ROWS, dw), lambda i, be, nu: (used(i, nu), 0)),
                  pl.BlockSpec((None, None, d, de), lambda i, be, nu: (layer, be[i], 0, 0)),
                  pl.BlockSpec((None, None, d, de), lambda i, be, nu: (layer, be[i], 0, 0)),
                  pl.BlockSpec((None, None, de, d), lambda i, be, nu: (layer, be[i], 0, 0))],
        out_specs=pl.BlockSpec((FFN_ROWS, d), lambda i, be, nu: (i, 0)),
        scratch_shapes=[pltpu.VMEM((d, de), BF16), pltpu.VMEM((d, de), BF16), pltpu.VMEM((de, d), BF16)],
    )
    return pl.pallas_call(
        _moe_kernel,
        grid_spec=grid_spec,
        out_shape=jax.ShapeDtypeStruct((nrow, d), F32),
        compiler_params=pltpu.CompilerParams(dimension_semantics=("arbitrary",), vmem_limit_bytes=FFN_VMEM_LIMIT),
        name="moe_ffn",
    )(block_e, n_used, buf, wg, wu, wd)


def _ln2_kernel(gb_ref, ob_ref, nb_ref, x_ref, qpos_ref, mod_ref, lg_ref, lb_ref, y_ref, *rest, with_inproj):
    if with_inproj:
        modn_ref, win_ref, o_ref, p_ref, ys_ref, sem = rest
    else:
        o_ref, ys_ref, sem = rest
    i = pl.program_id(0)
    slot = i % 2

    def copy_to(s):
        def copy(tile_row, buf_row, rows):
            return pltpu.make_async_copy(y_ref.at[pl.ds(buf_row, rows)], ys_ref.at[s, pl.ds(tile_row, rows)],
                                         sem.at[s])
        return copy

    @pl.when(i == 0)
    def _():
        ys_ref[...] = jnp.zeros_like(ys_ref)
        _start_run_copies(0, gb_ref, ob_ref, nb_ref, copy_to(0))

    @pl.when(i + 1 < pl.num_programs(0))
    def _():
        _start_run_copies(i + 1, gb_ref, ob_ref, nb_ref, copy_to(1 - slot))

    _wait_run_copies(i, nb_ref, copy_to(slot))
    m0, m1 = _sort_select(qpos_ref)
    sel = jnp.logical_or(m0, m1).astype(BF16)
    ys = ys_ref[slot].astype(BF16)

    def stream(rows):
        y = lax.dot_general(sel[:, rows], ys, (((0,), (0,)), ((), ())), preferred_element_type=F32)
        yield
        gate = mod_ref[0, 5:6, :]
        x_new = _ln(ALPHA * x_ref[rows, :] + gate * y, lg_ref[...], lb_ref[...])
        o_ref[rows, :] = x_new
        yield
        if with_inproj:
            u = (x_new * (1.0 + modn_ref[0, 1:2, :]) + modn_ref[0, 0:1, :]).astype(BF16)
            p_ref[rows, :] = jnp.dot(u, win_ref[...], preferred_element_type=F32).astype(p_ref.dtype)

    _alternate(*[stream(slice(r, r + ROW_STREAM)) for r in range(0, TM, ROW_STREAM)])


def _ln2(gb, ob, nb8, x2, qpos, mod, lg, lb, ybuf, tps, nb, next_in):
    ntok, d = x2.shape
    latent_only = next_in is None
    if latent_only:
        out_map = lambda i, *_: ((i // tps) * (tps - 1) + jnp.maximum(i % tps - 1, 0), 0)
        out_rows = ntok - nb * TM
    else:
        out_map = lambda i, *_: (i, 0)
        out_rows = ntok
    in_specs = [pl.BlockSpec((TM, d), lambda i, *_: (i, 0)),
                pl.BlockSpec((1, TOP_K, TM), lambda i, *_: (i, 0, 0)),
                pl.BlockSpec((1, 6, d), lambda i, *_: (_mod_row(i, tps, nb), 0, 0)),
                pl.BlockSpec((1, d), lambda i, *_: (0, 0)),
                pl.BlockSpec((1, d), lambda i, *_: (0, 0)),
                pl.BlockSpec(memory_space=pl.ANY)]
    out_specs = [pl.BlockSpec((TM, d), out_map)]
    out_shape = [jax.ShapeDtypeStruct((out_rows, d), F32)]
    args = [gb, ob, nb8, x2, qpos, mod, lg, lb, ybuf]
    if next_in is not None:
        mod_next, w_next = next_in
        nw = w_next.shape[1]
        in_specs += [pl.BlockSpec((1, 6, d), lambda i, *_: (_mod_row(i, tps, nb), 0, 0)),
                     pl.BlockSpec((d, nw), lambda i, *_: (0, 0))]
        out_specs.append(pl.BlockSpec((TM, nw), lambda i, *_: (i, 0)))
        out_shape.append(jax.ShapeDtypeStruct((ntok, nw), BF16))
        args += [mod_next, w_next]
    grid_spec = pltpu.PrefetchScalarGridSpec(
        num_scalar_prefetch=3,
        grid=(ntok // TM,),
        in_specs=in_specs,
        out_specs=out_specs,
        scratch_shapes=[pltpu.VMEM((2, SORT_ROWS, d), F32), pltpu.SemaphoreType.DMA((2,))],
    )
    outs = pl.pallas_call(
        functools.partial(_ln2_kernel, with_inproj=next_in is not None),
        grid_spec=grid_spec,
        out_shape=out_shape,
        compiler_params=_cparams(("arbitrary",)),
        name="post_moe_ln",
    )(*args)
    return (outs[0], outs[1]) if next_in is not None else (outs[0], None)


def _even_w_in(w):
    o = np.cumsum((0, GLA_QK, GLA_QK, GLA_V, GLA_V, GLA_RANK, GLA_RANK, GQA_Q, GQA_KV, GQA_KV))
    q, k, v, r = w[:, o[0]:o[1]], w[:, o[1]:o[2]], w[:, o[2]:o[3]], w[:, o[3]:o[4]]
    lr = w[:, o[4]:o[6]]
    gq, gk, gv = w[:, o[6]:o[7]], w[:, o[7]:o[8]], w[:, o[8]:o[9]]
    d = w.shape[0]
    zeros = lambda n: jnp.zeros((d, n), w.dtype)
    out = jnp.concatenate([gq, gk, gv, q, k, lr, zeros(LANES - 2 * GLA_RANK), zeros(LANES), v, r], axis=1)
    assert out.shape[1] == EVEN_W
    return out.astype(BF16)


def _odd_w_in(w):
    o = np.cumsum((0, SSD_INNER, SSD_INNER, SSD_BC, SSD_BC, SSD_HEADS, SSD_HEADS, FNET_W))
    d = w.shape[0]
    out = jnp.concatenate([w[:, o[1]:o[4]], w[:, :o[1]], w[:, o[4]:o[6]],
                           jnp.zeros((d, LANES - 2 * SSD_HEADS), w.dtype), w[:, o[6]:o[7]]], axis=1)
    assert out.shape[1] == ODD_W
    return out.astype(BF16)


def _rope_tables(ctx_len, seq):
    rows = seq // GRID_W
    r = jnp.repeat(jnp.arange(rows, dtype=F32), GRID_W)
    col = jnp.tile(jnp.arange(GRID_W, dtype=F32), rows)
    half = GQA_HD // 2
    inv = ROPE_THETA ** (-jnp.arange(0, half, 2, dtype=F32) / half)
    ar = r[:, None] * inv
    ac = col[:, None] * inv
    ang = jnp.concatenate([ar, ar, ac, ac], -1)
    cos = jnp.concatenate([jnp.ones((ctx_len, GQA_HD), F32), jnp.cos(ang)], 0)
    sin = jnp.concatenate([jnp.zeros((ctx_len, GQA_HD), F32), jnp.sin(ang)], 0)
    return jnp.tile(cos, (1, 2)), jnp.tile(sin, (1, 2))


def _even_layer(x2, p2, mod, nb, t, tps, ctx_len, w_o, w_dec, b_dec, gla_g, qn_g, kn_g, cos2, sin2, lg, lb, rw_t, rb):
    p3 = p2.reshape(nb, t, EVEN_W)
    wd_pads = [jnp.zeros((LANES, GLA_QK), F32).at[di * GLA_RANK:(di + 1) * GLA_RANK].set(w_dec[di])
               for di in range(2)]
    outs = _gla_scan(p3, wd_pads, [b_dec[di][None, :] for di in range(2)])
    qh2, kg2, vt4 = _qkprep(p2, cos2, sin2, jnp.tile(qn_g, 2)[None, :], jnp.tile(kn_g, 2)[None, :], tps)
    att = _attention(qh2.reshape(nb, t, GQA_HEADS * LANES), kg2.reshape(nb, t, GQA_KV_HEADS * LANES), vt4, ctx_len)
    w1, w2 = w_o[:GLA_V].astype(BF16), w_o[GLA_V:].astype(BF16)
    ntok = nb * t
    return _evout(outs[0].reshape(ntok, GLA_V), outs[1].reshape(ntok, GLA_V), p2, att.reshape(ntok, GQA_Q),
                  jnp.tile(gla_g, GLA_HEADS)[None, :], w1, w2, x2, mod, lg, lb, rw_t, rb, tps, nb)


def _odd_layer(x2, p2, mod, nb, t, tps, ctx_len, w_o, conv_w, conv_b, dt_bias, a_log, d_skip, ssd_g, fnet_g,
               lg, lb, rw_t, rb):
    p3 = p2.reshape(nb, t, ODD_W)
    w8 = jnp.zeros((8, SSD_CONV_CH), F32).at[:SSD_CONV].set(conv_w)
    xbc2 = _conv(p2, w8, conv_b[None, :], tps)
    xbc3 = xbc2.reshape(nb, t, SSD_CONV_CH)
    params = []
    for di in range(2):
        lo = di * SSD_HEADS
        pad = (lo, LANES - lo - SSD_HEADS)
        bias_pad = jnp.pad(dt_bias[di], pad)[None, :]
        a_pad = jnp.pad(-jnp.exp(a_log[di]), pad)[None, :]
        eexp = np.zeros((LANES, SSD_INNER), np.float32)
        eexp[lo + np.arange(SSD_INNER) // SSD_HD, np.arange(SSD_INNER)] = 1.0
        esel = np.zeros((16, LANES), np.float32)
        esel[np.arange(SSD_HEADS), lo + np.arange(SSD_HEADS)] = 1.0
        params.append((bias_pad, a_pad, jnp.asarray(eexp), jnp.asarray(esel)))
    ys = _ssd_scan(xbc3, p3, params)
    f3 = p3[:, :, ODD_W - FNET_W:]
    g256 = fnet_g[None, :]
    fmix = jnp.concatenate([_fourier_ctx(f3[:, :ctx_len], g256), _fourier_latent(f3[:, ctx_len:], g256)], axis=1)
    ntok = nb * t
    return _odout(ys[0].reshape(ntok, SSD_INNER), ys[1].reshape(ntok, SSD_INNER), xbc2, p2,
                  fmix.reshape(ntok, FNET_W), jnp.repeat(d_skip, SSD_HD)[None, :], ssd_g[None, :],
                  w_o[:SSD_INNER].astype(BF16), w_o[SSD_INNER:].astype(BF16), x2, mod, lg, lb, rw_t, rb, tps, nb)


def _moe_layer(routed, mod, nb, tps, layer, wg, wu, wd, lg, lb, next_in):
    x2, v, idx_t, gw_t, cnt = routed
    ntok, d = x2.shape
    nt = ntok // TM
    rows = cnt[:, 0].astype(jnp.int32)
    padded = (rows + FFN_ROWS - 1) // FFN_ROWS * FFN_ROWS
    pad_end = jnp.cumsum(padded)
    pad_start = pad_end - padded
    nblk = -(-(ntok * TOP_K + nt * N_EXPERTS * (ROWG - 1)) // FFN_ROWS) + N_EXPERTS
    blk_row = jnp.arange(nblk, dtype=jnp.int32)[:, None] * FFN_ROWS
    block_e = jnp.minimum(jnp.sum((blk_row >= pad_end[None, :]).astype(jnp.int32), axis=1), N_EXPERTS - 1)
    n_used = (pad_end[-1:] // FFN_ROWS).astype(jnp.int32)
    start = jnp.broadcast_to(pad_start.astype(F32)[:, None], (N_EXPERTS, LANES))
    qpos, meta = _meta(idx_t, start)
    gb, ob, nb8 = (meta[:, :, j].reshape(nt * N_EXPERTS) for j in range(3))
    tail_s = ((pad_start + rows) // ROWG).astype(jnp.int32)
    tail_n = ((padded - rows) // ROWG).astype(jnp.int32)
    buf = _scatter(gb, ob, nb8, tail_s, tail_n, n_used, v, qpos, gw_t, nblk * FFN_ROWS)
    y = _moe_ffn(block_e, n_used, buf, layer, wg, wu, wd)
    return _ln2(gb, ob, nb8, x2, qpos, mod, lg, lb, y, tps, nb, next_in)


def kernel(x, c, ctx, c_ctx, ada_w, ada_b, ln_g, ln_b, ev_w_in, ev_w_o, gla_w_decay, gla_b_decay, gla_norm_g,
           gqa_q_norm_g, gqa_k_norm_g, od_w_in, od_w_o, ssd_conv_w, ssd_conv_b, ssd_dt_bias, ssd_a_log, ssd_d,
           ssd_norm_g, fnet_norm_g, router_w, router_b, exp_w_gate, exp_w_up, exp_w_down):
    nb, seq, d = x.shape
    ctx_len = ctx.shape[1]
    assert ctx_len == TM and seq % TM == 0 and d == D_MODEL and nb <= 7
    t = ctx_len + seq
    tps = t // TM
    ntok = nb * t
    cc = jnp.zeros((8, d), F32).at[:nb].set(c).at[nb].set(c_ctx)
    mod_all = _ada_all(cc, ada_w, ada_b).reshape(DEPTH, 8, 6, d)
    cos2, sin2 = _rope_tables(ctx_len, seq)
    rw_t = router_w.T
    rb = jnp.broadcast_to(router_b[:, None], (N_EXPERTS, LANES))
    x2 = jnp.concatenate([ctx, x], axis=1).reshape(ntok, d)

    def w_in_of(layer):
        return _even_w_in(ev_w_in[layer // 2]) if layer % 2 == 0 else _odd_w_in(od_w_in[layer // 2])

    p2 = _inproj(x2, mod_all[0], w_in_of(0), tps, nb)
    for layer in range(DEPTH):
        mod = mod_all[layer]
        i = layer // 2
        lg0, lb0 = ln_g[layer, 0][None, :], ln_b[layer, 0][None, :]
        lg1, lb1 = ln_g[layer, 1][None, :], ln_b[layer, 1][None, :]
        if layer % 2 == 0:
            routed = _even_layer(x2, p2, mod, nb, t, tps, ctx_len, ev_w_o[i], gla_w_decay[i], gla_b_decay[i],
                                 gla_norm_g[i], gqa_q_norm_g[i], gqa_k_norm_g[i], cos2, sin2, lg0, lb0, rw_t, rb)
        else:
            routed = _odd_layer(x2, p2, mod, nb, t, tps, ctx_len, od_w_o[i], ssd_conv_w[i], ssd_conv_b[i],
                                ssd_dt_bias[i], ssd_a_log[i], ssd_d[i], ssd_norm_g[i], fnet_norm_g[i], lg0, lb0,
                                rw_t, rb)
        next_in = (mod_all[layer + 1], w_in_of(layer + 1)) if layer + 1 < DEPTH else None
        x2, p2 = _moe_layer(routed, mod, nb, tps, layer, exp_w_gate, exp_w_up, exp_w_down, lg1, lb1, next_in)
    return x2.reshape(nb, seq, d)
```

```python
import functools
import itertools
import math

import jax
import jax.numpy as jnp
import numpy as np
from jax import lax
from jax.experimental import pallas as pl
from jax.experimental.pallas import tpu as pltpu

F32 = jnp.float32
BF16 = jnp.bfloat16
HI = lax.Precision.HIGHEST

D_MODEL = 1024
DEPTH = 4
GRID_W = 64
GLA_HEADS, GLA_DK, GLA_DV, GLA_RANK, GLA_TAU, GLA_CHUNK = 4, 64, 128, 16, 16.0, 64
GQA_HEADS, GQA_KV_HEADS, GQA_HD = 8, 2, 64
ROPE_THETA = 10000.0
SSD_HEADS, SSD_HD, SSD_GROUPS, SSD_STATE, SSD_CONV, SSD_CHUNK = 12, 64, 2, 128, 5, 64
SSD_HPG = SSD_HEADS // SSD_GROUPS
SSD_INNER = SSD_HEADS * SSD_HD
SSD_BC = SSD_GROUPS * SSD_STATE
SSD_CONV_CH = SSD_INNER + 2 * SSD_BC
FNET_GROUPS, FNET_GC = 4, 64
FNET_W = FNET_GROUPS * FNET_GC
N_EXPERTS, N_EXPERT_GROUPS, TOP_K, D_EXPERT = 16, 4, 2, 768
EXPERTS_PER_GROUP = N_EXPERTS // N_EXPERT_GROUPS
GLA_QK = GLA_HEADS * GLA_DK
GLA_V = GLA_HEADS * GLA_DV
GQA_Q = GQA_HEADS * GQA_HD
GQA_KV = GQA_KV_HEADS * GQA_HD
EPS = 1e-6
ALPHA = (2.0 * DEPTH) ** 0.25

LANES = 128
TM = 256
EVEN_W = 2560
ODD_W = 2432
VMEM_LIMIT = 48 * 1024 * 1024
LOG2E = 1.4426950408889634
ATT_TQ = 256
ATT_HEADROOM = 64.0
ATT_MIN_DENOM = 2.0 ** -60
ATT_VROWS = LANES
ROWG = 8
SORT_ROWS = TOP_K * TM + N_EXPERTS * (ROWG - 1)
ROW_STREAM = 128
COPY_SHIFT = 2
COPY_GROUPS = 1 << COPY_SHIFT
FFN_ROWS = 512
FFN_VMEM_LIMIT = 56 * 1024 * 1024


def _cparams(sem):
    return pltpu.CompilerParams(dimension_semantics=sem, vmem_limit_bytes=VMEM_LIMIT)


def _silu(x):
    return x * (1.0 / (1.0 + jnp.exp(-x)))


def _softplus(x):
    return jnp.maximum(x, 0.0) + jnp.log1p(jnp.exp(-jnp.abs(x)))


def _mod_row(i, tiles_per_seq, n_batch):
    return jnp.where(i % tiles_per_seq == 0, n_batch, i // tiles_per_seq)


def _ada_kernel(c_ref, w_ref, b_ref, o_ref):
    s = _silu(c_ref[...])
    o_ref[0] = jnp.dot(s, w_ref[0], precision=HI, preferred_element_type=F32) + b_ref[0]


def _ada_all(cc, ada_w, ada_b):
    depth, d, n = ada_w.shape
    tn = 1536
    return pl.pallas_call(
        _ada_kernel,
        grid=(depth, n // tn),
        in_specs=[pl.BlockSpec((8, d), lambda l, j: (0, 0)),
                  pl.BlockSpec((1, d, tn), lambda l, j: (l, 0, j)),
                  pl.BlockSpec((1, 1, tn), lambda l, j: (l, 0, j))],
        out_specs=pl.BlockSpec((1, 8, tn), lambda l, j: (l, 0, j)),
        out_shape=jax.ShapeDtypeStruct((depth, 8, n), F32),
        compiler_params=_cparams(("arbitrary", "arbitrary")),
        name="adaln",
    )(cc, ada_w, ada_b.reshape(depth, 1, n))


def _inproj_kernel(x_ref, mod_ref, w_ref, o_ref):
    shift = mod_ref[0, 0:1, :]
    scale = mod_ref[0, 1:2, :]
    u = (x_ref[...] * (1.0 + scale) + shift).astype(BF16)
    o_ref[...] = jnp.dot(u, w_ref[...], preferred_element_type=F32).astype(o_ref.dtype)


def _inproj(x2, mod, w, tps, nb):
    ntok, d = x2.shape
    nw = w.shape[1]
    return pl.pallas_call(
        _inproj_kernel,
        grid=(ntok // TM,),
        in_specs=[pl.BlockSpec((TM, d), lambda i: (i, 0)),
                  pl.BlockSpec((1, 6, d), lambda i: (_mod_row(i, tps, nb), 0, 0)),
                  pl.BlockSpec((d, nw), lambda i: (0, 0))],
        out_specs=pl.BlockSpec((TM, nw), lambda i: (i, 0)),
        out_shape=jax.ShapeDtypeStruct((ntok, nw), BF16),
        compiler_params=_cparams(("arbitrary",)),
        name="inproj",
    )(x2, mod, w)


def _scan_block(i, nblk, reverse):
    if not reverse:
        return i
    return jnp.where(i == 0, 0, nblk - i)


def _tri(n, reverse):
    r = lax.broadcasted_iota(jnp.int32, (n, n), 0)
    c = lax.broadcasted_iota(jnp.int32, (n, n), 1)
    return (c >= r) if reverse else (c <= r)


def _chunk_sum_mat(chunk, reverse):
    l = np.arange(TM)[:, None]
    m = np.arange(TM)[None, :]
    same = (l // chunk) == (m // chunk)
    return (same & ((m >= l) if reverse else (m <= l))).astype(np.float32)


def _split3(x):
    hi = x.astype(BF16)
    r1 = x - hi.astype(F32)
    mid = r1.astype(BF16)
    return hi, mid, (r1 - mid.astype(F32)).astype(BF16)


def _chunk_rows(x, chunk, idx):
    return jnp.concatenate([jnp.broadcast_to(x[c * chunk + idx:c * chunk + idx + 1, :], (chunk, x.shape[1]))
                            for c in range(x.shape[0] // chunk)], axis=0)


def _gla_dir(q_ref, k_ref, v_ref, lr_ref, wd_ref, bd_ref, cm_ref, o_ref, s_ref, *, reverse):
    L = GLA_CHUNK
    nchunk = TM // L
    nt = (((1,), (1,)), ((), ()))
    tn = (((0,), (0,)), ((), ()))
    end = 0 if reverse else L - 1
    mid = L // 2 if reverse else L // 2 - 1

    @pl.when(pl.program_id(1) == 0)
    def _():
        s_ref[...] = jnp.zeros_like(s_ref)

    q = q_ref[...].astype(F32) * (GLA_DK ** -0.5)
    k = k_ref[...].astype(F32)
    v = v_ref[...]
    lr = lr_ref[...]
    z = bd_ref[...] + sum(jnp.dot(lr, wd_ref[j], preferred_element_type=F32) for j in range(3))
    yield
    la = -_softplus(-z) * (1.0 / GLA_TAU)
    cm = cm_ref[...]
    b = sum(jnp.dot(cm, part, preferred_element_type=F32) for part in _split3(la))
    yield
    bmid = _chunk_rows(b, L, mid)
    bend = _chunk_rows(b, L, end)
    qs = q * jnp.exp(b - bmid)
    ks = (k * jnp.exp(bmid - b)).astype(BF16)
    qi = q * jnp.exp(b)
    kend = (k * jnp.exp(bend - b)).astype(BF16)
    r = lax.broadcasted_iota(jnp.int32, (TM, TM), 0)
    c = lax.broadcasted_iota(jnp.int32, (TM, TM), 1)
    mask = jnp.logical_and(r // L == c // L, (c >= r) if reverse else (c <= r))
    lane = lax.broadcasted_iota(jnp.int32, (TM, LANES), 1)
    heads = (lane < GLA_DK, lane >= GLA_DK)
    yield
    att = [lax.dot_general(jnp.where(heads[h], qs, 0.0).astype(BF16), ks, nt, preferred_element_type=F32)
           for h in range(2)]
    yield
    intra = []
    for h in range(2):
        att_h = jnp.where(mask, att[h], 0.0).astype(BF16)
        intra.append(jnp.dot(att_h, v[:, h * GLA_DV:(h + 1) * GLA_DV], preferred_element_type=F32))
    yield
    qi_h = [jnp.where(heads[h], qi, 0.0).astype(BF16) for h in range(2)]
    tot = jnp.concatenate([b[ch * L + end:ch * L + end + 1, :] for ch in range(nchunk)]
                          + [jnp.zeros((ROWG - nchunk, LANES), F32)], axis=0).T
    ds, dec = [], []
    for ch in range(nchunk):
        rows = slice(ch * L, (ch + 1) * L)
        ds.append(lax.dot_general(kend[rows], v[rows], tn, preferred_element_type=F32))
        dec.append(jnp.exp(jnp.broadcast_to(tot[:, ch:ch + 1], (2 * GLA_DK, 2 * GLA_DV))))
    yield
    s = s_ref[...]
    inter = [None] * nchunk
    for ch in (range(nchunk - 1, -1, -1) if reverse else range(nchunk)):
        rows = slice(ch * L, (ch + 1) * L)
        s_bf = s.astype(BF16)
        inter[ch] = jnp.concatenate(
            [jnp.dot(qi_h[h][rows], s_bf[:, h * GLA_DV:(h + 1) * GLA_DV], preferred_element_type=F32)
             for h in range(2)], axis=1)
        s = dec[ch] * s + ds[ch]
        yield
    s_ref[...] = s
    o_ref[...] = (jnp.concatenate(intra, axis=1) + jnp.concatenate(inter, axis=0)).astype(o_ref.dtype)


def _alternate(*stage_generators):
    for _ in itertools.zip_longest(*stage_generators):
        pass


GLA_PAIRS = GLA_HEADS // 2


def _gla_kernel(*refs):
    n_streams = 2 * GLA_PAIRS
    n_in = (len(refs) - 2 - n_streams) // n_streams
    outs = refs[n_streams * n_in:n_streams * n_in + 2]
    states = refs[n_streams * n_in + 2:]
    scans = []
    for s in range(n_streams):
        reverse, pair = divmod(s, GLA_PAIRS)
        o_view = outs[reverse].at[:, pl.ds(pair * 2 * GLA_DV, 2 * GLA_DV)]
        scans.append(_gla_dir(*refs[s * n_in:(s + 1) * n_in], o_view, states[s], reverse=bool(reverse)))
    _alternate(*scans)


def _gla_scan(p3, wd_pads, bds):
    nb, t, _ = p3.shape
    nblk = t // TM
    in_specs, args, out_specs = [], [], []
    for reverse in (False, True):
        blk = functools.partial(_scan_block, nblk=nblk, reverse=reverse)
        wd3 = jnp.stack(_split3(wd_pads[int(reverse)]))
        cm = jnp.asarray(_chunk_sum_mat(GLA_CHUNK, reverse), BF16)
        for p in range(GLA_PAIRS):
            in_specs += [pl.BlockSpec((None, TM, LANES), lambda b, i, blk=blk, p=p: (b, blk(i), 6 + p)),
                         pl.BlockSpec((None, TM, LANES), lambda b, i, blk=blk, p=p: (b, blk(i), 8 + p)),
                         pl.BlockSpec((None, TM, 2 * GLA_DV), lambda b, i, blk=blk, p=p: (b, blk(i), 6 + p)),
                         pl.BlockSpec((None, TM, LANES), lambda b, i, blk=blk: (b, blk(i), 10)),
                         pl.BlockSpec((3, LANES, LANES), lambda b, i, p=p: (0, 0, p)),
                         pl.BlockSpec((1, LANES), lambda b, i, p=p: (0, p)),
                         pl.BlockSpec((TM, TM), lambda b, i: (0, 0))]
            args += [p3, p3, p3, p3, wd3, bds[int(reverse)], cm]
        out_specs.append(pl.BlockSpec((None, TM, GLA_V), lambda b, i, blk=blk: (b, blk(i), 0)))
    state = pltpu.VMEM((2 * GLA_DK, 2 * GLA_DV), F32)
    return pl.pallas_call(
        _gla_kernel,
        grid=(nb, nblk),
        in_specs=in_specs,
        out_specs=out_specs,
        out_shape=[jax.ShapeDtypeStruct((nb, t, GLA_V), BF16)] * 2,
        scratch_shapes=[state] * (2 * GLA_PAIRS),
        compiler_params=_cparams(("arbitrary", "arbitrary")),
        name="gla_scan",
    )(*args)


def _seg_ones(width, seg):
    r = lax.broadcasted_iota(jnp.int32, (width, width), 0) // seg
    c = lax.broadcasted_iota(jnp.int32, (width, width), 1) // seg
    return jnp.where(r == c, 1.0 / seg, 0.0).astype(BF16)


def _seg_mean(x, bd):
    hi = x.astype(BF16)
    lo = (x - hi.astype(F32)).astype(BF16)
    return jnp.dot(hi, bd, preferred_element_type=F32) + jnp.dot(lo, bd, preferred_element_type=F32)


def _norm_rope(x, g, cos, sin, bd):
    ms = _seg_mean(x * x, bd)
    xn = x * lax.rsqrt(ms + EPS) * g
    lane = lax.broadcasted_iota(jnp.int32, xn.shape, 1)
    quarter = GQA_HD // 4
    up = pltpu.roll(xn, LANES - quarter, 1)
    dn = pltpu.roll(xn, quarter, 1)
    rot = jnp.where(lane % (2 * quarter) < quarter, -up, dn)
    return xn * cos + rot * sin


def _qkprep_kernel(q_ref, k_ref, v_ref, cos_ref, sin_ref, gq_ref, gk_ref, qo_ref, ko_ref, vo_ref):
    bd = _seg_ones(LANES, GQA_HD)
    cos = cos_ref[...]
    sin = sin_ref[...]
    lane = lax.broadcasted_iota(jnp.int32, (TM, LANES), 1)
    low = lane < GQA_HD
    extra = lane == GQA_HD
    one_hot = jnp.where(extra, 1.0, 0.0)
    k_bound = math.sqrt(GQA_HD) * jnp.max(jnp.abs(gk_ref[...]), axis=-1, keepdims=True) * 1.02
    for j in range(GQA_Q // LANES):
        x = q_ref[:, j * LANES:(j + 1) * LANES].astype(F32)
        y = _norm_rope(x, gq_ref[...], cos, sin, bd) * (GQA_HD ** -0.5 * LOG2E)
        y = y.astype(BF16).astype(F32)
        norm = jnp.sqrt(_seg_mean(y * y, bd) * GQA_HD)
        shift = ATT_HEADROOM - norm * k_bound
        lo = jnp.where(low, y, jnp.where(extra, pltpu.roll(shift, GQA_HD, 1), 0.0))
        hi = jnp.where(low, pltpu.roll(y, GQA_HD, 1), jnp.where(extra, shift, 0.0))
        qo_ref[:, (2 * j) * LANES:(2 * j + 1) * LANES] = lo.astype(BF16)
        qo_ref[:, (2 * j + 1) * LANES:(2 * j + 2) * LANES] = hi.astype(BF16)
    k = _norm_rope(k_ref[...].astype(F32), gk_ref[...], cos, sin, bd)
    ko_ref[:, 0:LANES] = jnp.where(low, k, one_hot).astype(BF16)
    ko_ref[:, LANES:2 * LANES] = jnp.where(low, pltpu.roll(k, GQA_HD, 1), one_hot).astype(BF16)
    v = v_ref[...].astype(F32)
    vo_ref[0] = jnp.where(low, v, one_hot).T[0:ATT_VROWS].astype(BF16)
    vo_ref[1] = jnp.where(low, pltpu.roll(v, GQA_HD, 1), one_hot).T[0:ATT_VROWS].astype(BF16)


def _qkprep(p2, cos2, sin2, gq2, gk2, tps):
    ntok = p2.shape[0]
    nb = ntok // (tps * TM)
    return pl.pallas_call(
        _qkprep_kernel,
        grid=(ntok // TM,),
        in_specs=[pl.BlockSpec((TM, GQA_Q), lambda i: (i, 0)),
                  pl.BlockSpec((TM, LANES), lambda i: (i, 4)),
                  pl.BlockSpec((TM, LANES), lambda i: (i, 5)),
                  pl.BlockSpec((TM, LANES), lambda i: (i % tps, 0)),
                  pl.BlockSpec((TM, LANES), lambda i: (i % tps, 0)),
                  pl.BlockSpec((1, LANES), lambda i: (0, 0)),
                  pl.BlockSpec((1, LANES), lambda i: (0, 0))],
        out_specs=[pl.BlockSpec((TM, GQA_HEADS * LANES), lambda i: (i, 0)),
                   pl.BlockSpec((TM, GQA_KV_HEADS * LANES), lambda i: (i, 0)),
                   pl.BlockSpec((None, GQA_KV_HEADS, ATT_VROWS, TM), lambda i: (i // tps, 0, 0, i % tps))],
        out_shape=[jax.ShapeDtypeStruct((ntok, GQA_HEADS * LANES), BF16),
                   jax.ShapeDtypeStruct((ntok, GQA_KV_HEADS * LANES), BF16),
                   jax.ShapeDtypeStruct((nb, GQA_KV_HEADS, ATT_VROWS, tps * TM), BF16)],
        compiler_params=_cparams(("arbitrary",)),
        name="qkprep",
    )(p2, p2, p2, cos2, sin2, gq2, gk2)


def _attn_kernel(q_ref, k_ref, vt_ref, o_ref, *, ctx_len, n_lat_chunks, ck, n_blocks):
    i = pl.program_id(2)
    rep = GQA_HEADS // GQA_KV_HEADS
    is_ctx = i < ctx_len // ATT_TQ
    q4 = jnp.concatenate([q_ref[:, r * LANES:(r + 1) * LANES] for r in range(rep)], axis=0)
    nt = (((1,), (1,)), ((), ()))

    def chunk(acc, kc, vtc):
        st = lax.dot_general(kc, q4, nt, preferred_element_type=F32)
        return acc + jnp.dot(vtc, jnp.exp2(st).astype(BF16), preferred_element_type=F32)

    def store(acc):
        out = (acc * (1.0 / acc[GQA_HD:GQA_HD + 1, :])).T
        low = lax.broadcasted_iota(jnp.int32, (ATT_TQ, LANES), 1) < GQA_HD
        hs = [out[r * ATT_TQ:(r + 1) * ATT_TQ] for r in range(rep)]
        for u in range(rep // 2):
            o_ref[:, u * LANES:(u + 1) * LANES] = jnp.where(
                low, hs[2 * u], pltpu.roll(hs[2 * u + 1], GQA_HD, 1)).astype(o_ref.dtype)

    acc = chunk(jnp.zeros((ATT_VROWS, rep * ATT_TQ), F32), k_ref[0:ctx_len, :], vt_ref[:, 0:ctx_len])

    def body(c, acc):
        off = pl.multiple_of(ctx_len + c * ck, TM)
        return chunk(acc, k_ref[pl.ds(off, ck), :], vt_ref[:, pl.ds(off, ck)])

    acc = lax.fori_loop(0, jnp.where(is_ctx, 0, n_lat_chunks), body, acc)
    healthy = jnp.min(acc[GQA_HD:GQA_HD + 1, :]) >= ATT_MIN_DENOM

    @pl.when(healthy)
    def _():
        store(acc)

    @pl.when(jnp.logical_not(healthy))
    def _():
        def online(c, carry):
            m, acc = carry
            off = pl.multiple_of(c * TM, TM)
            st = lax.dot_general(k_ref[pl.ds(off, TM), :], q4, nt, preferred_element_type=F32)
            mn = jnp.maximum(m, jnp.max(st, axis=0, keepdims=True))
            pt = jnp.exp2(st - mn).astype(BF16)
            acc = jnp.exp2(m - mn) * acc + jnp.dot(vt_ref[:, pl.ds(off, TM)], pt, preferred_element_type=F32)
            return mn, acc

        init = (jnp.full((1, rep * ATT_TQ), -1e30, F32), jnp.zeros((ATT_VROWS, rep * ATT_TQ), F32))
        _, acc2 = lax.fori_loop(0, jnp.where(is_ctx, ctx_len // TM, n_blocks), online, init)
        store(acc2)


def _attention(qh3, kg3, vt4, ctx_len):
    nb, t, _ = qh3.shape
    seq = t - ctx_len
    ck = next(c for c in (4096, 2048, 1024, 512, 256) if seq % c == 0)
    rep = GQA_HEADS // GQA_KV_HEADS
    return pl.pallas_call(
        functools.partial(_attn_kernel, ctx_len=ctx_len, n_lat_chunks=seq // ck, ck=ck, n_blocks=t // TM),
        grid=(nb, GQA_KV_HEADS, t // ATT_TQ),
        in_specs=[pl.BlockSpec((None, ATT_TQ, rep * LANES), lambda b, g, i: (b, i, g)),
                  pl.BlockSpec((None, t, LANES), lambda b, g, i: (b, 0, g)),
                  pl.BlockSpec((None, None, ATT_VROWS, t), lambda b, g, i: (b, g, 0, 0))],
        out_specs=pl.BlockSpec((None, ATT_TQ, rep * GQA_HD), lambda b, g, i: (b, i, g)),
        out_shape=jax.ShapeDtypeStruct((nb, t, GQA_Q), BF16),
        compiler_params=_cparams(("arbitrary", "arbitrary", "arbitrary")),
        name="gqa_attn",
    )(qh3, kg3, vt4)


def _ln(x, g, b):
    mu = jnp.mean(x, axis=-1, keepdims=True)
    xc = x - mu
    var = jnp.mean(xc * xc, axis=-1, keepdims=True)
    return xc * lax.rsqrt(var + EPS) * g + b


def _evout_kernel(of_ref, ob_ref, r_ref, att_ref, g_ref, w1_ref, w2_ref, x_ref, mod_ref, lg_ref, lb_ref, rw_ref,
                  rb_ref, o_ref, v_ref, idx_ref, gw_ref, cnt_ref):
    def stream(rows, counts):
        o = of_ref[rows, :].astype(F32) + ob_ref[rows, :].astype(F32)
        r = r_ref[rows, :].astype(F32)
        parts = []
        for h in range(GLA_HEADS):
            oh = o[:, h * GLA_DV:(h + 1) * GLA_DV]
            ms = jnp.mean(oh * oh, axis=-1, keepdims=True)
            parts.append(oh * lax.rsqrt(ms + EPS))
        gl = (jnp.concatenate(parts, axis=1) * g_ref[...] * _silu(r)).astype(BF16)
        yield
        y = jnp.dot(gl, w1_ref[...], preferred_element_type=F32)
        y = y + jnp.dot(att_ref[rows, :], w2_ref[...], preferred_element_type=F32)
        yield
        gate = mod_ref[0, 2:3, :]
        x_new = _ln(ALPHA * x_ref[rows, :] + gate * y, lg_ref[...], lb_ref[...])
        o_ref[rows, :] = x_new
        yield
        yield from _route_rows(x_new, rows, mod_ref, rw_ref, rb_ref, v_ref, idx_ref, gw_ref, counts)

    _alternate_row_streams(stream, cnt_ref)


def _evout(of2, ob2, p2, att2, g512, w1, w2, x2, mod, lg, lb, rw_t, rb, tps, nb):
    ntok, d = x2.shape
    row = lambda i: (i, 0)
    const = lambda i: (0, 0)
    r_in, r_out, r_shapes = _router_specs(ntok, d)
    return pl.pallas_call(
        _evout_kernel,
        grid=(ntok // TM,),
        in_specs=[pl.BlockSpec((TM, GLA_V), row), pl.BlockSpec((TM, GLA_V), row),
                  pl.BlockSpec((TM, GLA_V), lambda i: (i, 4)),
                  pl.BlockSpec((TM, GQA_Q), row),
                  pl.BlockSpec((1, GLA_V), const),
                  pl.BlockSpec((GLA_V, d), const), pl.BlockSpec((GQA_Q, d), const),
                  pl.BlockSpec((TM, d), row),
                  pl.BlockSpec((1, 6, d), lambda i: (_mod_row(i, tps, nb), 0, 0)),
                  pl.BlockSpec((1, d), const), pl.BlockSpec((1, d), const)] + r_in,
        out_specs=[pl.BlockSpec((TM, d), row)] + r_out,
        out_shape=[jax.ShapeDtypeStruct((ntok, d), F32)] + r_shapes,
        compiler_params=_cparams(("arbitrary",)),
        name="even_out",
    )(of2, ob2, p2, att2, g512, w1, w2, x2, mod, lg, lb, rw_t, rb)


def _conv_kernel(prev_ref, cur_ref, next_ref, w_ref, b_ref, o_ref, pad_ref, *, tps):
    j = pl.program_id(0) % tps
    has_prev = (j >= 2).astype(F32)
    has_next = jnp.logical_and(j >= 1, j <= tps - 2).astype(F32)
    pad_ref[0:8, :] = prev_ref[...].astype(F32) * has_prev
    pad_ref[8:8 + TM, :] = cur_ref[...].astype(F32)
    pad_ref[8 + TM:16 + TM, :] = next_ref[...].astype(F32) * has_next
    half = (SSD_CONV - 1) // 2
    cw = 256
    for c0 in range(0, SSD_CONV_CH, cw):
        cols = slice(c0, c0 + cw)
        acc = jnp.zeros((TM, cw), F32) + b_ref[:, cols]
        for tap in range(SSD_CONV):
            acc = acc + pad_ref[pl.ds(8 + tap - half, TM), cols] * w_ref[tap:tap + 1, cols]
        o_ref[:, cols] = _silu(acc).astype(o_ref.dtype)


def _conv(p2, w8, bias, tps):
    ntok = p2.shape[0]
    cw = SSD_CONV_CH
    r8 = TM // 8
    nrow8 = ntok // 8
    return pl.pallas_call(
        functools.partial(_conv_kernel, tps=tps),
        grid=(ntok // TM,),
        in_specs=[pl.BlockSpec((8, cw), lambda i: (jnp.maximum(i * r8 - 1, 0), 0)),
                  pl.BlockSpec((TM, cw), lambda i: (i, 0)),
                  pl.BlockSpec((8, cw), lambda i: (jnp.minimum((i + 1) * r8, nrow8 - 1), 0)),
                  pl.BlockSpec((8, cw), lambda i: (0, 0)),
                  pl.BlockSpec((1, cw), lambda i: (0, 0))],
        out_specs=pl.BlockSpec((TM, cw), lambda i: (i, 0)),
        out_shape=jax.ShapeDtypeStruct((ntok, SSD_CONV_CH), BF16),
        scratch_shapes=[pltpu.VMEM((TM + 16, cw), F32)],
        compiler_params=_cparams(("arbitrary",)),
        name="ssd_conv",
    )(p2, p2, p2, w8, bias)


def _ssd_dir(xbc_ref, dt_ref, bias_ref, a_ref, eexp_ref, esel_ref, cm_ref, o_ref, h_ref, *, reverse):
    L = SSD_CHUNK
    GW = SSD_HPG * SSD_HD
    nchunk = TM // L
    nt = (((1,), (1,)), ((), ()))
    tn = (((0,), (0,)), ((), ()))
    end = 0 if reverse else L - 1
    lane0 = SSD_HEADS if reverse else 0

    @pl.when(pl.program_id(1) == 0)
    def _():
        h_ref[...] = jnp.zeros_like(h_ref)

    dt = _softplus(dt_ref[...].astype(F32) + bias_ref[...])
    a = dt * a_ref[...]
    cm = cm_ref[...]
    cum = sum(jnp.dot(cm, part, preferred_element_type=F32) for part in _split3(a))
    yield
    cum_t = sum(lax.dot_general(esel_ref[...], part, nt, preferred_element_type=F32)
                for part in _split3(cum))
    cend = _chunk_rows(cum, L, end)
    eexp = eexp_ref[...]

    def expand(m):
        return jnp.dot(m.astype(BF16), eexp, preferred_element_type=F32)

    dt_b = dt.astype(BF16)
    eend_b = jnp.exp(cend - cum).astype(BF16)
    ecum_b = jnp.exp(cum).astype(BF16)
    tot = jnp.concatenate([cum[ch * L + end:ch * L + end + 1, :] for ch in range(nchunk)]
                          + [jnp.zeros((ROWG - nchunk, LANES), F32)], axis=0)
    edec = jnp.exp(sum(jnp.dot(part, eexp, preferred_element_type=F32) for part in _split3(tot)))

    tri = _tri(L, reverse)
    low = lax.broadcasted_iota(jnp.int32, (L, LANES), 1) < SSD_HD
    intra = [[None] * SSD_GROUPS for _ in range(nchunk)]
    dh = [[None] * SSD_GROUPS for _ in range(nchunk)]
    for ch in range(nchunk):
        rows = slice(ch * L, (ch + 1) * L)
        xdt = xbc_ref[rows, 0:SSD_INNER].astype(F32) * expand(dt_b[rows])
        xdt_b = xdt.astype(BF16)
        xend = (xdt * expand(eend_b[rows])).astype(BF16)
        yield
        for g in range(SSD_GROUPS):
            bg = xbc_ref[rows, SSD_INNER + g * SSD_STATE:SSD_INNER + (g + 1) * SSD_STATE]
            cg = xbc_ref[rows, SSD_INNER + SSD_BC + g * SSD_STATE:SSD_INNER + SSD_BC + (g + 1) * SSD_STATE]
            cb = lax.dot_general(cg, bg, nt, preferred_element_type=F32)
            pairs = []
            for pp in range(SSD_HPG // 2):
                yh = []
                for u in range(2):
                    h = g * SSD_HPG + 2 * pp + u
                    seg = cum[rows, lane0 + h:lane0 + h + 1] - cum_t[h:h + 1, rows]
                    dec = jnp.exp(jnp.where(tri, seg, -1e30))
                    mat = (cb * dec).astype(BF16)
                    col = (h - u) * SSD_HD
                    yh.append(jnp.dot(mat, xdt_b[:, col:col + LANES], preferred_element_type=F32))
                pairs.append(jnp.where(low, yh[0], yh[1]))
                yield
            intra[ch][g] = jnp.concatenate(pairs, axis=1)
            dh[ch][g] = lax.dot_general(bg, xend[:, g * GW:(g + 1) * GW], tn, preferred_element_type=F32)

    hs = [h_ref[g] for g in range(SSD_GROUPS)]
    for ch in (range(nchunk - 1, -1, -1) if reverse else range(nchunk)):
        rows = slice(ch * L, (ch + 1) * L)
        ecum = expand(ecum_b[rows])
        ys = []
        for g in range(SSD_GROUPS):
            cg = xbc_ref[rows, SSD_INNER + SSD_BC + g * SSD_STATE:SSD_INNER + SSD_BC + (g + 1) * SSD_STATE]
            y_inter = jnp.dot(cg, hs[g].astype(BF16), preferred_element_type=F32) * ecum[:, g * GW:(g + 1) * GW]
            ys.append(intra[ch][g] + y_inter)
            hs[g] = hs[g] * edec[ch:ch + 1, g * GW:(g + 1) * GW] + dh[ch][g]
        o_ref[rows, :] = jnp.concatenate(ys, axis=1).astype(o_ref.dtype)
        yield
    for g in range(SSD_GROUPS):
        h_ref[g] = hs[g]


def _ssd_kernel(*refs):
    n_in = (len(refs) - 4) // 2
    o_f, o_b, h_f, h_b = refs[2 * n_in:]
    _alternate(_ssd_dir(*refs[:n_in], o_f, h_f, reverse=False),
               _ssd_dir(*refs[n_in:2 * n_in], o_b, h_b, reverse=True))


def _ssd_scan(xbc3, p3, params):
    nb, t, _ = xbc3.shape
    nblk = t // TM
    const = lambda b, i: (0, 0)
    in_specs, args, out_specs = [], [], []
    for reverse in (False, True):
        blk = functools.partial(_scan_block, nblk=nblk, reverse=reverse)
        bias_pad, a_pad, eexp, esel = params[int(reverse)]
        in_specs += [pl.BlockSpec((None, TM, SSD_CONV_CH), lambda b, i, blk=blk: (b, blk(i), 0)),
                     pl.BlockSpec((None, TM, LANES), lambda b, i, blk=blk: (b, blk(i), 16)),
                     pl.BlockSpec((1, LANES), const), pl.BlockSpec((1, LANES), const),
                     pl.BlockSpec((LANES, SSD_INNER), const), pl.BlockSpec((16, LANES), const),
                     pl.BlockSpec((TM, TM), const)]
        args += [xbc3, p3, bias_pad, a_pad, eexp.astype(BF16), esel.astype(BF16),
                 jnp.asarray(_chunk_sum_mat(SSD_CHUNK, reverse), BF16)]
        out_specs.append(pl.BlockSpec((None, TM, SSD_INNER), lambda b, i, blk=blk: (b, blk(i), 0)))
    state = pltpu.VMEM((SSD_GROUPS, SSD_STATE, SSD_HPG * SSD_HD), F32)
    return pl.pallas_call(
        _ssd_kernel,
        grid=(nb, nblk),
        in_specs=in_specs,
        out_specs=out_specs,
        out_shape=[jax.ShapeDtypeStruct((nb, t, SSD_INNER), BF16)] * 2,
        scratch_shapes=[state, state],
        compiler_params=_cparams(("arbitrary", "arbitrary")),
        name="ssd_scan",
    )(*args)


def _dft_mats(n):
    k = jnp.arange(n, dtype=jnp.int32)
    ang = ((k[:, None] * k[None, :]) % n).astype(F32) * (2.0 * math.pi / n)
    return jnp.cos(ang), jnp.sin(ang)


def _chan_mats():
    cc, sc = _dft_mats(FNET_GC)
    eye = jnp.eye(FNET_GROUPS, dtype=F32)
    return jnp.kron(eye, cc), jnp.kron(eye, sc)


def _group_rms(x, g, bd):
    return x * lax.rsqrt(_seg_mean(x * x, bd) + EPS) * g


def _split2(x):
    hi = x.astype(BF16)
    return hi, (x - hi.astype(F32)).astype(BF16)


def _mm(a, b):
    a_hi, a_lo = _split2(a)
    b_hi, b_lo = _split2(b)
    dot = functools.partial(jnp.dot, preferred_element_type=F32)
    return dot(a_hi, b_hi) + (dot(a_hi, b_lo) + dot(a_lo, b_hi))


def _ffta_kernel(x_ref, g_ref, cc_ref, sc_ref, c1_ref, s1_ref, twc_ref, tws_ref, yr_ref, yi_ref, *, nb2):
    bd = _seg_ones(FNET_W, FNET_GC)
    for j in range(nb2):
        x = x_ref[:, j * FNET_W:(j + 1) * FNET_W].astype(F32)
        xn = _group_rms(x, g_ref[...], bd)
        vr = _mm(xn, cc_ref[...])
        vi = -_mm(xn, sc_ref[...])
        c1 = c1_ref[...]
        s1 = s1_ref[...]
        yr = _mm(c1, vr) + _mm(s1, vi)
        yi = _mm(c1, vi) - _mm(s1, vr)
        tc = twc_ref[j]
        ts = tws_ref[j]
        yr_ref[j] = yr * tc + yi * ts
        yi_ref[j] = yi * tc - yr * ts


def _fftb_kernel(yr_ref, yi_ref, c2_ref, s2_ref, o_ref, *, scale):
    o_ref[...] = (_mm(c2_ref[...], yr_ref[...]) + _mm(s2_ref[...], yi_ref[...])) * scale


def _fftc_kernel(x_ref, g_ref, cc_ref, sc_ref, ct_ref, st_ref, o_ref, *, scale):
    bd = _seg_ones(FNET_W, FNET_GC)
    xn = _group_rms(x_ref[...].astype(F32), g_ref[...], bd)
    a = _mm(xn, cc_ref[...])
    b = _mm(xn, sc_ref[...])
    o_ref[...] = (_mm(ct_ref[...], a) - _mm(st_ref[...], b)) * scale


def _fourier_latent(f_lat, g256):
    nb, s, _ = f_lat.shape
    n2 = 64
    n1 = s // n2
    nb2 = 8
    ccm, scm = _chan_mats()
    c1, s1 = _dft_mats(n1)
    c2, s2 = _dft_mats(n2)
    t2 = jnp.arange(n2, dtype=jnp.int32)[:, None]
    k1 = jnp.arange(n1, dtype=jnp.int32)[None, :]
    tw = ((t2 * k1) % s).astype(F32) * (2.0 * math.pi / s)
    twc = jnp.cos(tw)[:, :, None]
    tws = jnp.sin(tw)[:, :, None]
    x2 = f_lat.reshape(nb, n1, n2 * FNET_W)
    const2 = lambda b, j: (0, 0)
    yr, yi = pl.pallas_call(
        functools.partial(_ffta_kernel, nb2=nb2),
        grid=(nb, n2 // nb2),
        in_specs=[pl.BlockSpec((None, n1, nb2 * FNET_W), lambda b, j: (b, 0, j)),
                  pl.BlockSpec((1, FNET_W), const2),
                  pl.BlockSpec((FNET_W, FNET_W), const2), pl.BlockSpec((FNET_W, FNET_W), const2),
                  pl.BlockSpec((n1, n1), const2), pl.BlockSpec((n1, n1), const2),
                  pl.BlockSpec((nb2, n1, 1), lambda b, j: (j, 0, 0)),
                  pl.BlockSpec((nb2, n1, 1), lambda b, j: (j, 0, 0))],
        out_specs=[pl.BlockSpec((None, nb2, n1, FNET_W), lambda b, j: (b, j, 0, 0)),
                   pl.BlockSpec((None, nb2, n1, FNET_W), lambda b, j: (b, j, 0, 0))],
        out_shape=[jax.ShapeDtypeStruct((nb, n2, n1, FNET_W), F32),
                   jax.ShapeDtypeStruct((nb, n2, n1, FNET_W), F32)],
        compiler_params=_cparams(("arbitrary", "arbitrary")),
        name="fft_a",
    )(x2, g256, ccm, scm, c1, s1, twc, tws)
    ncol = n1 * FNET_W
    tn = min(8192, ncol)
    out = pl.pallas_call(
        functools.partial(_fftb_kernel, scale=1.0 / math.sqrt(s * FNET_GC)),
        grid=(nb, ncol // tn),
        in_specs=[pl.BlockSpec((None, n2, tn), lambda b, j: (b, 0, j)),
                  pl.BlockSpec((None, n2, tn), lambda b, j: (b, 0, j)),
                  pl.BlockSpec((n2, n2), const2), pl.BlockSpec((n2, n2), const2)],
        out_specs=pl.BlockSpec((None, n2, tn), lambda b, j: (b, 0, j)),
        out_shape=jax.ShapeDtypeStruct((nb, n2, ncol), F32),
        compiler_params=_cparams(("arbitrary", "arbitrary")),
        name="fft_b",
    )(yr.reshape(nb, n2, ncol), yi.reshape(nb, n2, ncol), c2, s2)
    return out.reshape(nb, s, FNET_W)


def _fourier_ctx(f_ctx, g256):
    nb, tc, _ = f_ctx.shape
    ccm, scm = _chan_mats()
    ct, st = _dft_mats(tc)
    const = lambda b: (0, 0)
    return pl.pallas_call(
        functools.partial(_fftc_kernel, scale=1.0 / math.sqrt(tc * FNET_GC)),
        grid=(nb,),
        in_specs=[pl.BlockSpec((None, tc, FNET_W), lambda b: (b, 0, 0)),
                  pl.BlockSpec((1, FNET_W), const),
                  pl.BlockSpec((FNET_W, FNET_W), const), pl.BlockSpec((FNET_W, FNET_W), const),
                  pl.BlockSpec((tc, tc), const), pl.BlockSpec((tc, tc), const)],
        out_specs=pl.BlockSpec((None, tc, FNET_W), lambda b: (b, 0, 0)),
        out_shape=jax.ShapeDtypeStruct((nb, tc, FNET_W), F32),
        compiler_params=_cparams(("arbitrary",)),
        name="fft_ctx",
    )(f_ctx, g256, ccm, scm, ct, st)


def _odout_kernel(yf_ref, yb_ref, xs_ref, z0_ref, z1_ref, z2_ref, f_ref, dsk_ref, g_ref, w1_ref, w2_ref, x_ref,
                  mod_ref, lg_ref, lb_ref, rw_ref, rb_ref, o_ref, v_ref, idx_ref, gw_ref, cnt_ref):
    GW = SSD_HPG * SSD_HD

    def stream(rows, counts):
        y = yf_ref[rows, :].astype(F32) + yb_ref[rows, :].astype(F32) + xs_ref[rows, :].astype(F32) * dsk_ref[...]
        z = jnp.concatenate([z0_ref[rows, :], z1_ref[rows, :], z2_ref[rows, :]], axis=1).astype(F32)
        y = y * _silu(z)
        parts = []
        for g in range(SSD_GROUPS):
            yg = y[:, g * GW:(g + 1) * GW]
            ms = jnp.mean(yg * yg, axis=-1, keepdims=True)
            parts.append(yg * lax.rsqrt(ms + EPS))
        yn = (jnp.concatenate(parts, axis=1) * g_ref[...]).astype(BF16)
        yield
        o = jnp.dot(yn, w1_ref[...], preferred_element_type=F32)
        o = o + jnp.dot(f_ref[rows, :].astype(BF16), w2_ref[...], preferred_element_type=F32)
        yield
        gate = mod_ref[0, 2:3, :]
        x_new = _ln(ALPHA * x_ref[rows, :] + gate * o, lg_ref[...], lb_ref[...])
        o_ref[rows, :] = x_new
        yield
        yield from _route_rows(x_new, rows, mod_ref, rw_ref, rb_ref, v_ref, idx_ref, gw_ref, counts)

    _alternate_row_streams(stream, cnt_ref)


def _odout(yf2, yb2, xbc2, p2, f2, dsk, g768, w1, w2, x2, mod, lg, lb, rw_t, rb, tps, nb):
    ntok, d = x2.shape
    row = lambda i: (i, 0)
    const = lambda i: (0, 0)
    r_in, r_out, r_shapes = _router_specs(ntok, d)
    zw = 256
    return pl.pallas_call(
        _odout_kernel,
        grid=(ntok // TM,),
        in_specs=[pl.BlockSpec((TM, SSD_INNER), row), pl.BlockSpec((TM, SSD_INNER), row),
                  pl.BlockSpec((TM, SSD_INNER), row)]
        + [pl.BlockSpec((TM, zw), lambda i, j=j: (i, SSD_CONV_CH // zw + j)) for j in range(SSD_INNER // zw)]
        + [pl.BlockSpec((TM, FNET_W), row),
                  pl.BlockSpec((1, SSD_INNER), const), pl.BlockSpec((1, SSD_INNER), const),
                  pl.BlockSpec((SSD_INNER, d), const), pl.BlockSpec((FNET_W, d), const),
                  pl.BlockSpec((TM, d), row),
                  pl.BlockSpec((1, 6, d), lambda i: (_mod_row(i, tps, nb), 0, 0)),
                  pl.BlockSpec((1, d), const), pl.BlockSpec((1, d), const)] + r_in,
        out_specs=[pl.BlockSpec((TM, d), row)] + r_out,
        out_shape=[jax.ShapeDtypeStruct((ntok, d), F32)] + r_shapes,
        compiler_params=_cparams(("arbitrary",)),
        name="odd_out",
    )(yf2, yb2, xbc2, *([p2] * (SSD_INNER // zw)), f2, dsk, g768, w1, w2, x2, mod, lg, lb, rw_t, rb)


def _route_rows(x, rows, mod_ref, rw_ref, rb_ref, v_ref, idx_ref, gw_ref, counts):
    n = x.shape[0]
    shift = mod_ref[0, 3:4, :]
    scale = mod_ref[0, 4:5, :]
    v = x * (1.0 + scale) + shift
    v_hi = v.astype(BF16)
    v_ref[rows, :] = v_hi
    v_lo = (v - v_hi.astype(F32)).astype(BF16)
    w_hi, w_lo = _split2(rw_ref[...])
    nt_dot = functools.partial(lax.dot_general, dimension_numbers=(((1,), (1,)), ((), ())),
                               preferred_element_type=F32)
    logits = nt_dot(w_hi, v_hi) + (nt_dot(w_hi, v_lo) + nt_dot(w_lo, v_hi))
    yield
    s = 1.0 / (1.0 + jnp.exp(-logits))
    sel = s + rb_ref[:, 0:1]
    izero = jnp.zeros((1, n), jnp.int32)
    best = None
    for g in range(N_EXPERT_GROUPS):
        a = [sel[g * EXPERTS_PER_GROUP + j:g * EXPERTS_PER_GROUP + j + 1, :] for j in range(EXPERTS_PER_GROUP)]
        sv = [s[g * EXPERTS_PER_GROUP + j:g * EXPERTS_PER_GROUP + j + 1, :] for j in range(EXPERTS_PER_GROUP)]
        m1, i1, s1 = a[0], izero, sv[0]
        for j in range(1, EXPERTS_PER_GROUP):
            gt = a[j] > m1
            m1 = jnp.where(gt, a[j], m1)
            i1 = jnp.where(gt, j, i1)
            s1 = jnp.where(gt, sv[j], s1)
        m2 = jnp.full((1, n), -jnp.inf, F32)
        i2, s2 = izero, jnp.zeros((1, n), F32)
        for j in range(EXPERTS_PER_GROUP):
            gt = jnp.logical_and(i1 != j, a[j] > m2)
            m2 = jnp.where(gt, a[j], m2)
            i2 = jnp.where(gt, j, i2)
            s2 = jnp.where(gt, sv[j], s2)
        cand = (m1 + m2, i1 + g * EXPERTS_PER_GROUP, i2 + g * EXPERTS_PER_GROUP, s1, s2)
        if best is None:
            best = cand
        else:
            gt = cand[0] > best[0]
            best = tuple(jnp.where(gt, cn, bs) for cn, bs in zip(cand, best))
        yield
    _, e1, e2, w1, w2 = best
    tot = w1 + w2
    idx_ref[0, :, rows] = jnp.concatenate([e1, e2], axis=0)
    gw_ref[0, :, rows] = jnp.concatenate([w1 / tot, w2 / tot], axis=0)
    eio = lax.broadcasted_iota(jnp.int32, (N_EXPERTS, n), 0)
    oh = jnp.logical_or(eio == e1, eio == e2).astype(F32)
    counts.append(jnp.sum(oh, axis=1, keepdims=True))


def _alternate_row_streams(make_stream, cnt_ref):
    @pl.when(pl.program_id(0) == 0)
    def _():
        cnt_ref[...] = jnp.zeros_like(cnt_ref)

    counts = []
    _alternate(*[make_stream(slice(r, r + ROW_STREAM), counts) for r in range(0, TM, ROW_STREAM)])
    cnt_ref[...] += _ceil_rows(sum(counts))


def _router_specs(ntok, d):
    nt = ntok // TM
    in_specs = [pl.BlockSpec((N_EXPERTS, d), lambda i: (0, 0)),
                pl.BlockSpec((N_EXPERTS, LANES), lambda i: (0, 0))]
    out_specs = [pl.BlockSpec((TM, d), lambda i: (i, 0)),
                 pl.BlockSpec((1, TOP_K, TM), lambda i: (i, 0, 0)),
                 pl.BlockSpec((1, TOP_K, TM), lambda i: (i, 0, 0)),
                 pl.BlockSpec((N_EXPERTS, LANES), lambda i: (0, 0))]
    out_shapes = [jax.ShapeDtypeStruct((ntok, d), BF16),
                  jax.ShapeDtypeStruct((nt, TOP_K, TM), jnp.int32),
                  jax.ShapeDtypeStruct((nt, TOP_K, TM), F32),
                  jax.ShapeDtypeStruct((N_EXPERTS, LANES), F32)]
    return in_specs, out_specs, out_shapes


def _ceil_rows(c):
    return jnp.ceil(c * (1.0 / ROWG)) * ROWG


def _meta_kernel(idx_ref, start_ref, qpos_ref, meta_ref, run_ref):
    @pl.when(pl.program_id(0) == 0)
    def _():
        run_ref[...] = jnp.zeros_like(run_ref)

    eio = lax.broadcasted_iota(jnp.int32, (N_EXPERTS, TM), 0)
    r = lax.broadcasted_iota(jnp.int32, (TM, TM), 0)
    c = lax.broadcasted_iota(jnp.int32, (TM, TM), 1)
    before = (r < c).astype(BF16)
    er = lax.broadcasted_iota(jnp.int32, (N_EXPERTS, N_EXPERTS), 0)
    ec = lax.broadcasted_iota(jnp.int32, (N_EXPERTS, N_EXPERTS), 1)
    lower = (ec < er).astype(F32)
    lane = lax.broadcasted_iota(jnp.int32, (N_EXPERTS, LANES), 1)
    run = run_ref[...]
    for g in range(idx_ref.shape[0]):
        oh1 = eio == idx_ref[g, 0:1, :]
        oh2 = eio == idx_ref[g, 1:2, :]
        oh = jnp.logical_or(oh1, oh2)
        rank = jnp.dot(oh.astype(BF16), before, preferred_element_type=F32)
        run_len = jnp.broadcast_to(_ceil_rows(jnp.sum(oh.astype(F32), axis=1, keepdims=True)), (N_EXPERTS, LANES))
        off = jnp.dot(lower, run_len, precision=HI, preferred_element_type=F32)
        pos = rank + off[:, 0:1]
        q1 = jnp.sum(jnp.where(oh1, pos, 0.0), axis=0, keepdims=True)
        q2 = jnp.sum(jnp.where(oh2, pos, 0.0), axis=0, keepdims=True)
        qpos_ref[g] = jnp.concatenate([q1, q2], axis=0).astype(jnp.int32)
        meta = jnp.where(lane == 0, start_ref[...] + run, jnp.where(lane == 1, off, run_len))
        meta_ref[g] = (meta * (1.0 / ROWG)).astype(jnp.int32)
        run = run + run_len
    run_ref[...] = run


def _meta(idx_t, start):
    nt = idx_t.shape[0]
    grp = next(g for g in (4, 3, 2, 1) if nt % g == 0)
    return pl.pallas_call(
        _meta_kernel,
        grid=(nt // grp,),
        in_specs=[pl.BlockSpec((grp, TOP_K, TM), lambda i: (i, 0, 0)),
                  pl.BlockSpec((N_EXPERTS, LANES), lambda i: (0, 0))],
        out_specs=[pl.BlockSpec((grp, TOP_K, TM), lambda i: (i, 0, 0)),
                   pl.BlockSpec((grp, N_EXPERTS, LANES), lambda i: (i, 0, 0))],
        out_shape=[jax.ShapeDtypeStruct((nt, TOP_K, TM), jnp.int32),
                   jax.ShapeDtypeStruct((nt, N_EXPERTS, LANES), jnp.int32)],
        scratch_shapes=[pltpu.VMEM((N_EXPERTS, LANES), F32)],
        compiler_params=_cparams(("arbitrary",)),
        name="moe_meta",
    )(idx_t, start)


def _sort_select(qpos_ref):
    r = lax.broadcasted_iota(jnp.int32, (SORT_ROWS, TM), 0)
    return r == qpos_ref[0, 0:1, :], r == qpos_ref[0, 1:2, :]


def _start_run_copies(i, gb_ref, ob_ref, nb_ref, make_copy):
    big = COPY_GROUPS * ROWG
    for e in range(N_EXPERTS):
        n = nb_ref[i * N_EXPERTS + e]
        buf_row = gb_ref[i * N_EXPERTS + e] * ROWG
        tile_row = ob_ref[i * N_EXPERTS + e] * ROWG
        n_big = lax.shift_right_logical(n, COPY_SHIFT)
        prio = e % 2

        def body_big(k, carry, buf_row=buf_row, tile_row=tile_row, prio=prio):
            make_copy(pl.multiple_of(tile_row + k * big, ROWG), pl.multiple_of(buf_row + k * big, ROWG),
                      big).start(priority=prio)
            return carry

        def body_one(k, carry, buf_row=buf_row, tile_row=tile_row, prio=prio):
            make_copy(pl.multiple_of(tile_row + k * ROWG, ROWG), pl.multiple_of(buf_row + k * ROWG, ROWG),
                      ROWG).start(priority=prio)
            return carry

        lax.fori_loop(0, n_big, body_big, 0)
        lax.fori_loop(lax.shift_left(n_big, COPY_SHIFT), n, body_one, 0)


def _wait_run_copies(i, nb_ref, make_copy):
    n_big = 0
    n_one = 0
    for e in range(N_EXPERTS):
        n = nb_ref[i * N_EXPERTS + e]
        n_big = n_big + lax.shift_right_logical(n, COPY_SHIFT)
        n_one = n_one + lax.bitwise_and(n, COPY_GROUPS - 1)

    def wait_big(k, carry):
        make_copy(0, 0, COPY_GROUPS * ROWG).wait()
        return carry

    def wait_one(k, carry):
        make_copy(0, 0, ROWG).wait()
        return carry

    lax.fori_loop(0, n_big, wait_big, 0)
    lax.fori_loop(0, n_one, wait_one, 0)


def _scatter_kernel(gb_ref, ob_ref, nb_ref, ts_ref, tn_ref, nu_ref, v_ref, qpos_ref, gw_ref, buf_ref, xs_ref, z_ref,
                    sem, zsem):
    i = pl.program_id(0)
    last = pl.num_programs(0) - 1
    slot = i % 2
    d = v_ref.shape[1]
    m0, m1 = _sort_select(qpos_ref)
    sel = jnp.logical_or(m0, m1).astype(BF16)
    xs_ref[slot, :, 0:d] = jnp.dot(sel, v_ref[...], preferred_element_type=F32)
    gate = jnp.sum(jnp.where(m0, gw_ref[0, 0:1, :], 0.0) + jnp.where(m1, gw_ref[0, 1:2, :], 0.0),
                   axis=1, keepdims=True)
    xs_ref[slot, :, d:d + LANES] = jnp.broadcast_to(gate, (SORT_ROWS, LANES))

    def copy_from(s):
        def copy(tile_row, buf_row, rows):
            return pltpu.make_async_copy(xs_ref.at[s, pl.ds(tile_row, rows)], buf_ref.at[pl.ds(buf_row, rows)],
                                         sem.at[s])
        return copy

    _start_run_copies(i, gb_ref, ob_ref, nb_ref, copy_from(slot))

    @pl.when(i > 0)
    def _():
        _wait_run_copies(i - 1, nb_ref, copy_from(1 - slot))

    @pl.when(i == last)
    def _():
        _wait_run_copies(i, nb_ref, copy_from(slot))

    @pl.when(i == last)
    def _():
        z_ref[...] = jnp.zeros_like(z_ref)

        def zcopy(buf_row):
            return pltpu.make_async_copy(z_ref.at[pl.ds(0, ROWG)], buf_ref.at[pl.ds(buf_row, ROWG)], zsem)

        total = 0
        for e in range(N_EXPERTS):
            n = tn_ref[e]
            row0 = ts_ref[e] * ROWG

            def body(k, carry, row0=row0):
                zcopy(pl.multiple_of(row0 + k * ROWG, ROWG)).start()
                return carry

            lax.fori_loop(0, n, body, 0)
            total = total + n

        def wbody(k, carry):
            zcopy(0).wait()
            return carry

        lax.fori_loop(0, total, wbody, 0)

        def zblock(blk):
            return pltpu.make_async_copy(z_ref, buf_ref.at[pl.ds(pl.multiple_of(blk * FFN_ROWS, FFN_ROWS), FFN_ROWS)],
                                         zsem)

        def bbody(blk, carry):
            zblock(blk).start()
            return carry

        def bwait(blk, carry):
            zblock(blk).wait()
            return carry

        nblk = buf_ref.shape[0] // FFN_ROWS
        lax.fori_loop(nu_ref[0], nblk, bbody, 0)
        lax.fori_loop(nu_ref[0], nblk, bwait, 0)


def _scatter(gb, ob, nb8, tail_s, tail_n, n_used, v, qpos, gw_t, nrow):
    ntok, d = v.shape
    grid_spec = pltpu.PrefetchScalarGridSpec(
        num_scalar_prefetch=6,
        grid=(ntok // TM,),
        in_specs=[pl.BlockSpec((TM, d), lambda i, *_: (i, 0)),
                  pl.BlockSpec((1, TOP_K, TM), lambda i, *_: (i, 0, 0)),
                  pl.BlockSpec((1, TOP_K, TM), lambda i, *_: (i, 0, 0))],
        out_specs=pl.BlockSpec(memory_space=pl.ANY),
        scratch_shapes=[pltpu.VMEM((2, SORT_ROWS, d + LANES), F32), pltpu.VMEM((FFN_ROWS, d + LANES), F32),
                        pltpu.SemaphoreType.DMA((2,)), pltpu.SemaphoreType.DMA(())],
    )
    return pl.pallas_call(
        _scatter_kernel,
        grid_spec=grid_spec,
        out_shape=jax.ShapeDtypeStruct((nrow, d + LANES), F32),
        compiler_params=_cparams(("arbitrary",)),
        name="moe_scatter",
    )(gb, ob, nb8, tail_s, tail_n, n_used, v, qpos, gw_t)


def _moe_kernel(be_ref, nu_ref, x_ref, wg_ref, wu_ref, wd_ref, o_ref, wg_bf, wu_bf, wd_bf):
    i = pl.program_id(0)
    d = o_ref.shape[1]

    @pl.when(jnp.logical_or(i == 0, be_ref[i] != be_ref[jnp.maximum(i - 1, 0)]))
    def _():
        wg_bf[...] = wg_ref[...].astype(BF16)
        wu_bf[...] = wu_ref[...].astype(BF16)
        wd_bf[...] = wd_ref[...].astype(BF16)

    @pl.when(i < nu_ref[0])
    def _():
        xb = x_ref[...]
        x = xb[:, 0:d].astype(BF16)
        g = jnp.dot(x, wg_bf[...], preferred_element_type=F32)
        u = jnp.dot(x, wu_bf[...], preferred_element_type=F32)
        h = (_silu(g) * u).astype(BF16)
        y = jnp.dot(h, wd_bf[...], preferred_element_type=F32)
        o_ref[...] = y * xb[:, d:d + 1]

    @pl.when(pl.program_id(0) >= nu_ref[0])
    def _():
        o_ref[...] = jnp.zeros_like(o_ref)


def _moe_ffn(block_e, n_used, buf, layer, wg, wu, wd):
    nrow, dw = buf.shape
    d = dw - LANES
    nblk = nrow // FFN_ROWS
    de = wg.shape[3]
    used = lambda i, nu: jnp.minimum(i, nu[0] - 1)
    grid_spec = pltpu.PrefetchScalarGridSpec(
        num_scalar_prefetch=2,
        grid=(nblk,),
        in_specs=[pl.BlockSpec((FFN_ROWS, dw), lambda i, be, nu: (used(i, nu), 0)),
                  pl.BlockSpec((None, None, d, de), lambda i, be, nu: (layer, be[i], 0, 0)),
                  pl.BlockSpec((None, None, d, de), lambda i, be, nu: (layer, be[i], 0, 0)),
                  pl.BlockSpec((None, None, de, d), lambda i, be, nu: (layer, be[i], 0, 0))],
        out_specs=pl.BlockSpec((FFN_ROWS, d), lambda i, be, nu: (i, 0)),
        scratch_shapes=[pltpu.VMEM((d, de), BF16), pltpu.VMEM((d, de), BF16), pltpu.VMEM((de, d), BF16)],
    )
    return pl.pallas_call(
        _moe_kernel,
        grid_spec=grid_spec,
        out_shape=jax.ShapeDtypeStruct((nrow, d), F32),
        compiler_params=pltpu.CompilerParams(dimension_semantics=("arbitrary",), vmem_limit_bytes=FFN_VMEM_LIMIT),
        name="moe_ffn",
    )(block_e, n_used, buf, wg, wu, wd)


def _ln2_kernel(gb_ref, ob_ref, nb_ref, x_ref, qpos_ref, mod_ref, lg_ref, lb_ref, y_ref, *rest, with_inproj):
    if with_inproj:
        modn_ref, win_ref, o_ref, p_ref, ys_ref, sem = rest
    else:
        o_ref, ys_ref, sem = rest
    i = pl.program_id(0)
    slot = i % 2

    def copy_to(s):
        def copy(tile_row, buf_row, rows):
            return pltpu.make_async_copy(y_ref.at[pl.ds(buf_row, rows)], ys_ref.at[s, pl.ds(tile_row, rows)],
                                         sem.at[s])
        return copy

    @pl.when(i == 0)
    def _():
        ys_ref[...] = jnp.zeros_like(ys_ref)
        _start_run_copies(0, gb_ref, ob_ref, nb_ref, copy_to(0))

    @pl.when(i + 1 < pl.num_programs(0))
    def _():
        _start_run_copies(i + 1, gb_ref, ob_ref, nb_ref, copy_to(1 - slot))

    _wait_run_copies(i, nb_ref, copy_to(slot))
    m0, m1 = _sort_select(qpos_ref)
    sel = jnp.logical_or(m0, m1).astype(BF16)
    ys = ys_ref[slot].astype(BF16)

    def stream(rows):
        y = lax.dot_general(sel[:, rows], ys, (((0,), (0,)), ((), ())), preferred_element_type=F32)
        yield
        gate = mod_ref[0, 5:6, :]
        x_new = _ln(ALPHA * x_ref[rows, :] + gate * y, lg_ref[...], lb_ref[...])
        o_ref[rows, :] = x_new
        yield
        if with_inproj:
            u = (x_new * (1.0 + modn_ref[0, 1:2, :]) + modn_ref[0, 0:1, :]).astype(BF16)
            p_ref[rows, :] = jnp.dot(u, win_ref[...], preferred_element_type=F32).astype(p_ref.dtype)

    _alternate(*[stream(slice(r, r + ROW_STREAM)) for r in range(0, TM, ROW_STREAM)])


def _ln2(gb, ob, nb8, x2, qpos, mod, lg, lb, ybuf, tps, nb, next_in):
    ntok, d = x2.shape
    latent_only = next_in is None
    if latent_only:
        out_map = lambda i, *_: ((i // tps) * (tps - 1) + jnp.maximum(i % tps - 1, 0), 0)
        out_rows = ntok - nb * TM
    else:
        out_map = lambda i, *_: (i, 0)
        out_rows = ntok
    in_specs = [pl.BlockSpec((TM, d), lambda i, *_: (i, 0)),
                pl.BlockSpec((1, TOP_K, TM), lambda i, *_: (i, 0, 0)),
                pl.BlockSpec((1, 6, d), lambda i, *_: (_mod_row(i, tps, nb), 0, 0)),
                pl.BlockSpec((1, d), lambda i, *_: (0, 0)),
                pl.BlockSpec((1, d), lambda i, *_: (0, 0)),
                pl.BlockSpec(memory_space=pl.ANY)]
    out_specs = [pl.BlockSpec((TM, d), out_map)]
    out_shape = [jax.ShapeDtypeStruct((out_rows, d), F32)]
    args = [gb, ob, nb8, x2, qpos, mod, lg, lb, ybuf]
    if next_in is not None:
        mod_next, w_next = next_in
        nw = w_next.shape[1]
        in_specs += [pl.BlockSpec((1, 6, d), lambda i, *_: (_mod_row(i, tps, nb), 0, 0)),
                     pl.BlockSpec((d, nw), lambda i, *_: (0, 0))]
        out_specs.append(pl.BlockSpec((TM, nw), lambda i, *_: (i, 0)))
        out_shape.append(jax.ShapeDtypeStruct((ntok, nw), BF16))
        args += [mod_next, w_next]
    grid_spec = pltpu.PrefetchScalarGridSpec(
        num_scalar_prefetch=3,
        grid=(ntok // TM,),
        in_specs=in_specs,
        out_specs=out_specs,
        scratch_shapes=[pltpu.VMEM((2, SORT_ROWS, d), F32), pltpu.SemaphoreType.DMA((2,))],
    )
    outs = pl.pallas_call(
        functools.partial(_ln2_kernel, with_inproj=next_in is not None),
        grid_spec=grid_spec,
        out_shape=out_shape,
        compiler_params=_cparams(("arbitrary",)),
        name="post_moe_ln",
    )(*args)
    return (outs[0], outs[1]) if next_in is not None else (outs[0], None)


def _even_w_in(w):
    o = np.cumsum((0, GLA_QK, GLA_QK, GLA_V, GLA_V, GLA_RANK, GLA_RANK, GQA_Q, GQA_KV, GQA_KV))
    q, k, v, r = w[:, o[0]:o[1]], w[:, o[1]:o[2]], w[:, o[2]:o[3]], w[:, o[3]:o[4]]
    lr = w[:, o[4]:o[6]]
    gq, gk, gv = w[:, o[6]:o[7]], w[:, o[7]:o[8]], w[:, o[8]:o[9]]
    d = w.shape[0]
    zeros = lambda n: jnp.zeros((d, n), w.dtype)
    out = jnp.concatenate([gq, gk, gv, q, k, lr, zeros(LANES - 2 * GLA_RANK), zeros(LANES), v, r], axis=1)
    assert out.shape[1] == EVEN_W
    return out.astype(BF16)


def _odd_w_in(w):
    o = np.cumsum((0, SSD_INNER, SSD_INNER, SSD_BC, SSD_BC, SSD_HEADS, SSD_HEADS, FNET_W))
    d = w.shape[0]
    out = jnp.concatenate([w[:, o[1]:o[4]], w[:, :o[1]], w[:, o[4]:o[6]],
                           jnp.zeros((d, LANES - 2 * SSD_HEADS), w.dtype), w[:, o[6]:o[7]]], axis=1)
    assert out.shape[1] == ODD_W
    return out.astype(BF16)


def _rope_tables(ctx_len, seq):
    rows = seq // GRID_W
    r = jnp.repeat(jnp.arange(rows, dtype=F32), GRID_W)
    col = jnp.tile(jnp.arange(GRID_W, dtype=F32), rows)
    half = GQA_HD // 2
    inv = ROPE_THETA ** (-jnp.arange(0, half, 2, dtype=F32) / half)
    ar = r[:, None] * inv
    ac = col[:, None] * inv
    ang = jnp.concatenate([ar, ar, ac, ac], -1)
    cos = jnp.concatenate([jnp.ones((ctx_len, GQA_HD), F32), jnp.cos(ang)], 0)
    sin = jnp.concatenate([jnp.zeros((ctx_len, GQA_HD), F32), jnp.sin(ang)], 0)
    return jnp.tile(cos, (1, 2)), jnp.tile(sin, (1, 2))


def _even_layer(x2, p2, mod, nb, t, tps, ctx_len, w_o, w_dec, b_dec, gla_g, qn_g, kn_g, cos2, sin2, lg, lb, rw_t, rb):
    p3 = p2.reshape(nb, t, EVEN_W)
    wd_pads = [jnp.zeros((LANES, GLA_QK), F32).at[di * GLA_RANK:(di + 1) * GLA_RANK].set(w_dec[di])
               for di in range(2)]
    outs = _gla_scan(p3, wd_pads, [b_dec[di][None, :] for di in range(2)])
    qh2, kg2, vt4 = _qkprep(p2, cos2, sin2, jnp.tile(qn_g, 2)[None, :], jnp.tile(kn_g, 2)[None, :], tps)
    att = _attention(qh2.reshape(nb, t, GQA_HEADS * LANES), kg2.reshape(nb, t, GQA_KV_HEADS * LANES), vt4, ctx_len)
    w1, w2 = w_o[:GLA_V].astype(BF16), w_o[GLA_V:].astype(BF16)
    ntok = nb * t
    return _evout(outs[0].reshape(ntok, GLA_V), outs[1].reshape(ntok, GLA_V), p2, att.reshape(ntok, GQA_Q),
                  jnp.tile(gla_g, GLA_HEADS)[None, :], w1, w2, x2, mod, lg, lb, rw_t, rb, tps, nb)


def _odd_layer(x2, p2, mod, nb, t, tps, ctx_len, w_o, conv_w, conv_b, dt_bias, a_log, d_skip, ssd_g, fnet_g,
               lg, lb, rw_t, rb):
    p3 = p2.reshape(nb, t, ODD_W)
    w8 = jnp.zeros((8, SSD_CONV_CH), F32).at[:SSD_CONV].set(conv_w)
    xbc2 = _conv(p2, w8, conv_b[None, :], tps)
    xbc3 = xbc2.reshape(nb, t, SSD_CONV_CH)
    params = []
    for di in range(2):
        lo = di * SSD_HEADS
        pad = (lo, LANES - lo - SSD_HEADS)
        bias_pad = jnp.pad(dt_bias[di], pad)[None, :]
        a_pad = jnp.pad(-jnp.exp(a_log[di]), pad)[None, :]
        eexp = np.zeros((LANES, SSD_INNER), np.float32)
        eexp[lo + np.arange(SSD_INNER) // SSD_HD, np.arange(SSD_INNER)] = 1.0
        esel = np.zeros((16, LANES), np.float32)
        esel[np.arange(SSD_HEADS), lo + np.arange(SSD_HEADS)] = 1.0
        params.append((bias_pad, a_pad, jnp.asarray(eexp), jnp.asarray(esel)))
    ys = _ssd_scan(xbc3, p3, params)
    f3 = p3[:, :, ODD_W - FNET_W:]
    g256 = fnet_g[None, :]
    fmix = jnp.concatenate([_fourier_ctx(f3[:, :ctx_len], g256), _fourier_latent(f3[:, ctx_len:], g256)], axis=1)
    ntok = nb * t
    return _odout(ys[0].reshape(ntok, SSD_INNER), ys[1].reshape(ntok, SSD_INNER), xbc2, p2,
                  fmix.reshape(ntok, FNET_W), jnp.repeat(d_skip, SSD_HD)[None, :], ssd_g[None, :],
                  w_o[:SSD_INNER].astype(BF16), w_o[SSD_INNER:].astype(BF16), x2, mod, lg, lb, rw_t, rb, tps, nb)


def _moe_layer(routed, mod, nb, tps, layer, wg, wu, wd, lg, lb, next_in):
    x2, v, idx_t, gw_t, cnt = routed
    ntok, d = x2.shape
    nt = ntok // TM
    rows = cnt[:, 0].astype(jnp.int32)
    padded = (rows + FFN_ROWS - 1) // FFN_ROWS * FFN_ROWS
    pad_end = jnp.cumsum(padded)
    pad_start = pad_end - padded
    nblk = -(-(ntok * TOP_K + nt * N_EXPERTS * (ROWG - 1)) // FFN_ROWS) + N_EXPERTS
    blk_row = jnp.arange(nblk, dtype=jnp.int32)[:, None] * FFN_ROWS
    block_e = jnp.minimum(jnp.sum((blk_row >= pad_end[None, :]).astype(jnp.int32), axis=1), N_EXPERTS - 1)
    n_used = (pad_end[-1:] // FFN_ROWS).astype(jnp.int32)
    start = jnp.broadcast_to(pad_start.astype(F32)[:, None], (N_EXPERTS, LANES))
    qpos, meta = _meta(idx_t, start)
    gb, ob, nb8 = (meta[:, :, j].reshape(nt * N_EXPERTS) for j in range(3))
    tail_s = ((pad_start + rows) // ROWG).astype(jnp.int32)
    tail_n = ((padded - rows) // ROWG).astype(jnp.int32)
    buf = _scatter(gb, ob, nb8, tail_s, tail_n, n_used, v, qpos, gw_t, nblk * FFN_ROWS)
    y = _moe_ffn(block_e, n_used, buf, layer, wg, wu, wd)
    return _ln2(gb, ob, nb8, x2, qpos, mod, lg, lb, y, tps, nb, next_in)


def kernel(x, c, ctx, c_ctx, ada_w, ada_b, ln_g, ln_b, ev_w_in, ev_w_o, gla_w_decay, gla_b_decay, gla_norm_g,
           gqa_q_norm_g, gqa_k_norm_g, od_w_in, od_w_o, ssd_conv_w, ssd_conv_b, ssd_dt_bias, ssd_a_log, ssd_d,
           ssd_norm_g, fnet_norm_g, router_w, router_b, exp_w_gate, exp_w_up, exp_w_down):
    nb, seq, d = x.shape
    ctx_len = ctx.shape[1]
    assert ctx_len == TM and seq % TM == 0 and d == D_MODEL and nb <= 7
    t = ctx_len + seq
    tps = t // TM
    ntok = nb * t
    cc = jnp.zeros((8, d), F32).at[:nb].set(c).at[nb].set(c_ctx)
    mod_all = _ada_all(cc, ada_w, ada_b).reshape(DEPTH, 8, 6, d)
    cos2, sin2 = _rope_tables(ctx_len, seq)
    rw_t = router_w.T
    rb = jnp.broadcast_to(router_b[:, None], (N_EXPERTS, LANES))
    x2 = jnp.concatenate([ctx, x], axis=1).reshape(ntok, d)

    def w_in_of(layer):
        return _even_w_in(ev_w_in[layer // 2]) if layer % 2 == 0 else _odd_w_in(od_w_in[layer // 2])

    p2 = _inproj(x2, mod_all[0], w_in_of(0), tps, nb)
    for layer in range(DEPTH):
        mod = mod_all[layer]
        i = layer // 2
        lg0, lb0 = ln_g[layer, 0][None, :], ln_b[layer, 0][None, :]
        lg1, lb1 = ln_g[layer, 1][None, :], ln_b[layer, 1][None, :]
        if layer % 2 == 0:
            routed = _even_layer(x2, p2, mod, nb, t, tps, ctx_len, ev_w_o[i], gla_w_decay[i], gla_b_decay[i],
                                 gla_norm_g[i], gqa_q_norm_g[i], gqa_k_norm_g[i], cos2, sin2, lg0, lb0, rw_t, rb)
        else:
            routed = _odd_layer(x2, p2, mod, nb, t, tps, ctx_len, od_w_o[i], ssd_conv_w[i], ssd_conv_b[i],
                                ssd_dt_bias[i], ssd_a_log[i], ssd_d[i], ssd_norm_g[i], fnet_norm_g[i], lg0, lb0,
                                rw_t, rb)
        next_in = (mod_all[layer + 1], w_in_of(layer + 1)) if layer + 1 < DEPTH else None
        x2, p2 = _moe_layer(routed, mod, nb, tps, layer, exp_w_gate, exp_w_up, exp_w_down, lg1, lb1, next_in)
    return x2.reshape(nb, seq, d)
```
